```python
import jax, jax.numpy as jnp
from jax import lax
import numpy as np

D_MODEL = 1024
BATCH = 16
SEQ = 256
DEPTH = 4
DEC_BATCH = 2
DEC_SEQ = 1024
PAST_LEN = 256

GRID_W = 64
N_MIXERS = 4
N_MOD = 6
D_FF = 4 * D_MODEL
EPS = 1e-6
FFT_GROUPS = 8
RWKV_HS = 64
RWKV_H = D_MODEL // RWKV_HS
LORA_W = 64
LORA_A = 64
LORA_G = 128
GN_EPS = 64e-5
MLA_H = 8
D_NOPE = 128
D_ROPE = 64
D_V = 128
KV_RANK = 256
Q_RANK = 384
ROPE_BASE = 10000.0
MLA_SCALE = (D_NOPE + D_ROPE) ** -0.5
N_FFT_LAYERS = (DEPTH + 3) // 4
N_CONV_LAYERS = (DEPTH + 2) // 4
N_RWKV_LAYERS = (DEPTH + 1) // 4
N_MLA_LAYERS = DEPTH // 4

kernel_name = "hybrid_fourier_conv_rwkv7_mla_diffusion_step"


def rmsnorm(x, g):
    xf = x.astype(jnp.float32)
    y = xf * lax.rsqrt(jnp.mean(xf * xf, axis=-1, keepdims=True) + EPS)
    return (y * g).astype(x.dtype)


def modulation(cvec, w, b):
    m = jax.nn.silu(cvec) @ w + b
    return jnp.split(m[:, None, :], N_MOD, axis=-1)


def shift_neighbours(x):
    xp = jnp.pad(x, ((0, 0), (1, 1), (0, 0)))
    return xp[:, :-2], xp[:, 2:]


def fourier_mix(x, w_out):
    b, n, d = x.shape
    xg = x.astype(jnp.float32).reshape(b, n, FFT_GROUPS, d // FFT_GROUPS)
    f = jnp.fft.fft2(xg, axes=(1, 3), norm="ortho").real
    return f.reshape(b, n, d).astype(x.dtype) @ w_out


def short_conv_mix(x, w_in, w_conv, w_out):
    bg, cg, u = jnp.split(x @ w_in, 3, axis=-1)
    z = cg * u
    z_prev, z_next = shift_neighbours(z)
    conv = z_prev * w_conv[0] + z * w_conv[1] + z_next * w_conv[2]
    return (bg * conv) @ w_out


def wkv7_scan(s0, r, w, k, v, kk, a, reverse):
    def step(s, inp):
        rt, wt, kt, vt, kkt, at = inp
        sa = jnp.einsum('bhvk,bhk->bhv', s, -kkt)
        s = s * wt[:, :, None, :] + sa[..., None] * (kkt * at)[:, :, None, :] + vt[..., None] * kt[:, :, None, :]
        return s, jnp.einsum('bhvk,bhk->bhv', s, rt)
    xs = tuple(jnp.moveaxis(t, 1, 0) for t in (r, w, k, v, kk, a))
    s_fin, ys = lax.scan(step, s0, xs, reverse=reverse)
    return s_fin, jnp.moveaxis(ys, 0, 1)


def rwkv7_mix(x, s_init, mu, w_r, w_k, w_v, w_o, w0, w_l1, w_l2, a0, a_l1, a_l2,
              g_l1, g_l2, k_k, k_a, r_k, ln_g, ln_b):
    b, n, d = x.shape
    x_prev, x_next = shift_neighbours(x)
    dx = 0.5 * (x_prev + x_next) - x
    xr, xw, xk, xv, xa, xg = [x + dx * mu[i] for i in range(6)]
    heads = lambda t: t.reshape(b, n, RWKV_H, RWKV_HS).astype(jnp.float32)
    r = heads(xr @ w_r)
    k = heads(xk @ w_k)
    v = heads(xv @ w_v)
    kk = k * k_k.reshape(RWKV_H, RWKV_HS).astype(jnp.float32)
    kk = kk * lax.rsqrt(jnp.sum(kk * kk, axis=-1, keepdims=True) + 1e-12)
    k_a_h = k_a.reshape(RWKV_H, RWKV_HS).astype(jnp.float32)
    r_k_f = r_k.astype(jnp.float32)
    g = jax.nn.sigmoid(xg @ g_l1) @ g_l2
    outs, bonuses, finals = [], [], []
    for dirn in range(2):
        logw = -jax.nn.softplus(-(w0[dirn] + jnp.tanh(xw @ w_l1[dirn]) @ w_l2[dirn])) - 0.5
        decay = heads(jnp.exp(-jnp.exp(logw.astype(jnp.float32))))
        a = heads(jax.nn.sigmoid(a0[dirn] + (xa @ a_l1[dirn]) @ a_l2[dirn]))
        k_eff = k * (1.0 + (a - 1.0) * k_a_h)
        s_fin, yd = wkv7_scan(s_init[:, dirn].astype(jnp.float32), r, decay, k_eff, v, kk, a, dirn == 1)
        outs.append(yd)
        bonuses.append(jnp.sum(r * k_eff * r_k_f, axis=-1, keepdims=True) * v)
        finals.append(s_fin)
    y = outs[0] + outs[1]
    mean = jnp.mean(y, axis=-1, keepdims=True)
    var = jnp.mean(jnp.square(y - mean), axis=-1, keepdims=True)
    y = ((y - mean) * lax.rsqrt(var + GN_EPS)).reshape(b, n, d) * ln_g + ln_b
    y = y + (bonuses[0] + bonuses[1]).reshape(b, n, d)
    out = (y.astype(x.dtype) * g) @ w_o
    return out, jnp.stack(finals, axis=1)


def rope_2d(x):
    n = x.shape[1]
    rows = n // GRID_W
    row = jnp.repeat(jnp.arange(rows), GRID_W)
    col = jnp.tile(jnp.arange(GRID_W), rows)
    pos = jnp.stack([row, col], axis=-1).astype(jnp.float32)
    quarter = D_ROPE // 4
    inv = ROPE_BASE ** (-jnp.arange(quarter, dtype=jnp.float32) / quarter)
    ang = pos[:, :, None] * inv
    cos = jnp.cos(ang)[:, None]
    sin = jnp.sin(ang)[:, None]
    xr = x.astype(jnp.float32).reshape(x.shape[:-1] + (2, 2, quarter))
    x1, x2 = xr[..., 0, :], xr[..., 1, :]
    out = jnp.stack([x1 * cos - x2 * sin, x1 * sin + x2 * cos], axis=-2)
    return out.reshape(x.shape).astype(x.dtype)


def mla_queries(x, w_dq, g_q, w_uq, positional):
    b, n, _ = x.shape
    q = (rmsnorm(x @ w_dq, g_q) @ w_uq).reshape(b, n, MLA_H, D_NOPE + D_ROPE)
    q_nope, q_rope = q[..., :D_NOPE], q[..., D_NOPE:]
    if positional:
        q_rope = rope_2d(q_rope)
    return q_nope, q_rope


def mla_compress(x, w_dkv, g_kv, positional):
    ckr = x @ w_dkv
    ckv = rmsnorm(ckr[..., :KV_RANK], g_kv)
    k_rope = ckr[..., KV_RANK:]
    if positional:
        k_rope = rope_2d(k_rope[:, :, None, :])[:, :, 0, :]
    return ckv, k_rope


def mla_attend(q_nope, q_rope, ckv, k_rope, w_uk, w_uv, w_o):
    b, nk, _ = ckv.shape
    k_nope = (ckv @ w_uk).reshape(b, nk, MLA_H, D_NOPE)
    v = (ckv @ w_uv).reshape(b, nk, MLA_H, D_V)
    s = (jnp.einsum('bqhd,bkhd->bhqk', q_nope, k_nope)
         + jnp.einsum('bqhr,bkr->bhqk', q_rope, k_rope)).astype(jnp.float32) * MLA_SCALE
    p = jax.nn.softmax(s, axis=-1).astype(v.dtype)
    o = jnp.einsum('bhqk,bkhd->bqhd', p, v)
    return o.reshape(b, -1, MLA_H * D_V) @ w_o


def setup_inputs(seed: int = 0) -> dict:
    key = jax.random.key(seed)
    ks = iter(jax.random.split(key, 64))
    D = D_MODEL

    def nrm(shape, scale):
        return jax.random.normal(next(ks), shape, jnp.float32) * scale

    def gain(shape):
        return 1.0 + nrm(shape, 0.1)

    return {
        "x_prompt": nrm((BATCH, SEQ, D), 1.0),
        "x_sample": nrm((DEC_BATCH, DEC_SEQ, D), 1.0),
        "state_rwkv": nrm((DEC_BATCH, N_RWKV_LAYERS, 2, RWKV_H, RWKV_HS, RWKV_HS), 0.3),
        "cache_mla_ckv": nrm((DEC_BATCH, N_MLA_LAYERS, PAST_LEN, KV_RANK), 1.0),
        "cache_mla_krope": nrm((DEC_BATCH, N_MLA_LAYERS, PAST_LEN, D_ROPE), 1.0),
        "c": nrm((DEC_BATCH, D), 1.0),
        "c_ctx": nrm((D,), 1.0),
        "mod_w": nrm((DEPTH, D, N_MOD * D), D ** -0.5),
        "mod_b": nrm((DEPTH, N_MOD * D), 0.1),
        "norm_g": gain((DEPTH, 4, D)),
        "mlp_w1": nrm((DEPTH, D, D_FF), D ** -0.5),
        "mlp_w2": nrm((DEPTH, D_FF, D), D_FF ** -0.5),
        "fft_w_out": nrm((N_FFT_LAYERS, D, D), D ** -0.5),
        "conv_w_in": nrm((N_CONV_LAYERS, D, 3 * D), D ** -0.5),
        "conv_w": nrm((N_CONV_LAYERS, 3, D), 3 ** -0.5),
        "conv_w_out": nrm((N_CONV_LAYERS, D, D), D ** -0.5),
        "rwkv_mu": jax.random.uniform(next(ks), (N_RWKV_LAYERS, 6, D), jnp.float32),
        "rwkv_w_r": nrm((N_RWKV_LAYERS, D, D), D ** -0.5),
        "rwkv_w_k": nrm((N_RWKV_LAYERS, D, D), D ** -0.5),
        "rwkv_w_v": nrm((N_RWKV_LAYERS, D, D), D ** -0.5),
        "rwkv_w_o": nrm((N_RWKV_LAYERS, D, D), D ** -0.5),
        "rwkv_w0": nrm((N_RWKV_LAYERS, 2, D), 0.5),
        "rwkv_w_l1": nrm((N_RWKV_LAYERS, 2, D, LORA_W), D ** -0.5),
        "rwkv_w_l2": nrm((N_RWKV_LAYERS, 2, LORA_W, D), 0.3 * LORA_W ** -0.5),
        "rwkv_a0": nrm((N_RWKV_LAYERS, 2, D), 0.5),
        "rwkv_a_l1": nrm((N_RWKV_LAYERS, 2, D, LORA_A), D ** -0.5),
        "rwkv_a_l2": nrm((N_RWKV_LAYERS, 2, LORA_A, D), 0.3 * LORA_A ** -0.5),
        "rwkv_g_l1": nrm((N_RWKV_LAYERS, D, LORA_G), D ** -0.5),
        "rwkv_g_l2": nrm((N_RWKV_LAYERS, LORA_G, D), LORA_G ** -0.5),
        "rwkv_k_k": gain((N_RWKV_LAYERS, D)),
        "rwkv_k_a": gain((N_RWKV_LAYERS, D)),
        "rwkv_r_k": nrm((N_RWKV_LAYERS, RWKV_H, RWKV_HS), 0.1),
        "rwkv_ln_g": gain((N_RWKV_LAYERS, D)),
        "rwkv_ln_b": nrm((N_RWKV_LAYERS, D), 0.02),
        "mla_w_dq": nrm((N_MLA_LAYERS, D, Q_RANK), D ** -0.5),
        "mla_g_q": gain((N_MLA_LAYERS, Q_RANK)),
        "mla_w_uq": nrm((N_MLA_LAYERS, Q_RANK, MLA_H * (D_NOPE + D_ROPE)), Q_RANK ** -0.5),
        "mla_w_dkv": nrm((N_MLA_LAYERS, D, KV_RANK + D_ROPE), D ** -0.5),
        "mla_g_kv": gain((N_MLA_LAYERS, KV_RANK)),
        "mla_w_uk": nrm((N_MLA_LAYERS, KV_RANK, MLA_H * D_NOPE), KV_RANK ** -0.5),
        "mla_w_uv": nrm((N_MLA_LAYERS, KV_RANK, MLA_H * D_V), KV_RANK ** -0.5),
        "mla_w_o": nrm((N_MLA_LAYERS, MLA_H * D_V, D), (MLA_H * D_V) ** -0.5),
    }


def reference(x_prompt, x_sample, state_rwkv, cache_mla_ckv, cache_mla_krope, c, c_ctx,
              mod_w, mod_b, norm_g, mlp_w1, mlp_w2,
              fft_w_out, conv_w_in, conv_w, conv_w_out,
              rwkv_mu, rwkv_w_r, rwkv_w_k, rwkv_w_v, rwkv_w_o, rwkv_w0, rwkv_w_l1, rwkv_w_l2,
              rwkv_a0, rwkv_a_l1, rwkv_a_l2, rwkv_g_l1, rwkv_g_l2, rwkv_k_k, rwkv_k_a, rwkv_r_k,
              rwkv_ln_g, rwkv_ln_b,
              mla_w_dq, mla_g_q, mla_w_uq, mla_w_dkv, mla_g_kv, mla_w_uk, mla_w_uv, mla_w_o):
    xp, xs = x_prompt, x_sample
    new_rwkv, new_ckv, new_krope = [], [], []
    for i in range(DEPTH):
        kind, j = i % N_MIXERS, i // N_MIXERS
        g4 = norm_g[i]
        m_ctx = modulation(c_ctx[None, :], mod_w[i], mod_b[i])
        m_lat = modulation(c, mod_w[i], mod_b[i])
        hp = rmsnorm(xp, g4[0]) * (1.0 + m_ctx[1]) + m_ctx[0]
        hs = rmsnorm(xs, g4[0]) * (1.0 + m_lat[1]) + m_lat[0]
        if kind == 0:
            op = fourier_mix(hp, fft_w_out[j])
            os_ = fourier_mix(hs, fft_w_out[j])
        elif kind == 1:
            op = short_conv_mix(hp, conv_w_in[j], conv_w[j], conv_w_out[j])
            os_ = short_conv_mix(hs, conv_w_in[j], conv_w[j], conv_w_out[j])
        elif kind == 2:
            rp = (rwkv_mu[j], rwkv_w_r[j], rwkv_w_k[j], rwkv_w_v[j], rwkv_w_o[j], rwkv_w0[j],
                  rwkv_w_l1[j], rwkv_w_l2[j], rwkv_a0[j], rwkv_a_l1[j], rwkv_a_l2[j],
                  rwkv_g_l1[j], rwkv_g_l2[j], rwkv_k_k[j], rwkv_k_a[j], rwkv_r_k[j],
                  rwkv_ln_g[j], rwkv_ln_b[j])
            s_zero = jnp.zeros((xp.shape[0], 2, RWKV_H, RWKV_HS, RWKV_HS), jnp.float32)
            op, s_ctx = rwkv7_mix(hp, s_zero, *rp)
            new_rwkv.append(s_ctx)
            os_, _ = rwkv7_mix(hs, state_rwkv[:, j], *rp)
        else:
            qn_p, qr_p = mla_queries(hp, mla_w_dq[j], mla_g_q[j], mla_w_uq[j], False)
            ckv_p, kr_p = mla_compress(hp, mla_w_dkv[j], mla_g_kv[j], False)
            new_ckv.append(ckv_p)
            new_krope.append(kr_p)
            op = mla_attend(qn_p, qr_p, ckv_p, kr_p, mla_w_uk[j], mla_w_uv[j], mla_w_o[j])
            qn_s, qr_s = mla_queries(hs, mla_w_dq[j], mla_g_q[j], mla_w_uq[j], True)
            ckv_s, kr_s = mla_compress(hs, mla_w_dkv[j], mla_g_kv[j], True)
            ckv_all = jnp.concatenate([cache_mla_ckv[:, j], ckv_s], axis=1)
            kr_all = jnp.concatenate([cache_mla_krope[:, j], kr_s], axis=1)
            os_ = mla_attend(qn_s, qr_s, ckv_all, kr_all, mla_w_uk[j], mla_w_uv[j], mla_w_o[j])
        xp = xp + m_ctx[2] * rmsnorm(op, g4[1])
        xs = xs + m_lat[2] * rmsnorm(os_, g4[1])
        hp = rmsnorm(xp, g4[2]) * (1.0 + m_ctx[4]) + m_ctx[3]
        hs = rmsnorm(xs, g4[2]) * (1.0 + m_lat[4]) + m_lat[3]
        fp = jnp.square(jax.nn.relu(hp @ mlp_w1[i])) @ mlp_w2[i]
        fs = jnp.square(jax.nn.relu(hs @ mlp_w1[i])) @ mlp_w2[i]
        xp = xp + m_ctx[5] * rmsnorm(fp, g4[3])
        xs = xs + m_lat[5] * rmsnorm(fs, g4[3])
    new_state_rwkv = jnp.stack(new_rwkv, axis=1)
    new_cache_mla_ckv = jnp.stack(new_ckv, axis=1)
    new_cache_mla_krope = jnp.stack(new_krope, axis=1)
    return (xp, xs, new_state_rwkv, new_cache_mla_ckv, new_cache_mla_krope)
```

```python
import functools

import numpy as np
import jax
import jax.numpy as jnp
from jax import lax
from jax.experimental import pallas as pl
from jax.experimental.pallas import tpu as pltpu

D = 1024
DEPTH = 4
N_MOD = 6
D_FF = 4 * D
EPS = 1e-6
GRID_W = 64
FFT_GROUPS = 8
FFT_GW = D // FFT_GROUPS
HS = 64
NH = D // HS
LORA_W = 64
LORA_A = 64
LORA_G = 128
GN_EPS = 64e-5
MLA_H = 8
D_NOPE = 128
D_ROPE = 64
D_V = 128
KV_RANK = 256
Q_RANK = 384
ROPE_BASE = 10000.0
MLA_SCALE = (D_NOPE + D_ROPE) ** -0.5

F32 = jnp.float32
BF16 = jnp.bfloat16

TM = 1024
LANES = 128
CHUNK = 64
VMEM_LIMIT = 56 * 1024 * 1024


def _cparams(sem):
    return pltpu.CompilerParams(dimension_semantics=sem, vmem_limit_bytes=VMEM_LIMIT)


def _dot(a, b):
    return jnp.dot(a.astype(BF16), b.astype(BF16), preferred_element_type=F32)


def _dot_nt(a, b):
    return lax.dot_general(a.astype(BF16), b.astype(BF16), (((1,), (1,)), ((), ())),
                           preferred_element_type=F32)


def _dot_tn(a, b):
    return lax.dot_general(a.astype(BF16), b.astype(BF16), (((0,), (0,)), ((), ())),
                           preferred_element_type=F32)


def _split3(x):
    hi = x.astype(BF16)
    r1 = x - hi.astype(F32)
    mid = r1.astype(BF16)
    lo = (r1 - mid.astype(F32)).astype(BF16)
    return hi, mid, lo


def _dot_exact_lhs(e, x):
    hi, mid, lo = _split3(x)
    return (jnp.dot(e, hi, preferred_element_type=F32)
            + jnp.dot(e, mid, preferred_element_type=F32)
            + jnp.dot(e, lo, preferred_element_type=F32))


def _dot_exact_rhs(x, e):
    hi, mid, lo = _split3(x)
    return (jnp.dot(hi, e, preferred_element_type=F32)
            + jnp.dot(mid, e, preferred_element_type=F32)
            + jnp.dot(lo, e, preferred_element_type=F32))


def _rms(x, g):
    return x * lax.rsqrt(jnp.mean(x * x, axis=-1, keepdims=True) + EPS) * g


def _sigmoid(x):
    return 1.0 / (1.0 + jnp.exp(-x))


def _softplus(x):
    return jnp.maximum(x, 0.0) + jnp.log(1.0 + jnp.exp(-jnp.abs(x)))


def _head_indicator(n):
    r = lax.broadcasted_iota(jnp.int32, (n, n), 0) // HS
    c = lax.broadcasted_iota(jnp.int32, (n, n), 1) // HS
    return jnp.where(r == c, 1.0, 0.0).astype(BF16)


def _shift_rows(z, seq_len):
    n = z.shape[0]
    pos = lax.broadcasted_iota(jnp.int32, z.shape, 0) % seq_len
    prev = jnp.where(pos == 0, 0.0, pltpu.roll(z, 1, 0))
    nxt = jnp.where(pos == seq_len - 1, 0.0, pltpu.roll(z, n - 1, 0))
    return prev, nxt


MOD_TN = 1536


def _mod_kernel(cs_ref, w_ref, b_ref, o_ref):
    cs = cs_ref[...]
    s = cs * _sigmoid(cs)
    o_ref[0] = _dot(s, w_ref[0]) + b_ref[0]


def _modulation(cs, mod_w, mod_b):
    nj = (N_MOD * D) // MOD_TN
    out = pl.pallas_call(
        _mod_kernel,
        grid=(DEPTH, nj),
        in_specs=[
            pl.BlockSpec((8, D), lambda l, j: (0, 0)),
            pl.BlockSpec((1, D, MOD_TN), lambda l, j: (l, 0, j)),
            pl.BlockSpec((1, 1, MOD_TN), lambda l, j: (l, 0, j)),
        ],
        out_specs=pl.BlockSpec((1, 8, MOD_TN), lambda l, j: (l, 0, j)),
        out_shape=jax.ShapeDtypeStruct((DEPTH, 8, N_MOD * D), F32),
        compiler_params=_cparams(("arbitrary", "arbitrary")),
        name="modulation",
    )(cs, mod_w, mod_b.reshape(DEPTH, 1, N_MOD * D))
    return out.reshape(DEPTH, 8, N_MOD, D)


def _mod_spec(layer, group_of_tile, ngrid):
    if ngrid == 1:
        return pl.BlockSpec((1, 1, N_MOD, D), lambda i: (layer, group_of_tile(i), 0, 0))
    return pl.BlockSpec((1, 1, N_MOD, D), lambda i, j: (layer, group_of_tile(i), 0, 0))


def _normg_spec(layer, ngrid):
    if ngrid == 1:
        return pl.BlockSpec((1, 4, D), lambda i: (layer, 0, 0))
    return pl.BlockSpec((1, 4, D), lambda i, j: (layer, 0, 0))


MLP_FC = 512


def _mlp_kernel(x_ref, m_ref, g_ref, w1_ref, w2_ref, o_ref, h_scr, acc_scr):
    j = pl.program_id(1)
    m = m_ref[0, 0]
    g = g_ref[0]

    @pl.when(j == 0)
    def _():
        h = _rms(x_ref[...], g[2:3]) * (1.0 + m[4:5]) + m[3:4]
        h_scr[...] = h.astype(BF16)
        acc_scr[...] = jnp.zeros_like(acc_scr)

    a = jnp.dot(h_scr[...], w1_ref[0].astype(BF16), preferred_element_type=F32)
    a = jnp.maximum(a, 0.0)
    a = a * a
    acc_scr[...] += _dot(a, w2_ref[0])

    @pl.when(j == pl.num_programs(1) - 1)
    def _():
        o_ref[...] = x_ref[...] + m[5:6] * _rms(acc_scr[...], g[3:4])


def _mlp(x, mod, norm_g, w1, w2, layer, group_of_tile):
    n = x.shape[0]
    nj = D_FF // MLP_FC
    return pl.pallas_call(
        _mlp_kernel,
        grid=(n // TM, nj),
        in_specs=[
            pl.BlockSpec((TM, D), lambda i, j: (i, 0)),
            _mod_spec(layer, group_of_tile, 2),
            _normg_spec(layer, 2),
            pl.BlockSpec((1, D, MLP_FC), lambda i, j: (layer, 0, j)),
            pl.BlockSpec((1, MLP_FC, D), lambda i, j: (layer, j, 0)),
        ],
        out_specs=pl.BlockSpec((TM, D), lambda i, j: (i, 0)),
        out_shape=jax.ShapeDtypeStruct((n, D), F32),
        scratch_shapes=[pltpu.VMEM((TM, D), BF16), pltpu.VMEM((TM, D), F32)],
        compiler_params=_cparams(("arbitrary", "arbitrary")),
        name="mlp",
    )(x, mod, norm_g, w1, w2)


def _dft_mats(n):
    idx = np.arange(n, dtype=np.int64)
    ang = (2.0 * np.pi / n) * ((idx[:, None] * idx[None, :]) % n).astype(np.float64)
    scale = 1.0 / np.sqrt(n)
    return np.cos(ang) * scale, np.sin(ang) * scale


def _fnet_kernel(x_ref, m_ref, g_ref, cs_ref, cn_ref, sn_ref, w_ref, o_ref, p_scr, q_scr, f_scr,
                 *, seq_len):
    m = m_ref[0, 0]
    g = g_ref[0]
    x = x_ref[...]
    h = (_rms(x, g[0:1]) * (1.0 + m[1:2]) + m[0:1]).astype(BF16)
    cs = cs_ref[...].astype(BF16)
    for gi in range(FFT_GROUPS):
        pq = jnp.dot(h[:, gi * FFT_GW:(gi + 1) * FFT_GW], cs, preferred_element_type=F32)
        p_scr[:, gi * FFT_GW:(gi + 1) * FFT_GW] = pq[:, :FFT_GW].astype(BF16)
        q_scr[:, gi * FFT_GW:(gi + 1) * FFT_GW] = pq[:, FFT_GW:].astype(BF16)
    cn = cn_ref[...].astype(BF16)
    sn = sn_ref[...].astype(BF16)
    for s in range(TM // seq_len):
        rows = slice(s * seq_len, (s + 1) * seq_len)
        f = (jnp.dot(cn, p_scr[rows, :], preferred_element_type=F32)
             - jnp.dot(sn, q_scr[rows, :], preferred_element_type=F32))
        f_scr[rows, :] = f.astype(BF16)
    o = jnp.dot(f_scr[...], w_ref[0].astype(BF16), preferred_element_type=F32)
    o_ref[...] = x + m[2:3] * _rms(o, g[1:2])


def _fnet(x, mod, norm_g, w_out, layer, j, seq_len, group_of_tile):
    n = x.shape[0]
    cg, sg = _dft_mats(FFT_GW)
    cs = jnp.asarray(np.concatenate([cg, sg], axis=1), F32)
    cn_np, sn_np = _dft_mats(seq_len)
    cn = jnp.asarray(cn_np, F32)
    sn = jnp.asarray(sn_np, F32)
    return pl.pallas_call(
        functools.partial(_fnet_kernel, seq_len=seq_len),
        grid=(n // TM,),
        in_specs=[
            pl.BlockSpec((TM, D), lambda i: (i, 0)),
            _mod_spec(layer, group_of_tile, 1),
            _normg_spec(layer, 1),
            pl.BlockSpec((FFT_GW, 2 * FFT_GW), lambda i: (0, 0)),
            pl.BlockSpec((seq_len, seq_len), lambda i: (0, 0)),
            pl.BlockSpec((seq_len, seq_len), lambda i: (0, 0)),
            pl.BlockSpec((1, D, D), lambda i: (j, 0, 0)),
        ],
        out_specs=pl.BlockSpec((TM, D), lambda i: (i, 0)),
        out_shape=jax.ShapeDtypeStruct((n, D), F32),
        scratch_shapes=[pltpu.VMEM((TM, D), BF16), pltpu.VMEM((TM, D), BF16),
                        pltpu.VMEM((TM, D), BF16)],
        compiler_params=_cparams(("arbitrary",)),
        name="fourier_mix",
    )(x, mod, norm_g, cs, cn, sn, w_out)


CONV_CW = 256


def _conv_kernel(x_ref, m_ref, g_ref, wb_ref, wc_ref, wu_ref, cw_ref, wo_ref, o_ref, h_scr, acc_scr,
                 *, seq_len):
    j = pl.program_id(1)
    m = m_ref[0, 0]
    g = g_ref[0]

    @pl.when(j == 0)
    def _():
        h = _rms(x_ref[...], g[0:1]) * (1.0 + m[1:2]) + m[0:1]
        h_scr[...] = h.astype(BF16)
        acc_scr[...] = jnp.zeros_like(acc_scr)

    h = h_scr[...]
    bg = jnp.dot(h, wb_ref[0].astype(BF16), preferred_element_type=F32)
    cg = jnp.dot(h, wc_ref[0].astype(BF16), preferred_element_type=F32)
    u = jnp.dot(h, wu_ref[0].astype(BF16), preferred_element_type=F32)
    z = cg * u
    z_prev, z_next = _shift_rows(z, seq_len)
    cw = cw_ref[0]
    conv = z_prev * cw[0:1] + z * cw[1:2] + z_next * cw[2:3]
    acc_scr[...] += _dot(bg * conv, wo_ref[0])

    @pl.when(j == pl.num_programs(1) - 1)
    def _():
        o_ref[...] = x_ref[...] + m[2:3] * _rms(acc_scr[...], g[1:2])


def _conv(x, mod, norm_g, w_in, w_conv, w_out, layer, j, seq_len, group_of_tile):
    n = x.shape[0]
    nj = D // CONV_CW
    return pl.pallas_call(
        functools.partial(_conv_kernel, seq_len=seq_len),
        grid=(n // TM, nj),
        in_specs=[
            pl.BlockSpec((TM, D), lambda i, c: (i, 0)),
            _mod_spec(layer, group_of_tile, 2),
            _normg_spec(layer, 2),
            pl.BlockSpec((1, D, CONV_CW), lambda i, c: (j, 0, c)),
            pl.BlockSpec((1, D, CONV_CW), lambda i, c: (j, 0, nj + c)),
            pl.BlockSpec((1, D, CONV_CW), lambda i, c: (j, 0, 2 * nj + c)),
            pl.BlockSpec((1, 3, CONV_CW), lambda i, c: (j, 0, c)),
            pl.BlockSpec((1, CONV_CW, D), lambda i, c: (j, c, 0)),
        ],
        out_specs=pl.BlockSpec((TM, D), lambda i, c: (i, 0)),
        out_shape=jax.ShapeDtypeStruct((n, D), F32),
        scratch_shapes=[pltpu.VMEM((TM, D), BF16), pltpu.VMEM((TM, D), F32)],
        compiler_params=_cparams(("arbitrary", "arbitrary")),
        name="short_conv",
    )(x, mod, norm_g, w_in, w_in, w_in, w_conv, w_out)


RWP_CW = 128
RW_CW = 256
N_RW_OUT = 11


def _rwkv_proj_kernel(x_ref, m_ref, g_ref, mu_ref, wr_ref, wk_ref, wv_ref, w0_ref, wl1_ref, wl2_ref,
                      a0_ref, al1_ref, al2_ref, gl1_ref, gl2_ref, kk_ref, ka_ref, rk_ref,
                      r_o, v_o, kk_o, ld0_o, b0_o, kt0_o, ld1_o, b1_o, kt1_o, g_o, bonus_o,
                      xr_scr, xk_scr, xv_scr, tw_scr, ta_scr, sg_scr, *, seq_len):
    j = pl.program_id(1)

    @pl.when(j == 0)
    def _():
        m = m_ref[0, 0]
        g = g_ref[0]
        mu = mu_ref[0]
        h = _rms(x_ref[...], g[0:1]) * (1.0 + m[1:2]) + m[0:1]
        h_prev, h_next = _shift_rows(h, seq_len)
        dx = 0.5 * (h_prev + h_next) - h
        xr_scr[...] = (h + dx * mu[0:1]).astype(BF16)
        xk_scr[...] = (h + dx * mu[2:3]).astype(BF16)
        xv_scr[...] = (h + dx * mu[3:4]).astype(BF16)
        xw = (h + dx * mu[1:2]).astype(BF16)
        xa = (h + dx * mu[4:5]).astype(BF16)
        xg = (h + dx * mu[5:6]).astype(BF16)
        for d in range(2):
            tw_scr[d] = jnp.tanh(_dot(xw, wl1_ref[0, d])).astype(BF16)
            ta_scr[d] = _dot(xa, al1_ref[0, d]).astype(BF16)
        sg_scr[...] = _sigmoid(_dot(xg, gl1_ref[0])).astype(BF16)

    e = _head_indicator(RWP_CW)
    r = jnp.dot(xr_scr[...], wr_ref[0].astype(BF16), preferred_element_type=F32)
    k = jnp.dot(xk_scr[...], wk_ref[0].astype(BF16), preferred_element_type=F32)
    v = jnp.dot(xv_scr[...], wv_ref[0].astype(BF16), preferred_element_type=F32)
    kk = k * kk_ref[...]
    kk = kk * lax.rsqrt(_dot_exact_rhs(kk * kk, e) + 1e-12)
    r_o[...] = r
    v_o[...] = v
    kk_o[...] = kk
    g_o[...] = jnp.dot(sg_scr[...], gl2_ref[0].astype(BF16), preferred_element_type=F32)
    k_a = ka_ref[...]
    rrk = r * rk_ref[...]
    bsum = None
    for d, (ld_o, b_o, kt_o) in enumerate(((ld0_o, b0_o, kt0_o), (ld1_o, b1_o, kt1_o))):
        zw = w0_ref[0, d:d + 1, :] + jnp.dot(tw_scr[d], wl2_ref[0, d].astype(BF16),
                                             preferred_element_type=F32)
        logw = -_softplus(-zw) - 0.5
        ld_o[...] = -jnp.exp(logw)
        a = _sigmoid(a0_ref[0, d:d + 1, :] + jnp.dot(ta_scr[d], al2_ref[0, d].astype(BF16),
                                                      preferred_element_type=F32))
        kt = k * (1.0 + (a - 1.0) * k_a)
        b_o[...] = kk * a
        kt_o[...] = kt
        bs = _dot_exact_rhs(rrk * kt, e)
        bsum = bs if bsum is None else bsum + bs
    bonus_o[...] = bsum * v


def _rwkv_proj(x, mod, norm_g, p, layer, j, seq_len, group_of_tile):
    n = x.shape[0]
    nj = D // RWP_CW
    tile = pl.BlockSpec((TM, RWP_CW), lambda i, c: (i, c))
    wcol = lambda: pl.BlockSpec((1, D, RWP_CW), lambda i, c: (j, 0, c))
    vec = lambda: pl.BlockSpec((1, RWP_CW), lambda i, c: (j, c))
    return pl.pallas_call(
        functools.partial(_rwkv_proj_kernel, seq_len=seq_len),
        grid=(n // TM, nj),
        in_specs=[
            pl.BlockSpec((TM, D), lambda i, c: (i, 0)),
            _mod_spec(layer, group_of_tile, 2),
            _normg_spec(layer, 2),
            pl.BlockSpec((1, 6, D), lambda i, c: (j, 0, 0)),
            wcol(), wcol(), wcol(),
            pl.BlockSpec((1, 2, RWP_CW), lambda i, c: (j, 0, c)),
            pl.BlockSpec((1, 2, D, LORA_W), lambda i, c: (j, 0, 0, 0)),
            pl.BlockSpec((1, 2, LORA_W, RWP_CW), lambda i, c: (j, 0, 0, c)),
            pl.BlockSpec((1, 2, RWP_CW), lambda i, c: (j, 0, c)),
            pl.BlockSpec((1, 2, D, LORA_A), lambda i, c: (j, 0, 0, 0)),
            pl.BlockSpec((1, 2, LORA_A, RWP_CW), lambda i, c: (j, 0, 0, c)),
            pl.BlockSpec((1, D, LORA_G), lambda i, c: (j, 0, 0)),
            pl.BlockSpec((1, LORA_G, RWP_CW), lambda i, c: (j, 0, c)),
            vec(), vec(), vec(),
        ],
        out_specs=[tile] * N_RW_OUT,
        out_shape=[jax.ShapeDtypeStruct((n, D), F32)] * N_RW_OUT,
        scratch_shapes=[pltpu.VMEM((TM, D), BF16), pltpu.VMEM((TM, D), BF16), pltpu.VMEM((TM, D), BF16),
                        pltpu.VMEM((2, TM, LORA_W), BF16), pltpu.VMEM((2, TM, LORA_A), BF16),
                        pltpu.VMEM((TM, LORA_G), BF16)],
        compiler_params=_cparams(("arbitrary", "arbitrary")),
        name="rwkv_proj",
    )(x, mod, norm_g, p["mu"], p["w_r"], p["w_k"], p["w_v"], p["w0"], p["w_l1"], p["w_l2"],
      p["a0"], p["a_l1"], p["a_l2"], p["g_l1"], p["g_l2"], p["k_k"], p["k_a"], p["r_k"])


def _scan_chunk(gt, ld, kk, beta, kt, r, v, reverse):
    c = ld.shape[0]
    row = lax.broadcasted_iota(jnp.int32, (2 * c, 2 * c), 0)
    col = lax.broadcasted_iota(jnp.int32, (2 * c, 2 * c), 1)
    rc = lax.broadcasted_iota(jnp.int32, (c, c), 0)
    cc = lax.broadcasted_iota(jnp.int32, (c, c), 1)
    if reverse:
        incl_c = cc >= rc
        strict = (col % c) > (row % c)
        incl = (col % c) >= (row % c)
    else:
        incl_c = cc <= rc
        strict = (col % c) < (row % c)
        incl = (col % c) <= (row % c)
    head_a = lax.broadcasted_iota(jnp.int32, (c, LANES), 1) < HS

    def two_heads(t):
        return jnp.concatenate([jnp.where(head_a, t, 0.0), jnp.where(head_a, 0.0, t)], axis=0)

    def pick(t2):
        return jnp.where(head_a, t2[:c], t2[c:])

    cum = _dot_exact_lhs(jnp.where(incl_c, 1.0, 0.0).astype(BF16), ld)
    tot = cum[0:1] if reverse else cum[c - 1:c]
    gam = jnp.exp(cum)
    ginv = jnp.exp(-cum)
    a_t = two_heads(-kk * jnp.exp(cum - ld))
    r_t = two_heads(r * gam)
    b_h = two_heads(beta * ginv)
    k_h = two_heads(kt * ginv)
    l_ab = jnp.where(strict, _dot_nt(a_t, b_h), 0.0)
    l_ak = jnp.where(strict, _dot_nt(a_t, k_h), 0.0)
    t_rb = jnp.where(incl, _dot_nt(r_t, b_h), 0.0)
    t_rk = jnp.where(incl, _dot_nt(r_t, k_h), 0.0)
    minv = jnp.where(row == col, 1.0, 0.0)
    s = 1
    while s < c:
        same = (row // (2 * s)) == (col // (2 * s))
        first, second = (col % (2 * s)) < s, (row % (2 * s)) >= s
        if reverse:
            first, second = (row % (2 * s)) < s, (col % (2 * s)) >= s
        e_s = jnp.where(same & first & second, l_ab, 0.0)
        minv = minv + (e_s if s == 1 else _dot(_dot(minv, e_s), minv))
        s *= 2
    v2 = jnp.concatenate([v, v], axis=0)
    ah = _dot_nt(a_t, gt)
    rh = _dot_nt(r_t, gt)
    w2 = ah + _dot(l_ak, v2)
    u2 = _dot(minv, w2)
    u = pick(u2)
    y = pick(rh + _dot(t_rb, jnp.concatenate([u, u], axis=0)) + _dot(t_rk, v2))
    decay_tail = jnp.exp(tot - cum)
    upd = _dot_tn(two_heads(u), two_heads(beta * decay_tail)) + _dot_tn(two_heads(v), two_heads(kt * decay_tail))
    return gt * jnp.exp(tot) + upd, y


def _rwkv_scan_kernel(*refs, seq_len, has_init, want_final):
    r_ref, v_ref, kk_ref, ld0_ref, b0_ref, kt0_ref, ld1_ref, b1_ref, kt1_ref = refs[:9]
    pos = 9
    if has_init:
        s0_ref = refs[pos]
        pos += 1
    y_ref = refs[pos]
    pos += 1
    if want_final:
        sf_ref = refs[pos]
        pos += 1
    g_scr = refs[pos]
    nc = seq_len // CHUNK
    per_dir = ((ld0_ref, b0_ref, kt0_ref), (ld1_ref, b1_ref, kt1_ref))

    for d in range(2):
        g_scr[d] = s0_ref[0, d, 0] if has_init else jnp.zeros((LANES, LANES), F32)
    y_ref[...] = jnp.zeros_like(y_ref)

    def body(ci, carry):
        for d in range(2):
            ld_ref, b_ref, kt_ref = per_dir[d]
            cidx = ci if d == 0 else nc - 1 - ci
            rows = pl.ds(pl.multiple_of(cidx * CHUNK, CHUNK), CHUNK)
            g_new, y = _scan_chunk(g_scr[d], ld_ref[rows, :], kk_ref[rows, :], b_ref[rows, :],
                                   kt_ref[rows, :], r_ref[rows, :], v_ref[rows, :], d == 1)
            g_scr[d] = g_new
            y_ref[rows, :] += y
        return carry

    lax.fori_loop(0, nc, body, 0)
    if want_final:
        for d in range(2):
            sf_ref[0, d, 0] = g_scr[d]


def _rwkv_scan(proj, s_init, n_seq, seq_len, want_final):
    r, v, kk, ld0, b0, kt0, ld1, b1, kt1 = proj[:9]
    n = r.shape[0]
    npair = D // LANES
    blk = pl.BlockSpec((seq_len, LANES), lambda b, p: (b, p))
    st_spec = pl.BlockSpec((1, 2, 1, LANES, LANES), lambda b, p: (b, 0, p, 0, 0))
    in_specs = [blk] * 9
    args = [r, v, kk, ld0, b0, kt0, ld1, b1, kt1]
    has_init = s_init is not None
    if has_init:
        in_specs.append(st_spec)
        args.append(s_init)
    out_specs = [blk]
    out_shape = [jax.ShapeDtypeStruct((n, D), F32)]
    if want_final:
        out_specs.append(st_spec)
        out_shape.append(jax.ShapeDtypeStruct((n_seq, 2, npair, LANES, LANES), F32))
    res = pl.pallas_call(
        functools.partial(_rwkv_scan_kernel, seq_len=seq_len, has_init=has_init, want_final=want_final),
        grid=(n_seq, npair),
        in_specs=in_specs,
        out_specs=out_specs,
        out_shape=out_shape,
        scratch_shapes=[pltpu.VMEM((2, LANES, LANES), F32)],
        compiler_params=_cparams(("arbitrary", "arbitrary")),
        name="rwkv_scan",
    )(*args)
    return res


def _rwkv_out_kernel(x_ref, m_ref, g_ref, y_ref, bonus_ref, gate_ref, lng_ref, lnb_ref, wo_ref, o_ref,
                     acc_scr):
    j = pl.program_id(1)

    @pl.when(j == 0)
    def _():
        acc_scr[...] = jnp.zeros_like(acc_scr)

    e = _head_indicator(RW_CW)
    y = y_ref[...]
    mean = _dot_exact_rhs(y, e) * (1.0 / HS)
    yc = y - mean
    var = _dot_exact_rhs(yc * yc, e) * (1.0 / HS)
    yn = yc * lax.rsqrt(var + GN_EPS) * lng_ref[...] + lnb_ref[...]
    yn = (yn + bonus_ref[...]) * gate_ref[...]
    acc_scr[...] += _dot(yn, wo_ref[0])

    @pl.when(j == pl.num_programs(1) - 1)
    def _():
        m = m_ref[0, 0]
        g = g_ref[0]
        o_ref[...] = x_ref[...] + m[2:3] * _rms(acc_scr[...], g[1:2])


def _rwkv_out(x, mod, norm_g, y, bonus, gate, p, layer, j, group_of_tile):
    n = x.shape[0]
    nj = D // RW_CW
    tile = pl.BlockSpec((TM, RW_CW), lambda i, c: (i, c))
    vec = pl.BlockSpec((1, RW_CW), lambda i, c: (j, c))
    return pl.pallas_call(
        _rwkv_out_kernel,
        grid=(n // TM, nj),
        in_specs=[
            pl.BlockSpec((TM, D), lambda i, c: (i, 0)),
            _mod_spec(layer, group_of_tile, 2),
            _normg_spec(layer, 2),
            tile, tile, tile, vec, vec,
            pl.BlockSpec((1, RW_CW, D), lambda i, c: (j, c, 0)),
        ],
        out_specs=pl.BlockSpec((TM, D), lambda i, c: (i, 0)),
        out_shape=jax.ShapeDtypeStruct((n, D), F32),
        scratch_shapes=[pltpu.VMEM((TM, D), F32)],
        compiler_params=_cparams(("arbitrary", "arbitrary")),
        name="rwkv_out",
    )(x, mod, norm_g, y, bonus, gate, p["ln_g"], p["ln_b"], p["w_o"])


def _blockdiag_states(s):
    b = s.shape[0]
    sp = s.reshape(b, 2, NH // 2, 2, HS, HS)
    z = jnp.zeros((b, 2, NH // 2, HS, HS), s.dtype)
    top = jnp.concatenate([sp[:, :, :, 0], z], axis=-1)
    bot = jnp.concatenate([z, sp[:, :, :, 1]], axis=-1)
    return jnp.concatenate([top, bot], axis=-2)


def _unblock_states(g):
    a = g[:, :, :, :HS, :HS]
    b = g[:, :, :, HS:, HS:]
    return jnp.stack([a, b], axis=3).reshape(g.shape[0], 2, NH, HS, HS)


def _rope_tables(n):
    rows = n // GRID_W
    row = np.repeat(np.arange(rows), GRID_W)
    col = np.tile(np.arange(GRID_W), rows)
    pos = np.stack([row, col], axis=-1).astype(np.float64)
    quarter = D_ROPE // 4
    inv = ROPE_BASE ** (-np.arange(quarter, dtype=np.float64) / quarter)
    ang = pos[:, :, None] * inv
    cos = np.cos(ang)
    sin = np.sin(ang)
    cos_t = np.concatenate([cos, cos], axis=-1).reshape(n, D_ROPE)
    sin_t = np.concatenate([-sin, sin], axis=-1).reshape(n, D_ROPE)
    return cos_t.astype(np.float32), sin_t.astype(np.float32)


def _rope_swap_perm():
    quarter = D_ROPE // 4
    base = np.arange(D_ROPE)
    return np.where((base % (2 * quarter)) < quarter, base + quarter, base - quarter)


def _mla_proj_kernel(*refs, positional):
    (x_ref, m_ref, g_ref, wdq_ref, gq_ref, wqn_ref, wqr_ref, wqs_ref, wdkv_ref, wkr_ref, wks_ref,
     gkv_ref) = refs[:12]
    pos = 12
    if positional:
        cosq_ref, sinq_ref, cosk_ref, sink_ref = refs[pos:pos + 4]
        pos += 4
    qn_o, qr_o, ckv_o, kr_o = refs[pos:pos + 4]
    m = m_ref[0, 0]
    g = g_ref[0]
    h = (_rms(x_ref[...], g[0:1]) * (1.0 + m[1:2]) + m[0:1]).astype(BF16)
    ql = jnp.dot(h, wdq_ref[0].astype(BF16), preferred_element_type=F32)
    ql = (ql * lax.rsqrt(jnp.mean(ql * ql, axis=-1, keepdims=True) + EPS) * gq_ref[...]).astype(BF16)
    qn_o[...] = jnp.dot(ql, wqn_ref[...].astype(BF16), preferred_element_type=F32)
    qr = jnp.dot(ql, wqr_ref[...].astype(BF16), preferred_element_type=F32)
    ckv = jnp.dot(h, wdkv_ref[...].astype(BF16), preferred_element_type=F32)
    ckv_o[...] = ckv * lax.rsqrt(jnp.mean(ckv * ckv, axis=-1, keepdims=True) + EPS) * gkv_ref[...]
    kr = jnp.dot(h, wkr_ref[...].astype(BF16), preferred_element_type=F32)
    if positional:
        qs = jnp.dot(ql, wqs_ref[...].astype(BF16), preferred_element_type=F32)
        ks = jnp.dot(h, wks_ref[...].astype(BF16), preferred_element_type=F32)
        qr = qr * cosq_ref[...] + qs * sinq_ref[...]
        kr = kr * cosk_ref[...] + ks * sink_ref[...]
    qr_o[...] = qr
    kr_o[...] = kr


def _mla_proj(x, mod, norm_g, p, layer, j, positional, group_of_tile):
    n = x.shape[0]
    full = lambda shape: pl.BlockSpec(shape, lambda i: (0,) * len(shape))
    in_specs = [
        pl.BlockSpec((TM, D), lambda i: (i, 0)),
        _mod_spec(layer, group_of_tile, 1),
        _normg_spec(layer, 1),
        pl.BlockSpec((1, D, Q_RANK), lambda i: (j, 0, 0)),
        pl.BlockSpec((1, Q_RANK), lambda i: (j, 0)),
        full((Q_RANK, MLA_H * D_NOPE)), full((Q_RANK, MLA_H * D_ROPE)), full((Q_RANK, MLA_H * D_ROPE)),
        full((D, KV_RANK)), full((D, D_ROPE)), full((D, D_ROPE)),
        pl.BlockSpec((1, KV_RANK), lambda i: (j, 0)),
    ]
    args = [x, mod, norm_g, p["w_dq"], p["g_q"], p["w_uq_nope"], p["w_uq_rope"], p["w_uq_rope_sw"],
            p["w_dkv_c"], p["w_dkv_r"], p["w_dkv_r_sw"], p["g_kv"]]
    if positional:
        cos_t, sin_t = _rope_tables(TM)
        in_specs += [full((TM, MLA_H * D_ROPE)), full((TM, MLA_H * D_ROPE)),
                     full((TM, D_ROPE)), full((TM, D_ROPE))]
        args += [jnp.asarray(np.tile(cos_t, (1, MLA_H))), jnp.asarray(np.tile(sin_t, (1, MLA_H))),
                 jnp.asarray(cos_t), jnp.asarray(sin_t)]
    widths = (MLA_H * D_NOPE, MLA_H * D_ROPE, KV_RANK, D_ROPE)
    return pl.pallas_call(
        functools.partial(_mla_proj_kernel, positional=positional),
        grid=(n // TM,),
        in_specs=in_specs,
        out_specs=[pl.BlockSpec((TM, w), lambda i: (i, 0)) for w in widths],
        out_shape=[jax.ShapeDtypeStruct((n, w), F32) for w in widths],
        compiler_params=_cparams(("arbitrary",)),
        name="mla_proj",
    )(*args)


def _mla_attn_kernel(x_ref, m_ref, g_ref, qn_ref, qr_ref, ckv_ref, kr_ref, wuk_ref, wuv_ref, wo_ref,
                     o_ref, kn_scr, vv_scr, oh_scr):
    qi = pl.program_id(1)

    @pl.when(qi == 0)
    def _():
        ckv = ckv_ref[...].astype(BF16)
        kn_scr[...] = jnp.dot(ckv, wuk_ref[0].astype(BF16), preferred_element_type=F32).astype(BF16)
        vv_scr[...] = jnp.dot(ckv, wuv_ref[0].astype(BF16), preferred_element_type=F32).astype(BF16)

    kr = kr_ref[...].astype(BF16)
    for hd in range(MLA_H):
        qn = qn_ref[:, hd * D_NOPE:(hd + 1) * D_NOPE]
        qr = qr_ref[:, hd * D_ROPE:(hd + 1) * D_ROPE]
        s = (_dot_nt(qn, kn_scr[:, hd * D_NOPE:(hd + 1) * D_NOPE]) + _dot_nt(qr, kr)) * MLA_SCALE
        s = s - jnp.max(s, axis=-1, keepdims=True)
        pexp = jnp.exp(s)
        pr = pexp / jnp.sum(pexp, axis=-1, keepdims=True)
        oh_scr[:, hd * D_V:(hd + 1) * D_V] = jnp.dot(
            pr.astype(BF16), vv_scr[:, hd * D_V:(hd + 1) * D_V], preferred_element_type=F32).astype(BF16)
    o = jnp.dot(oh_scr[...], wo_ref[0].astype(BF16), preferred_element_type=F32)
    m = m_ref[0, 0]
    g = g_ref[0]
    o_ref[...] = x_ref[...] + m[2:3] * _rms(o, g[1:2])


def _mla_attn(x, mod, norm_g, qn, qr, ckv_all, kr_all, p, layer, j, n_seq, q_len, k_len, tq, group_of_seq):
    n = x.shape[0]
    nq = q_len // tq
    return pl.pallas_call(
        _mla_attn_kernel,
        grid=(n_seq, nq),
        in_specs=[
            pl.BlockSpec((tq, D), lambda b, q: (b * nq + q, 0)),
            pl.BlockSpec((1, 1, N_MOD, D), lambda b, q: (layer, group_of_seq(b), 0, 0)),
            pl.BlockSpec((1, 4, D), lambda b, q: (layer, 0, 0)),
            pl.BlockSpec((tq, MLA_H * D_NOPE), lambda b, q: (b * nq + q, 0)),
            pl.BlockSpec((tq, MLA_H * D_ROPE), lambda b, q: (b * nq + q, 0)),
            pl.BlockSpec((k_len, KV_RANK), lambda b, q: (b, 0)),
            pl.BlockSpec((k_len, D_ROPE), lambda b, q: (b, 0)),
            pl.BlockSpec((1, KV_RANK, MLA_H * D_NOPE), lambda b, q: (j, 0, 0)),
            pl.BlockSpec((1, KV_RANK, MLA_H * D_V), lambda b, q: (j, 0, 0)),
            pl.BlockSpec((1, MLA_H * D_V, D), lambda b, q: (j, 0, 0)),
        ],
        out_specs=pl.BlockSpec((tq, D), lambda b, q: (b * nq + q, 0)),
        out_shape=jax.ShapeDtypeStruct((n, D), F32),
        scratch_shapes=[pltpu.VMEM((k_len, MLA_H * D_NOPE), BF16), pltpu.VMEM((k_len, MLA_H * D_V), BF16),
                        pltpu.VMEM((tq, MLA_H * D_V), BF16)],
        compiler_params=_cparams(("arbitrary", "arbitrary")),
        name="mla_attn",
    )(x, mod, norm_g, qn, qr, ckv_all, kr_all, p["w_uk"], p["w_uv"], p["w_o"])


def kernel(x_prompt, x_sample, state_rwkv, cache_mla_ckv, cache_mla_krope, c, c_ctx, mod_w, mod_b, norm_g,
           mlp_w1, mlp_w2, fft_w_out, conv_w_in, conv_w, conv_w_out, rwkv_mu, rwkv_w_r, rwkv_w_k, rwkv_w_v,
           rwkv_w_o, rwkv_w0, rwkv_w_l1, rwkv_w_l2, rwkv_a0, rwkv_a_l1, rwkv_a_l2, rwkv_g_l1, rwkv_g_l2,
           rwkv_k_k, rwkv_k_a, rwkv_r_k, rwkv_ln_g, rwkv_ln_b, mla_w_dq, mla_g_q, mla_w_uq, mla_w_dkv,
           mla_g_kv, mla_w_uk, mla_w_uv, mla_w_o):
    batch, seq, _ = x_prompt.shape
    dec_batch, dec_seq, _ = x_sample.shape
    past_len = cache_mla_ckv.shape[2]
    assert (batch * seq) % TM == 0 and TM % seq == 0 and dec_seq == TM and seq % CHUNK == 0

    xp = x_prompt.reshape(batch * seq, D)
    xs = x_sample.reshape(dec_batch * dec_seq, D)
    cs = jnp.concatenate([c_ctx[None, :], c, jnp.zeros((8 - 1 - dec_batch, D), F32)], axis=0)
    mod = _modulation(cs, mod_w, mod_b)

    grp_p = lambda i: 0
    grp_s = lambda i: 1 + i
    new_rwkv, new_ckv, new_krope = [], [], []
    streams = ((True, seq, batch, grp_p), (False, dec_seq, dec_batch, grp_s))

    for i in range(DEPTH):
        kind, j = i % 4, i // 4
        outs = []
        for is_prompt, slen, nseq, grp in streams:
            x = xp if is_prompt else xs
            if kind == 0:
                x = _fnet(x, mod, norm_g, fft_w_out, i, j, slen, grp)
            elif kind == 1:
                x = _conv(x, mod, norm_g, conv_w_in, conv_w, conv_w_out, i, j, slen, grp)
            elif kind == 2:
                p = dict(mu=rwkv_mu, w_r=rwkv_w_r, w_k=rwkv_w_k, w_v=rwkv_w_v, w_o=rwkv_w_o, w0=rwkv_w0,
                         w_l1=rwkv_w_l1, w_l2=rwkv_w_l2, a0=rwkv_a0, a_l1=rwkv_a_l1, a_l2=rwkv_a_l2,
                         g_l1=rwkv_g_l1, g_l2=rwkv_g_l2, k_k=rwkv_k_k, k_a=rwkv_k_a,
                         r_k=rwkv_r_k.reshape(-1, D), ln_g=rwkv_ln_g, ln_b=rwkv_ln_b)
                proj = _rwkv_proj(x, mod, norm_g, p, i, j, slen, grp)
                if is_prompt:
                    y, s_fin = _rwkv_scan(proj, None, nseq, slen, True)
                    new_rwkv.append(_unblock_states(s_fin))
                else:
                    (y,) = _rwkv_scan(proj, _blockdiag_states(state_rwkv[:, j]), nseq, slen, False)
                x = _rwkv_out(x, mod, norm_g, y, proj[10], proj[9], p, i, j, grp)
            else:
                perm = _rope_swap_perm()
                w_uq = mla_w_uq[j].reshape(Q_RANK, MLA_H, D_NOPE + D_ROPE)
                w_uq_rope = w_uq[:, :, D_NOPE:]
                w_dkv_r = mla_w_dkv[j][:, KV_RANK:]
                p = dict(w_dq=mla_w_dq, g_q=mla_g_q, g_kv=mla_g_kv, w_uk=mla_w_uk, w_uv=mla_w_uv, w_o=mla_w_o,
                         w_uq_nope=w_uq[:, :, :D_NOPE].reshape(Q_RANK, MLA_H * D_NOPE),
                         w_uq_rope=w_uq_rope.reshape(Q_RANK, MLA_H * D_ROPE),
                         w_uq_rope_sw=w_uq_rope[:, :, perm].reshape(Q_RANK, MLA_H * D_ROPE),
                         w_dkv_c=mla_w_dkv[j][:, :KV_RANK], w_dkv_r=w_dkv_r, w_dkv_r_sw=w_dkv_r[:, perm])
                qn, qr, ckv, kr = _mla_proj(x, mod, norm_g, p, i, j, not is_prompt, grp)
                if is_prompt:
                    new_ckv.append(ckv.reshape(batch, seq, KV_RANK))
                    new_krope.append(kr.reshape(batch, seq, D_ROPE))
                    x = _mla_attn(x, mod, norm_g, qn, qr, ckv, kr, p, i, j, nseq, slen, slen, slen,
                                  lambda b: 0)
                else:
                    klen = past_len + slen
                    ckv_all = jnp.concatenate([cache_mla_ckv[:, j], ckv.reshape(nseq, slen, KV_RANK)], axis=1)
                    kr_all = jnp.concatenate([cache_mla_krope[:, j], kr.reshape(nseq, slen, D_ROPE)], axis=1)
                    x = _mla_attn(x, mod, norm_g, qn, qr, ckv_all.reshape(nseq * klen, KV_RANK),
                                  kr_all.reshape(nseq * klen, D_ROPE), p, i, j, nseq, slen, klen, 512,
                                  lambda b: 1 + b)
            x = _mlp(x, mod, norm_g, mlp_w1, mlp_w2, i, grp)
            outs.append(x)
        xp, xs = outs

    return (xp.reshape(batch, seq, D), xs.reshape(dec_batch, dec_seq, D),
            jnp.stack(new_rwkv, axis=1), jnp.stack(new_ckv, axis=1), jnp.stack(new_krope, axis=1))
```

```python
import functools

import numpy as np
import jax
import jax.numpy as jnp
from jax import lax
from jax.experimental import pallas as pl
from jax.experimental.pallas import tpu as pltpu

D = 1024
DEPTH = 4
N_MOD = 6
D_FF = 4 * D
EPS = 1e-6
GRID_W = 64
FFT_GROUPS = 8
FFT_GW = D // FFT_GROUPS
HS = 64
NH = D // HS
LORA_W = 64
LORA_A = 64
LORA_G = 128
GN_EPS = 64e-5
MLA_H = 8
D_NOPE = 128
D_ROPE = 64
D_V = 128
KV_RANK = 256
Q_RANK = 384
ROPE_BASE = 10000.0
MLA_SCALE = (D_NOPE + D_ROPE) ** -0.5

F32 = jnp.float32
BF16 = jnp.bfloat16

TM = 1024
LANES = 128
CHUNK = 64
SCAN_UNROLL = 4
VMEM_LIMIT = 56 * 1024 * 1024


def _cparams(sem):
    return pltpu.CompilerParams(dimension_semantics=sem, vmem_limit_bytes=VMEM_LIMIT)


def _dot(a, b):
    return jnp.dot(a.astype(BF16), b.astype(BF16), preferred_element_type=F32)


def _dot_nt(a, b):
    return lax.dot_general(a.astype(BF16), b.astype(BF16), (((1,), (1,)), ((), ())),
                           preferred_element_type=F32)


def _dot_tn(a, b):
    return lax.dot_general(a.astype(BF16), b.astype(BF16), (((0,), (0,)), ((), ())),
                           preferred_element_type=F32)


def _split3(x):
    hi = x.astype(BF16)
    r1 = x - hi.astype(F32)
    mid = r1.astype(BF16)
    lo = (r1 - mid.astype(F32)).astype(BF16)
    return hi, mid, lo


def _dot_exact_lhs(e, x):
    hi, mid, lo = _split3(x)
    return (jnp.dot(e, hi, preferred_element_type=F32)
            + jnp.dot(e, mid, preferred_element_type=F32)
            + jnp.dot(e, lo, preferred_element_type=F32))


def _dot_exact_rhs(x, e):
    hi, mid, lo = _split3(x)
    return (jnp.dot(hi, e, preferred_element_type=F32)
            + jnp.dot(mid, e, preferred_element_type=F32)
            + jnp.dot(lo, e, preferred_element_type=F32))


def _rms(x, g):
    return x * lax.rsqrt(jnp.mean(x * x, axis=-1, keepdims=True) + EPS) * g


def _sigmoid(x):
    return 1.0 / (1.0 + jnp.exp(-x))


def _softplus(x):
    return jnp.maximum(x, 0.0) + jnp.log(1.0 + jnp.exp(-jnp.abs(x)))


def _head_indicator(n):
    r = lax.broadcasted_iota(jnp.int32, (n, n), 0) // HS
    c = lax.broadcasted_iota(jnp.int32, (n, n), 1) // HS
    return jnp.where(r == c, 1.0, 0.0).astype(BF16)


def _shift_rows(z, seq_len):
    n = z.shape[0]
    pos = lax.broadcasted_iota(jnp.int32, z.shape, 0) % seq_len
    prev = jnp.where(pos == 0, 0.0, pltpu.roll(z, 1, 0))
    nxt = jnp.where(pos == seq_len - 1, 0.0, pltpu.roll(z, n - 1, 0))
    return prev, nxt


MOD_TN = 1536


def _mod_kernel(cs_ref, w_ref, b_ref, o_ref):
    cs = cs_ref[...]
    s = cs * _sigmoid(cs)
    o_ref[0] = _dot(s, w_ref[0]) + b_ref[0]


def _modulation(cs, mod_w, mod_b):
    nj = (N_MOD * D) // MOD_TN
    out = pl.pallas_call(
        _mod_kernel,
        grid=(DEPTH, nj),
        in_specs=[
            pl.BlockSpec((8, D), lambda l, j: (0, 0)),
            pl.BlockSpec((1, D, MOD_TN), lambda l, j: (l, 0, j)),
            pl.BlockSpec((1, 1, MOD_TN), lambda l, j: (l, 0, j)),
        ],
        out_specs=pl.BlockSpec((1, 8, MOD_TN), lambda l, j: (l, 0, j)),
        out_shape=jax.ShapeDtypeStruct((DEPTH, 8, N_MOD * D), F32),
        compiler_params=_cparams(("arbitrary", "arbitrary")),
        name="modulation",
    )(cs, mod_w, mod_b.reshape(DEPTH, 1, N_MOD * D))
    return out.reshape(DEPTH, 8, N_MOD, D)


def _mod_spec(layer, group_of_tile, ngrid):
    if ngrid == 1:
        return pl.BlockSpec((1, 1, N_MOD, D), lambda i: (layer, group_of_tile(i), 0, 0))
    return pl.BlockSpec((1, 1, N_MOD, D), lambda i, j: (layer, group_of_tile(i), 0, 0))


def _normg_spec(layer, ngrid):
    if ngrid == 1:
        return pl.BlockSpec((1, 4, D), lambda i: (layer, 0, 0))
    return pl.BlockSpec((1, 4, D), lambda i, j: (layer, 0, 0))


MLP_FC = 512


def _mlp_kernel(x_ref, m_ref, g_ref, w1_ref, w2_ref, o_ref, h_scr, acc_scr):
    j = pl.program_id(1)
    m = m_ref[0, 0]
    g = g_ref[0]

    @pl.when(j == 0)
    def _():
        h = _rms(x_ref[...], g[2:3]) * (1.0 + m[4:5]) + m[3:4]
        h_scr[...] = h.astype(BF16)
        acc_scr[...] = jnp.zeros_like(acc_scr)

    a = jnp.dot(h_scr[...], w1_ref[0].astype(BF16), preferred_element_type=F32)
    a = jnp.maximum(a, 0.0)
    a = a * a
    acc_scr[...] += _dot(a, w2_ref[0])

    @pl.when(j == pl.num_programs(1) - 1)
    def _():
        o_ref[...] = x_ref[...] + m[5:6] * _rms(acc_scr[...], g[3:4])


def _mlp(x, mod, norm_g, w1, w2, layer, group_of_tile):
    n = x.shape[0]
    nj = D_FF // MLP_FC
    return pl.pallas_call(
        _mlp_kernel,
        grid=(n // TM, nj),
        in_specs=[
            pl.BlockSpec((TM, D), lambda i, j: (i, 0)),
            _mod_spec(layer, group_of_tile, 2),
            _normg_spec(layer, 2),
            pl.BlockSpec((1, D, MLP_FC), lambda i, j: (layer, 0, j)),
            pl.BlockSpec((1, MLP_FC, D), lambda i, j: (layer, j, 0)),
        ],
        out_specs=pl.BlockSpec((TM, D), lambda i, j: (i, 0)),
        out_shape=jax.ShapeDtypeStruct((n, D), F32),
        scratch_shapes=[pltpu.VMEM((TM, D), BF16), pltpu.VMEM((TM, D), F32)],
        compiler_params=_cparams(("arbitrary", "arbitrary")),
        name="mlp",
    )(x, mod, norm_g, w1, w2)


def _dft_mats(n):
    idx = np.arange(n, dtype=np.int64)
    ang = (2.0 * np.pi / n) * ((idx[:, None] * idx[None, :]) % n).astype(np.float64)
    scale = 1.0 / np.sqrt(n)
    return np.cos(ang) * scale, np.sin(ang) * scale


def _fnet_kernel(x_ref, m_ref, g_ref, cs_ref, cn_ref, sn_ref, w_ref, o_ref, p_scr, q_scr, f_scr,
                 *, seq_len):
    m = m_ref[0, 0]
    g = g_ref[0]
    x = x_ref[...]
    h = (_rms(x, g[0:1]) * (1.0 + m[1:2]) + m[0:1]).astype(BF16)
    cs = cs_ref[...].astype(BF16)
    for gi in range(FFT_GROUPS):
        pq = jnp.dot(h[:, gi * FFT_GW:(gi + 1) * FFT_GW], cs, preferred_element_type=F32)
        p_scr[:, gi * FFT_GW:(gi + 1) * FFT_GW] = pq[:, :FFT_GW].astype(BF16)
        q_scr[:, gi * FFT_GW:(gi + 1) * FFT_GW] = pq[:, FFT_GW:].astype(BF16)
    cn = cn_ref[...].astype(BF16)
    sn = sn_ref[...].astype(BF16)
    for s in range(TM // seq_len):
        rows = slice(s * seq_len, (s + 1) * seq_len)
        f = (jnp.dot(cn, p_scr[rows, :], preferred_element_type=F32)
             - jnp.dot(sn, q_scr[rows, :], preferred_element_type=F32))
        f_scr[rows, :] = f.astype(BF16)
    o = jnp.dot(f_scr[...], w_ref[0].astype(BF16), preferred_element_type=F32)
    o_ref[...] = x + m[2:3] * _rms(o, g[1:2])


def _fnet(x, mod, norm_g, w_out, layer, j, seq_len, group_of_tile):
    n = x.shape[0]
    cg, sg = _dft_mats(FFT_GW)
    cs = jnp.asarray(np.concatenate([cg, sg], axis=1), F32)
    cn_np, sn_np = _dft_mats(seq_len)
    cn = jnp.asarray(cn_np, F32)
    sn = jnp.asarray(sn_np, F32)
    return pl.pallas_call(
        functools.partial(_fnet_kernel, seq_len=seq_len),
        grid=(n // TM,),
        in_specs=[
            pl.BlockSpec((TM, D), lambda i: (i, 0)),
            _mod_spec(layer, group_of_tile, 1),
            _normg_spec(layer, 1),
            pl.BlockSpec((FFT_GW, 2 * FFT_GW), lambda i: (0, 0)),
            pl.BlockSpec((seq_len, seq_len), lambda i: (0, 0)),
            pl.BlockSpec((seq_len, seq_len), lambda i: (0, 0)),
            pl.BlockSpec((1, D, D), lambda i: (j, 0, 0)),
        ],
        out_specs=pl.BlockSpec((TM, D), lambda i: (i, 0)),
        out_shape=jax.ShapeDtypeStruct((n, D), F32),
        scratch_shapes=[pltpu.VMEM((TM, D), BF16), pltpu.VMEM((TM, D), BF16),
                        pltpu.VMEM((TM, D), BF16)],
        compiler_params=_cparams(("arbitrary",)),
        name="fourier_mix",
    )(x, mod, norm_g, cs, cn, sn, w_out)


CONV_CW = 256


def _conv_kernel(x_ref, m_ref, g_ref, wb_ref, wc_ref, wu_ref, cw_ref, wo_ref, o_ref, h_scr, acc_scr,
                 *, seq_len):
    j = pl.program_id(1)
    m = m_ref[0, 0]
    g = g_ref[0]

    @pl.when(j == 0)
    def _():
        h = _rms(x_ref[...], g[0:1]) * (1.0 + m[1:2]) + m[0:1]
        h_scr[...] = h.astype(BF16)
        acc_scr[...] = jnp.zeros_like(acc_scr)

    h = h_scr[...]
    bg = jnp.dot(h, wb_ref[0].astype(BF16), preferred_element_type=F32)
    cg = jnp.dot(h, wc_ref[0].astype(BF16), preferred_element_type=F32)
    u = jnp.dot(h, wu_ref[0].astype(BF16), preferred_element_type=F32)
    z = cg * u
    z_prev, z_next = _shift_rows(z, seq_len)
    cw = cw_ref[0]
    conv = z_prev * cw[0:1] + z * cw[1:2] + z_next * cw[2:3]
    acc_scr[...] += _dot(bg * conv, wo_ref[0])

    @pl.when(j == pl.num_programs(1) - 1)
    def _():
        o_ref[...] = x_ref[...] + m[2:3] * _rms(acc_scr[...], g[1:2])


def _conv(x, mod, norm_g, w_in, w_conv, w_out, layer, j, seq_len, group_of_tile):
    n = x.shape[0]
    nj = D // CONV_CW
    return pl.pallas_call(
        functools.partial(_conv_kernel, seq_len=seq_len),
        grid=(n // TM, nj),
        in_specs=[
            pl.BlockSpec((TM, D), lambda i, c: (i, 0)),
            _mod_spec(layer, group_of_tile, 2),
            _normg_spec(layer, 2),
            pl.BlockSpec((1, D, CONV_CW), lambda i, c: (j, 0, c)),
            pl.BlockSpec((1, D, CONV_CW), lambda i, c: (j, 0, nj + c)),
            pl.BlockSpec((1, D, CONV_CW), lambda i, c: (j, 0, 2 * nj + c)),
            pl.BlockSpec((1, 3, CONV_CW), lambda i, c: (j, 0, c)),
            pl.BlockSpec((1, CONV_CW, D), lambda i, c: (j, c, 0)),
        ],
        out_specs=pl.BlockSpec((TM, D), lambda i, c: (i, 0)),
        out_shape=jax.ShapeDtypeStruct((n, D), F32),
        scratch_shapes=[pltpu.VMEM((TM, D), BF16), pltpu.VMEM((TM, D), F32)],
        compiler_params=_cparams(("arbitrary", "arbitrary")),
        name="short_conv",
    )(x, mod, norm_g, w_in, w_in, w_in, w_conv, w_out)


RWP_CW = 128
RW_CW = 256
N_RW_OUT = 11


def _rwkv_proj_kernel(x_ref, m_ref, g_ref, mu_ref, wr_ref, wk_ref, wv_ref, w0_ref, wl1_ref, wl2_ref,
                      a0_ref, al1_ref, al2_ref, gl1_ref, gl2_ref, kk_ref, ka_ref, rk_ref,
                      r_o, v_o, kk_o, ld0_o, b0_o, kt0_o, ld1_o, b1_o, kt1_o, g_o, bonus_o,
                      xr_scr, xk_scr, xv_scr, tw_scr, ta_scr, sg_scr, *, seq_len):
    j = pl.program_id(1)

    @pl.when(j == 0)
    def _():
        m = m_ref[0, 0]
        g = g_ref[0]
        mu = mu_ref[0]
        h = _rms(x_ref[...], g[0:1]) * (1.0 + m[1:2]) + m[0:1]
        h_prev, h_next = _shift_rows(h, seq_len)
        dx = 0.5 * (h_prev + h_next) - h
        xr_scr[...] = (h + dx * mu[0:1]).astype(BF16)
        xk_scr[...] = (h + dx * mu[2:3]).astype(BF16)
        xv_scr[...] = (h + dx * mu[3:4]).astype(BF16)
        xw = (h + dx * mu[1:2]).astype(BF16)
        xa = (h + dx * mu[4:5]).astype(BF16)
        xg = (h + dx * mu[5:6]).astype(BF16)
        for d in range(2):
            tw_scr[d] = jnp.tanh(_dot(xw, wl1_ref[0, d])).astype(BF16)
            ta_scr[d] = _dot(xa, al1_ref[0, d]).astype(BF16)
        sg_scr[...] = _sigmoid(_dot(xg, gl1_ref[0])).astype(BF16)

    e = _head_indicator(RWP_CW)
    r = jnp.dot(xr_scr[...], wr_ref[0].astype(BF16), preferred_element_type=F32)
    k = jnp.dot(xk_scr[...], wk_ref[0].astype(BF16), preferred_element_type=F32)
    v = jnp.dot(xv_scr[...], wv_ref[0].astype(BF16), preferred_element_type=F32)
    kk = k * kk_ref[...]
    kk = kk * lax.rsqrt(_dot_exact_rhs(kk * kk, e) + 1e-12)
    r_o[...] = r
    v_o[...] = v
    kk_o[...] = kk
    g_o[...] = jnp.dot(sg_scr[...], gl2_ref[0].astype(BF16), preferred_element_type=F32)
    k_a = ka_ref[...]
    rrk = r * rk_ref[...]
    bsum = None
    for d, (ld_o, b_o, kt_o) in enumerate(((ld0_o, b0_o, kt0_o), (ld1_o, b1_o, kt1_o))):
        zw = w0_ref[0, d:d + 1, :] + jnp.dot(tw_scr[d], wl2_ref[0, d].astype(BF16),
                                             preferred_element_type=F32)
        logw = -_softplus(-zw) - 0.5
        ld_o[...] = -jnp.exp(logw)
        a = _sigmoid(a0_ref[0, d:d + 1, :] + jnp.dot(ta_scr[d], al2_ref[0, d].astype(BF16),
                                                      preferred_element_type=F32))
        kt = k * (1.0 + (a - 1.0) * k_a)
        b_o[...] = kk * a
        kt_o[...] = kt
        bs = _dot_exact_rhs(rrk * kt, e)
        bsum = bs if bsum is None else bsum + bs
    bonus_o[...] = bsum * v


def _rwkv_proj(x, mod, norm_g, p, layer, j, seq_len, group_of_tile):
    n = x.shape[0]
    nj = D // RWP_CW
    tile = pl.BlockSpec((TM, RWP_CW), lambda i, c: (i, c))
    wcol = lambda: pl.BlockSpec((1, D, RWP_CW), lambda i, c: (j, 0, c))
    vec = lambda: pl.BlockSpec((1, RWP_CW), lambda i, c: (j, c))
    return pl.pallas_call(
        functools.partial(_rwkv_proj_kernel, seq_len=seq_len),
        grid=(n // TM, nj),
        in_specs=[
            pl.BlockSpec((TM, D), lambda i, c: (i, 0)),
            _mod_spec(layer, group_of_tile, 2),
            _normg_spec(layer, 2),
            pl.BlockSpec((1, 6, D), lambda i, c: (j, 0, 0)),
            wcol(), wcol(), wcol(),
            pl.BlockSpec((1, 2, RWP_CW), lambda i, c: (j, 0, c)),
            pl.BlockSpec((1, 2, D, LORA_W), lambda i, c: (j, 0, 0, 0)),
            pl.BlockSpec((1, 2, LORA_W, RWP_CW), lambda i, c: (j, 0, 0, c)),
            pl.BlockSpec((1, 2, RWP_CW), lambda i, c: (j, 0, c)),
            pl.BlockSpec((1, 2, D, LORA_A), lambda i, c: (j, 0, 0, 0)),
            pl.BlockSpec((1, 2, LORA_A, RWP_CW), lambda i, c: (j, 0, 0, c)),
            pl.BlockSpec((1, D, LORA_G), lambda i, c: (j, 0, 0)),
            pl.BlockSpec((1, LORA_G, RWP_CW), lambda i, c: (j, 0, c)),
            vec(), vec(), vec(),
        ],
        out_specs=[tile] * N_RW_OUT,
        out_shape=[jax.ShapeDtypeStruct((n, D), F32)] * N_RW_OUT,
        scratch_shapes=[pltpu.VMEM((TM, D), BF16), pltpu.VMEM((TM, D), BF16), pltpu.VMEM((TM, D), BF16),
                        pltpu.VMEM((2, TM, LORA_W), BF16), pltpu.VMEM((2, TM, LORA_A), BF16),
                        pltpu.VMEM((TM, LORA_G), BF16)],
        compiler_params=_cparams(("arbitrary", "arbitrary")),
        name="rwkv_proj",
    )(x, mod, norm_g, p["mu"], p["w_r"], p["w_k"], p["w_v"], p["w0"], p["w_l1"], p["w_l2"],
      p["a0"], p["a_l1"], p["a_l2"], p["g_l1"], p["g_l2"], p["k_k"], p["k_a"], p["r_k"])


def _scan_precompute(units):
    c = CHUNK
    c2 = 2 * c
    row = lax.broadcasted_iota(jnp.int32, (c2, c2), 0)
    col = lax.broadcasted_iota(jnp.int32, (c2, c2), 1)
    rc = lax.broadcasted_iota(jnp.int32, (c, c), 0)
    cc = lax.broadcasted_iota(jnp.int32, (c, c), 1)
    head_a = lax.broadcasted_iota(jnp.int32, (c, LANES), 1) < HS
    own_lanes = jnp.concatenate([head_a, jnp.logical_not(head_a)], axis=0)
    eye = jnp.where(row == col, 1.0, 0.0)

    def two_heads(t):
        return jnp.concatenate([jnp.where(head_a, t, 0.0), jnp.where(head_a, 0.0, t)], axis=0)

    def causal(reverse):
        if reverse:
            return (jnp.where(cc >= rc, 1.0, 0.0).astype(BF16), (col % c) > (row % c), (col % c) >= (row % c))
        return (jnp.where(cc <= rc, 1.0, 0.0).astype(BF16), (col % c) < (row % c), (col % c) <= (row % c))

    masks = {rev: causal(rev) for rev in sorted({u[6] for u in units})}

    cums = [_dot_exact_lhs(masks[u[6]][0], u[0]) for u in units]
    st = []
    for (ld, kk, beta, kt, r, v, rev), cum in zip(units, cums):
        tot = cum[0:1] if rev else cum[c - 1:c]
        ginv = jnp.exp(-cum)
        tail = jnp.exp(tot - cum)
        st.append(dict(
            rev=rev, etot=jnp.exp(tot),
            a_t=two_heads(-kk * jnp.exp(cum - ld)), r_t=two_heads(r * jnp.exp(cum)),
            bk=jnp.concatenate([two_heads(beta * ginv), two_heads(kt * ginv)], axis=0),
            bkg=jnp.concatenate([two_heads(beta * tail), two_heads(kt * tail)], axis=0),
            v2=jnp.concatenate([v, v], axis=0), vh=two_heads(v)))
    grams = [_dot_nt(jnp.concatenate([s["a_t"], s["r_t"]], axis=0), s["bk"]) for s in st]
    for s, gram in zip(st, grams):
        _, strict, incl = masks[s["rev"]]
        s["l_ab"] = jnp.where(strict, gram[:c2, :c2], 0.0)
        s["l_ak"] = jnp.where(strict, gram[:c2, c2:], 0.0)
        s["t_rb"] = jnp.where(incl, gram[c2:, :c2], 0.0)
        s["t_rk"] = jnp.where(incl, gram[c2:, c2:], 0.0)
    lvs = [_dot(s["l_ak"], s["v2"]) for s in st]
    minvs = [eye for _ in st]
    b = 1
    while b < c:
        same = (row // (2 * b)) == (col // (2 * b))
        es = []
        for s in st:
            first, second = (col % (2 * b)) < b, (row % (2 * b)) >= b
            if s["rev"]:
                first, second = (row % (2 * b)) < b, (col % (2 * b)) >= b
            es.append(jnp.where(same & first & second, s["l_ab"], 0.0))
        if b == 1:
            minvs = [m + e for m, e in zip(minvs, es)]
        else:
            half = [_dot(m, e) for m, e in zip(minvs, es)]
            minvs = [m + _dot(h, m) for m, h in zip(minvs, half)]
        b *= 2
    mms = [_dot(m, jnp.concatenate([s["a_t"], lv], axis=1)) for m, s, lv in zip(minvs, st, lvs)]
    tts = [_dot(s["t_rb"], mm) for s, mm in zip(st, mms)]
    tkv = [_dot(s["t_rk"], s["v2"]) for s in st]
    ps = [_dot_tn(mm[:, :LANES], s["bkg"][:c2]) for s, mm in zip(st, mms)]
    qs = [_dot_tn(jnp.concatenate([jnp.where(own_lanes, mm[:, LANES:], 0.0), s["vh"]], axis=0), s["bkg"])
          for s, mm in zip(st, mms)]
    return [dict(etot=s["etot"], p=p, q=q, reff=s["r_t"] + tt[:, :LANES], y0=tt[:, LANES:] + kv)
            for s, p, q, tt, kv in zip(st, ps, qs, tts, tkv)]


def _rwkv_scan_kernel(*refs, seq_len, has_init, want_final):
    r_ref, v_ref, kk_ref, ld0_ref, b0_ref, kt0_ref, ld1_ref, b1_ref, kt1_ref = refs[:9]
    pos = 9
    if has_init:
        s0_ref = refs[pos]
        pos += 1
    y_ref = refs[pos]
    pos += 1
    if want_final:
        sf_ref = refs[pos]
        pos += 1
    g_scr, yb_scr = refs[pos:pos + 2]
    nc = seq_len // CHUNK
    ngroups = nc // SCAN_UNROLL
    per_dir = ((ld0_ref, b0_ref, kt0_ref), (ld1_ref, b1_ref, kt1_ref))
    y_dst = (y_ref, yb_scr)

    head_a = lax.broadcasted_iota(jnp.int32, (CHUNK, LANES), 1) < HS

    def group(gi, states):
        rows, units = [], []
        for u in range(SCAN_UNROLL):
            for d in range(2):
                ld_ref, b_ref, kt_ref = per_dir[d]
                cidx = gi * SCAN_UNROLL + u
                if d == 1:
                    cidx = nc - 1 - cidx
                start = cidx * CHUNK
                rw = slice(start, start + CHUNK) if isinstance(start, int) else pl.ds(
                    pl.multiple_of(start, CHUNK), CHUNK)
                rows.append(rw)
                units.append((ld_ref[rw, :], kk_ref[rw, :], b_ref[rw, :], kt_ref[rw, :], r_ref[rw, :],
                              v_ref[rw, :], d == 1))
        pre = _scan_precompute(units)
        states = list(states)
        for u in range(SCAN_UNROLL):
            cur = [pre[2 * u + d] for d in range(2)]
            y2 = [_dot_nt(cur[d]["reff"], states[d]) + cur[d]["y0"] for d in range(2)]
            states = [states[d] * cur[d]["etot"] + _dot(states[d], cur[d]["p"]) + cur[d]["q"]
                      for d in range(2)]
            for d in range(2):
                y_dst[d][rows[2 * u + d], :] = jnp.where(head_a, y2[d][:CHUNK], y2[d][CHUNK:])
        return states

    init = [s0_ref[0, d, 0] if has_init else jnp.zeros((LANES, LANES), F32) for d in range(2)]
    if ngroups == 1:
        final = group(0, init)
    else:
        for d in range(2):
            g_scr[d] = init[d]

        def body(gi, carry):
            new = group(gi, [g_scr[0], g_scr[1]])
            for d in range(2):
                g_scr[d] = new[d]
            return carry

        lax.fori_loop(0, ngroups, body, 0)
        final = [g_scr[0], g_scr[1]]
    y_ref[...] += yb_scr[...]
    if want_final:
        for d in range(2):
            sf_ref[0, d, 0] = final[d]


def _rwkv_scan(proj, s_init, n_seq, seq_len, want_final):
    r, v, kk, ld0, b0, kt0, ld1, b1, kt1 = proj[:9]
    n = r.shape[0]
    npair = D // LANES
    blk = pl.BlockSpec((seq_len, LANES), lambda b, p: (b, p))
    st_spec = pl.BlockSpec((1, 2, 1, LANES, LANES), lambda b, p: (b, 0, p, 0, 0))
    in_specs = [blk] * 9
    args = [r, v, kk, ld0, b0, kt0, ld1, b1, kt1]
    has_init = s_init is not None
    if has_init:
        in_specs.append(st_spec)
        args.append(s_init)
    out_specs = [blk]
    out_shape = [jax.ShapeDtypeStruct((n, D), F32)]
    if want_final:
        out_specs.append(st_spec)
        out_shape.append(jax.ShapeDtypeStruct((n_seq, 2, npair, LANES, LANES), F32))
    res = pl.pallas_call(
        functools.partial(_rwkv_scan_kernel, seq_len=seq_len, has_init=has_init, want_final=want_final),
        grid=(n_seq, npair),
        in_specs=in_specs,
        out_specs=out_specs,
        out_shape=out_shape,
        scratch_shapes=[pltpu.VMEM((2, LANES, LANES), F32), pltpu.VMEM((seq_len, LANES), F32)],
        compiler_params=_cparams(("arbitrary", "arbitrary")),
        name="rwkv_scan",
    )(*args)
    return res


def _rwkv_out_kernel(x_ref, m_ref, g_ref, y_ref, bonus_ref, gate_ref, lng_ref, lnb_ref, wo_ref, o_ref,
                     acc_scr):
    j = pl.program_id(1)

    @pl.when(j == 0)
    def _():
        acc_scr[...] = jnp.zeros_like(acc_scr)

    e = _head_indicator(RW_CW)
    y = y_ref[...]
    mean = _dot_exact_rhs(y, e) * (1.0 / HS)
    yc = y - mean
    var = _dot_exact_rhs(yc * yc, e) * (1.0 / HS)
    yn = yc * lax.rsqrt(var + GN_EPS) * lng_ref[...] + lnb_ref[...]
    yn = (yn + bonus_ref[...]) * gate_ref[...]
    acc_scr[...] += _dot(yn, wo_ref[0])

    @pl.when(j == pl.num_programs(1) - 1)
    def _():
        m = m_ref[0, 0]
        g = g_ref[0]
        o_ref[...] = x_ref[...] + m[2:3] * _rms(acc_scr[...], g[1:2])


def _rwkv_out(x, mod, norm_g, y, bonus, gate, p, layer, j, group_of_tile):
    n = x.shape[0]
    nj = D // RW_CW
    tile = pl.BlockSpec((TM, RW_CW), lambda i, c: (i, c))
    vec = pl.BlockSpec((1, RW_CW), lambda i, c: (j, c))
    return pl.pallas_call(
        _rwkv_out_kernel,
        grid=(n // TM, nj),
        in_specs=[
            pl.BlockSpec((TM, D), lambda i, c: (i, 0)),
            _mod_spec(layer, group_of_tile, 2),
            _normg_spec(layer, 2),
            tile, tile, tile, vec, vec,
            pl.BlockSpec((1, RW_CW, D), lambda i, c: (j, c, 0)),
        ],
        out_specs=pl.BlockSpec((TM, D), lambda i, c: (i, 0)),
        out_shape=jax.ShapeDtypeStruct((n, D), F32),
        scratch_shapes=[pltpu.VMEM((TM, D), F32)],
        compiler_params=_cparams(("arbitrary", "arbitrary")),
        name="rwkv_out",
    )(x, mod, norm_g, y, bonus, gate, p["ln_g"], p["ln_b"], p["w_o"])


def _blockdiag_states(s):
    b = s.shape[0]
    sp = s.reshape(b, 2, NH // 2, 2, HS, HS)
    z = jnp.zeros((b, 2, NH // 2, HS, HS), s.dtype)
    top = jnp.concatenate([sp[:, :, :, 0], z], axis=-1)
    bot = jnp.concatenate([z, sp[:, :, :, 1]], axis=-1)
    return jnp.concatenate([top, bot], axis=-2)


def _unblock_states(g):
    a = g[:, :, :, :HS, :HS]
    b = g[:, :, :, HS:, HS:]
    return jnp.stack([a, b], axis=3).reshape(g.shape[0], 2, NH, HS, HS)


def _rope_tables(n):
    rows = n // GRID_W
    row = np.repeat(np.arange(rows), GRID_W)
    col = np.tile(np.arange(GRID_W), rows)
    pos = np.stack([row, col], axis=-1).astype(np.float64)
    quarter = D_ROPE // 4
    inv = ROPE_BASE ** (-np.arange(quarter, dtype=np.float64) / quarter)
    ang = pos[:, :, None] * inv
    cos = np.cos(ang)
    sin = np.sin(ang)
    cos_t = np.concatenate([cos, cos], axis=-1).reshape(n, D_ROPE)
    sin_t = np.concatenate([-sin, sin], axis=-1).reshape(n, D_ROPE)
    return cos_t.astype(np.float32), sin_t.astype(np.float32)


def _rope_swap_perm():
    quarter = D_ROPE // 4
    base = np.arange(D_ROPE)
    return np.where((base % (2 * quarter)) < quarter, base + quarter, base - quarter)


def _mla_proj_kernel(*refs, positional):
    (x_ref, m_ref, g_ref, wdq_ref, gq_ref, wqn_ref, wqr_ref, wqs_ref, wdkv_ref, wkr_ref, wks_ref,
     gkv_ref) = refs[:12]
    pos = 12
    if positional:
        cosq_ref, sinq_ref, cosk_ref, sink_ref = refs[pos:pos + 4]
        pos += 4
    qn_o, qr_o, ckv_o, kr_o = refs[pos:pos + 4]
    m = m_ref[0, 0]
    g = g_ref[0]
    h = (_rms(x_ref[...], g[0:1]) * (1.0 + m[1:2]) + m[0:1]).astype(BF16)
    ql = jnp.dot(h, wdq_ref[0].astype(BF16), preferred_element_type=F32)
    ql = (ql * lax.rsqrt(jnp.mean(ql * ql, axis=-1, keepdims=True) + EPS) * gq_ref[...]).astype(BF16)
    qn_o[...] = jnp.dot(ql, wqn_ref[...].astype(BF16), preferred_element_type=F32)
    qr = jnp.dot(ql, wqr_ref[...].astype(BF16), preferred_element_type=F32)
    ckv = jnp.dot(h, wdkv_ref[...].astype(BF16), preferred_element_type=F32)
    ckv_o[...] = ckv * lax.rsqrt(jnp.mean(ckv * ckv, axis=-1, keepdims=True) + EPS) * gkv_ref[...]
    kr = jnp.dot(h, wkr_ref[...].astype(BF16), preferred_element_type=F32)
    if positional:
        qs = jnp.dot(ql, wqs_ref[...].astype(BF16), preferred_element_type=F32)
        ks = jnp.dot(h, wks_ref[...].astype(BF16), preferred_element_type=F32)
        qr = qr * cosq_ref[...] + qs * sinq_ref[...]
        kr = kr * cosk_ref[...] + ks * sink_ref[...]
    qr_o[...] = qr
    kr_o[...] = kr


def _mla_proj(x, mod, norm_g, p, layer, j, positional, group_of_tile):
    n = x.shape[0]
    full = lambda shape: pl.BlockSpec(shape, lambda i: (0,) * len(shape))
    in_specs = [
        pl.BlockSpec((TM, D), lambda i: (i, 0)),
        _mod_spec(layer, group_of_tile, 1),
        _normg_spec(layer, 1),
        pl.BlockSpec((1, D, Q_RANK), lambda i: (j, 0, 0)),
        pl.BlockSpec((1, Q_RANK), lambda i: (j, 0)),
        full((Q_RANK, MLA_H * D_NOPE)), full((Q_RANK, MLA_H * D_ROPE)), full((Q_RANK, MLA_H * D_ROPE)),
        full((D, KV_RANK)), full((D, D_ROPE)), full((D, D_ROPE)),
        pl.BlockSpec((1, KV_RANK), lambda i: (j, 0)),
    ]
    args = [x, mod, norm_g, p["w_dq"], p["g_q"], p["w_uq_nope"], p["w_uq_rope"], p["w_uq_rope_sw"],
            p["w_dkv_c"], p["w_dkv_r"], p["w_dkv_r_sw"], p["g_kv"]]
    if positional:
        cos_t, sin_t = _rope_tables(TM)
        in_specs += [full((TM, MLA_H * D_ROPE)), full((TM, MLA_H * D_ROPE)),
                     full((TM, D_ROPE)), full((TM, D_ROPE))]
        args += [jnp.asarray(np.tile(cos_t, (1, MLA_H))), jnp.asarray(np.tile(sin_t, (1, MLA_H))),
                 jnp.asarray(cos_t), jnp.asarray(sin_t)]
    widths = (MLA_H * D_NOPE, MLA_H * D_ROPE, KV_RANK, D_ROPE)
    return pl.pallas_call(
        functools.partial(_mla_proj_kernel, positional=positional),
        grid=(n // TM,),
        in_specs=in_specs,
        out_specs=[pl.BlockSpec((TM, w), lambda i: (i, 0)) for w in widths],
        out_shape=[jax.ShapeDtypeStruct((n, w), F32) for w in widths],
        compiler_params=_cparams(("arbitrary",)),
        name="mla_proj",
    )(*args)


def _mla_attn_kernel(x_ref, m_ref, g_ref, qn_ref, qr_ref, ckv_ref, kr_ref, wuk_ref, wuv_ref, wo_ref,
                     o_ref, kn_scr, vv_scr, oh_scr):
    qi = pl.program_id(1)

    @pl.when(qi == 0)
    def _():
        ckv = ckv_ref[...].astype(BF16)
        kn_scr[...] = jnp.dot(ckv, wuk_ref[0].astype(BF16), preferred_element_type=F32).astype(BF16)
        vv_scr[...] = jnp.dot(ckv, wuv_ref[0].astype(BF16), preferred_element_type=F32).astype(BF16)

    kr = kr_ref[...].astype(BF16)
    for hd in range(MLA_H):
        qn = qn_ref[:, hd * D_NOPE:(hd + 1) * D_NOPE]
        qr = qr_ref[:, hd * D_ROPE:(hd + 1) * D_ROPE]
        s = (_dot_nt(qn, kn_scr[:, hd * D_NOPE:(hd + 1) * D_NOPE]) + _dot_nt(qr, kr)) * MLA_SCALE
        s = s - jnp.max(s, axis=-1, keepdims=True)
        pexp = jnp.exp(s)
        pr = pexp / jnp.sum(pexp, axis=-1, keepdims=True)
        oh_scr[:, hd * D_V:(hd + 1) * D_V] = jnp.dot(
            pr.astype(BF16), vv_scr[:, hd * D_V:(hd + 1) * D_V], preferred_element_type=F32).astype(BF16)
    o = jnp.dot(oh_scr[...], wo_ref[0].astype(BF16), preferred_element_type=F32)
    m = m_ref[0, 0]
    g = g_ref[0]
    o_ref[...] = x_ref[...] + m[2:3] * _rms(o, g[1:2])


def _mla_attn(x, mod, norm_g, qn, qr, ckv_all, kr_all, p, layer, j, n_seq, q_len, k_len, tq, group_of_seq):
    n = x.shape[0]
    nq = q_len // tq
    return pl.pallas_call(
        _mla_attn_kernel,
        grid=(n_seq, nq),
        in_specs=[
            pl.BlockSpec((tq, D), lambda b, q: (b * nq + q, 0)),
            pl.BlockSpec((1, 1, N_MOD, D), lambda b, q: (layer, group_of_seq(b), 0, 0)),
            pl.BlockSpec((1, 4, D), lambda b, q: (layer, 0, 0)),
            pl.BlockSpec((tq, MLA_H * D_NOPE), lambda b, q: (b * nq + q, 0)),
            pl.BlockSpec((tq, MLA_H * D_ROPE), lambda b, q: (b * nq + q, 0)),
            pl.BlockSpec((k_len, KV_RANK), lambda b, q: (b, 0)),
            pl.BlockSpec((k_len, D_ROPE), lambda b, q: (b, 0)),
            pl.BlockSpec((1, KV_RANK, MLA_H * D_NOPE), lambda b, q: (j, 0, 0)),
            pl.BlockSpec((1, KV_RANK, MLA_H * D_V), lambda b, q: (j, 0, 0)),
            pl.BlockSpec((1, MLA_H * D_V, D), lambda b, q: (j, 0, 0)),
        ],
        out_specs=pl.BlockSpec((tq, D), lambda b, q: (b * nq + q, 0)),
        out_shape=jax.ShapeDtypeStruct((n, D), F32),
        scratch_shapes=[pltpu.VMEM((k_len, MLA_H * D_NOPE), BF16), pltpu.VMEM((k_len, MLA_H * D_V), BF16),
                        pltpu.VMEM((tq, MLA_H * D_V), BF16)],
        compiler_params=_cparams(("arbitrary", "arbitrary")),
        name="mla_attn",
    )(x, mod, norm_g, qn, qr, ckv_all, kr_all, p["w_uk"], p["w_uv"], p["w_o"])


def kernel(x_prompt, x_sample, state_rwkv, cache_mla_ckv, cache_mla_krope, c, c_ctx, mod_w, mod_b, norm_g,
           mlp_w1, mlp_w2, fft_w_out, conv_w_in, conv_w, conv_w_out, rwkv_mu, rwkv_w_r, rwkv_w_k, rwkv_w_v,
           rwkv_w_o, rwkv_w0, rwkv_w_l1, rwkv_w_l2, rwkv_a0, rwkv_a_l1, rwkv_a_l2, rwkv_g_l1, rwkv_g_l2,
           rwkv_k_k, rwkv_k_a, rwkv_r_k, rwkv_ln_g, rwkv_ln_b, mla_w_dq, mla_g_q, mla_w_uq, mla_w_dkv,
           mla_g_kv, mla_w_uk, mla_w_uv, mla_w_o):
    batch, seq, _ = x_prompt.shape
    dec_batch, dec_seq, _ = x_sample.shape
    past_len = cache_mla_ckv.shape[2]
    assert (batch * seq) % TM == 0 and TM % seq == 0 and dec_seq == TM and seq % CHUNK == 0

    xp = x_prompt.reshape(batch * seq, D)
    xs = x_sample.reshape(dec_batch * dec_seq, D)
    cs = jnp.concatenate([c_ctx[None, :], c, jnp.zeros((8 - 1 - dec_batch, D), F32)], axis=0)
    mod = _modulation(cs, mod_w, mod_b)

    grp_p = lambda i: 0
    grp_s = lambda i: 1 + i
    new_rwkv, new_ckv, new_krope = [], [], []
    streams = ((True, seq, batch, grp_p), (False, dec_seq, dec_batch, grp_s))

    for i in range(DEPTH):
        kind, j = i % 4, i // 4
        outs = []
        for is_prompt, slen, nseq, grp in streams:
            x = xp if is_prompt else xs
            if kind == 0:
                x = _fnet(x, mod, norm_g, fft_w_out, i, j, slen, grp)
            elif kind == 1:
                x = _conv(x, mod, norm_g, conv_w_in, conv_w, conv_w_out, i, j, slen, grp)
            elif kind == 2:
                p = dict(mu=rwkv_mu, w_r=rwkv_w_r, w_k=rwkv_w_k, w_v=rwkv_w_v, w_o=rwkv_w_o, w0=rwkv_w0,
                         w_l1=rwkv_w_l1, w_l2=rwkv_w_l2, a0=rwkv_a0, a_l1=rwkv_a_l1, a_l2=rwkv_a_l2,
                         g_l1=rwkv_g_l1, g_l2=rwkv_g_l2, k_k=rwkv_k_k, k_a=rwkv_k_a,
                         r_k=rwkv_r_k.reshape(-1, D), ln_g=rwkv_ln_g, ln_b=rwkv_ln_b)
                proj = _rwkv_proj(x, mod, norm_g, p, i, j, slen, grp)
                if is_prompt:
                    y, s_fin = _rwkv_scan(proj, None, nseq, slen, True)
                    new_rwkv.append(_unblock_states(s_fin))
                else:
                    (y,) = _rwkv_scan(proj, _blockdiag_states(state_rwkv[:, j]), nseq, slen, False)
                x = _rwkv_out(x, mod, norm_g, y, proj[10], proj[9], p, i, j, grp)
            else:
                perm = _rope_swap_perm()
                w_uq = mla_w_uq[j].reshape(Q_RANK, MLA_H, D_NOPE + D_ROPE)
                w_uq_rope = w_uq[:, :, D_NOPE:]
                w_dkv_r = mla_w_dkv[j][:, KV_RANK:]
                p = dict(w_dq=mla_w_dq, g_q=mla_g_q, g_kv=mla_g_kv, w_uk=mla_w_uk, w_uv=mla_w_uv, w_o=mla_w_o,
                         w_uq_nope=w_uq[:, :, :D_NOPE].reshape(Q_RANK, MLA_H * D_NOPE),
                         w_uq_rope=w_uq_rope.reshape(Q_RANK, MLA_H * D_ROPE),
                         w_uq_rope_sw=w_uq_rope[:, :, perm].reshape(Q_RANK, MLA_H * D_ROPE),
                         w_dkv_c=mla_w_dkv[j][:, :KV_RANK], w_dkv_r=w_dkv_r, w_dkv_r_sw=w_dkv_r[:, perm])
                qn, qr, ckv, kr = _mla_proj(x, mod, norm_g, p, i, j, not is_prompt, grp)
                if is_prompt:
                    new_ckv.append(ckv.reshape(batch, seq, KV_RANK))
                    new_krope.append(kr.reshape(batch, seq, D_ROPE))
                    x = _mla_attn(x, mod, norm_g, qn, qr, ckv, kr, p, i, j, nseq, slen, slen, slen,
                                  lambda b: 0)
                else:
                    klen = past_len + slen
                    ckv_all = jnp.concatenate([cache_mla_ckv[:, j], ckv.reshape(nseq, slen, KV_RANK)], axis=1)
                    kr_all = jnp.concatenate([cache_mla_krope[:, j], kr.reshape(nseq, slen, D_ROPE)], axis=1)
                    x = _mla_attn(x, mod, norm_g, qn, qr, ckv_all.reshape(nseq * klen, KV_RANK),
                                  kr_all.reshape(nseq * klen, D_ROPE), p, i, j, nseq, slen, klen, 512,
                                  lambda b: 1 + b)
            x = _mlp(x, mod, norm_g, mlp_w1, mlp_w2, i, grp)
            outs.append(x)
        xp, xs = outs

    return (xp.reshape(batch, seq, D), xs.reshape(dec_batch, dec_seq, D),
            jnp.stack(new_rwkv, axis=1), jnp.stack(new_ckv, axis=1), jnp.stack(new_krope, axis=1))
```

```python
import functools

import numpy as np
import jax
import jax.numpy as jnp
from jax import lax
from jax.experimental import pallas as pl
from jax.experimental.pallas import tpu as pltpu

D = 1024
DEPTH = 4
N_MOD = 6
D_FF = 4 * D
EPS = 1e-6
GRID_W = 64
FFT_GROUPS = 8
FFT_GW = D // FFT_GROUPS
HS = 64
NH = D // HS
LORA_W = 64
LORA_A = 64
LORA_G = 128
GN_EPS = 64e-5
MLA_H = 8
D_NOPE = 128
D_ROPE = 64
D_V = 128
KV_RANK = 256
Q_RANK = 384
ROPE_BASE = 10000.0
MLA_SCALE = (D_NOPE + D_ROPE) ** -0.5

F32 = jnp.float32
BF16 = jnp.bfloat16

TM = 1024
LANES = 128
CHUNK = 64
SCAN_UNROLL = 4
SCAN_PAIRS = 2
VMEM_LIMIT = 56 * 1024 * 1024


def _cparams(sem):
    return pltpu.CompilerParams(dimension_semantics=sem, vmem_limit_bytes=VMEM_LIMIT)


def _dot(a, b):
    return jnp.dot(a.astype(BF16), b.astype(BF16), preferred_element_type=F32)


def _dot_nt(a, b):
    return lax.dot_general(a.astype(BF16), b.astype(BF16), (((1,), (1,)), ((), ())),
                           preferred_element_type=F32)


def _dot_tn(a, b):
    return lax.dot_general(a.astype(BF16), b.astype(BF16), (((0,), (0,)), ((), ())),
                           preferred_element_type=F32)


def _split3(x):
    hi = x.astype(BF16)
    r1 = x - hi.astype(F32)
    mid = r1.astype(BF16)
    lo = (r1 - mid.astype(F32)).astype(BF16)
    return hi, mid, lo


def _dot_exact_lhs(e, x):
    hi, mid, lo = _split3(x)
    return (jnp.dot(e, hi, preferred_element_type=F32)
            + jnp.dot(e, mid, preferred_element_type=F32)
            + jnp.dot(e, lo, preferred_element_type=F32))


def _dot_hilo_rhs(x, e):
    hi = x.astype(BF16)
    lo = (x - hi.astype(F32)).astype(BF16)
    return jnp.dot(hi, e, preferred_element_type=F32) + jnp.dot(lo, e, preferred_element_type=F32)


def _rms(x, g):
    return x * lax.rsqrt(jnp.mean(x * x, axis=-1, keepdims=True) + EPS) * g


def _sigmoid(x):
    return 1.0 / (1.0 + jnp.exp(-x))


def _softplus(x):
    return jnp.maximum(x, 0.0) + jnp.log(1.0 + jnp.exp(-jnp.abs(x)))


def _head_indicator(n):
    r = lax.broadcasted_iota(jnp.int32, (n, n), 0) // HS
    c = lax.broadcasted_iota(jnp.int32, (n, n), 1) // HS
    return jnp.where(r == c, 1.0, 0.0).astype(BF16)


def _shift_rows(z, seq_len):
    n = z.shape[0]
    pos = lax.broadcasted_iota(jnp.int32, z.shape, 0) % seq_len
    prev = jnp.where(pos == 0, 0.0, pltpu.roll(z, 1, 0))
    nxt = jnp.where(pos == seq_len - 1, 0.0, pltpu.roll(z, n - 1, 0))
    return prev, nxt


MOD_TN = 1536


def _mod_kernel(cs_ref, w_ref, b_ref, o_ref):
    cs = cs_ref[...]
    s = cs * _sigmoid(cs)
    o_ref[0] = _dot(s, w_ref[0]) + b_ref[0]


def _modulation(cs, mod_w, mod_b):
    nj = (N_MOD * D) // MOD_TN
    out = pl.pallas_call(
        _mod_kernel,
        grid=(DEPTH, nj),
        in_specs=[
            pl.BlockSpec((8, D), lambda l, j: (0, 0)),
            pl.BlockSpec((1, D, MOD_TN), lambda l, j: (l, 0, j)),
            pl.BlockSpec((1, 1, MOD_TN), lambda l, j: (l, 0, j)),
        ],
        out_specs=pl.BlockSpec((1, 8, MOD_TN), lambda l, j: (l, 0, j)),
        out_shape=jax.ShapeDtypeStruct((DEPTH, 8, N_MOD * D), F32),
        compiler_params=_cparams(("arbitrary", "arbitrary")),
        name="modulation",
    )(cs, mod_w, mod_b.reshape(DEPTH, 1, N_MOD * D))
    return out.reshape(DEPTH, 8, N_MOD, D)


def _mod_spec(layer, group_of_tile, ngrid):
    if ngrid == 1:
        return pl.BlockSpec((1, 1, N_MOD, D), lambda i: (layer, group_of_tile(i), 0, 0))
    return pl.BlockSpec((1, 1, N_MOD, D), lambda i, j: (layer, group_of_tile(i), 0, 0))


def _normg_spec(layer, ngrid):
    if ngrid == 1:
        return pl.BlockSpec((1, 4, D), lambda i: (layer, 0, 0))
    return pl.BlockSpec((1, 4, D), lambda i, j: (layer, 0, 0))


MLP_FC = 512


def _mlp_kernel(x_ref, m_ref, g_ref, w1_ref, w2_ref, o_ref, h_scr, acc_scr):
    j = pl.program_id(1)
    m = m_ref[0, 0]
    g = g_ref[0]

    @pl.when(j == 0)
    def _():
        h = _rms(x_ref[...], g[2:3]) * (1.0 + m[4:5]) + m[3:4]
        h_scr[...] = h.astype(BF16)
        acc_scr[...] = jnp.zeros_like(acc_scr)

    a = jnp.dot(h_scr[...], w1_ref[0].astype(BF16), preferred_element_type=F32)
    a = jnp.maximum(a, 0.0)
    a = a * a
    acc_scr[...] += _dot(a, w2_ref[0])

    @pl.when(j == pl.num_programs(1) - 1)
    def _():
        o_ref[...] = x_ref[...] + m[5:6] * _rms(acc_scr[...], g[3:4])


def _mlp(x, mod, norm_g, w1, w2, layer, group_of_tile):
    n = x.shape[0]
    nj = D_FF // MLP_FC
    return pl.pallas_call(
        _mlp_kernel,
        grid=(n // TM, nj),
        in_specs=[
            pl.BlockSpec((TM, D), lambda i, j: (i, 0)),
            _mod_spec(layer, group_of_tile, 2),
            _normg_spec(layer, 2),
            pl.BlockSpec((1, D, MLP_FC), lambda i, j: (layer, 0, j)),
            pl.BlockSpec((1, MLP_FC, D), lambda i, j: (layer, j, 0)),
        ],
        out_specs=pl.BlockSpec((TM, D), lambda i, j: (i, 0)),
        out_shape=jax.ShapeDtypeStruct((n, D), F32),
        scratch_shapes=[pltpu.VMEM((TM, D), BF16), pltpu.VMEM((TM, D), F32)],
        compiler_params=_cparams(("arbitrary", "arbitrary")),
        name="mlp",
    )(x, mod, norm_g, w1, w2)


def _dft_mats(n):
    idx = np.arange(n, dtype=np.int64)
    ang = (2.0 * np.pi / n) * ((idx[:, None] * idx[None, :]) % n).astype(np.float64)
    scale = 1.0 / np.sqrt(n)
    return np.cos(ang) * scale, np.sin(ang) * scale


def _fnet_kernel(x_ref, m_ref, g_ref, cs_ref, cn_ref, sn_ref, w_ref, o_ref, p_scr, q_scr, f_scr,
                 *, seq_len):
    m = m_ref[0, 0]
    g = g_ref[0]
    x = x_ref[...]
    h = (_rms(x, g[0:1]) * (1.0 + m[1:2]) + m[0:1]).astype(BF16)
    cs = cs_ref[...].astype(BF16)
    for gi in range(FFT_GROUPS):
        pq = jnp.dot(h[:, gi * FFT_GW:(gi + 1) * FFT_GW], cs, preferred_element_type=F32)
        p_scr[:, gi * FFT_GW:(gi + 1) * FFT_GW] = pq[:, :FFT_GW].astype(BF16)
        q_scr[:, gi * FFT_GW:(gi + 1) * FFT_GW] = pq[:, FFT_GW:].astype(BF16)
    cn = cn_ref[...].astype(BF16)
    sn = sn_ref[...].astype(BF16)
    for s in range(TM // seq_len):
        rows = slice(s * seq_len, (s + 1) * seq_len)
        f = (jnp.dot(cn, p_scr[rows, :], preferred_element_type=F32)
             - jnp.dot(sn, q_scr[rows, :], preferred_element_type=F32))
        f_scr[rows, :] = f.astype(BF16)
    o = jnp.dot(f_scr[...], w_ref[0].astype(BF16), preferred_element_type=F32)
    o_ref[...] = x + m[2:3] * _rms(o, g[1:2])


def _fnet(x, mod, norm_g, w_out, layer, j, seq_len, group_of_tile):
    n = x.shape[0]
    cg, sg = _dft_mats(FFT_GW)
    cs = jnp.asarray(np.concatenate([cg, sg], axis=1), F32)
    cn_np, sn_np = _dft_mats(seq_len)
    cn = jnp.asarray(cn_np, F32)
    sn = jnp.asarray(sn_np, F32)
    return pl.pallas_call(
        functools.partial(_fnet_kernel, seq_len=seq_len),
        grid=(n // TM,),
        in_specs=[
            pl.BlockSpec((TM, D), lambda i: (i, 0)),
            _mod_spec(layer, group_of_tile, 1),
            _normg_spec(layer, 1),
            pl.BlockSpec((FFT_GW, 2 * FFT_GW), lambda i: (0, 0)),
            pl.BlockSpec((seq_len, seq_len), lambda i: (0, 0)),
            pl.BlockSpec((seq_len, seq_len), lambda i: (0, 0)),
            pl.BlockSpec((1, D, D), lambda i: (j, 0, 0)),
        ],
        out_specs=pl.BlockSpec((TM, D), lambda i: (i, 0)),
        out_shape=jax.ShapeDtypeStruct((n, D), F32),
        scratch_shapes=[pltpu.VMEM((TM, D), BF16), pltpu.VMEM((TM, D), BF16),
                        pltpu.VMEM((TM, D), BF16)],
        compiler_params=_cparams(("arbitrary",)),
        name="fourier_mix",
    )(x, mod, norm_g, cs, cn, sn, w_out)


CONV_CW = 256


def _conv_kernel(x_ref, m_ref, g_ref, wb_ref, wc_ref, wu_ref, cw_ref, wo_ref, o_ref, h_scr, acc_scr,
                 *, seq_len):
    j = pl.program_id(1)
    m = m_ref[0, 0]
    g = g_ref[0]

    @pl.when(j == 0)
    def _():
        h = _rms(x_ref[...], g[0:1]) * (1.0 + m[1:2]) + m[0:1]
        h_scr[...] = h.astype(BF16)
        acc_scr[...] = jnp.zeros_like(acc_scr)

    h = h_scr[...]
    bg = jnp.dot(h, wb_ref[0].astype(BF16), preferred_element_type=F32)
    cg = jnp.dot(h, wc_ref[0].astype(BF16), preferred_element_type=F32)
    u = jnp.dot(h, wu_ref[0].astype(BF16), preferred_element_type=F32)
    z = cg * u
    z_prev, z_next = _shift_rows(z, seq_len)
    cw = cw_ref[0]
    conv = z_prev * cw[0:1] + z * cw[1:2] + z_next * cw[2:3]
    acc_scr[...] += _dot(bg * conv, wo_ref[0])

    @pl.when(j == pl.num_programs(1) - 1)
    def _():
        o_ref[...] = x_ref[...] + m[2:3] * _rms(acc_scr[...], g[1:2])


def _conv(x, mod, norm_g, w_in, w_conv, w_out, layer, j, seq_len, group_of_tile):
    n = x.shape[0]
    nj = D // CONV_CW
    return pl.pallas_call(
        functools.partial(_conv_kernel, seq_len=seq_len),
        grid=(n // TM, nj),
        in_specs=[
            pl.BlockSpec((TM, D), lambda i, c: (i, 0)),
            _mod_spec(layer, group_of_tile, 2),
            _normg_spec(layer, 2),
            pl.BlockSpec((1, D, CONV_CW), lambda i, c: (j, 0, c)),
            pl.BlockSpec((1, D, CONV_CW), lambda i, c: (j, 0, nj + c)),
            pl.BlockSpec((1, D, CONV_CW), lambda i, c: (j, 0, 2 * nj + c)),
            pl.BlockSpec((1, 3, CONV_CW), lambda i, c: (j, 0, c)),
            pl.BlockSpec((1, CONV_CW, D), lambda i, c: (j, c, 0)),
        ],
        out_specs=pl.BlockSpec((TM, D), lambda i, c: (i, 0)),
        out_shape=jax.ShapeDtypeStruct((n, D), F32),
        scratch_shapes=[pltpu.VMEM((TM, D), BF16), pltpu.VMEM((TM, D), F32)],
        compiler_params=_cparams(("arbitrary", "arbitrary")),
        name="short_conv",
    )(x, mod, norm_g, w_in, w_in, w_in, w_conv, w_out)


RWP_CW = 256
RW_OUT_DTYPES = (BF16, BF16, BF16, F32, BF16, BF16, F32, BF16, BF16, F32, F32)
RW_CW = 256


def _rwkv_proj_kernel(x_ref, m_ref, g_ref, mu_ref, wr_ref, wk_ref, wv_ref, w0_ref, wl1_ref, wl2_ref,
                      a0_ref, al1_ref, al2_ref, gl1_ref, gl2_ref, kk_ref, ka_ref, rk_ref,
                      r_o, v_o, kk_o, ld0_o, b0_o, kt0_o, ld1_o, b1_o, kt1_o, g_o, bonus_o,
                      xr_scr, xk_scr, xv_scr, tw_scr, ta_scr, sg_scr, *, seq_len):
    j = pl.program_id(1)

    @pl.when(j == 0)
    def _():
        m = m_ref[0, 0]
        g = g_ref[0]
        mu = mu_ref[0]
        h = _rms(x_ref[...], g[0:1]) * (1.0 + m[1:2]) + m[0:1]
        h_prev, h_next = _shift_rows(h, seq_len)
        dx = 0.5 * (h_prev + h_next) - h
        xr_scr[...] = (h + dx * mu[0:1]).astype(BF16)
        xk_scr[...] = (h + dx * mu[2:3]).astype(BF16)
        xv_scr[...] = (h + dx * mu[3:4]).astype(BF16)
        xw = (h + dx * mu[1:2]).astype(BF16)
        xa = (h + dx * mu[4:5]).astype(BF16)
        xg = (h + dx * mu[5:6]).astype(BF16)
        for d in range(2):
            tw_scr[d] = jnp.tanh(_dot(xw, wl1_ref[0, d])).astype(BF16)
            ta_scr[d] = _dot(xa, al1_ref[0, d]).astype(BF16)
        sg_scr[...] = _sigmoid(_dot(xg, gl1_ref[0])).astype(BF16)

    e = _head_indicator(RWP_CW)
    r = jnp.dot(xr_scr[...], wr_ref[0].astype(BF16), preferred_element_type=F32)
    k = jnp.dot(xk_scr[...], wk_ref[0].astype(BF16), preferred_element_type=F32)
    v = jnp.dot(xv_scr[...], wv_ref[0].astype(BF16), preferred_element_type=F32)
    kk = k * kk_ref[...]
    kk = kk * lax.rsqrt(_dot_hilo_rhs(kk * kk, e) + 1e-12)
    r_o[...] = r.astype(r_o.dtype)
    v_o[...] = v.astype(v_o.dtype)
    kk_o[...] = kk.astype(kk_o.dtype)
    g_o[...] = jnp.dot(sg_scr[...], gl2_ref[0].astype(BF16), preferred_element_type=F32)
    k_a = ka_ref[...]
    kt_sum = None
    for d, (ld_o, b_o, kt_o) in enumerate(((ld0_o, b0_o, kt0_o), (ld1_o, b1_o, kt1_o))):
        zw = w0_ref[0, d:d + 1, :] + jnp.dot(tw_scr[d], wl2_ref[0, d].astype(BF16),
                                             preferred_element_type=F32)
        logw = -_softplus(-zw) - 0.5
        ld_o[...] = -jnp.exp(logw)
        a = _sigmoid(a0_ref[0, d:d + 1, :] + jnp.dot(ta_scr[d], al2_ref[0, d].astype(BF16),
                                                      preferred_element_type=F32))
        kt = k * (1.0 + (a - 1.0) * k_a)
        b_o[...] = (kk * a).astype(b_o.dtype)
        kt_o[...] = kt.astype(kt_o.dtype)
        kt_sum = kt if kt_sum is None else kt_sum + kt
    bonus_o[...] = _dot_hilo_rhs(r * rk_ref[...] * kt_sum, e) * v


def _rwkv_proj(x, mod, norm_g, p, layer, j, seq_len, group_of_tile):
    n = x.shape[0]
    nj = D // RWP_CW
    tile = pl.BlockSpec((TM, RWP_CW), lambda i, c: (i, c))
    wcol = lambda: pl.BlockSpec((1, D, RWP_CW), lambda i, c: (j, 0, c))
    vec = lambda: pl.BlockSpec((1, RWP_CW), lambda i, c: (j, c))
    return pl.pallas_call(
        functools.partial(_rwkv_proj_kernel, seq_len=seq_len),
        grid=(n // TM, nj),
        in_specs=[
            pl.BlockSpec((TM, D), lambda i, c: (i, 0)),
            _mod_spec(layer, group_of_tile, 2),
            _normg_spec(layer, 2),
            pl.BlockSpec((1, 6, D), lambda i, c: (j, 0, 0)),
            wcol(), wcol(), wcol(),
            pl.BlockSpec((1, 2, RWP_CW), lambda i, c: (j, 0, c)),
            pl.BlockSpec((1, 2, D, LORA_W), lambda i, c: (j, 0, 0, 0)),
            pl.BlockSpec((1, 2, LORA_W, RWP_CW), lambda i, c: (j, 0, 0, c)),
            pl.BlockSpec((1, 2, RWP_CW), lambda i, c: (j, 0, c)),
            pl.BlockSpec((1, 2, D, LORA_A), lambda i, c: (j, 0, 0, 0)),
            pl.BlockSpec((1, 2, LORA_A, RWP_CW), lambda i, c: (j, 0, 0, c)),
            pl.BlockSpec((1, D, LORA_G), lambda i, c: (j, 0, 0)),
            pl.BlockSpec((1, LORA_G, RWP_CW), lambda i, c: (j, 0, c)),
            vec(), vec(), vec(),
        ],
        out_specs=[tile] * len(RW_OUT_DTYPES),
        out_shape=[jax.ShapeDtypeStruct((n, D), dt) for dt in RW_OUT_DTYPES],
        scratch_shapes=[pltpu.VMEM((TM, D), BF16), pltpu.VMEM((TM, D), BF16), pltpu.VMEM((TM, D), BF16),
                        pltpu.VMEM((2, TM, LORA_W), BF16), pltpu.VMEM((2, TM, LORA_A), BF16),
                        pltpu.VMEM((TM, LORA_G), BF16)],
        compiler_params=_cparams(("arbitrary", "arbitrary")),
        name="rwkv_proj",
    )(x, mod, norm_g, p["mu"], p["w_r"], p["w_k"], p["w_v"], p["w0"], p["w_l1"], p["w_l2"],
      p["a0"], p["a_l1"], p["a_l2"], p["g_l1"], p["g_l2"], p["k_k"], p["k_a"], p["r_k"])


def _scan_precompute(units):
    c = CHUNK
    c2 = 2 * c
    row = lax.broadcasted_iota(jnp.int32, (c2, c2), 0)
    col = lax.broadcasted_iota(jnp.int32, (c2, c2), 1)
    rc = lax.broadcasted_iota(jnp.int32, (c, c), 0)
    cc = lax.broadcasted_iota(jnp.int32, (c, c), 1)
    head_a = lax.broadcasted_iota(jnp.int32, (c, LANES), 1) < HS
    own_lanes = jnp.concatenate([head_a, jnp.logical_not(head_a)], axis=0)
    eye = jnp.where(row == col, 1.0, 0.0)

    def two_heads(t):
        return jnp.concatenate([jnp.where(head_a, t, 0.0), jnp.where(head_a, 0.0, t)], axis=0)

    def causal(reverse):
        if reverse:
            return (jnp.where(cc >= rc, 1.0, 0.0).astype(BF16), (col % c) > (row % c), (col % c) >= (row % c))
        return (jnp.where(cc <= rc, 1.0, 0.0).astype(BF16), (col % c) < (row % c), (col % c) <= (row % c))

    masks = {rev: causal(rev) for rev in sorted({u[6] for u in units})}

    cums = [_dot_exact_lhs(masks[u[6]][0], u[0]) for u in units]
    st = []
    for (ld, kk, beta, kt, r, v, rev), cum in zip(units, cums):
        tot = cum[0:1] if rev else cum[c - 1:c]
        ginv = jnp.exp(-cum)
        tail = jnp.exp(tot - cum)
        st.append(dict(
            rev=rev, etot=jnp.exp(tot),
            a_t=two_heads(-kk * jnp.exp(cum - ld)), r_t=two_heads(r * jnp.exp(cum)),
            bk=jnp.concatenate([two_heads(beta * ginv), two_heads(kt * ginv)], axis=0),
            bkg=jnp.concatenate([two_heads(beta * tail), two_heads(kt * tail)], axis=0),
            v2=jnp.concatenate([v, v], axis=0), vh=two_heads(v)))
    grams = [_dot_nt(jnp.concatenate([s["a_t"], s["r_t"]], axis=0), s["bk"]) for s in st]
    for s, gram in zip(st, grams):
        _, strict, incl = masks[s["rev"]]
        s["l_ab"] = jnp.where(strict, gram[:c2, :c2], 0.0)
        s["l_ak"] = jnp.where(strict, gram[:c2, c2:], 0.0)
        s["t_rb"] = jnp.where(incl, gram[c2:, :c2], 0.0)
        s["t_rk"] = jnp.where(incl, gram[c2:, c2:], 0.0)
    lvs = [_dot(s["l_ak"], s["v2"]) for s in st]
    minvs = [eye for _ in st]
    b = 1
    while b < c:
        same = (row // (2 * b)) == (col // (2 * b))
        es = []
        for s in st:
            first, second = (col % (2 * b)) < b, (row % (2 * b)) >= b
            if s["rev"]:
                first, second = (row % (2 * b)) < b, (col % (2 * b)) >= b
            es.append(jnp.where(same & first & second, s["l_ab"], 0.0))
        if b == 1:
            minvs = [m + e for m, e in zip(minvs, es)]
        else:
            half = [_dot(m, e) for m, e in zip(minvs, es)]
            minvs = [m + _dot(h, m) for m, h in zip(minvs, half)]
        b *= 2
    mms = [_dot(m, jnp.concatenate([s["a_t"], lv], axis=1)) for m, s, lv in zip(minvs, st, lvs)]
    tts = [_dot(s["t_rb"], mm) for s, mm in zip(st, mms)]
    tkv = [_dot(s["t_rk"], s["v2"]) for s in st]
    ps = [_dot_tn(mm[:, :LANES], s["bkg"][:c2]) for s, mm in zip(st, mms)]
    qs = [_dot_tn(jnp.concatenate([jnp.where(own_lanes, mm[:, LANES:], 0.0), s["vh"]], axis=0), s["bkg"])
          for s, mm in zip(st, mms)]
    return [dict(etot=s["etot"], p=p, q=q, reff=s["r_t"] + tt[:, :LANES], y0=tt[:, LANES:] + kv)
            for s, p, q, tt, kv in zip(st, ps, qs, tts, tkv)]


def _rwkv_scan_kernel(*refs, seq_len, has_init, want_final):
    r_ref, v_ref, kk_ref, ld0_ref, b0_ref, kt0_ref, ld1_ref, b1_ref, kt1_ref = refs[:9]
    pos = 9
    if has_init:
        s0_ref = refs[pos]
        pos += 1
    y_ref = refs[pos]
    pos += 1
    if want_final:
        sf_ref = refs[pos]
        pos += 1
    g_scr, yb_scr = refs[pos:pos + 2]
    nc = seq_len // CHUNK
    ngroups = nc // SCAN_UNROLL
    per_dir = ((ld0_ref, b0_ref, kt0_ref), (ld1_ref, b1_ref, kt1_ref))
    y_dst = (y_ref, yb_scr)
    nchain = 2 * SCAN_PAIRS
    head_a = lax.broadcasted_iota(jnp.int32, (CHUNK, LANES), 1) < HS

    def group(gi, states):
        where, units = [], []
        for u in range(SCAN_UNROLL):
            for pp in range(SCAN_PAIRS):
                lanes = slice(pp * LANES, (pp + 1) * LANES)
                for d in range(2):
                    ld_ref, b_ref, kt_ref = per_dir[d]
                    cidx = gi * SCAN_UNROLL + u
                    if d == 1:
                        cidx = nc - 1 - cidx
                    start = cidx * CHUNK
                    rw = slice(start, start + CHUNK) if isinstance(start, int) else pl.ds(
                        pl.multiple_of(start, CHUNK), CHUNK)
                    where.append((rw, lanes))
                    units.append(tuple(ref[rw, lanes].astype(F32) for ref in
                                       (ld_ref, kk_ref, b_ref, kt_ref, r_ref, v_ref)) + (d == 1,))
        pre = _scan_precompute(units)
        states = list(states)
        for u in range(SCAN_UNROLL):
            cur = pre[u * nchain:(u + 1) * nchain]
            y2 = [_dot_nt(cur[ch]["reff"], states[ch]) + cur[ch]["y0"] for ch in range(nchain)]
            states = [states[ch] * cur[ch]["etot"] + _dot(states[ch], cur[ch]["p"]) + cur[ch]["q"]
                      for ch in range(nchain)]
            for ch in range(nchain):
                rw, lanes = where[u * nchain + ch]
                y_dst[ch % 2][rw, lanes] = jnp.where(head_a, y2[ch][:CHUNK], y2[ch][CHUNK:])
        return states

    init = [s0_ref[0, ch % 2, ch // 2] if has_init else jnp.zeros((LANES, LANES), F32)
            for ch in range(nchain)]
    if ngroups == 1:
        final = group(0, init)
    else:
        for ch in range(nchain):
            g_scr[ch] = init[ch]

        def body(gi, carry):
            new = group(gi, [g_scr[ch] for ch in range(nchain)])
            for ch in range(nchain):
                g_scr[ch] = new[ch]
            return carry

        lax.fori_loop(0, ngroups, body, 0)
        final = [g_scr[ch] for ch in range(nchain)]
    y_ref[...] += yb_scr[...]
    if want_final:
        for ch in range(nchain):
            sf_ref[0, ch % 2, ch // 2] = final[ch]


def _rwkv_scan(proj, s_init, n_seq, seq_len, want_final):
    r, v, kk, ld0, b0, kt0, ld1, b1, kt1 = proj[:9]
    n = r.shape[0]
    npair = D // LANES
    width = SCAN_PAIRS * LANES
    blk = pl.BlockSpec((seq_len, width), lambda b, p: (b, p))
    st_spec = pl.BlockSpec((1, 2, SCAN_PAIRS, LANES, LANES), lambda b, p: (b, 0, p, 0, 0))
    in_specs = [blk] * 9
    args = [r, v, kk, ld0, b0, kt0, ld1, b1, kt1]
    has_init = s_init is not None
    if has_init:
        in_specs.append(st_spec)
        args.append(s_init)
    out_specs = [blk]
    out_shape = [jax.ShapeDtypeStruct((n, D), F32)]
    if want_final:
        out_specs.append(st_spec)
        out_shape.append(jax.ShapeDtypeStruct((n_seq, 2, npair, LANES, LANES), F32))
    res = pl.pallas_call(
        functools.partial(_rwkv_scan_kernel, seq_len=seq_len, has_init=has_init, want_final=want_final),
        grid=(n_seq, npair // SCAN_PAIRS),
        in_specs=in_specs,
        out_specs=out_specs,
        out_shape=out_shape,
        scratch_shapes=[pltpu.VMEM((2 * SCAN_PAIRS, LANES, LANES), F32), pltpu.VMEM((seq_len, width), F32)],
        compiler_params=_cparams(("arbitrary", "arbitrary")),
        name="rwkv_scan",
    )(*args)
    return res


def _rwkv_out_kernel(x_ref, m_ref, g_ref, y_ref, bonus_ref, gate_ref, lng_ref, lnb_ref, wo_ref, o_ref,
                     acc_scr):
    j = pl.program_id(1)

    @pl.when(j == 0)
    def _():
        acc_scr[...] = jnp.zeros_like(acc_scr)

    e = _head_indicator(RW_CW)
    y = y_ref[...]
    mean = _dot_hilo_rhs(y, e) * (1.0 / HS)
    yc = y - mean
    var = _dot_hilo_rhs(yc * yc, e) * (1.0 / HS)
    yn = yc * lax.rsqrt(var + GN_EPS) * lng_ref[...] + lnb_ref[...]
    yn = (yn + bonus_ref[...]) * gate_ref[...]
    acc_scr[...] += _dot(yn, wo_ref[0])

    @pl.when(j == pl.num_programs(1) - 1)
    def _():
        m = m_ref[0, 0]
        g = g_ref[0]
        o_ref[...] = x_ref[...] + m[2:3] * _rms(acc_scr[...], g[1:2])


def _rwkv_out(x, mod, norm_g, y, bonus, gate, p, layer, j, group_of_tile):
    n = x.shape[0]
    nj = D // RW_CW
    tile = pl.BlockSpec((TM, RW_CW), lambda i, c: (i, c))
    vec = pl.BlockSpec((1, RW_CW), lambda i, c: (j, c))
    return pl.pallas_call(
        _rwkv_out_kernel,
        grid=(n // TM, nj),
        in_specs=[
            pl.BlockSpec((TM, D), lambda i, c: (i, 0)),
            _mod_spec(layer, group_of_tile, 2),
            _normg_spec(layer, 2),
            tile, tile, tile, vec, vec,
            pl.BlockSpec((1, RW_CW, D), lambda i, c: (j, c, 0)),
        ],
        out_specs=pl.BlockSpec((TM, D), lambda i, c: (i, 0)),
        out_shape=jax.ShapeDtypeStruct((n, D), F32),
        scratch_shapes=[pltpu.VMEM((TM, D), F32)],
        compiler_params=_cparams(("arbitrary", "arbitrary")),
        name="rwkv_out",
    )(x, mod, norm_g, y, bonus, gate, p["ln_g"], p["ln_b"], p["w_o"])


def _blockdiag_states(s):
    b = s.shape[0]
    sp = s.reshape(b, 2, NH // 2, 2, HS, HS)
    z = jnp.zeros((b, 2, NH // 2, HS, HS), s.dtype)
    top = jnp.concatenate([sp[:, :, :, 0], z], axis=-1)
    bot = jnp.concatenate([z, sp[:, :, :, 1]], axis=-1)
    return jnp.concatenate([top, bot], axis=-2)


def _unblock_states(g):
    a = g[:, :, :, :HS, :HS]
    b = g[:, :, :, HS:, HS:]
    return jnp.stack([a, b], axis=3).reshape(g.shape[0], 2, NH, HS, HS)


def _rope_tables(n):
    rows = n // GRID_W
    row = np.repeat(np.arange(rows), GRID_W)
    col = np.tile(np.arange(GRID_W), rows)
    pos = np.stack([row, col], axis=-1).astype(np.float64)
    quarter = D_ROPE // 4
    inv = ROPE_BASE ** (-np.arange(quarter, dtype=np.float64) / quarter)
    ang = pos[:, :, None] * inv
    cos = np.cos(ang)
    sin = np.sin(ang)
    cos_t = np.concatenate([cos, cos], axis=-1).reshape(n, D_ROPE)
    sin_t = np.concatenate([-sin, sin], axis=-1).reshape(n, D_ROPE)
    return cos_t.astype(np.float32), sin_t.astype(np.float32)


def _rope_swap_perm():
    quarter = D_ROPE // 4
    base = np.arange(D_ROPE)
    return np.where((base % (2 * quarter)) < quarter, base + quarter, base - quarter)


def _mla_proj_kernel(*refs, positional):
    (x_ref, m_ref, g_ref, wdq_ref, gq_ref, wqn_ref, wqr_ref, wqs_ref, wdkv_ref, wkr_ref, wks_ref,
     gkv_ref) = refs[:12]
    pos = 12
    if positional:
        cosq_ref, sinq_ref, cosk_ref, sink_ref = refs[pos:pos + 4]
        pos += 4
    qn_o, qr_o, ckv_o, kr_o = refs[pos:pos + 4]
    m = m_ref[0, 0]
    g = g_ref[0]
    h = (_rms(x_ref[...], g[0:1]) * (1.0 + m[1:2]) + m[0:1]).astype(BF16)
    ql = jnp.dot(h, wdq_ref[0].astype(BF16), preferred_element_type=F32)
    ql = (ql * lax.rsqrt(jnp.mean(ql * ql, axis=-1, keepdims=True) + EPS) * gq_ref[...]).astype(BF16)
    qn_o[...] = jnp.dot(ql, wqn_ref[...].astype(BF16), preferred_element_type=F32)
    qr = jnp.dot(ql, wqr_ref[...].astype(BF16), preferred_element_type=F32)
    ckv = jnp.dot(h, wdkv_ref[...].astype(BF16), preferred_element_type=F32)
    ckv_o[...] = ckv * lax.rsqrt(jnp.mean(ckv * ckv, axis=-1, keepdims=True) + EPS) * gkv_ref[...]
    kr = jnp.dot(h, wkr_ref[...].astype(BF16), preferred_element_type=F32)
    if positional:
        qs = jnp.dot(ql, wqs_ref[...].astype(BF16), preferred_element_type=F32)
        ks = jnp.dot(h, wks_ref[...].astype(BF16), preferred_element_type=F32)
        qr = qr * cosq_ref[...] + qs * sinq_ref[...]
        kr = kr * cosk_ref[...] + ks * sink_ref[...]
    qr_o[...] = qr
    kr_o[...] = kr


def _mla_proj(x, mod, norm_g, p, layer, j, positional, group_of_tile):
    n = x.shape[0]
    full = lambda shape: pl.BlockSpec(shape, lambda i: (0,) * len(shape))
    in_specs = [
        pl.BlockSpec((TM, D), lambda i: (i, 0)),
        _mod_spec(layer, group_of_tile, 1),
        _normg_spec(layer, 1),
        pl.BlockSpec((1, D, Q_RANK), lambda i: (j, 0, 0)),
        pl.BlockSpec((1, Q_RANK), lambda i: (j, 0)),
        full((Q_RANK, MLA_H * D_NOPE)), full((Q_RANK, MLA_H * D_ROPE)), full((Q_RANK, MLA_H * D_ROPE)),
        full((D, KV_RANK)), full((D, D_ROPE)), full((D, D_ROPE)),
        pl.BlockSpec((1, KV_RANK), lambda i: (j, 0)),
    ]
    args = [x, mod, norm_g, p["w_dq"], p["g_q"], p["w_uq_nope"], p["w_uq_rope"], p["w_uq_rope_sw"],
            p["w_dkv_c"], p["w_dkv_r"], p["w_dkv_r_sw"], p["g_kv"]]
    if positional:
        cos_t, sin_t = _rope_tables(TM)
        in_specs += [full((TM, MLA_H * D_ROPE)), full((TM, MLA_H * D_ROPE)),
                     full((TM, D_ROPE)), full((TM, D_ROPE))]
        args += [jnp.asarray(np.tile(cos_t, (1, MLA_H))), jnp.asarray(np.tile(sin_t, (1, MLA_H))),
                 jnp.asarray(cos_t), jnp.asarray(sin_t)]
    widths = (MLA_H * D_NOPE, MLA_H * D_ROPE, KV_RANK, D_ROPE)
    return pl.pallas_call(
        functools.partial(_mla_proj_kernel, positional=positional),
        grid=(n // TM,),
        in_specs=in_specs,
        out_specs=[pl.BlockSpec((TM, w), lambda i: (i, 0)) for w in widths],
        out_shape=[jax.ShapeDtypeStruct((n, w), F32) for w in widths],
        compiler_params=_cparams(("arbitrary",)),
        name="mla_proj",
    )(*args)


def _mla_attn_kernel(x_ref, m_ref, g_ref, qn_ref, qr_ref, ckv_ref, kr_ref, wuk_ref, wuv_ref, wo_ref,
                     o_ref, kn_scr, vv_scr, oh_scr, *, nb, tq, k_len):
    qi = pl.program_id(1)

    @pl.when(qi == 0)
    def _():
        ckv = ckv_ref[...].astype(BF16)
        kn_scr[...] = jnp.dot(ckv, wuk_ref[0].astype(BF16), preferred_element_type=F32).astype(BF16)
        vv_scr[...] = jnp.dot(ckv, wuv_ref[0].astype(BF16), preferred_element_type=F32).astype(BF16)

    units = [(b, hd) for b in range(nb) for hd in range(MLA_H)]
    scores = []
    for b, hd in units:
        qrows = slice(b * tq, (b + 1) * tq)
        krows = slice(b * k_len, (b + 1) * k_len)
        s_nope = _dot_nt(qn_ref[qrows, hd * D_NOPE:(hd + 1) * D_NOPE],
                         kn_scr[krows, hd * D_NOPE:(hd + 1) * D_NOPE])
        s_rope = _dot_nt(qr_ref[qrows, hd * D_ROPE:(hd + 1) * D_ROPE], kr_ref[krows, :])
        scores.append((s_nope + s_rope) * MLA_SCALE)
    probs = []
    for s in scores:
        pexp = jnp.exp(s - jnp.max(s, axis=-1, keepdims=True))
        probs.append((pexp / jnp.sum(pexp, axis=-1, keepdims=True)).astype(BF16))
    for (b, hd), pr in zip(units, probs):
        oh_scr[b * tq:(b + 1) * tq, hd * D_V:(hd + 1) * D_V] = jnp.dot(
            pr, vv_scr[b * k_len:(b + 1) * k_len, hd * D_V:(hd + 1) * D_V],
            preferred_element_type=F32).astype(BF16)
    o = jnp.dot(oh_scr[...], wo_ref[0].astype(BF16), preferred_element_type=F32)
    m = m_ref[0, 0]
    g = g_ref[0]
    o_ref[...] = x_ref[...] + m[2:3] * _rms(o, g[1:2])


def _mla_attn(x, mod, norm_g, qn, qr, ckv_all, kr_all, p, layer, j, n_seq, q_len, k_len, nb, tq,
              group_of_step):
    n = x.shape[0]
    nq = q_len // tq
    assert nb == 1 or nq == 1
    return pl.pallas_call(
        functools.partial(_mla_attn_kernel, nb=nb, tq=tq, k_len=k_len),
        grid=(n_seq // nb, nq),
        in_specs=[
            pl.BlockSpec((nb * tq, D), lambda s, q: (s * nq + q, 0)),
            pl.BlockSpec((1, 1, N_MOD, D), lambda s, q: (layer, group_of_step(s), 0, 0)),
            pl.BlockSpec((1, 4, D), lambda s, q: (layer, 0, 0)),
            pl.BlockSpec((nb * tq, MLA_H * D_NOPE), lambda s, q: (s * nq + q, 0)),
            pl.BlockSpec((nb * tq, MLA_H * D_ROPE), lambda s, q: (s * nq + q, 0)),
            pl.BlockSpec((nb * k_len, KV_RANK), lambda s, q: (s, 0)),
            pl.BlockSpec((nb * k_len, D_ROPE), lambda s, q: (s, 0)),
            pl.BlockSpec((1, KV_RANK, MLA_H * D_NOPE), lambda s, q: (j, 0, 0)),
            pl.BlockSpec((1, KV_RANK, MLA_H * D_V), lambda s, q: (j, 0, 0)),
            pl.BlockSpec((1, MLA_H * D_V, D), lambda s, q: (j, 0, 0)),
        ],
        out_specs=pl.BlockSpec((nb * tq, D), lambda s, q: (s * nq + q, 0)),
        out_shape=jax.ShapeDtypeStruct((n, D), F32),
        scratch_shapes=[pltpu.VMEM((nb * k_len, MLA_H * D_NOPE), BF16),
                        pltpu.VMEM((nb * k_len, MLA_H * D_V), BF16),
                        pltpu.VMEM((nb * tq, MLA_H * D_V), BF16)],
        compiler_params=_cparams(("arbitrary", "arbitrary")),
        name="mla_attn",
    )(x, mod, norm_g, qn, qr, ckv_all, kr_all, p["w_uk"], p["w_uv"], p["w_o"])


def kernel(x_prompt, x_sample, state_rwkv, cache_mla_ckv, cache_mla_krope, c, c_ctx, mod_w, mod_b, norm_g,
           mlp_w1, mlp_w2, fft_w_out, conv_w_in, conv_w, conv_w_out, rwkv_mu, rwkv_w_r, rwkv_w_k, rwkv_w_v,
           rwkv_w_o, rwkv_w0, rwkv_w_l1, rwkv_w_l2, rwkv_a0, rwkv_a_l1, rwkv_a_l2, rwkv_g_l1, rwkv_g_l2,
           rwkv_k_k, rwkv_k_a, rwkv_r_k, rwkv_ln_g, rwkv_ln_b, mla_w_dq, mla_g_q, mla_w_uq, mla_w_dkv,
           mla_g_kv, mla_w_uk, mla_w_uv, mla_w_o):
    batch, seq, _ = x_prompt.shape
    dec_batch, dec_seq, _ = x_sample.shape
    past_len = cache_mla_ckv.shape[2]
    assert (batch * seq) % TM == 0 and TM % seq == 0 and dec_seq == TM and seq % CHUNK == 0

    xp = x_prompt.reshape(batch * seq, D)
    xs = x_sample.reshape(dec_batch * dec_seq, D)
    cs = jnp.concatenate([c_ctx[None, :], c, jnp.zeros((8 - 1 - dec_batch, D), F32)], axis=0)
    mod = _modulation(cs, mod_w, mod_b)

    grp_p = lambda i: 0
    grp_s = lambda i: 1 + i
    new_rwkv, new_ckv, new_krope = [], [], []
    streams = ((True, seq, batch, grp_p), (False, dec_seq, dec_batch, grp_s))

    for i in range(DEPTH):
        kind, j = i % 4, i // 4
        outs = []
        for is_prompt, slen, nseq, grp in streams:
            x = xp if is_prompt else xs
            if kind == 0:
                x = _fnet(x, mod, norm_g, fft_w_out, i, j, slen, grp)
            elif kind == 1:
                x = _conv(x, mod, norm_g, conv_w_in, conv_w, conv_w_out, i, j, slen, grp)
            elif kind == 2:
                p = dict(mu=rwkv_mu, w_r=rwkv_w_r, w_k=rwkv_w_k, w_v=rwkv_w_v, w_o=rwkv_w_o, w0=rwkv_w0,
                         w_l1=rwkv_w_l1, w_l2=rwkv_w_l2, a0=rwkv_a0, a_l1=rwkv_a_l1, a_l2=rwkv_a_l2,
                         g_l1=rwkv_g_l1, g_l2=rwkv_g_l2, k_k=rwkv_k_k, k_a=rwkv_k_a,
                         r_k=rwkv_r_k.reshape(-1, D), ln_g=rwkv_ln_g, ln_b=rwkv_ln_b)
                proj = _rwkv_proj(x, mod, norm_g, p, i, j, slen, grp)
                if is_prompt:
                    y, s_fin = _rwkv_scan(proj, None, nseq, slen, True)
                    new_rwkv.append(_unblock_states(s_fin))
                else:
                    (y,) = _rwkv_scan(proj, _blockdiag_states(state_rwkv[:, j]), nseq, slen, False)
                x = _rwkv_out(x, mod, norm_g, y, proj[10], proj[9], p, i, j, grp)
            else:
                perm = _rope_swap_perm()
                w_uq = mla_w_uq[j].reshape(Q_RANK, MLA_H, D_NOPE + D_ROPE)
                w_uq_rope = w_uq[:, :, D_NOPE:]
                w_dkv_r = mla_w_dkv[j][:, KV_RANK:]
                p = dict(w_dq=mla_w_dq, g_q=mla_g_q, g_kv=mla_g_kv, w_uk=mla_w_uk, w_uv=mla_w_uv, w_o=mla_w_o,
                         w_uq_nope=w_uq[:, :, :D_NOPE].reshape(Q_RANK, MLA_H * D_NOPE),
                         w_uq_rope=w_uq_rope.reshape(Q_RANK, MLA_H * D_ROPE),
                         w_uq_rope_sw=w_uq_rope[:, :, perm].reshape(Q_RANK, MLA_H * D_ROPE),
                         w_dkv_c=mla_w_dkv[j][:, :KV_RANK], w_dkv_r=w_dkv_r, w_dkv_r_sw=w_dkv_r[:, perm])
                qn, qr, ckv, kr = _mla_proj(x, mod, norm_g, p, i, j, not is_prompt, grp)
                if is_prompt:
                    new_ckv.append(ckv.reshape(batch, seq, KV_RANK))
                    new_krope.append(kr.reshape(batch, seq, D_ROPE))
                    x = _mla_attn(x, mod, norm_g, qn, qr, ckv, kr, p, i, j, nseq, slen, slen, TM // slen, slen,
                                  lambda s: 0)
                else:
                    klen = past_len + slen
                    ckv_all = jnp.concatenate([cache_mla_ckv[:, j], ckv.reshape(nseq, slen, KV_RANK)], axis=1)
                    kr_all = jnp.concatenate([cache_mla_krope[:, j], kr.reshape(nseq, slen, D_ROPE)], axis=1)
                    x = _mla_attn(x, mod, norm_g, qn, qr, ckv_all.reshape(nseq * klen, KV_RANK),
                                  kr_all.reshape(nseq * klen, D_ROPE), p, i, j, nseq, slen, klen, 1, 256,
                                  lambda s: 1 + s)
            x = _mlp(x, mod, norm_g, mlp_w1, mlp_w2, i, grp)
            outs.append(x)
        xp, xs = outs

    return (xp.reshape(batch, seq, D), xs.reshape(dec_batch, dec_seq, D),
            jnp.stack(new_rwkv, axis=1), jnp.stack(new_ckv, axis=1), jnp.stack(new_krope, axis=1))
```

```python
import functools

import numpy as np
import jax
import jax.numpy as jnp
from jax import lax
from jax.experimental import pallas as pl
from jax.experimental.pallas import tpu as pltpu

D = 1024
DEPTH = 4
N_MOD = 6
D_FF = 4 * D
EPS = 1e-6
GRID_W = 64
FFT_GROUPS = 8
FFT_GW = D // FFT_GROUPS
HS = 64
NH = D // HS
LORA_W = 64
LORA_A = 64
LORA_G = 128
GN_EPS = 64e-5
MLA_H = 8
D_NOPE = 128
D_ROPE = 64
D_V = 128
KV_RANK = 256
Q_RANK = 384
ROPE_BASE = 10000.0
MLA_SCALE = (D_NOPE + D_ROPE) ** -0.5

F32 = jnp.float32
BF16 = jnp.bfloat16

TM = 1024
LANES = 128
CHUNK = 64
SCAN_UNROLL = 4
SCAN_PAIRS = 2
VMEM_LIMIT = 56 * 1024 * 1024


def _cparams(sem):
    return pltpu.CompilerParams(dimension_semantics=sem, vmem_limit_bytes=VMEM_LIMIT)


def _dot(a, b):
    return jnp.dot(a.astype(BF16), b.astype(BF16), preferred_element_type=F32)


def _dot_nt(a, b):
    return lax.dot_general(a.astype(BF16), b.astype(BF16), (((1,), (1,)), ((), ())),
                           preferred_element_type=F32)


def _dot_tn(a, b):
    return lax.dot_general(a.astype(BF16), b.astype(BF16), (((0,), (0,)), ((), ())),
                           preferred_element_type=F32)


def _block_diag2(a, b):
    za = jnp.zeros((a.shape[0], b.shape[1]), a.dtype)
    zb = jnp.zeros((b.shape[0], a.shape[1]), a.dtype)
    return jnp.concatenate([jnp.concatenate([a, za], axis=1), jnp.concatenate([zb, b], axis=1)], axis=0)


def _split3(x):
    hi = x.astype(BF16)
    r1 = x - hi.astype(F32)
    mid = r1.astype(BF16)
    lo = (r1 - mid.astype(F32)).astype(BF16)
    return hi, mid, lo


def _dot_exact_lhs(e, x):
    hi, mid, lo = _split3(x)
    return (jnp.dot(e, hi, preferred_element_type=F32)
            + jnp.dot(e, mid, preferred_element_type=F32)
            + jnp.dot(e, lo, preferred_element_type=F32))


def _dot_hilo_rhs(x, e):
    hi = x.astype(BF16)
    lo = (x - hi.astype(F32)).astype(BF16)
    return jnp.dot(hi, e, preferred_element_type=F32) + jnp.dot(lo, e, preferred_element_type=F32)


def _rms(x, g):
    return x * lax.rsqrt(jnp.mean(x * x, axis=-1, keepdims=True) + EPS) * g


def _sigmoid(x):
    return 1.0 / (1.0 + jnp.exp(-x))


def _softplus(x):
    return jnp.maximum(x, 0.0) + jnp.log(1.0 + jnp.exp(-jnp.abs(x)))


def _head_indicator(n):
    r = lax.broadcasted_iota(jnp.int32, (n, n), 0) // HS
    c = lax.broadcasted_iota(jnp.int32, (n, n), 1) // HS
    return jnp.where(r == c, 1.0, 0.0).astype(BF16)


def _shift_rows(z, seq_len):
    n = z.shape[0]
    pos = lax.broadcasted_iota(jnp.int32, z.shape, 0) % seq_len
    prev = jnp.where(pos == 0, 0.0, pltpu.roll(z, 1, 0))
    nxt = jnp.where(pos == seq_len - 1, 0.0, pltpu.roll(z, n - 1, 0))
    return prev, nxt


MOD_TN = 1536


def _mod_kernel(cs_ref, w_ref, b_ref, o_ref):
    cs = cs_ref[...]
    s = cs * _sigmoid(cs)
    o_ref[0] = _dot(s, w_ref[0]) + b_ref[0]


def _modulation(cs, mod_w, mod_b):
    nj = (N_MOD * D) // MOD_TN
    out = pl.pallas_call(
        _mod_kernel,
        grid=(DEPTH, nj),
        in_specs=[
            pl.BlockSpec((8, D), lambda l, j: (0, 0)),
            pl.BlockSpec((1, D, MOD_TN), lambda l, j: (l, 0, j)),
            pl.BlockSpec((1, 1, MOD_TN), lambda l, j: (l, 0, j)),
        ],
        out_specs=pl.BlockSpec((1, 8, MOD_TN), lambda l, j: (l, 0, j)),
        out_shape=jax.ShapeDtypeStruct((DEPTH, 8, N_MOD * D), F32),
        compiler_params=_cparams(("arbitrary", "arbitrary")),
        name="modulation",
    )(cs, mod_w, mod_b.reshape(DEPTH, 1, N_MOD * D))
    return out.reshape(DEPTH, 8, N_MOD, D)


def _mod_spec(layer, group_of_tile, ngrid):
    if ngrid == 1:
        return pl.BlockSpec((1, 1, N_MOD, D), lambda i: (layer, group_of_tile(i), 0, 0))
    return pl.BlockSpec((1, 1, N_MOD, D), lambda i, j: (layer, group_of_tile(i), 0, 0))


def _normg_spec(layer, ngrid):
    if ngrid == 1:
        return pl.BlockSpec((1, 4, D), lambda i: (layer, 0, 0))
    return pl.BlockSpec((1, 4, D), lambda i, j: (layer, 0, 0))


MLP_FC = 1024


def _mlp_kernel(x_ref, m_ref, g_ref, w1_ref, w2_ref, o_ref, h_scr, acc_scr):
    j = pl.program_id(1)
    m = m_ref[0, 0]
    g = g_ref[0]

    @pl.when(j == 0)
    def _():
        h = _rms(x_ref[...], g[2:3]) * (1.0 + m[4:5]) + m[3:4]
        h_scr[...] = h.astype(BF16)
        acc_scr[...] = jnp.zeros_like(acc_scr)

    a = jnp.dot(h_scr[...], w1_ref[0].astype(BF16), preferred_element_type=F32)
    a = jnp.maximum(a, 0.0)
    a = a * a
    acc_scr[...] += _dot(a, w2_ref[0])

    @pl.when(j == pl.num_programs(1) - 1)
    def _():
        o_ref[...] = x_ref[...] + m[5:6] * _rms(acc_scr[...], g[3:4])


def _mlp(x, mod, norm_g, w1, w2, layer, group_of_tile):
    n = x.shape[0]
    nj = D_FF // MLP_FC
    return pl.pallas_call(
        _mlp_kernel,
        grid=(n // TM, nj),
        in_specs=[
            pl.BlockSpec((TM, D), lambda i, j: (i, 0)),
            _mod_spec(layer, group_of_tile, 2),
            _normg_spec(layer, 2),
            pl.BlockSpec((1, D, MLP_FC), lambda i, j: (layer, 0, j)),
            pl.BlockSpec((1, MLP_FC, D), lambda i, j: (layer, j, 0)),
        ],
        out_specs=pl.BlockSpec((TM, D), lambda i, j: (i, 0)),
        out_shape=jax.ShapeDtypeStruct((n, D), F32),
        scratch_shapes=[pltpu.VMEM((TM, D), BF16), pltpu.VMEM((TM, D), F32)],
        compiler_params=_cparams(("arbitrary", "arbitrary")),
        name="mlp",
    )(x, mod, norm_g, w1, w2)


def _dft_mats(n):
    idx = np.arange(n, dtype=np.int64)
    ang = (2.0 * np.pi / n) * ((idx[:, None] * idx[None, :]) % n).astype(np.float64)
    scale = 1.0 / np.sqrt(n)
    return np.cos(ang) * scale, np.sin(ang) * scale


def _fnet_kernel(x_ref, m_ref, g_ref, cs_ref, cn_ref, sn_ref, w_ref, o_ref, p_scr, q_scr, f_scr,
                 *, seq_len):
    m = m_ref[0, 0]
    g = g_ref[0]
    x = x_ref[...]
    h = (_rms(x, g[0:1]) * (1.0 + m[1:2]) + m[0:1]).astype(BF16)
    cs = cs_ref[...].astype(BF16)
    for gi in range(FFT_GROUPS):
        pq = jnp.dot(h[:, gi * FFT_GW:(gi + 1) * FFT_GW], cs, preferred_element_type=F32)
        p_scr[:, gi * FFT_GW:(gi + 1) * FFT_GW] = pq[:, :FFT_GW].astype(BF16)
        q_scr[:, gi * FFT_GW:(gi + 1) * FFT_GW] = pq[:, FFT_GW:].astype(BF16)
    cn = cn_ref[...].astype(BF16)
    sn = sn_ref[...].astype(BF16)
    for s in range(TM // seq_len):
        rows = slice(s * seq_len, (s + 1) * seq_len)
        f = (jnp.dot(cn, p_scr[rows, :], preferred_element_type=F32)
             - jnp.dot(sn, q_scr[rows, :], preferred_element_type=F32))
        f_scr[rows, :] = f.astype(BF16)
    o = jnp.dot(f_scr[...], w_ref[0].astype(BF16), preferred_element_type=F32)
    o_ref[...] = x + m[2:3] * _rms(o, g[1:2])


def _fnet(x, mod, norm_g, w_out, layer, j, seq_len, group_of_tile):
    n = x.shape[0]
    cg, sg = _dft_mats(FFT_GW)
    cs = jnp.asarray(np.concatenate([cg, sg], axis=1), F32)
    cn_np, sn_np = _dft_mats(seq_len)
    cn = jnp.asarray(cn_np, F32)
    sn = jnp.asarray(sn_np, F32)
    return pl.pallas_call(
        functools.partial(_fnet_kernel, seq_len=seq_len),
        grid=(n // TM,),
        in_specs=[
            pl.BlockSpec((TM, D), lambda i: (i, 0)),
            _mod_spec(layer, group_of_tile, 1),
            _normg_spec(layer, 1),
            pl.BlockSpec((FFT_GW, 2 * FFT_GW), lambda i: (0, 0)),
            pl.BlockSpec((seq_len, seq_len), lambda i: (0, 0)),
            pl.BlockSpec((seq_len, seq_len), lambda i: (0, 0)),
            pl.BlockSpec((1, D, D), lambda i: (j, 0, 0)),
        ],
        out_specs=pl.BlockSpec((TM, D), lambda i: (i, 0)),
        out_shape=jax.ShapeDtypeStruct((n, D), F32),
        scratch_shapes=[pltpu.VMEM((TM, D), BF16), pltpu.VMEM((TM, D), BF16),
                        pltpu.VMEM((TM, D), BF16)],
        compiler_params=_cparams(("arbitrary",)),
        name="fourier_mix",
    )(x, mod, norm_g, cs, cn, sn, w_out)


CONV_CW = 512


def _conv_kernel(x_ref, m_ref, g_ref, wb_ref, wc_ref, wu_ref, cw_ref, wo_ref, o_ref, h_scr, acc_scr,
                 *, seq_len):
    j = pl.program_id(1)
    m = m_ref[0, 0]
    g = g_ref[0]

    @pl.when(j == 0)
    def _():
        h = _rms(x_ref[...], g[0:1]) * (1.0 + m[1:2]) + m[0:1]
        h_scr[...] = h.astype(BF16)
        acc_scr[...] = jnp.zeros_like(acc_scr)

    h = h_scr[...]
    bg = jnp.dot(h, wb_ref[0].astype(BF16), preferred_element_type=F32)
    cg = jnp.dot(h, wc_ref[0].astype(BF16), preferred_element_type=F32)
    u = jnp.dot(h, wu_ref[0].astype(BF16), preferred_element_type=F32)
    z = cg * u
    z_prev, z_next = _shift_rows(z, seq_len)
    cw = cw_ref[0]
    conv = z_prev * cw[0:1] + z * cw[1:2] + z_next * cw[2:3]
    acc_scr[...] += _dot(bg * conv, wo_ref[0])

    @pl.when(j == pl.num_programs(1) - 1)
    def _():
        o_ref[...] = x_ref[...] + m[2:3] * _rms(acc_scr[...], g[1:2])


def _conv(x, mod, norm_g, w_in, w_conv, w_out, layer, j, seq_len, group_of_tile):
    n = x.shape[0]
    nj = D // CONV_CW
    return pl.pallas_call(
        functools.partial(_conv_kernel, seq_len=seq_len),
        grid=(n // TM, nj),
        in_specs=[
            pl.BlockSpec((TM, D), lambda i, c: (i, 0)),
            _mod_spec(layer, group_of_tile, 2),
            _normg_spec(layer, 2),
            pl.BlockSpec((1, D, CONV_CW), lambda i, c: (j, 0, c)),
            pl.BlockSpec((1, D, CONV_CW), lambda i, c: (j, 0, nj + c)),
            pl.BlockSpec((1, D, CONV_CW), lambda i, c: (j, 0, 2 * nj + c)),
            pl.BlockSpec((1, 3, CONV_CW), lambda i, c: (j, 0, c)),
            pl.BlockSpec((1, CONV_CW, D), lambda i, c: (j, c, 0)),
        ],
        out_specs=pl.BlockSpec((TM, D), lambda i, c: (i, 0)),
        out_shape=jax.ShapeDtypeStruct((n, D), F32),
        scratch_shapes=[pltpu.VMEM((TM, D), BF16), pltpu.VMEM((TM, D), F32)],
        compiler_params=_cparams(("arbitrary", "arbitrary")),
        name="short_conv",
    )(x, mod, norm_g, w_in, w_in, w_in, w_conv, w_out)


RWP_CW = 256
RW_OUT_DTYPES = (BF16, BF16, BF16, F32, BF16, BF16, F32, BF16, BF16, F32, F32)
RW_CW = 512


def _rwkv_proj_kernel(x_ref, m_ref, g_ref, mu_ref, wr_ref, wk_ref, wv_ref, w0_ref, wl1_ref, wl2_ref,
                      a0_ref, al1_ref, al2_ref, gl1_ref, gl2_ref, kk_ref, ka_ref, rk_ref,
                      r_o, v_o, kk_o, ld0_o, b0_o, kt0_o, ld1_o, b1_o, kt1_o, g_o, bonus_o,
                      xr_scr, xk_scr, xv_scr, tw_scr, ta_scr, sg_scr, *, seq_len):
    j = pl.program_id(1)

    @pl.when(j == 0)
    def _():
        m = m_ref[0, 0]
        g = g_ref[0]
        mu = mu_ref[0]
        h = _rms(x_ref[...], g[0:1]) * (1.0 + m[1:2]) + m[0:1]
        h_prev, h_next = _shift_rows(h, seq_len)
        dx = 0.5 * (h_prev + h_next) - h
        xr_scr[...] = (h + dx * mu[0:1]).astype(BF16)
        xk_scr[...] = (h + dx * mu[2:3]).astype(BF16)
        xv_scr[...] = (h + dx * mu[3:4]).astype(BF16)
        xw = (h + dx * mu[1:2]).astype(BF16)
        xa = (h + dx * mu[4:5]).astype(BF16)
        xg = (h + dx * mu[5:6]).astype(BF16)
        for d in range(2):
            tw_scr[d] = jnp.tanh(_dot(xw, wl1_ref[0, d])).astype(BF16)
            ta_scr[d] = _dot(xa, al1_ref[0, d]).astype(BF16)
        sg_scr[...] = _sigmoid(_dot(xg, gl1_ref[0])).astype(BF16)

    e = _head_indicator(RWP_CW)
    r = jnp.dot(xr_scr[...], wr_ref[0].astype(BF16), preferred_element_type=F32)
    k = jnp.dot(xk_scr[...], wk_ref[0].astype(BF16), preferred_element_type=F32)
    v = jnp.dot(xv_scr[...], wv_ref[0].astype(BF16), preferred_element_type=F32)
    kk = k * kk_ref[...]
    kk = kk * lax.rsqrt(_dot(kk * kk, e) + 1e-12)
    r_o[...] = r.astype(r_o.dtype)
    v_o[...] = v.astype(v_o.dtype)
    kk_o[...] = kk.astype(kk_o.dtype)
    g_o[...] = jnp.dot(sg_scr[...], gl2_ref[0].astype(BF16), preferred_element_type=F32)
    k_a = ka_ref[...]
    kt_sum = None
    for d, (ld_o, b_o, kt_o) in enumerate(((ld0_o, b0_o, kt0_o), (ld1_o, b1_o, kt1_o))):
        zw = w0_ref[0, d:d + 1, :] + jnp.dot(tw_scr[d], wl2_ref[0, d].astype(BF16),
                                             preferred_element_type=F32)
        logw = -_softplus(-zw) - 0.5
        ld_o[...] = -jnp.exp(logw)
        a = _sigmoid(a0_ref[0, d:d + 1, :] + jnp.dot(ta_scr[d], al2_ref[0, d].astype(BF16),
                                                      preferred_element_type=F32))
        kt = k * (1.0 + (a - 1.0) * k_a)
        b_o[...] = (kk * a).astype(b_o.dtype)
        kt_o[...] = kt.astype(kt_o.dtype)
        kt_sum = kt if kt_sum is None else kt_sum + kt
    bonus_o[...] = _dot(r * rk_ref[...] * kt_sum, e) * v


def _rwkv_proj(x, mod, norm_g, p, layer, j, seq_len, group_of_tile):
    n = x.shape[0]
    nj = D // RWP_CW
    tile = pl.BlockSpec((TM, RWP_CW), lambda i, c: (i, c))
    wcol = lambda: pl.BlockSpec((1, D, RWP_CW), lambda i, c: (j, 0, c))
    vec = lambda: pl.BlockSpec((1, RWP_CW), lambda i, c: (j, c))
    return pl.pallas_call(
        functools.partial(_rwkv_proj_kernel, seq_len=seq_len),
        grid=(n // TM, nj),
        in_specs=[
            pl.BlockSpec((TM, D), lambda i, c: (i, 0)),
            _mod_spec(layer, group_of_tile, 2),
            _normg_spec(layer, 2),
            pl.BlockSpec((1, 6, D), lambda i, c: (j, 0, 0)),
            wcol(), wcol(), wcol(),
            pl.BlockSpec((1, 2, RWP_CW), lambda i, c: (j, 0, c)),
            pl.BlockSpec((1, 2, D, LORA_W), lambda i, c: (j, 0, 0, 0)),
            pl.BlockSpec((1, 2, LORA_W, RWP_CW), lambda i, c: (j, 0, 0, c)),
            pl.BlockSpec((1, 2, RWP_CW), lambda i, c: (j, 0, c)),
            pl.BlockSpec((1, 2, D, LORA_A), lambda i, c: (j, 0, 0, 0)),
            pl.BlockSpec((1, 2, LORA_A, RWP_CW), lambda i, c: (j, 0, 0, c)),
            pl.BlockSpec((1, D, LORA_G), lambda i, c: (j, 0, 0)),
            pl.BlockSpec((1, LORA_G, RWP_CW), lambda i, c: (j, 0, c)),
            vec(), vec(), vec(),
        ],
        out_specs=[tile] * len(RW_OUT_DTYPES),
        out_shape=[jax.ShapeDtypeStruct((n, D), dt) for dt in RW_OUT_DTYPES],
        scratch_shapes=[pltpu.VMEM((TM, D), BF16), pltpu.VMEM((TM, D), BF16), pltpu.VMEM((TM, D), BF16),
                        pltpu.VMEM((2, TM, LORA_W), BF16), pltpu.VMEM((2, TM, LORA_A), BF16),
                        pltpu.VMEM((TM, LORA_G), BF16)],
        compiler_params=_cparams(("arbitrary", "arbitrary")),
        name="rwkv_proj",
    )(x, mod, norm_g, p["mu"], p["w_r"], p["w_k"], p["w_v"], p["w0"], p["w_l1"], p["w_l2"],
      p["a0"], p["a_l1"], p["a_l2"], p["g_l1"], p["g_l2"], p["k_k"], p["k_a"], p["r_k"])


def _scan_precompute(units):
    c = CHUNK
    c2 = 2 * c
    row = lax.broadcasted_iota(jnp.int32, (c2, c2), 0)
    col = lax.broadcasted_iota(jnp.int32, (c2, c2), 1)
    rc = lax.broadcasted_iota(jnp.int32, (c, c), 0)
    cc = lax.broadcasted_iota(jnp.int32, (c, c), 1)
    head_a = lax.broadcasted_iota(jnp.int32, (c, LANES), 1) < HS
    own_lanes = jnp.concatenate([head_a, jnp.logical_not(head_a)], axis=0)
    eye = jnp.where(row == col, 1.0, 0.0)

    def two_heads(t):
        return jnp.concatenate([jnp.where(head_a, t, 0.0), jnp.where(head_a, 0.0, t)], axis=0)

    def causal(reverse):
        if reverse:
            return (jnp.where(cc >= rc, 1.0, 0.0).astype(BF16), (col % c) > (row % c), (col % c) >= (row % c))
        return (jnp.where(cc <= rc, 1.0, 0.0).astype(BF16), (col % c) < (row % c), (col % c) <= (row % c))

    masks = {rev: causal(rev) for rev in sorted({u[6] for u in units})}

    cums = [_dot_exact_lhs(masks[u[6]][0], u[0]) for u in units]
    st = []
    for (ld, kk, beta, kt, r, v, rev), cum in zip(units, cums):
        tot = cum[0:1] if rev else cum[c - 1:c]
        ginv = jnp.exp(-cum)
        tail = jnp.exp(tot - cum)
        st.append(dict(
            rev=rev, etot=jnp.exp(tot),
            a_t=two_heads(-kk * jnp.exp(cum - ld)), r_t=two_heads(r * jnp.exp(cum)),
            bk=jnp.concatenate([two_heads(beta * ginv), two_heads(kt * ginv)], axis=0),
            bkg=jnp.concatenate([two_heads(beta * tail), two_heads(kt * tail)], axis=0),
            v2=jnp.concatenate([v, v], axis=0), vh=two_heads(v)))
    grams = [_dot_nt(jnp.concatenate([s["a_t"], s["r_t"]], axis=0), s["bk"]) for s in st]
    for s, gram in zip(st, grams):
        _, strict, incl = masks[s["rev"]]
        s["l_ab"] = jnp.where(strict, gram[:c2, :c2], 0.0)
        s["l_ak"] = jnp.where(strict, gram[:c2, c2:], 0.0)
        s["t_rb"] = jnp.where(incl, gram[c2:, :c2], 0.0)
        s["t_rk"] = jnp.where(incl, gram[c2:, c2:], 0.0)
    lvs = [_dot(s["l_ak"], s["v2"]) for s in st]
    minvs = [eye for _ in st]
    b = 1
    while b < c:
        same = (row // (2 * b)) == (col // (2 * b))
        es = []
        for s in st:
            first, second = (col % (2 * b)) < b, (row % (2 * b)) >= b
            if s["rev"]:
                first, second = (row % (2 * b)) < b, (col % (2 * b)) >= b
            es.append(jnp.where(same & first & second, s["l_ab"], 0.0))
        if b == 1:
            minvs = [m + e for m, e in zip(minvs, es)]
        else:
            half = [_dot(m, e) for m, e in zip(minvs, es)]
            minvs = [m + _dot(h, m) for m, h in zip(minvs, half)]
        b *= 2
    mms = [_dot(m, jnp.concatenate([s["a_t"], lv], axis=1)) for m, s, lv in zip(minvs, st, lvs)]
    zero = jnp.zeros((c2, LANES), BF16)
    tts = [_dot(jnp.concatenate([s["t_rb"], s["t_rk"]], axis=1),
                jnp.concatenate([mm.astype(BF16), jnp.concatenate([zero, s["v2"].astype(BF16)], axis=1)], axis=0))
           for s, mm in zip(st, mms)]
    ps = [_dot_tn(mm[:, :LANES], s["bkg"][:c2]) for s, mm in zip(st, mms)]
    qs = [_dot_tn(jnp.concatenate([jnp.where(own_lanes, mm[:, LANES:], 0.0), s["vh"]], axis=0), s["bkg"])
          for s, mm in zip(st, mms)]
    out = []
    for s, p, q, tt in zip(st, ps, qs, tts):
        reff = s["r_t"] + tt[:, :LANES]
        out.append(dict(etot=s["etot"], p=p, q=q, reff=reff[:c] + reff[c:],
                        y0=jnp.where(head_a, tt[:c, LANES:], tt[c:, LANES:])))
    return out


def _rwkv_scan_kernel(*refs, seq_len, has_init, want_final):
    r_ref, v_ref, kk_ref, ld0_ref, b0_ref, kt0_ref, ld1_ref, b1_ref, kt1_ref = refs[:9]
    pos = 9
    if has_init:
        s0_ref = refs[pos]
        pos += 1
    y_ref = refs[pos]
    pos += 1
    if want_final:
        sf_ref = refs[pos]
        pos += 1
    g_scr, yb_scr = refs[pos:pos + 2]
    nc = seq_len // CHUNK
    ngroups = nc // SCAN_UNROLL
    per_dir = ((ld0_ref, b0_ref, kt0_ref), (ld1_ref, b1_ref, kt1_ref))
    y_dst = (y_ref, yb_scr)
    nchain = 2 * SCAN_PAIRS

    def group(gi, states):
        where, units = [], []
        for u in range(SCAN_UNROLL):
            for pp in range(SCAN_PAIRS):
                lanes = slice(pp * LANES, (pp + 1) * LANES)
                for d in range(2):
                    ld_ref, b_ref, kt_ref = per_dir[d]
                    cidx = gi * SCAN_UNROLL + u
                    if d == 1:
                        cidx = nc - 1 - cidx
                    start = cidx * CHUNK
                    rw = slice(start, start + CHUNK) if isinstance(start, int) else pl.ds(
                        pl.multiple_of(start, CHUNK), CHUNK)
                    where.append((rw, lanes))
                    units.append(tuple(ref[rw, lanes].astype(F32) for ref in
                                       (ld_ref, kk_ref, b_ref, kt_ref, r_ref, v_ref)) + (d == 1,))
        pre = _scan_precompute(units)
        states = list(states)
        for u in range(SCAN_UNROLL):
            cur = pre[u * nchain:(u + 1) * nchain]
            ys = [_dot_nt(cu["reff"], g) for cu, g in zip(cur, states)]
            gp = [_dot(g, cu["p"]) for cu, g in zip(cur, states)]
            states = [states[ch] * cur[ch]["etot"] + gp[ch] + cur[ch]["q"] for ch in range(nchain)]
            for ch in range(nchain):
                rw, lanes = where[u * nchain + ch]
                y_dst[ch % 2][rw, lanes] = ys[ch] + cur[ch]["y0"]
        return states

    init = [_block_diag2(s0_ref[0, ch % 2, 2 * (ch // 2)], s0_ref[0, ch % 2, 2 * (ch // 2) + 1])
            if has_init else jnp.zeros((LANES, LANES), F32) for ch in range(nchain)]
    if ngroups == 1:
        final = group(0, init)
    else:
        for ch in range(nchain):
            g_scr[ch] = init[ch]

        def body(gi, carry):
            new = group(gi, [g_scr[ch] for ch in range(nchain)])
            for ch in range(nchain):
                g_scr[ch] = new[ch]
            return carry

        lax.fori_loop(0, ngroups, body, 0)
        final = [g_scr[ch] for ch in range(nchain)]
    y_ref[...] += yb_scr[...]
    if want_final:
        for ch in range(nchain):
            sf_ref[0, ch % 2, 2 * (ch // 2)] = final[ch][:HS, :HS]
            sf_ref[0, ch % 2, 2 * (ch // 2) + 1] = final[ch][HS:, HS:]


def _rwkv_scan(proj, s_init, n_seq, seq_len, want_final):
    r, v, kk, ld0, b0, kt0, ld1, b1, kt1 = proj[:9]
    n = r.shape[0]
    npair = D // LANES
    width = SCAN_PAIRS * LANES
    blk = pl.BlockSpec((seq_len, width), lambda b, p: (b, p))
    st_spec = pl.BlockSpec((1, 2, 2 * SCAN_PAIRS, HS, HS), lambda b, p: (b, 0, p, 0, 0))
    in_specs = [blk] * 9
    args = [r, v, kk, ld0, b0, kt0, ld1, b1, kt1]
    has_init = s_init is not None
    if has_init:
        in_specs.append(st_spec)
        args.append(s_init)
    out_specs = [blk]
    out_shape = [jax.ShapeDtypeStruct((n, D), F32)]
    if want_final:
        out_specs.append(st_spec)
        out_shape.append(jax.ShapeDtypeStruct((n_seq, 2, NH, HS, HS), F32))
    res = pl.pallas_call(
        functools.partial(_rwkv_scan_kernel, seq_len=seq_len, has_init=has_init, want_final=want_final),
        grid=(n_seq, npair // SCAN_PAIRS),
        in_specs=in_specs,
        out_specs=out_specs,
        out_shape=out_shape,
        scratch_shapes=[pltpu.VMEM((2 * SCAN_PAIRS, LANES, LANES), F32), pltpu.VMEM((seq_len, width), F32)],
        compiler_params=_cparams(("arbitrary", "arbitrary")),
        name="rwkv_scan",
    )(*args)
    return res


def _rwkv_out_kernel(x_ref, m_ref, g_ref, y_ref, bonus_ref, gate_ref, lng_ref, lnb_ref, wo_ref, o_ref,
                     acc_scr):
    j = pl.program_id(1)

    @pl.when(j == 0)
    def _():
        acc_scr[...] = jnp.zeros_like(acc_scr)

    e = _head_indicator(RW_CW)
    y = y_ref[...]
    mean = _dot_hilo_rhs(y, e) * (1.0 / HS)
    yc = y - mean
    var = _dot(yc * yc, e) * (1.0 / HS)
    yn = yc * lax.rsqrt(var + GN_EPS) * lng_ref[...] + lnb_ref[...]
    yn = (yn + bonus_ref[...]) * gate_ref[...]
    acc_scr[...] += _dot(yn, wo_ref[0])

    @pl.when(j == pl.num_programs(1) - 1)
    def _():
        m = m_ref[0, 0]
        g = g_ref[0]
        o_ref[...] = x_ref[...] + m[2:3] * _rms(acc_scr[...], g[1:2])


def _rwkv_out(x, mod, norm_g, y, bonus, gate, p, layer, j, group_of_tile):
    n = x.shape[0]
    nj = D // RW_CW
    tile = pl.BlockSpec((TM, RW_CW), lambda i, c: (i, c))
    vec = pl.BlockSpec((1, RW_CW), lambda i, c: (j, c))
    return pl.pallas_call(
        _rwkv_out_kernel,
        grid=(n // TM, nj),
        in_specs=[
            pl.BlockSpec((TM, D), lambda i, c: (i, 0)),
            _mod_spec(layer, group_of_tile, 2),
            _normg_spec(layer, 2),
            tile, tile, tile, vec, vec,
            pl.BlockSpec((1, RW_CW, D), lambda i, c: (j, c, 0)),
        ],
        out_specs=pl.BlockSpec((TM, D), lambda i, c: (i, 0)),
        out_shape=jax.ShapeDtypeStruct((n, D), F32),
        scratch_shapes=[pltpu.VMEM((TM, D), F32)],
        compiler_params=_cparams(("arbitrary", "arbitrary")),
        name="rwkv_out",
    )(x, mod, norm_g, y, bonus, gate, p["ln_g"], p["ln_b"], p["w_o"])


def _rope_tables(n):
    rows = n // GRID_W
    row = np.repeat(np.arange(rows), GRID_W)
    col = np.tile(np.arange(GRID_W), rows)
    pos = np.stack([row, col], axis=-1).astype(np.float64)
    quarter = D_ROPE // 4
    inv = ROPE_BASE ** (-np.arange(quarter, dtype=np.float64) / quarter)
    ang = pos[:, :, None] * inv
    cos = np.cos(ang)
    sin = np.sin(ang)
    cos_t = np.concatenate([cos, cos], axis=-1).reshape(n, D_ROPE)
    sin_t = np.concatenate([-sin, sin], axis=-1).reshape(n, D_ROPE)
    return cos_t.astype(np.float32), sin_t.astype(np.float32)


def _rope_swap_perm():
    quarter = D_ROPE // 4
    base = np.arange(D_ROPE)
    return np.where((base % (2 * quarter)) < quarter, base + quarter, base - quarter)


def _mla_proj_kernel(*refs, positional):
    (x_ref, m_ref, g_ref, wdq_ref, gq_ref, wqn_ref, wqr_ref, wqs_ref, wdkv_ref, wkr_ref, wks_ref,
     gkv_ref) = refs[:12]
    pos = 12
    if positional:
        cosq_ref, sinq_ref, cosk_ref, sink_ref = refs[pos:pos + 4]
        pos += 4
    qn_o, qr_o, ckv_o, kr_o = refs[pos:pos + 4]
    m = m_ref[0, 0]
    g = g_ref[0]
    h = (_rms(x_ref[...], g[0:1]) * (1.0 + m[1:2]) + m[0:1]).astype(BF16)
    ql = jnp.dot(h, wdq_ref[0].astype(BF16), preferred_element_type=F32)
    ql = (ql * lax.rsqrt(jnp.mean(ql * ql, axis=-1, keepdims=True) + EPS) * gq_ref[...]).astype(BF16)
    qn_o[...] = jnp.dot(ql, wqn_ref[...].astype(BF16), preferred_element_type=F32)
    qr = jnp.dot(ql, wqr_ref[...].astype(BF16), preferred_element_type=F32)
    ckv = jnp.dot(h, wdkv_ref[...].astype(BF16), preferred_element_type=F32)
    ckv_o[...] = ckv * lax.rsqrt(jnp.mean(ckv * ckv, axis=-1, keepdims=True) + EPS) * gkv_ref[...]
    kr = jnp.dot(h, wkr_ref[...].astype(BF16), preferred_element_type=F32)
    if positional:
        qs = jnp.dot(ql, wqs_ref[...].astype(BF16), preferred_element_type=F32)
        ks = jnp.dot(h, wks_ref[...].astype(BF16), preferred_element_type=F32)
        qr = qr * cosq_ref[...] + qs * sinq_ref[...]
        kr = kr * cosk_ref[...] + ks * sink_ref[...]
    qr_o[...] = qr
    kr_o[...] = kr


def _mla_proj(x, mod, norm_g, p, layer, j, positional, group_of_tile):
    n = x.shape[0]
    full = lambda shape: pl.BlockSpec(shape, lambda i: (0,) * len(shape))
    in_specs = [
        pl.BlockSpec((TM, D), lambda i: (i, 0)),
        _mod_spec(layer, group_of_tile, 1),
        _normg_spec(layer, 1),
        pl.BlockSpec((1, D, Q_RANK), lambda i: (j, 0, 0)),
        pl.BlockSpec((1, Q_RANK), lambda i: (j, 0)),
        full((Q_RANK, MLA_H * D_NOPE)), full((Q_RANK, MLA_H * D_ROPE)), full((Q_RANK, MLA_H * D_ROPE)),
        full((D, KV_RANK)), full((D, D_ROPE)), full((D, D_ROPE)),
        pl.BlockSpec((1, KV_RANK), lambda i: (j, 0)),
    ]
    args = [x, mod, norm_g, p["w_dq"], p["g_q"], p["w_uq_nope"], p["w_uq_rope"], p["w_uq_rope_sw"],
            p["w_dkv_c"], p["w_dkv_r"], p["w_dkv_r_sw"], p["g_kv"]]
    if positional:
        cos_t, sin_t = _rope_tables(TM)
        in_specs += [full((TM, MLA_H * D_ROPE)), full((TM, MLA_H * D_ROPE)),
                     full((TM, D_ROPE)), full((TM, D_ROPE))]
        args += [jnp.asarray(np.tile(cos_t, (1, MLA_H))), jnp.asarray(np.tile(sin_t, (1, MLA_H))),
                 jnp.asarray(cos_t), jnp.asarray(sin_t)]
    widths = (MLA_H * D_NOPE, MLA_H * D_ROPE, KV_RANK, D_ROPE)
    return pl.pallas_call(
        functools.partial(_mla_proj_kernel, positional=positional),
        grid=(n // TM,),
        in_specs=in_specs,
        out_specs=[pl.BlockSpec((TM, w), lambda i: (i, 0)) for w in widths],
        out_shape=[jax.ShapeDtypeStruct((n, w), F32) for w in widths],
        compiler_params=_cparams(("arbitrary",)),
        name="mla_proj",
    )(*args)


def _mla_attn_kernel(x_ref, m_ref, g_ref, qn_ref, qr_ref, ckv_ref, kr_ref, wuk_ref, wuv_ref, wo_ref,
                     o_ref, kn_scr, vv_scr, oh_scr, *, nb, tq, k_len):
    qi = pl.program_id(1)

    @pl.when(qi == 0)
    def _():
        ckv = ckv_ref[...].astype(BF16)
        kn_scr[...] = jnp.dot(ckv, wuk_ref[0].astype(BF16), preferred_element_type=F32).astype(BF16)
        vv_scr[...] = jnp.dot(ckv, wuv_ref[0].astype(BF16), preferred_element_type=F32).astype(BF16)

    units = [(b, hd) for b in range(nb) for hd in range(MLA_H)]
    scores = []
    for b, hd in units:
        qrows = slice(b * tq, (b + 1) * tq)
        krows = slice(b * k_len, (b + 1) * k_len)
        q = jnp.concatenate([qn_ref[qrows, hd * D_NOPE:(hd + 1) * D_NOPE].astype(BF16),
                             qr_ref[qrows, hd * D_ROPE:(hd + 1) * D_ROPE].astype(BF16)], axis=1)
        k = jnp.concatenate([kn_scr[krows, hd * D_NOPE:(hd + 1) * D_NOPE],
                             kr_ref[krows, :].astype(BF16)], axis=1)
        scores.append(_dot_nt(q, k) * MLA_SCALE)
    probs = []
    for s in scores:
        pexp = jnp.exp(s - jnp.max(s, axis=-1, keepdims=True))
        probs.append((pexp / jnp.sum(pexp, axis=-1, keepdims=True)).astype(BF16))
    for (b, hd), pr in zip(units, probs):
        oh_scr[b * tq:(b + 1) * tq, hd * D_V:(hd + 1) * D_V] = jnp.dot(
            pr, vv_scr[b * k_len:(b + 1) * k_len, hd * D_V:(hd + 1) * D_V],
            preferred_element_type=F32).astype(BF16)
    o = jnp.dot(oh_scr[...], wo_ref[0].astype(BF16), preferred_element_type=F32)
    m = m_ref[0, 0]
    g = g_ref[0]
    o_ref[...] = x_ref[...] + m[2:3] * _rms(o, g[1:2])


def _mla_attn(x, mod, norm_g, qn, qr, ckv_all, kr_all, p, layer, j, n_seq, q_len, k_len, nb, tq,
              group_of_step):
    n = x.shape[0]
    nq = q_len // tq
    assert nb == 1 or nq == 1
    return pl.pallas_call(
        functools.partial(_mla_attn_kernel, nb=nb, tq=tq, k_len=k_len),
        grid=(n_seq // nb, nq),
        in_specs=[
            pl.BlockSpec((nb * tq, D), lambda s, q: (s * nq + q, 0)),
            pl.BlockSpec((1, 1, N_MOD, D), lambda s, q: (layer, group_of_step(s), 0, 0)),
            pl.BlockSpec((1, 4, D), lambda s, q: (layer, 0, 0)),
            pl.BlockSpec((nb * tq, MLA_H * D_NOPE), lambda s, q: (s * nq + q, 0)),
            pl.BlockSpec((nb * tq, MLA_H * D_ROPE), lambda s, q: (s * nq + q, 0)),
            pl.BlockSpec((nb * k_len, KV_RANK), lambda s, q: (s, 0)),
            pl.BlockSpec((nb * k_len, D_ROPE), lambda s, q: (s, 0)),
            pl.BlockSpec((1, KV_RANK, MLA_H * D_NOPE), lambda s, q: (j, 0, 0)),
            pl.BlockSpec((1, KV_RANK, MLA_H * D_V), lambda s, q: (j, 0, 0)),
            pl.BlockSpec((1, MLA_H * D_V, D), lambda s, q: (j, 0, 0)),
        ],
        out_specs=pl.BlockSpec((nb * tq, D), lambda s, q: (s * nq + q, 0)),
        out_shape=jax.ShapeDtypeStruct((n, D), F32),
        scratch_shapes=[pltpu.VMEM((nb * k_len, MLA_H * D_NOPE), BF16),
                        pltpu.VMEM((nb * k_len, MLA_H * D_V), BF16),
                        pltpu.VMEM((nb * tq, MLA_H * D_V), BF16)],
        compiler_params=_cparams(("arbitrary", "arbitrary")),
        name="mla_attn",
    )(x, mod, norm_g, qn, qr, ckv_all, kr_all, p["w_uk"], p["w_uv"], p["w_o"])


def kernel(x_prompt, x_sample, state_rwkv, cache_mla_ckv, cache_mla_krope, c, c_ctx, mod_w, mod_b, norm_g,
           mlp_w1, mlp_w2, fft_w_out, conv_w_in, conv_w, conv_w_out, rwkv_mu, rwkv_w_r, rwkv_w_k, rwkv_w_v,
           rwkv_w_o, rwkv_w0, rwkv_w_l1, rwkv_w_l2, rwkv_a0, rwkv_a_l1, rwkv_a_l2, rwkv_g_l1, rwkv_g_l2,
           rwkv_k_k, rwkv_k_a, rwkv_r_k, rwkv_ln_g, rwkv_ln_b, mla_w_dq, mla_g_q, mla_w_uq, mla_w_dkv,
           mla_g_kv, mla_w_uk, mla_w_uv, mla_w_o):
    batch, seq, _ = x_prompt.shape
    dec_batch, dec_seq, _ = x_sample.shape
    past_len = cache_mla_ckv.shape[2]
    assert (batch * seq) % TM == 0 and TM % seq == 0 and dec_seq == TM and seq % CHUNK == 0

    xp = x_prompt.reshape(batch * seq, D)
    xs = x_sample.reshape(dec_batch * dec_seq, D)
    cs = jnp.concatenate([c_ctx[None, :], c, jnp.zeros((8 - 1 - dec_batch, D), F32)], axis=0)
    mod = _modulation(cs, mod_w, mod_b)

    grp_p = lambda i: 0
    grp_s = lambda i: 1 + i
    new_rwkv, new_ckv, new_krope = [], [], []
    streams = ((True, seq, batch, grp_p), (False, dec_seq, dec_batch, grp_s))

    for i in range(DEPTH):
        kind, j = i % 4, i // 4
        outs = []
        for is_prompt, slen, nseq, grp in streams:
            x = xp if is_prompt else xs
            if kind == 0:
                x = _fnet(x, mod, norm_g, fft_w_out, i, j, slen, grp)
            elif kind == 1:
                x = _conv(x, mod, norm_g, conv_w_in, conv_w, conv_w_out, i, j, slen, grp)
            elif kind == 2:
                p = dict(mu=rwkv_mu, w_r=rwkv_w_r, w_k=rwkv_w_k, w_v=rwkv_w_v, w_o=rwkv_w_o, w0=rwkv_w0,
                         w_l1=rwkv_w_l1, w_l2=rwkv_w_l2, a0=rwkv_a0, a_l1=rwkv_a_l1, a_l2=rwkv_a_l2,
                         g_l1=rwkv_g_l1, g_l2=rwkv_g_l2, k_k=rwkv_k_k, k_a=rwkv_k_a,
                         r_k=rwkv_r_k.reshape(-1, D), ln_g=rwkv_ln_g, ln_b=rwkv_ln_b)
                proj = _rwkv_proj(x, mod, norm_g, p, i, j, slen, grp)
                if is_prompt:
                    y, s_fin = _rwkv_scan(proj, None, nseq, slen, True)
                    new_rwkv.append(s_fin)
                else:
                    (y,) = _rwkv_scan(proj, state_rwkv[:, j], nseq, slen, False)
                x = _rwkv_out(x, mod, norm_g, y, proj[10], proj[9], p, i, j, grp)
            else:
                perm = _rope_swap_perm()
                w_uq = mla_w_uq[j].reshape(Q_RANK, MLA_H, D_NOPE + D_ROPE)
                w_uq_rope = w_uq[:, :, D_NOPE:]
                w_dkv_r = mla_w_dkv[j][:, KV_RANK:]
                p = dict(w_dq=mla_w_dq, g_q=mla_g_q, g_kv=mla_g_kv, w_uk=mla_w_uk, w_uv=mla_w_uv, w_o=mla_w_o,
                         w_uq_nope=w_uq[:, :, :D_NOPE].reshape(Q_RANK, MLA_H * D_NOPE),
                         w_uq_rope=w_uq_rope.reshape(Q_RANK, MLA_H * D_ROPE),
                         w_uq_rope_sw=w_uq_rope[:, :, perm].reshape(Q_RANK, MLA_H * D_ROPE),
                         w_dkv_c=mla_w_dkv[j][:, :KV_RANK], w_dkv_r=w_dkv_r, w_dkv_r_sw=w_dkv_r[:, perm])
                qn, qr, ckv, kr = _mla_proj(x, mod, norm_g, p, i, j, not is_prompt, grp)
                if is_prompt:
                    new_ckv.append(ckv.reshape(batch, seq, KV_RANK))
                    new_krope.append(kr.reshape(batch, seq, D_ROPE))
                    x = _mla_attn(x, mod, norm_g, qn, qr, ckv, kr, p, i, j, nseq, slen, slen, TM // slen, slen,
                                  lambda s: 0)
                else:
                    klen = past_len + slen
                    ckv_all = jnp.concatenate([cache_mla_ckv[:, j], ckv.reshape(nseq, slen, KV_RANK)], axis=1)
                    kr_all = jnp.concatenate([cache_mla_krope[:, j], kr.reshape(nseq, slen, D_ROPE)], axis=1)
                    x = _mla_attn(x, mod, norm_g, qn, qr, ckv_all.reshape(nseq * klen, KV_RANK),
                                  kr_all.reshape(nseq * klen, D_ROPE), p, i, j, nseq, slen, klen, 1, 256,
                                  lambda s: 1 + s)
            x = _mlp(x, mod, norm_g, mlp_w1, mlp_w2, i, grp)
            outs.append(x)
        xp, xs = outs

    return (xp.reshape(batch, seq, D), xs.reshape(dec_batch, dec_seq, D),
            jnp.stack(new_rwkv, axis=1), jnp.stack(new_ckv, axis=1), jnp.stack(new_krope, axis=1))
```

```python
import functools

import numpy as np
import jax
import jax.numpy as jnp
from jax import lax
from jax.experimental import pallas as pl
from jax.experimental.pallas import tpu as pltpu

D = 1024
DEPTH = 4
N_MOD = 6
D_FF = 4 * D
EPS = 1e-6
GRID_W = 64
FFT_GROUPS = 8
FFT_GW = D // FFT_GROUPS
HS = 64
NH = D // HS
LORA_W = 64
LORA_A = 64
LORA_G = 128
GN_EPS = 64e-5
MLA_H = 8
D_NOPE = 128
D_ROPE = 64
D_V = 128
KV_RANK = 256
Q_RANK = 384
ROPE_BASE = 10000.0
MLA_SCALE = (D_NOPE + D_ROPE) ** -0.5

F32 = jnp.float32
BF16 = jnp.bfloat16

TM = 1024
LANES = 128
CHUNK = 64
SCAN_UNROLL = 4
SCAN_PAIRS = 2
VMEM_LIMIT = 56 * 1024 * 1024


def _cparams(sem):
    return pltpu.CompilerParams(dimension_semantics=sem, vmem_limit_bytes=VMEM_LIMIT)


def _dot(a, b):
    return jnp.dot(a.astype(BF16), b.astype(BF16), preferred_element_type=F32)


def _dot_nt(a, b):
    return lax.dot_general(a.astype(BF16), b.astype(BF16), (((1,), (1,)), ((), ())),
                           preferred_element_type=F32)


def _dot_tn(a, b):
    return lax.dot_general(a.astype(BF16), b.astype(BF16), (((0,), (0,)), ((), ())),
                           preferred_element_type=F32)


def _block_diag2(a, b):
    za = jnp.zeros((a.shape[0], b.shape[1]), a.dtype)
    zb = jnp.zeros((b.shape[0], a.shape[1]), a.dtype)
    return jnp.concatenate([jnp.concatenate([a, za], axis=1), jnp.concatenate([zb, b], axis=1)], axis=0)


def _cumsum_rows(x, reverse):
    n = x.shape[0]
    idx = lax.broadcasted_iota(jnp.int32, x.shape, 0)
    s = 1
    while s < n:
        if reverse:
            x = x + jnp.where(idx < n - s, pltpu.roll(x, n - s, 0), 0.0)
        else:
            x = x + jnp.where(idx >= s, pltpu.roll(x, s, 0), 0.0)
        s *= 2
    return x


def _dot_hilo_rhs(x, e):
    hi = x.astype(BF16)
    lo = (x - hi.astype(F32)).astype(BF16)
    return jnp.dot(hi, e, preferred_element_type=F32) + jnp.dot(lo, e, preferred_element_type=F32)


def _rms(x, g):
    return x * lax.rsqrt(jnp.mean(x * x, axis=-1, keepdims=True) + EPS) * g


def _sigmoid(x):
    return 1.0 / (1.0 + jnp.exp(-x))


def _softplus(x):
    return jnp.maximum(x, 0.0) + jnp.log(1.0 + jnp.exp(-jnp.abs(x)))


def _head_indicator(n):
    r = lax.broadcasted_iota(jnp.int32, (n, n), 0) // HS
    c = lax.broadcasted_iota(jnp.int32, (n, n), 1) // HS
    return jnp.where(r == c, 1.0, 0.0).astype(BF16)


def _shift_rows(z, seq_len):
    n = z.shape[0]
    pos = lax.broadcasted_iota(jnp.int32, z.shape, 0) % seq_len
    prev = jnp.where(pos == 0, 0.0, pltpu.roll(z, 1, 0))
    nxt = jnp.where(pos == seq_len - 1, 0.0, pltpu.roll(z, n - 1, 0))
    return prev, nxt


MOD_TN = 1536


def _mod_kernel(cs_ref, w_ref, b_ref, o_ref):
    cs = cs_ref[...]
    s = cs * _sigmoid(cs)
    o_ref[0] = _dot(s, w_ref[0]) + b_ref[0]


def _modulation(cs, mod_w, mod_b):
    nj = (N_MOD * D) // MOD_TN
    out = pl.pallas_call(
        _mod_kernel,
        grid=(DEPTH, nj),
        in_specs=[
            pl.BlockSpec((8, D), lambda l, j: (0, 0)),
            pl.BlockSpec((1, D, MOD_TN), lambda l, j: (l, 0, j)),
            pl.BlockSpec((1, 1, MOD_TN), lambda l, j: (l, 0, j)),
        ],
        out_specs=pl.BlockSpec((1, 8, MOD_TN), lambda l, j: (l, 0, j)),
        out_shape=jax.ShapeDtypeStruct((DEPTH, 8, N_MOD * D), F32),
        compiler_params=_cparams(("arbitrary", "arbitrary")),
        name="modulation",
    )(cs, mod_w, mod_b.reshape(DEPTH, 1, N_MOD * D))
    return out.reshape(DEPTH, 8, N_MOD, D)


def _mod_spec(layer, group_of_tile, ngrid):
    if ngrid == 1:
        return pl.BlockSpec((1, 1, N_MOD, D), lambda i: (layer, group_of_tile(i), 0, 0))
    return pl.BlockSpec((1, 1, N_MOD, D), lambda i, j: (layer, group_of_tile(i), 0, 0))


def _normg_spec(layer, ngrid):
    if ngrid == 1:
        return pl.BlockSpec((1, 4, D), lambda i: (layer, 0, 0))
    return pl.BlockSpec((1, 4, D), lambda i, j: (layer, 0, 0))


MLP_FC = 1024


def _mlp_kernel(x_ref, m_ref, g_ref, w1_ref, w2_ref, o_ref, h_scr, acc_scr):
    j = pl.program_id(1)
    m = m_ref[0, 0]
    g = g_ref[0]

    @pl.when(j == 0)
    def _():
        h = _rms(x_ref[...], g[2:3]) * (1.0 + m[4:5]) + m[3:4]
        h_scr[...] = h.astype(BF16)
        acc_scr[...] = jnp.zeros_like(acc_scr)

    a = jnp.dot(h_scr[...], w1_ref[0].astype(BF16), preferred_element_type=F32)
    a = jnp.maximum(a, 0.0)
    a = a * a
    acc_scr[...] += _dot(a, w2_ref[0])

    @pl.when(j == pl.num_programs(1) - 1)
    def _():
        o_ref[...] = x_ref[...] + m[5:6] * _rms(acc_scr[...], g[3:4])


def _mlp(x, mod, norm_g, w1, w2, layer, group_of_tile):
    n = x.shape[0]
    nj = D_FF // MLP_FC
    return pl.pallas_call(
        _mlp_kernel,
        grid=(n // TM, nj),
        in_specs=[
            pl.BlockSpec((TM, D), lambda i, j: (i, 0)),
            _mod_spec(layer, group_of_tile, 2),
            _normg_spec(layer, 2),
            pl.BlockSpec((1, D, MLP_FC), lambda i, j: (layer, 0, j)),
            pl.BlockSpec((1, MLP_FC, D), lambda i, j: (layer, j, 0)),
        ],
        out_specs=pl.BlockSpec((TM, D), lambda i, j: (i, 0)),
        out_shape=jax.ShapeDtypeStruct((n, D), F32),
        scratch_shapes=[pltpu.VMEM((TM, D), BF16), pltpu.VMEM((TM, D), F32)],
        compiler_params=_cparams(("arbitrary", "arbitrary")),
        name="mlp",
    )(x, mod, norm_g, w1, w2)


def _dft_mats(n):
    idx = np.arange(n, dtype=np.int64)
    ang = (2.0 * np.pi / n) * ((idx[:, None] * idx[None, :]) % n).astype(np.float64)
    scale = 1.0 / np.sqrt(n)
    return np.cos(ang) * scale, np.sin(ang) * scale


def _fnet_kernel(x_ref, m_ref, g_ref, cs_ref, cn_ref, sn_ref, w_ref, o_ref, p_scr, q_scr, f_scr,
                 *, seq_len):
    m = m_ref[0, 0]
    g = g_ref[0]
    x = x_ref[...]
    h = (_rms(x, g[0:1]) * (1.0 + m[1:2]) + m[0:1]).astype(BF16)
    cs = cs_ref[...].astype(BF16)
    for gi in range(FFT_GROUPS):
        pq = jnp.dot(h[:, gi * FFT_GW:(gi + 1) * FFT_GW], cs, preferred_element_type=F32)
        p_scr[:, gi * FFT_GW:(gi + 1) * FFT_GW] = pq[:, :FFT_GW].astype(BF16)
        q_scr[:, gi * FFT_GW:(gi + 1) * FFT_GW] = pq[:, FFT_GW:].astype(BF16)
    cn = cn_ref[...].astype(BF16)
    sn = sn_ref[...].astype(BF16)
    for s in range(TM // seq_len):
        rows = slice(s * seq_len, (s + 1) * seq_len)
        f = (jnp.dot(cn, p_scr[rows, :], preferred_element_type=F32)
             - jnp.dot(sn, q_scr[rows, :], preferred_element_type=F32))
        f_scr[rows, :] = f.astype(BF16)
    o = jnp.dot(f_scr[...], w_ref[0].astype(BF16), preferred_element_type=F32)
    o_ref[...] = x + m[2:3] * _rms(o, g[1:2])


def _fnet(x, mod, norm_g, w_out, layer, j, seq_len, group_of_tile):
    n = x.shape[0]
    cg, sg = _dft_mats(FFT_GW)
    cs = jnp.asarray(np.concatenate([cg, sg], axis=1), F32)
    cn_np, sn_np = _dft_mats(seq_len)
    cn = jnp.asarray(cn_np, F32)
    sn = jnp.asarray(sn_np, F32)
    return pl.pallas_call(
        functools.partial(_fnet_kernel, seq_len=seq_len),
        grid=(n // TM,),
        in_specs=[
            pl.BlockSpec((TM, D), lambda i: (i, 0)),
            _mod_spec(layer, group_of_tile, 1),
            _normg_spec(layer, 1),
            pl.BlockSpec((FFT_GW, 2 * FFT_GW), lambda i: (0, 0)),
            pl.BlockSpec((seq_len, seq_len), lambda i: (0, 0)),
            pl.BlockSpec((seq_len, seq_len), lambda i: (0, 0)),
            pl.BlockSpec((1, D, D), lambda i: (j, 0, 0)),
        ],
        out_specs=pl.BlockSpec((TM, D), lambda i: (i, 0)),
        out_shape=jax.ShapeDtypeStruct((n, D), F32),
        scratch_shapes=[pltpu.VMEM((TM, D), BF16), pltpu.VMEM((TM, D), BF16),
                        pltpu.VMEM((TM, D), BF16)],
        compiler_params=_cparams(("arbitrary",)),
        name="fourier_mix",
    )(x, mod, norm_g, cs, cn, sn, w_out)


CONV_CW = 512


def _conv_kernel(x_ref, m_ref, g_ref, wb_ref, wc_ref, wu_ref, cw_ref, wo_ref, o_ref, h_scr, acc_scr,
                 *, seq_len):
    j = pl.program_id(1)
    m = m_ref[0, 0]
    g = g_ref[0]

    @pl.when(j == 0)
    def _():
        h = _rms(x_ref[...], g[0:1]) * (1.0 + m[1:2]) + m[0:1]
        h_scr[...] = h.astype(BF16)
        acc_scr[...] = jnp.zeros_like(acc_scr)

    h = h_scr[...]
    hw = CONV_CW // 2
    proj = []
    for s in range(2):
        cols = slice(s * hw, (s + 1) * hw)
        proj.append([jnp.dot(h, w_ref[0, :, cols].astype(BF16), preferred_element_type=F32)
                     for w_ref in (wb_ref, wc_ref, wu_ref)])
    out = None
    for s in range(2):
        cols = slice(s * hw, (s + 1) * hw)
        bg, cg, u = proj[s]
        z = cg * u
        z_prev, z_next = _shift_rows(z, seq_len)
        conv = z_prev * cw_ref[0, 0:1, cols] + z * cw_ref[0, 1:2, cols] + z_next * cw_ref[0, 2:3, cols]
        part = _dot(bg * conv, wo_ref[0, cols, :])
        out = part if out is None else out + part
    acc_scr[...] += out

    @pl.when(j == pl.num_programs(1) - 1)
    def _():
        o_ref[...] = x_ref[...] + m[2:3] * _rms(acc_scr[...], g[1:2])


def _conv(x, mod, norm_g, w_in, w_conv, w_out, layer, j, seq_len, group_of_tile):
    n = x.shape[0]
    nj = D // CONV_CW
    return pl.pallas_call(
        functools.partial(_conv_kernel, seq_len=seq_len),
        grid=(n // TM, nj),
        in_specs=[
            pl.BlockSpec((TM, D), lambda i, c: (i, 0)),
            _mod_spec(layer, group_of_tile, 2),
            _normg_spec(layer, 2),
            pl.BlockSpec((1, D, CONV_CW), lambda i, c: (j, 0, c)),
            pl.BlockSpec((1, D, CONV_CW), lambda i, c: (j, 0, nj + c)),
            pl.BlockSpec((1, D, CONV_CW), lambda i, c: (j, 0, 2 * nj + c)),
            pl.BlockSpec((1, 3, CONV_CW), lambda i, c: (j, 0, c)),
            pl.BlockSpec((1, CONV_CW, D), lambda i, c: (j, c, 0)),
        ],
        out_specs=pl.BlockSpec((TM, D), lambda i, c: (i, 0)),
        out_shape=jax.ShapeDtypeStruct((n, D), F32),
        scratch_shapes=[pltpu.VMEM((TM, D), BF16), pltpu.VMEM((TM, D), F32)],
        compiler_params=_cparams(("arbitrary", "arbitrary")),
        name="short_conv",
    )(x, mod, norm_g, w_in, w_in, w_in, w_conv, w_out)


RWP_CW = 256
RW_OUT_DTYPES = (BF16, BF16, BF16, F32, BF16, BF16, F32, BF16, BF16, F32, F32)
RW_CW = 512


def _rwkv_proj_kernel(x_ref, m_ref, g_ref, mu_ref, wr_ref, wk_ref, wv_ref, w0_ref, wl1_ref, wl2_ref,
                      a0_ref, al1_ref, al2_ref, gl1_ref, gl2_ref, kk_ref, ka_ref, rk_ref,
                      r_o, v_o, kk_o, ld0_o, b0_o, kt0_o, ld1_o, b1_o, kt1_o, g_o, bonus_o,
                      xr_scr, xk_scr, xv_scr, tw_scr, ta_scr, sg_scr, *, seq_len):
    j = pl.program_id(1)

    @pl.when(j == 0)
    def _():
        m = m_ref[0, 0]
        g = g_ref[0]
        mu = mu_ref[0]
        h = _rms(x_ref[...], g[0:1]) * (1.0 + m[1:2]) + m[0:1]
        h_prev, h_next = _shift_rows(h, seq_len)
        dx = 0.5 * (h_prev + h_next) - h
        xr_scr[...] = (h + dx * mu[0:1]).astype(BF16)
        xk_scr[...] = (h + dx * mu[2:3]).astype(BF16)
        xv_scr[...] = (h + dx * mu[3:4]).astype(BF16)
        xw = (h + dx * mu[1:2]).astype(BF16)
        xa = (h + dx * mu[4:5]).astype(BF16)
        xg = (h + dx * mu[5:6]).astype(BF16)
        for d in range(2):
            tw_scr[d] = jnp.tanh(_dot(xw, wl1_ref[0, d])).astype(BF16)
            ta_scr[d] = _dot(xa, al1_ref[0, d]).astype(BF16)
        sg_scr[...] = _sigmoid(_dot(xg, gl1_ref[0])).astype(BF16)

    e = _head_indicator(RWP_CW)
    r = jnp.dot(xr_scr[...], wr_ref[0].astype(BF16), preferred_element_type=F32)
    k = jnp.dot(xk_scr[...], wk_ref[0].astype(BF16), preferred_element_type=F32)
    v = jnp.dot(xv_scr[...], wv_ref[0].astype(BF16), preferred_element_type=F32)
    g_o[...] = jnp.dot(sg_scr[...], gl2_ref[0].astype(BF16), preferred_element_type=F32)
    zws = [jnp.dot(tw_scr[d], wl2_ref[0, d].astype(BF16), preferred_element_type=F32) for d in range(2)]
    zas = [jnp.dot(ta_scr[d], al2_ref[0, d].astype(BF16), preferred_element_type=F32) for d in range(2)]
    kk = k * kk_ref[...]
    kk = kk * lax.rsqrt(_dot(kk * kk, e) + 1e-12)
    r_o[...] = r.astype(r_o.dtype)
    v_o[...] = v.astype(v_o.dtype)
    kk_o[...] = kk.astype(kk_o.dtype)
    k_a = ka_ref[...]
    kt_sum = None
    for d, (ld_o, b_o, kt_o) in enumerate(((ld0_o, b0_o, kt0_o), (ld1_o, b1_o, kt1_o))):
        logw = -_softplus(-(w0_ref[0, d:d + 1, :] + zws[d])) - 0.5
        ld_o[...] = -jnp.exp(logw)
        a = _sigmoid(a0_ref[0, d:d + 1, :] + zas[d])
        kt = k * (1.0 + (a - 1.0) * k_a)
        b_o[...] = (kk * a).astype(b_o.dtype)
        kt_o[...] = kt.astype(kt_o.dtype)
        kt_sum = kt if kt_sum is None else kt_sum + kt
    bonus_o[...] = _dot(r * rk_ref[...] * kt_sum, e) * v


def _rwkv_proj(x, mod, norm_g, p, layer, j, seq_len, group_of_tile):
    n = x.shape[0]
    nj = D // RWP_CW
    tile = pl.BlockSpec((TM, RWP_CW), lambda i, c: (i, c))
    wcol = lambda: pl.BlockSpec((1, D, RWP_CW), lambda i, c: (j, 0, c))
    vec = lambda: pl.BlockSpec((1, RWP_CW), lambda i, c: (j, c))
    return pl.pallas_call(
        functools.partial(_rwkv_proj_kernel, seq_len=seq_len),
        grid=(n // TM, nj),
        in_specs=[
            pl.BlockSpec((TM, D), lambda i, c: (i, 0)),
            _mod_spec(layer, group_of_tile, 2),
            _normg_spec(layer, 2),
            pl.BlockSpec((1, 6, D), lambda i, c: (j, 0, 0)),
            wcol(), wcol(), wcol(),
            pl.BlockSpec((1, 2, RWP_CW), lambda i, c: (j, 0, c)),
            pl.BlockSpec((1, 2, D, LORA_W), lambda i, c: (j, 0, 0, 0)),
            pl.BlockSpec((1, 2, LORA_W, RWP_CW), lambda i, c: (j, 0, 0, c)),
            pl.BlockSpec((1, 2, RWP_CW), lambda i, c: (j, 0, c)),
            pl.BlockSpec((1, 2, D, LORA_A), lambda i, c: (j, 0, 0, 0)),
            pl.BlockSpec((1, 2, LORA_A, RWP_CW), lambda i, c: (j, 0, 0, c)),
            pl.BlockSpec((1, D, LORA_G), lambda i, c: (j, 0, 0)),
            pl.BlockSpec((1, LORA_G, RWP_CW), lambda i, c: (j, 0, c)),
            vec(), vec(), vec(),
        ],
        out_specs=[tile] * len(RW_OUT_DTYPES),
        out_shape=[jax.ShapeDtypeStruct((n, D), dt) for dt in RW_OUT_DTYPES],
        scratch_shapes=[pltpu.VMEM((TM, D), BF16), pltpu.VMEM((TM, D), BF16), pltpu.VMEM((TM, D), BF16),
                        pltpu.VMEM((2, TM, LORA_W), BF16), pltpu.VMEM((2, TM, LORA_A), BF16),
                        pltpu.VMEM((TM, LORA_G), BF16)],
        compiler_params=_cparams(("arbitrary", "arbitrary")),
        name="rwkv_proj",
    )(x, mod, norm_g, p["mu"], p["w_r"], p["w_k"], p["w_v"], p["w0"], p["w_l1"], p["w_l2"],
      p["a0"], p["a_l1"], p["a_l2"], p["g_l1"], p["g_l2"], p["k_k"], p["k_a"], p["r_k"])


def _scan_precompute(units):
    c = CHUNK
    c2 = 2 * c
    row = lax.broadcasted_iota(jnp.int32, (c2, c2), 0)
    col = lax.broadcasted_iota(jnp.int32, (c2, c2), 1)
    head_a = lax.broadcasted_iota(jnp.int32, (c, LANES), 1) < HS
    own_lanes = jnp.concatenate([head_a, jnp.logical_not(head_a)], axis=0)
    eye = jnp.where(row == col, 1.0, 0.0)

    def two_heads(t):
        return jnp.concatenate([jnp.where(head_a, t, 0.0), jnp.where(head_a, 0.0, t)], axis=0)

    def causal(reverse):
        if reverse:
            return (col % c) > (row % c), (col % c) >= (row % c)
        return (col % c) < (row % c), (col % c) <= (row % c)

    masks = {rev: causal(rev) for rev in sorted({u[6] for u in units})}

    cums = [_cumsum_rows(u[0], u[6]) for u in units]
    st = []
    for (ld, kk, beta, kt, r, v, rev), cum in zip(units, cums):
        tot = cum[0:1] if rev else cum[c - 1:c]
        ginv = jnp.exp(-cum)
        tail = jnp.exp(tot - cum)
        st.append(dict(
            rev=rev, etot=jnp.exp(tot),
            a_t=two_heads(-kk * jnp.exp(cum - ld)), r_t=two_heads(r * jnp.exp(cum)),
            bk=jnp.concatenate([two_heads(beta * ginv), two_heads(kt * ginv)], axis=0),
            bkg=jnp.concatenate([two_heads(beta * tail), two_heads(kt * tail)], axis=0),
            v2=jnp.concatenate([v, v], axis=0), vh=two_heads(v)))
    grams = [_dot_nt(jnp.concatenate([s["a_t"], s["r_t"]], axis=0), s["bk"]) for s in st]
    for s, gram in zip(st, grams):
        strict, incl = masks[s["rev"]]
        s["l_ab"] = jnp.where(strict, gram[:c2, :c2], 0.0)
        s["l_ak"] = jnp.where(strict, gram[:c2, c2:], 0.0)
        s["t_rb"] = jnp.where(incl, gram[c2:, :c2], 0.0)
        s["t_rk"] = jnp.where(incl, gram[c2:, c2:], 0.0)
    lvs = [_dot(s["l_ak"], s["v2"]) for s in st]
    minvs = [eye for _ in st]
    b = 1
    while b < c:
        same = (row // (2 * b)) == (col // (2 * b))
        es = []
        for s in st:
            first, second = (col % (2 * b)) < b, (row % (2 * b)) >= b
            if s["rev"]:
                first, second = (row % (2 * b)) < b, (col % (2 * b)) >= b
            es.append(jnp.where(same & first & second, s["l_ab"], 0.0))
        if b == 1:
            minvs = [m + e for m, e in zip(minvs, es)]
        else:
            half = [_dot(m, e) for m, e in zip(minvs, es)]
            minvs = [m + _dot(h, m) for m, h in zip(minvs, half)]
        b *= 2
    mms = [_dot(m, jnp.concatenate([s["a_t"], lv], axis=1)) for m, s, lv in zip(minvs, st, lvs)]
    zero = jnp.zeros((c2, LANES), BF16)
    tts = [_dot(jnp.concatenate([s["t_rb"], s["t_rk"]], axis=1),
                jnp.concatenate([mm.astype(BF16), jnp.concatenate([zero, s["v2"].astype(BF16)], axis=1)], axis=0))
           for s, mm in zip(st, mms)]
    ps = [_dot_tn(mm[:, :LANES], s["bkg"][:c2]) for s, mm in zip(st, mms)]
    qs = [_dot_tn(jnp.concatenate([jnp.where(own_lanes, mm[:, LANES:], 0.0), s["vh"]], axis=0), s["bkg"])
          for s, mm in zip(st, mms)]
    out = []
    for s, p, q, tt in zip(st, ps, qs, tts):
        reff = s["r_t"] + tt[:, :LANES]
        out.append(dict(etot=s["etot"], p=p, q=q, reff=reff[:c] + reff[c:],
                        y0=jnp.where(head_a, tt[:c, LANES:], tt[c:, LANES:])))
    return out


def _rwkv_scan_kernel(*refs, seq_len, has_init, want_final):
    r_ref, v_ref, kk_ref, ld0_ref, b0_ref, kt0_ref, ld1_ref, b1_ref, kt1_ref = refs[:9]
    pos = 9
    if has_init:
        s0_ref = refs[pos]
        pos += 1
    y_ref = refs[pos]
    pos += 1
    if want_final:
        sf_ref = refs[pos]
        pos += 1
    g_scr, yb_scr = refs[pos:pos + 2]
    nc = seq_len // CHUNK
    ngroups = nc // SCAN_UNROLL
    per_dir = ((ld0_ref, b0_ref, kt0_ref), (ld1_ref, b1_ref, kt1_ref))
    y_dst = (y_ref, yb_scr)
    nchain = 2 * SCAN_PAIRS

    def group(gi, states):
        where, units = [], []
        for u in range(SCAN_UNROLL):
            for pp in range(SCAN_PAIRS):
                lanes = slice(pp * LANES, (pp + 1) * LANES)
                for d in range(2):
                    ld_ref, b_ref, kt_ref = per_dir[d]
                    cidx = gi * SCAN_UNROLL + u
                    if d == 1:
                        cidx = nc - 1 - cidx
                    start = cidx * CHUNK
                    rw = slice(start, start + CHUNK) if isinstance(start, int) else pl.ds(
                        pl.multiple_of(start, CHUNK), CHUNK)
                    where.append((rw, lanes))
                    units.append(tuple(ref[rw, lanes].astype(F32) for ref in
                                       (ld_ref, kk_ref, b_ref, kt_ref, r_ref, v_ref)) + (d == 1,))
        pre = _scan_precompute(units)
        states = list(states)
        for u in range(SCAN_UNROLL):
            cur = pre[u * nchain:(u + 1) * nchain]
            ys = [_dot_nt(cu["reff"], g) for cu, g in zip(cur, states)]
            gp = [_dot(g, cu["p"]) for cu, g in zip(cur, states)]
            states = [states[ch] * cur[ch]["etot"] + gp[ch] + cur[ch]["q"] for ch in range(nchain)]
            for ch in range(nchain):
                rw, lanes = where[u * nchain + ch]
                y_dst[ch % 2][rw, lanes] = ys[ch] + cur[ch]["y0"]
        return states

    init = [_block_diag2(s0_ref[0, ch % 2, 2 * (ch // 2)], s0_ref[0, ch % 2, 2 * (ch // 2) + 1])
            if has_init else jnp.zeros((LANES, LANES), F32) for ch in range(nchain)]
    if ngroups == 1:
        final = group(0, init)
    else:
        for ch in range(nchain):
            g_scr[ch] = init[ch]

        def body(gi, carry):
            new = group(gi, [g_scr[ch] for ch in range(nchain)])
            for ch in range(nchain):
                g_scr[ch] = new[ch]
            return carry

        lax.fori_loop(0, ngroups, body, 0)
        final = [g_scr[ch] for ch in range(nchain)]
    y_ref[...] += yb_scr[...]
    if want_final:
        for ch in range(nchain):
            sf_ref[0, ch % 2, 2 * (ch // 2)] = final[ch][:HS, :HS]
            sf_ref[0, ch % 2, 2 * (ch // 2) + 1] = final[ch][HS:, HS:]


def _rwkv_scan(proj, s_init, n_seq, seq_len, want_final):
    r, v, kk, ld0, b0, kt0, ld1, b1, kt1 = proj[:9]
    n = r.shape[0]
    npair = D // LANES
    width = SCAN_PAIRS * LANES
    blk = pl.BlockSpec((seq_len, width), lambda b, p: (b, p))
    st_spec = pl.BlockSpec((1, 2, 2 * SCAN_PAIRS, HS, HS), lambda b, p: (b, 0, p, 0, 0))
    in_specs = [blk] * 9
    args = [r, v, kk, ld0, b0, kt0, ld1, b1, kt1]
    has_init = s_init is not None
    if has_init:
        in_specs.append(st_spec)
        args.append(s_init)
    out_specs = [blk]
    out_shape = [jax.ShapeDtypeStruct((n, D), F32)]
    if want_final:
        out_specs.append(st_spec)
        out_shape.append(jax.ShapeDtypeStruct((n_seq, 2, NH, HS, HS), F32))
    res = pl.pallas_call(
        functools.partial(_rwkv_scan_kernel, seq_len=seq_len, has_init=has_init, want_final=want_final),
        grid=(n_seq, npair // SCAN_PAIRS),
        in_specs=in_specs,
        out_specs=out_specs,
        out_shape=out_shape,
        scratch_shapes=[pltpu.VMEM((2 * SCAN_PAIRS, LANES, LANES), F32), pltpu.VMEM((seq_len, width), F32)],
        compiler_params=_cparams(("arbitrary", "arbitrary")),
        name="rwkv_scan",
    )(*args)
    return res


def _rwkv_out_kernel(x_ref, m_ref, g_ref, y_ref, bonus_ref, gate_ref, lng_ref, lnb_ref, wo_ref, o_ref,
                     acc_scr):
    j = pl.program_id(1)

    @pl.when(j == 0)
    def _():
        acc_scr[...] = jnp.zeros_like(acc_scr)

    e = _head_indicator(RW_CW)
    y = y_ref[...]
    mean = _dot_hilo_rhs(y, e) * (1.0 / HS)
    yc = y - mean
    var = _dot(yc * yc, e) * (1.0 / HS)
    yn = yc * lax.rsqrt(var + GN_EPS) * lng_ref[...] + lnb_ref[...]
    yn = (yn + bonus_ref[...]) * gate_ref[...]
    acc_scr[...] += _dot(yn, wo_ref[0])

    @pl.when(j == pl.num_programs(1) - 1)
    def _():
        m = m_ref[0, 0]
        g = g_ref[0]
        o_ref[...] = x_ref[...] + m[2:3] * _rms(acc_scr[...], g[1:2])


def _rwkv_out(x, mod, norm_g, y, bonus, gate, p, layer, j, group_of_tile):
    n = x.shape[0]
    nj = D // RW_CW
    tile = pl.BlockSpec((TM, RW_CW), lambda i, c: (i, c))
    vec = pl.BlockSpec((1, RW_CW), lambda i, c: (j, c))
    return pl.pallas_call(
        _rwkv_out_kernel,
        grid=(n // TM, nj),
        in_specs=[
            pl.BlockSpec((TM, D), lambda i, c: (i, 0)),
            _mod_spec(layer, group_of_tile, 2),
            _normg_spec(layer, 2),
            tile, tile, tile, vec, vec,
            pl.BlockSpec((1, RW_CW, D), lambda i, c: (j, c, 0)),
        ],
        out_specs=pl.BlockSpec((TM, D), lambda i, c: (i, 0)),
        out_shape=jax.ShapeDtypeStruct((n, D), F32),
        scratch_shapes=[pltpu.VMEM((TM, D), F32)],
        compiler_params=_cparams(("arbitrary", "arbitrary")),
        name="rwkv_out",
    )(x, mod, norm_g, y, bonus, gate, p["ln_g"], p["ln_b"], p["w_o"])


def _rope_tables(n):
    rows = n // GRID_W
    row = np.repeat(np.arange(rows), GRID_W)
    col = np.tile(np.arange(GRID_W), rows)
    pos = np.stack([row, col], axis=-1).astype(np.float64)
    quarter = D_ROPE // 4
    inv = ROPE_BASE ** (-np.arange(quarter, dtype=np.float64) / quarter)
    ang = pos[:, :, None] * inv
    cos = np.cos(ang)
    sin = np.sin(ang)
    cos_t = np.concatenate([cos, cos], axis=-1).reshape(n, D_ROPE)
    sin_t = np.concatenate([-sin, sin], axis=-1).reshape(n, D_ROPE)
    return cos_t.astype(np.float32), sin_t.astype(np.float32)


def _rope_swap_perm():
    quarter = D_ROPE // 4
    base = np.arange(D_ROPE)
    return np.where((base % (2 * quarter)) < quarter, base + quarter, base - quarter)


def _mla_proj_kernel(*refs, positional):
    (x_ref, m_ref, g_ref, wdq_ref, gq_ref, wqn_ref, wqr_ref, wqs_ref, wdkv_ref, wkr_ref, wks_ref,
     gkv_ref) = refs[:12]
    pos = 12
    if positional:
        cosq_ref, sinq_ref, cosk_ref, sink_ref = refs[pos:pos + 4]
        pos += 4
    qn_o, qr_o, ckv_o, kr_o = refs[pos:pos + 4]
    m = m_ref[0, 0]
    g = g_ref[0]
    h = (_rms(x_ref[...], g[0:1]) * (1.0 + m[1:2]) + m[0:1]).astype(BF16)
    ql = jnp.dot(h, wdq_ref[0].astype(BF16), preferred_element_type=F32)
    ql = (ql * lax.rsqrt(jnp.mean(ql * ql, axis=-1, keepdims=True) + EPS) * gq_ref[...]).astype(BF16)
    qn_o[...] = jnp.dot(ql, wqn_ref[...].astype(BF16), preferred_element_type=F32)
    qr = jnp.dot(ql, wqr_ref[...].astype(BF16), preferred_element_type=F32)
    ckv = jnp.dot(h, wdkv_ref[...].astype(BF16), preferred_element_type=F32)
    ckv_o[...] = ckv * lax.rsqrt(jnp.mean(ckv * ckv, axis=-1, keepdims=True) + EPS) * gkv_ref[...]
    kr = jnp.dot(h, wkr_ref[...].astype(BF16), preferred_element_type=F32)
    if positional:
        qs = jnp.dot(ql, wqs_ref[...].astype(BF16), preferred_element_type=F32)
        ks = jnp.dot(h, wks_ref[...].astype(BF16), preferred_element_type=F32)
        qr = qr * cosq_ref[...] + qs * sinq_ref[...]
        kr = kr * cosk_ref[...] + ks * sink_ref[...]
    qr_o[...] = qr
    kr_o[...] = kr


def _mla_proj(x, mod, norm_g, p, layer, j, positional, group_of_tile):
    n = x.shape[0]
    full = lambda shape: pl.BlockSpec(shape, lambda i: (0,) * len(shape))
    in_specs = [
        pl.BlockSpec((TM, D), lambda i: (i, 0)),
        _mod_spec(layer, group_of_tile, 1),
        _normg_spec(layer, 1),
        pl.BlockSpec((1, D, Q_RANK), lambda i: (j, 0, 0)),
        pl.BlockSpec((1, Q_RANK), lambda i: (j, 0)),
        full((Q_RANK, MLA_H * D_NOPE)), full((Q_RANK, MLA_H * D_ROPE)), full((Q_RANK, MLA_H * D_ROPE)),
        full((D, KV_RANK)), full((D, D_ROPE)), full((D, D_ROPE)),
        pl.BlockSpec((1, KV_RANK), lambda i: (j, 0)),
    ]
    args = [x, mod, norm_g, p["w_dq"], p["g_q"], p["w_uq_nope"], p["w_uq_rope"], p["w_uq_rope_sw"],
            p["w_dkv_c"], p["w_dkv_r"], p["w_dkv_r_sw"], p["g_kv"]]
    if positional:
        cos_t, sin_t = _rope_tables(TM)
        in_specs += [full((TM, MLA_H * D_ROPE)), full((TM, MLA_H * D_ROPE)),
                     full((TM, D_ROPE)), full((TM, D_ROPE))]
        args += [jnp.asarray(np.tile(cos_t, (1, MLA_H))), jnp.asarray(np.tile(sin_t, (1, MLA_H))),
                 jnp.asarray(cos_t), jnp.asarray(sin_t)]
    widths = (MLA_H * D_NOPE, MLA_H * D_ROPE, KV_RANK, D_ROPE)
    return pl.pallas_call(
        functools.partial(_mla_proj_kernel, positional=positional),
        grid=(n // TM,),
        in_specs=in_specs,
        out_specs=[pl.BlockSpec((TM, w), lambda i: (i, 0)) for w in widths],
        out_shape=[jax.ShapeDtypeStruct((n, w), F32) for w in widths],
        compiler_params=_cparams(("arbitrary",)),
        name="mla_proj",
    )(*args)


def _mla_attn_kernel(x_ref, m_ref, g_ref, qn_ref, qr_ref, ckv_ref, kr_ref, wuk_ref, wuv_ref, wo_ref,
                     o_ref, kn_scr, vv_scr, oh_scr, *, nb, tq, k_len):
    qi = pl.program_id(1)

    @pl.when(qi == 0)
    def _():
        ckv = ckv_ref[...].astype(BF16)
        kn_scr[...] = jnp.dot(ckv, wuk_ref[0].astype(BF16), preferred_element_type=F32).astype(BF16)
        vv_scr[...] = jnp.dot(ckv, wuv_ref[0].astype(BF16), preferred_element_type=F32).astype(BF16)

    units = [(b, hd) for b in range(nb) for hd in range(MLA_H)]
    scores = []
    for b, hd in units:
        qrows = slice(b * tq, (b + 1) * tq)
        krows = slice(b * k_len, (b + 1) * k_len)
        q = jnp.concatenate([qn_ref[qrows, hd * D_NOPE:(hd + 1) * D_NOPE].astype(BF16),
                             qr_ref[qrows, hd * D_ROPE:(hd + 1) * D_ROPE].astype(BF16)], axis=1)
        k = jnp.concatenate([kn_scr[krows, hd * D_NOPE:(hd + 1) * D_NOPE],
                             kr_ref[krows, :].astype(BF16)], axis=1)
        scores.append(_dot_nt(q, k) * MLA_SCALE)
    probs = []
    for s in scores:
        pexp = jnp.exp(s - jnp.max(s, axis=-1, keepdims=True))
        probs.append((pexp / jnp.sum(pexp, axis=-1, keepdims=True)).astype(BF16))
    for (b, hd), pr in zip(units, probs):
        oh_scr[b * tq:(b + 1) * tq, hd * D_V:(hd + 1) * D_V] = jnp.dot(
            pr, vv_scr[b * k_len:(b + 1) * k_len, hd * D_V:(hd + 1) * D_V],
            preferred_element_type=F32).astype(BF16)
    o = jnp.dot(oh_scr[...], wo_ref[0].astype(BF16), preferred_element_type=F32)
    m = m_ref[0, 0]
    g = g_ref[0]
    o_ref[...] = x_ref[...] + m[2:3] * _rms(o, g[1:2])


def _mla_attn(x, mod, norm_g, qn, qr, ckv_all, kr_all, p, layer, j, n_seq, q_len, k_len, nb, tq,
              group_of_step):
    n = x.shape[0]
    nq = q_len // tq
    assert nb == 1 or nq == 1
    return pl.pallas_call(
        functools.partial(_mla_attn_kernel, nb=nb, tq=tq, k_len=k_len),
        grid=(n_seq // nb, nq),
        in_specs=[
            pl.BlockSpec((nb * tq, D), lambda s, q: (s * nq + q, 0)),
            pl.BlockSpec((1, 1, N_MOD, D), lambda s, q: (layer, group_of_step(s), 0, 0)),
            pl.BlockSpec((1, 4, D), lambda s, q: (layer, 0, 0)),
            pl.BlockSpec((nb * tq, MLA_H * D_NOPE), lambda s, q: (s * nq + q, 0)),
            pl.BlockSpec((nb * tq, MLA_H * D_ROPE), lambda s, q: (s * nq + q, 0)),
            pl.BlockSpec((nb * k_len, KV_RANK), lambda s, q: (s, 0)),
            pl.BlockSpec((nb * k_len, D_ROPE), lambda s, q: (s, 0)),
            pl.BlockSpec((1, KV_RANK, MLA_H * D_NOPE), lambda s, q: (j, 0, 0)),
            pl.BlockSpec((1, KV_RANK, MLA_H * D_V), lambda s, q: (j, 0, 0)),
            pl.BlockSpec((1, MLA_H * D_V, D), lambda s, q: (j, 0, 0)),
        ],
        out_specs=pl.BlockSpec((nb * tq, D), lambda s, q: (s * nq + q, 0)),
        out_shape=jax.ShapeDtypeStruct((n, D), F32),
        scratch_shapes=[pltpu.VMEM((nb * k_len, MLA_H * D_NOPE), BF16),
                        pltpu.VMEM((nb * k_len, MLA_H * D_V), BF16),
                        pltpu.VMEM((nb * tq, MLA_H * D_V), BF16)],
        compiler_params=_cparams(("arbitrary", "arbitrary")),
        name="mla_attn",
    )(x, mod, norm_g, qn, qr, ckv_all, kr_all, p["w_uk"], p["w_uv"], p["w_o"])


def kernel(x_prompt, x_sample, state_rwkv, cache_mla_ckv, cache_mla_krope, c, c_ctx, mod_w, mod_b, norm_g,
           mlp_w1, mlp_w2, fft_w_out, conv_w_in, conv_w, conv_w_out, rwkv_mu, rwkv_w_r, rwkv_w_k, rwkv_w_v,
           rwkv_w_o, rwkv_w0, rwkv_w_l1, rwkv_w_l2, rwkv_a0, rwkv_a_l1, rwkv_a_l2, rwkv_g_l1, rwkv_g_l2,
           rwkv_k_k, rwkv_k_a, rwkv_r_k, rwkv_ln_g, rwkv_ln_b, mla_w_dq, mla_g_q, mla_w_uq, mla_w_dkv,
           mla_g_kv, mla_w_uk, mla_w_uv, mla_w_o):
    batch, seq, _ = x_prompt.shape
    dec_batch, dec_seq, _ = x_sample.shape
    past_len = cache_mla_ckv.shape[2]
    assert (batch * seq) % TM == 0 and TM % seq == 0 and dec_seq == TM and seq % CHUNK == 0

    xp = x_prompt.reshape(batch * seq, D)
    xs = x_sample.reshape(dec_batch * dec_seq, D)
    cs = jnp.concatenate([c_ctx[None, :], c, jnp.zeros((8 - 1 - dec_batch, D), F32)], axis=0)
    mod = _modulation(cs, mod_w, mod_b)

    grp_p = lambda i: 0
    grp_s = lambda i: 1 + i
    new_rwkv, new_ckv, new_krope = [], [], []
    streams = ((True, seq, batch, grp_p), (False, dec_seq, dec_batch, grp_s))

    for i in range(DEPTH):
        kind, j = i % 4, i // 4
        outs = []
        for is_prompt, slen, nseq, grp in streams:
            x = xp if is_prompt else xs
            if kind == 0:
                x = _fnet(x, mod, norm_g, fft_w_out, i, j, slen, grp)
            elif kind == 1:
                x = _conv(x, mod, norm_g, conv_w_in, conv_w, conv_w_out, i, j, slen, grp)
            elif kind == 2:
                p = dict(mu=rwkv_mu, w_r=rwkv_w_r, w_k=rwkv_w_k, w_v=rwkv_w_v, w_o=rwkv_w_o, w0=rwkv_w0,
                         w_l1=rwkv_w_l1, w_l2=rwkv_w_l2, a0=rwkv_a0, a_l1=rwkv_a_l1, a_l2=rwkv_a_l2,
                         g_l1=rwkv_g_l1, g_l2=rwkv_g_l2, k_k=rwkv_k_k, k_a=rwkv_k_a,
                         r_k=rwkv_r_k.reshape(-1, D), ln_g=rwkv_ln_g, ln_b=rwkv_ln_b)
                proj = _rwkv_proj(x, mod, norm_g, p, i, j, slen, grp)
                if is_prompt:
                    y, s_fin = _rwkv_scan(proj, None, nseq, slen, True)
                    new_rwkv.append(s_fin)
                else:
                    (y,) = _rwkv_scan(proj, state_rwkv[:, j], nseq, slen, False)
                x = _rwkv_out(x, mod, norm_g, y, proj[10], proj[9], p, i, j, grp)
            else:
                perm = _rope_swap_perm()
                w_uq = mla_w_uq[j].reshape(Q_RANK, MLA_H, D_NOPE + D_ROPE)
                w_uq_rope = w_uq[:, :, D_NOPE:]
                w_dkv_r = mla_w_dkv[j][:, KV_RANK:]
                p = dict(w_dq=mla_w_dq, g_q=mla_g_q, g_kv=mla_g_kv, w_uk=mla_w_uk, w_uv=mla_w_uv, w_o=mla_w_o,
                         w_uq_nope=w_uq[:, :, :D_NOPE].reshape(Q_RANK, MLA_H * D_NOPE),
                         w_uq_rope=w_uq_rope.reshape(Q_RANK, MLA_H * D_ROPE),
                         w_uq_rope_sw=w_uq_rope[:, :, perm].reshape(Q_RANK, MLA_H * D_ROPE),
                         w_dkv_c=mla_w_dkv[j][:, :KV_RANK], w_dkv_r=w_dkv_r, w_dkv_r_sw=w_dkv_r[:, perm])
                qn, qr, ckv, kr = _mla_proj(x, mod, norm_g, p, i, j, not is_prompt, grp)
                if is_prompt:
                    new_ckv.append(ckv.reshape(batch, seq, KV_RANK))
                    new_krope.append(kr.reshape(batch, seq, D_ROPE))
                    x = _mla_attn(x, mod, norm_g, qn, qr, ckv, kr, p, i, j, nseq, slen, slen, TM // slen, slen,
                                  lambda s: 0)
                else:
                    klen = past_len + slen
                    ckv_all = jnp.concatenate([cache_mla_ckv[:, j], ckv.reshape(nseq, slen, KV_RANK)], axis=1)
                    kr_all = jnp.concatenate([cache_mla_krope[:, j], kr.reshape(nseq, slen, D_ROPE)], axis=1)
                    x = _mla_attn(x, mod, norm_g, qn, qr, ckv_all.reshape(nseq * klen, KV_RANK),
                                  kr_all.reshape(nseq * klen, D_ROPE), p, i, j, nseq, slen, klen, 1, 256,
                                  lambda s: 1 + s)
            x = _mlp(x, mod, norm_g, mlp_w1, mlp_w2, i, grp)
            outs.append(x)
        xp, xs = outs

    return (xp.reshape(batch, seq, D), xs.reshape(dec_batch, dec_seq, D),
            jnp.stack(new_rwkv, axis=1), jnp.stack(new_ckv, axis=1), jnp.stack(new_krope, axis=1))
```

```python
import functools
from typing import NamedTuple, Optional

import numpy as np
import jax
import jax.numpy as jnp
from jax import lax
from jax.experimental import pallas as pl
from jax.experimental.pallas import tpu as pltpu

D = 1024
DEPTH = 4
N_MOD = 6
D_FF = 4 * D
EPS = 1e-6
GRID_W = 64
FFT_GROUPS = 8
FFT_GW = D // FFT_GROUPS
HS = 64
NH = D // HS
LORA_W = 64
LORA_A = 64
LORA_G = 128
GN_EPS = 64e-5
MLA_H = 8
D_NOPE = 128
D_ROPE = 64
D_V = 128
KV_RANK = 256
Q_RANK = 384
ROPE_BASE = 10000.0
MLA_SCALE = (D_NOPE + D_ROPE) ** -0.5

F32 = jnp.float32
BF16 = jnp.bfloat16

TM = 1024
LANES = 128
CHUNK = 64
SCAN_UNROLL = 4
SCAN_PAIRS = 2
ATTN_LOOKAHEAD = 2
VMEM_LIMIT = 56 * 1024 * 1024


def _cparams(sem):
    return pltpu.CompilerParams(dimension_semantics=sem, vmem_limit_bytes=VMEM_LIMIT)


def _dot(a, b):
    return jnp.dot(a.astype(BF16), b.astype(BF16), preferred_element_type=F32)


def _dot_nt(a, b):
    return lax.dot_general(a.astype(BF16), b.astype(BF16), (((1,), (1,)), ((), ())),
                           preferred_element_type=F32)


def _dot_tn(a, b):
    return lax.dot_general(a.astype(BF16), b.astype(BF16), (((0,), (0,)), ((), ())),
                           preferred_element_type=F32)


def _block_diag2(a, b):
    za = jnp.zeros((a.shape[0], b.shape[1]), a.dtype)
    zb = jnp.zeros((b.shape[0], a.shape[1]), a.dtype)
    return jnp.concatenate([jnp.concatenate([a, za], axis=1), jnp.concatenate([zb, b], axis=1)], axis=0)


def _cumsum_rows(x, reverse):
    n = x.shape[0]
    idx = lax.broadcasted_iota(jnp.int32, x.shape, 0)
    s = 1
    while s < n:
        if reverse:
            x = x + jnp.where(idx < n - s, pltpu.roll(x, n - s, 0), 0.0)
        else:
            x = x + jnp.where(idx >= s, pltpu.roll(x, s, 0), 0.0)
        s *= 2
    return x


def _dot_hilo_rhs(x, e):
    hi = x.astype(BF16)
    lo = (x - hi.astype(F32)).astype(BF16)
    return jnp.dot(hi, e, preferred_element_type=F32) + jnp.dot(lo, e, preferred_element_type=F32)


def _rms(x, g):
    return x * lax.rsqrt(jnp.mean(x * x, axis=-1, keepdims=True) + EPS) * g


def _sigmoid(x):
    return 1.0 / (1.0 + jnp.exp(-x))


def _softplus(x):
    return jnp.maximum(x, 0.0) + jnp.log(1.0 + jnp.exp(-jnp.abs(x)))


def _head_indicator(n):
    r = lax.broadcasted_iota(jnp.int32, (n, n), 0) // HS
    c = lax.broadcasted_iota(jnp.int32, (n, n), 1) // HS
    return jnp.where(r == c, 1.0, 0.0).astype(BF16)


def _shift_rows(z, seq_len):
    n = z.shape[0]
    pos = lax.broadcasted_iota(jnp.int32, z.shape, 0) % seq_len
    prev = jnp.where(pos == 0, 0.0, pltpu.roll(z, 1, 0))
    nxt = jnp.where(pos == seq_len - 1, 0.0, pltpu.roll(z, n - 1, 0))
    return prev, nxt


MOD_TN = 1536


def _mod_kernel(cs_ref, w_ref, b_ref, o_ref):
    cs = cs_ref[...]
    s = cs * _sigmoid(cs)
    o_ref[0] = _dot(s, w_ref[0]) + b_ref[0]


def _modulation(cs, mod_w, mod_b):
    nj = (N_MOD * D) // MOD_TN
    out = pl.pallas_call(
        _mod_kernel,
        grid=(DEPTH, nj),
        in_specs=[
            pl.BlockSpec((8, D), lambda l, j: (0, 0)),
            pl.BlockSpec((1, D, MOD_TN), lambda l, j: (l, 0, j)),
            pl.BlockSpec((1, 1, MOD_TN), lambda l, j: (l, 0, j)),
        ],
        out_specs=pl.BlockSpec((1, 8, MOD_TN), lambda l, j: (l, 0, j)),
        out_shape=jax.ShapeDtypeStruct((DEPTH, 8, N_MOD * D), F32),
        compiler_params=_cparams(("arbitrary", "arbitrary")),
        name="modulation",
    )(cs, mod_w, mod_b.reshape(DEPTH, 1, N_MOD * D))
    return out.reshape(DEPTH, 8, N_MOD, D)


class _Rows(NamedTuple):
    array: jax.Array
    first: int
    ntiles: int


class _Dest(NamedTuple):
    rows: int
    first: int
    base: Optional[jax.Array]


def _without_ref(kernel, pos):
    def call(*refs):
        return kernel(*refs[:pos], *refs[pos + 1:])
    return call


def _activation_call(kernel, *, grid, in_specs, args, place, block_rows, out_index, scratch_shapes, name):
    out_rows, base = place
    aliases = {}
    if base is not None:
        pos = len(in_specs)
        in_specs = list(in_specs) + [pl.BlockSpec(memory_space=pl.ANY)]
        args = list(args) + [base]
        aliases = {pos: 0}
        kernel = _without_ref(kernel, pos)
    return pl.pallas_call(
        kernel, grid=grid, in_specs=in_specs,
        out_specs=pl.BlockSpec((block_rows, D), out_index),
        out_shape=jax.ShapeDtypeStruct((out_rows, D), F32),
        scratch_shapes=scratch_shapes, input_output_aliases=aliases,
        compiler_params=_cparams(("arbitrary",) * len(grid)), name=name)(*args)


def _mod_spec(layer, group_of_tile, ngrid):
    if ngrid == 1:
        return pl.BlockSpec((1, 1, N_MOD, D), lambda i: (layer, group_of_tile(i), 0, 0))
    return pl.BlockSpec((1, 1, N_MOD, D), lambda i, j: (layer, group_of_tile(i), 0, 0))


def _normg_spec(layer, ngrid):
    if ngrid == 1:
        return pl.BlockSpec((1, 4, D), lambda i: (layer, 0, 0))
    return pl.BlockSpec((1, 4, D), lambda i, j: (layer, 0, 0))


MLP_FC = 1024


def _mlp_kernel(x_ref, m_ref, g_ref, w1_ref, w2_ref, o_ref, h_scr, acc_scr):
    j = pl.program_id(1)
    m = m_ref[0, 0]
    g = g_ref[0]

    @pl.when(j == 0)
    def _():
        h = _rms(x_ref[...], g[2:3]) * (1.0 + m[4:5]) + m[3:4]
        h_scr[...] = h.astype(BF16)
        acc_scr[...] = jnp.zeros_like(acc_scr)

    a = jnp.dot(h_scr[...], w1_ref[0].astype(BF16), preferred_element_type=F32)
    a = jnp.maximum(a, 0.0)
    a = a * a
    acc_scr[...] += _dot(a, w2_ref[0])

    @pl.when(j == pl.num_programs(1) - 1)
    def _():
        o_ref[...] = x_ref[...] + m[5:6] * _rms(acc_scr[...], g[3:4])


def _mlp(src, dst, mod, norm_g, w1, w2, layer, group_of_tile):
    nj = D_FF // MLP_FC
    return _activation_call(
        _mlp_kernel,
        grid=(src.ntiles, nj),
        in_specs=[
            pl.BlockSpec((TM, D), lambda i, j: (i + src.first, 0)),
            _mod_spec(layer, group_of_tile, 2),
            _normg_spec(layer, 2),
            pl.BlockSpec((1, D, MLP_FC), lambda i, j: (layer, 0, j)),
            pl.BlockSpec((1, MLP_FC, D), lambda i, j: (layer, j, 0)),
        ],
        args=[src.array, mod, norm_g, w1, w2],
        place=(dst.rows, dst.base), block_rows=TM, out_index=lambda i, j: (i + dst.first, 0),
        scratch_shapes=[pltpu.VMEM((TM, D), BF16), pltpu.VMEM((TM, D), F32)],
        name="mlp")


def _dft_mats(n):
    idx = np.arange(n, dtype=np.int64)
    ang = (2.0 * np.pi / n) * ((idx[:, None] * idx[None, :]) % n).astype(np.float64)
    scale = 1.0 / np.sqrt(n)
    return np.cos(ang) * scale, np.sin(ang) * scale


def _fnet_kernel(x_ref, m_ref, g_ref, cs_ref, cn_ref, sn_ref, w_ref, o_ref, p_scr, q_scr, f_scr,
                 *, seq_len):
    m = m_ref[0, 0]
    g = g_ref[0]
    x = x_ref[...]
    h = (_rms(x, g[0:1]) * (1.0 + m[1:2]) + m[0:1]).astype(BF16)
    cs = cs_ref[...].astype(BF16)
    for gi in range(FFT_GROUPS):
        pq = jnp.dot(h[:, gi * FFT_GW:(gi + 1) * FFT_GW], cs, preferred_element_type=F32)
        p_scr[:, gi * FFT_GW:(gi + 1) * FFT_GW] = pq[:, :FFT_GW].astype(BF16)
        q_scr[:, gi * FFT_GW:(gi + 1) * FFT_GW] = pq[:, FFT_GW:].astype(BF16)
    cn = cn_ref[...].astype(BF16)
    sn = sn_ref[...].astype(BF16)
    for s in range(TM // seq_len):
        rows = slice(s * seq_len, (s + 1) * seq_len)
        f = (jnp.dot(cn, p_scr[rows, :], preferred_element_type=F32)
             - jnp.dot(sn, q_scr[rows, :], preferred_element_type=F32))
        f_scr[rows, :] = f.astype(BF16)
    o = jnp.dot(f_scr[...], w_ref[0].astype(BF16), preferred_element_type=F32)
    o_ref[...] = x + m[2:3] * _rms(o, g[1:2])


def _fnet(src, dst, mod, norm_g, w_out, layer, j, seq_len, group_of_tile):
    cg, sg = _dft_mats(FFT_GW)
    cs = jnp.asarray(np.concatenate([cg, sg], axis=1), F32)
    cn_np, sn_np = _dft_mats(seq_len)
    cn = jnp.asarray(cn_np, F32)
    sn = jnp.asarray(sn_np, F32)
    return _activation_call(
        functools.partial(_fnet_kernel, seq_len=seq_len),
        grid=(src.ntiles,),
        in_specs=[
            pl.BlockSpec((TM, D), lambda i: (i + src.first, 0)),
            _mod_spec(layer, group_of_tile, 1),
            _normg_spec(layer, 1),
            pl.BlockSpec((FFT_GW, 2 * FFT_GW), lambda i: (0, 0)),
            pl.BlockSpec((seq_len, seq_len), lambda i: (0, 0)),
            pl.BlockSpec((seq_len, seq_len), lambda i: (0, 0)),
            pl.BlockSpec((1, D, D), lambda i: (j, 0, 0)),
        ],
        args=[src.array, mod, norm_g, cs, cn, sn, w_out],
        place=(dst.rows, dst.base), block_rows=TM, out_index=lambda i: (i + dst.first, 0),
        scratch_shapes=[pltpu.VMEM((TM, D), BF16), pltpu.VMEM((TM, D), BF16),
                        pltpu.VMEM((TM, D), BF16)],
        name="fourier_mix")


CONV_CW = 512


def _conv_kernel(x_ref, m_ref, g_ref, wb_ref, wc_ref, wu_ref, cw_ref, wo_ref, o_ref, h_scr, acc_scr,
                 *, seq_len):
    j = pl.program_id(1)
    m = m_ref[0, 0]
    g = g_ref[0]

    @pl.when(j == 0)
    def _():
        h = _rms(x_ref[...], g[0:1]) * (1.0 + m[1:2]) + m[0:1]
        h_scr[...] = h.astype(BF16)
        acc_scr[...] = jnp.zeros_like(acc_scr)

    h = h_scr[...]
    hw = CONV_CW // 2
    proj = []
    for s in range(2):
        cols = slice(s * hw, (s + 1) * hw)
        proj.append([jnp.dot(h, w_ref[0, :, cols].astype(BF16), preferred_element_type=F32)
                     for w_ref in (wb_ref, wc_ref, wu_ref)])
    out = None
    for s in range(2):
        cols = slice(s * hw, (s + 1) * hw)
        bg, cg, u = proj[s]
        z = cg * u
        z_prev, z_next = _shift_rows(z, seq_len)
        conv = z_prev * cw_ref[0, 0:1, cols] + z * cw_ref[0, 1:2, cols] + z_next * cw_ref[0, 2:3, cols]
        part = _dot(bg * conv, wo_ref[0, cols, :])
        out = part if out is None else out + part
    acc_scr[...] += out

    @pl.when(j == pl.num_programs(1) - 1)
    def _():
        o_ref[...] = x_ref[...] + m[2:3] * _rms(acc_scr[...], g[1:2])


def _conv(src, dst, mod, norm_g, w_in, w_conv, w_out, layer, j, seq_len, group_of_tile):
    nj = D // CONV_CW
    return _activation_call(
        functools.partial(_conv_kernel, seq_len=seq_len),
        grid=(src.ntiles, nj),
        in_specs=[
            pl.BlockSpec((TM, D), lambda i, c: (i + src.first, 0)),
            _mod_spec(layer, group_of_tile, 2),
            _normg_spec(layer, 2),
            pl.BlockSpec((1, D, CONV_CW), lambda i, c: (j, 0, c)),
            pl.BlockSpec((1, D, CONV_CW), lambda i, c: (j, 0, nj + c)),
            pl.BlockSpec((1, D, CONV_CW), lambda i, c: (j, 0, 2 * nj + c)),
            pl.BlockSpec((1, 3, CONV_CW), lambda i, c: (j, 0, c)),
            pl.BlockSpec((1, CONV_CW, D), lambda i, c: (j, c, 0)),
        ],
        args=[src.array, mod, norm_g, w_in, w_in, w_in, w_conv, w_out],
        place=(dst.rows, dst.base), block_rows=TM, out_index=lambda i, c: (i + dst.first, 0),
        scratch_shapes=[pltpu.VMEM((TM, D), BF16), pltpu.VMEM((TM, D), F32)],
        name="short_conv")


RWP_CW = 256
RW_OUT_DTYPES = (BF16, BF16, BF16, F32, BF16, BF16, F32, BF16, BF16, F32, F32)
RW_CW = 512


def _rwkv_proj_kernel(x_ref, m_ref, g_ref, mu_ref, wr_ref, wk_ref, wv_ref, w0_ref, wl1_ref, wl2_ref,
                      a0_ref, al1_ref, al2_ref, gl1_ref, gl2_ref, kk_ref, ka_ref, rk_ref,
                      r_o, v_o, kk_o, ld0_o, b0_o, kt0_o, ld1_o, b1_o, kt1_o, g_o, bonus_o,
                      xr_scr, xk_scr, xv_scr, tw_scr, ta_scr, sg_scr, *, seq_len):
    j = pl.program_id(1)

    @pl.when(j == 0)
    def _():
        m = m_ref[0, 0]
        g = g_ref[0]
        mu = mu_ref[0]
        h = _rms(x_ref[...], g[0:1]) * (1.0 + m[1:2]) + m[0:1]
        h_prev, h_next = _shift_rows(h, seq_len)
        dx = 0.5 * (h_prev + h_next) - h
        xr_scr[...] = (h + dx * mu[0:1]).astype(BF16)
        xk_scr[...] = (h + dx * mu[2:3]).astype(BF16)
        xv_scr[...] = (h + dx * mu[3:4]).astype(BF16)
        xw = (h + dx * mu[1:2]).astype(BF16)
        xa = (h + dx * mu[4:5]).astype(BF16)
        xg = (h + dx * mu[5:6]).astype(BF16)
        for d in range(2):
            tw_scr[d] = jnp.tanh(_dot(xw, wl1_ref[0, d])).astype(BF16)
            ta_scr[d] = _dot(xa, al1_ref[0, d]).astype(BF16)
        sg_scr[...] = _sigmoid(_dot(xg, gl1_ref[0])).astype(BF16)

    e = _head_indicator(RWP_CW)
    r = jnp.dot(xr_scr[...], wr_ref[0].astype(BF16), preferred_element_type=F32)
    k = jnp.dot(xk_scr[...], wk_ref[0].astype(BF16), preferred_element_type=F32)
    v = jnp.dot(xv_scr[...], wv_ref[0].astype(BF16), preferred_element_type=F32)
    g_o[...] = jnp.dot(sg_scr[...], gl2_ref[0].astype(BF16), preferred_element_type=F32)
    zws = [jnp.dot(tw_scr[d], wl2_ref[0, d].astype(BF16), preferred_element_type=F32) for d in range(2)]
    zas = [jnp.dot(ta_scr[d], al2_ref[0, d].astype(BF16), preferred_element_type=F32) for d in range(2)]
    kk = k * kk_ref[...]
    kk = kk * lax.rsqrt(_dot(kk * kk, e) + 1e-12)
    r_o[...] = r.astype(r_o.dtype)
    v_o[...] = v.astype(v_o.dtype)
    kk_o[...] = kk.astype(kk_o.dtype)
    k_a = ka_ref[...]
    kt_sum = None
    for d, (ld_o, b_o, kt_o) in enumerate(((ld0_o, b0_o, kt0_o), (ld1_o, b1_o, kt1_o))):
        logw = -_softplus(-(w0_ref[0, d:d + 1, :] + zws[d])) - 0.5
        ld_o[...] = -jnp.exp(logw)
        a = _sigmoid(a0_ref[0, d:d + 1, :] + zas[d])
        kt = k * (1.0 + (a - 1.0) * k_a)
        b_o[...] = (kk * a).astype(b_o.dtype)
        kt_o[...] = kt.astype(kt_o.dtype)
        kt_sum = kt if kt_sum is None else kt_sum + kt
    bonus_o[...] = _dot(r * rk_ref[...] * kt_sum, e) * v


def _rwkv_proj(src, mod, norm_g, p, layer, j, seq_len, group_of_tile):
    n = src.ntiles * TM
    nj = D // RWP_CW
    tile = pl.BlockSpec((TM, RWP_CW), lambda i, c: (i, c))
    wcol = lambda: pl.BlockSpec((1, D, RWP_CW), lambda i, c: (j, 0, c))
    vec = lambda: pl.BlockSpec((1, RWP_CW), lambda i, c: (j, c))
    return pl.pallas_call(
        functools.partial(_rwkv_proj_kernel, seq_len=seq_len),
        grid=(src.ntiles, nj),
        in_specs=[
            pl.BlockSpec((TM, D), lambda i, c: (i + src.first, 0)),
            _mod_spec(layer, group_of_tile, 2),
            _normg_spec(layer, 2),
            pl.BlockSpec((1, 6, D), lambda i, c: (j, 0, 0)),
            wcol(), wcol(), wcol(),
            pl.BlockSpec((1, 2, RWP_CW), lambda i, c: (j, 0, c)),
            pl.BlockSpec((1, 2, D, LORA_W), lambda i, c: (j, 0, 0, 0)),
            pl.BlockSpec((1, 2, LORA_W, RWP_CW), lambda i, c: (j, 0, 0, c)),
            pl.BlockSpec((1, 2, RWP_CW), lambda i, c: (j, 0, c)),
            pl.BlockSpec((1, 2, D, LORA_A), lambda i, c: (j, 0, 0, 0)),
            pl.BlockSpec((1, 2, LORA_A, RWP_CW), lambda i, c: (j, 0, 0, c)),
            pl.BlockSpec((1, D, LORA_G), lambda i, c: (j, 0, 0)),
            pl.BlockSpec((1, LORA_G, RWP_CW), lambda i, c: (j, 0, c)),
            vec(), vec(), vec(),
        ],
        out_specs=[tile] * len(RW_OUT_DTYPES),
        out_shape=[jax.ShapeDtypeStruct((n, D), dt) for dt in RW_OUT_DTYPES],
        scratch_shapes=[pltpu.VMEM((TM, D), BF16), pltpu.VMEM((TM, D), BF16), pltpu.VMEM((TM, D), BF16),
                        pltpu.VMEM((2, TM, LORA_W), BF16), pltpu.VMEM((2, TM, LORA_A), BF16),
                        pltpu.VMEM((TM, LORA_G), BF16)],
        compiler_params=_cparams(("arbitrary", "arbitrary")),
        name="rwkv_proj",
    )(src.array, mod, norm_g, p["mu"], p["w_r"], p["w_k"], p["w_v"], p["w0"], p["w_l1"], p["w_l2"],
      p["a0"], p["a_l1"], p["a_l2"], p["g_l1"], p["g_l2"], p["k_k"], p["k_a"], p["r_k"])


def _scan_precompute(units):
    c = CHUNK
    c2 = 2 * c
    row = lax.broadcasted_iota(jnp.int32, (c2, c2), 0)
    col = lax.broadcasted_iota(jnp.int32, (c2, c2), 1)
    head_a = lax.broadcasted_iota(jnp.int32, (c, LANES), 1) < HS
    own_lanes = jnp.concatenate([head_a, jnp.logical_not(head_a)], axis=0)
    eye = jnp.where(row == col, 1.0, 0.0)

    def two_heads(t):
        return jnp.concatenate([jnp.where(head_a, t, 0.0), jnp.where(head_a, 0.0, t)], axis=0)

    def causal(reverse):
        if reverse:
            return (col % c) > (row % c), (col % c) >= (row % c)
        return (col % c) < (row % c), (col % c) <= (row % c)

    masks = {rev: causal(rev) for rev in sorted({u[6] for u in units})}

    cums = [_cumsum_rows(u[0], u[6]) for u in units]
    st = []
    for (ld, kk, beta, kt, r, v, rev), cum in zip(units, cums):
        tot = cum[0:1] if rev else cum[c - 1:c]
        ginv = jnp.exp(-cum)
        tail = jnp.exp(tot - cum)
        st.append(dict(
            rev=rev, etot=jnp.exp(tot),
            a_t=two_heads(-kk * jnp.exp(cum - ld)), r_t=two_heads(r * jnp.exp(cum)),
            bk=jnp.concatenate([two_heads(beta * ginv), two_heads(kt * ginv)], axis=0),
            bkg=jnp.concatenate([two_heads(beta * tail), two_heads(kt * tail)], axis=0),
            v2=jnp.concatenate([v, v], axis=0), vh=two_heads(v)))
    grams = [_dot_nt(jnp.concatenate([s["a_t"], s["r_t"]], axis=0), s["bk"]) for s in st]
    for s, gram in zip(st, grams):
        strict, incl = masks[s["rev"]]
        s["l_ab"] = jnp.where(strict, gram[:c2, :c2], 0.0)
        s["l_ak"] = jnp.where(strict, gram[:c2, c2:], 0.0)
        s["t_rb"] = jnp.where(incl, gram[c2:, :c2], 0.0)
        s["t_rk"] = jnp.where(incl, gram[c2:, c2:], 0.0)
    lvs = [_dot(s["l_ak"], s["v2"]) for s in st]
    minvs = [eye for _ in st]
    b = 1
    while b < c:
        same = (row // (2 * b)) == (col // (2 * b))
        es = []
        for s in st:
            first, second = (col % (2 * b)) < b, (row % (2 * b)) >= b
            if s["rev"]:
                first, second = (row % (2 * b)) < b, (col % (2 * b)) >= b
            es.append(jnp.where(same & first & second, s["l_ab"], 0.0))
        if b == 1:
            minvs = [m + e for m, e in zip(minvs, es)]
        else:
            half = [_dot(m, e) for m, e in zip(minvs, es)]
            minvs = [m + _dot(h, m) for m, h in zip(minvs, half)]
        b *= 2
    mms = [_dot(m, jnp.concatenate([s["a_t"], lv], axis=1)) for m, s, lv in zip(minvs, st, lvs)]
    zero = jnp.zeros((c2, LANES), BF16)
    tts = [_dot(jnp.concatenate([s["t_rb"], s["t_rk"]], axis=1),
                jnp.concatenate([mm.astype(BF16), jnp.concatenate([zero, s["v2"].astype(BF16)], axis=1)], axis=0))
           for s, mm in zip(st, mms)]
    ps = [_dot_tn(mm[:, :LANES], s["bkg"][:c2]) for s, mm in zip(st, mms)]
    qs = [_dot_tn(jnp.concatenate([jnp.where(own_lanes, mm[:, LANES:], 0.0), s["vh"]], axis=0), s["bkg"])
          for s, mm in zip(st, mms)]
    out = []
    for s, p, q, tt in zip(st, ps, qs, tts):
        reff = s["r_t"] + tt[:, :LANES]
        out.append(dict(etot=s["etot"], p=p, q=q, reff=reff[:c] + reff[c:],
                        y0=jnp.where(head_a, tt[:c, LANES:], tt[c:, LANES:])))
    return out


def _rwkv_scan_kernel(*refs, seq_len, has_init, want_final):
    r_ref, v_ref, kk_ref, ld0_ref, b0_ref, kt0_ref, ld1_ref, b1_ref, kt1_ref = refs[:9]
    pos = 9
    if has_init:
        s0_ref = refs[pos]
        pos += 1
    y_ref = refs[pos]
    pos += 1
    if want_final:
        sf_ref = refs[pos]
        pos += 1
    g_scr, yb_scr = refs[pos:pos + 2]
    nc = seq_len // CHUNK
    ngroups = nc // SCAN_UNROLL
    per_dir = ((ld0_ref, b0_ref, kt0_ref), (ld1_ref, b1_ref, kt1_ref))
    y_dst = (y_ref, yb_scr)
    nchain = 2 * SCAN_PAIRS

    def group(gi, states):
        where, units = [], []
        for u in range(SCAN_UNROLL):
            for pp in range(SCAN_PAIRS):
                lanes = slice(pp * LANES, (pp + 1) * LANES)
                for d in range(2):
                    ld_ref, b_ref, kt_ref = per_dir[d]
                    cidx = gi * SCAN_UNROLL + u
                    if d == 1:
                        cidx = nc - 1 - cidx
                    start = cidx * CHUNK
                    rw = slice(start, start + CHUNK) if isinstance(start, int) else pl.ds(
                        pl.multiple_of(start, CHUNK), CHUNK)
                    where.append((rw, lanes))
                    units.append(tuple(ref[rw, lanes].astype(F32) for ref in
                                       (ld_ref, kk_ref, b_ref, kt_ref, r_ref, v_ref)) + (d == 1,))
        pre = _scan_precompute(units)
        states = list(states)
        for u in range(SCAN_UNROLL):
            cur = pre[u * nchain:(u + 1) * nchain]
            ys = [_dot_nt(cu["reff"], g) for cu, g in zip(cur, states)]
            gp = [_dot(g, cu["p"]) for cu, g in zip(cur, states)]
            states = [states[ch] * cur[ch]["etot"] + gp[ch] + cur[ch]["q"] for ch in range(nchain)]
            for ch in range(nchain):
                rw, lanes = where[u * nchain + ch]
                y_dst[ch % 2][rw, lanes] = ys[ch] + cur[ch]["y0"]
        return states

    init = [_block_diag2(s0_ref[0, ch % 2, 2 * (ch // 2)], s0_ref[0, ch % 2, 2 * (ch // 2) + 1])
            if has_init else jnp.zeros((LANES, LANES), F32) for ch in range(nchain)]
    if ngroups == 1:
        final = group(0, init)
    else:
        for ch in range(nchain):
            g_scr[ch] = init[ch]

        def body(gi, carry):
            new = group(gi, [g_scr[ch] for ch in range(nchain)])
            for ch in range(nchain):
                g_scr[ch] = new[ch]
            return carry

        lax.fori_loop(0, ngroups, body, 0)
        final = [g_scr[ch] for ch in range(nchain)]
    y_ref[...] += yb_scr[...]
    if want_final:
        for ch in range(nchain):
            sf_ref[0, ch % 2, 2 * (ch // 2)] = final[ch][:HS, :HS]
            sf_ref[0, ch % 2, 2 * (ch // 2) + 1] = final[ch][HS:, HS:]


def _rwkv_scan(proj, s_init, n_seq, seq_len, want_final):
    r, v, kk, ld0, b0, kt0, ld1, b1, kt1 = proj[:9]
    n = r.shape[0]
    npair = D // LANES
    width = SCAN_PAIRS * LANES
    blk = pl.BlockSpec((seq_len, width), lambda b, p: (b, p))
    st_spec = pl.BlockSpec((1, 2, 2 * SCAN_PAIRS, HS, HS), lambda b, p: (b, 0, p, 0, 0))
    in_specs = [blk] * 9
    args = [r, v, kk, ld0, b0, kt0, ld1, b1, kt1]
    has_init = s_init is not None
    if has_init:
        in_specs.append(st_spec)
        args.append(s_init)
    out_specs = [blk]
    out_shape = [jax.ShapeDtypeStruct((n, D), F32)]
    if want_final:
        out_specs.append(st_spec)
        out_shape.append(jax.ShapeDtypeStruct((n_seq, 2, NH, HS, HS), F32))
    res = pl.pallas_call(
        functools.partial(_rwkv_scan_kernel, seq_len=seq_len, has_init=has_init, want_final=want_final),
        grid=(n_seq, npair // SCAN_PAIRS),
        in_specs=in_specs,
        out_specs=out_specs,
        out_shape=out_shape,
        scratch_shapes=[pltpu.VMEM((2 * SCAN_PAIRS, LANES, LANES), F32), pltpu.VMEM((seq_len, width), F32)],
        compiler_params=_cparams(("arbitrary", "arbitrary")),
        name="rwkv_scan",
    )(*args)
    return res


def _rwkv_out_kernel(x_ref, m_ref, g_ref, y_ref, bonus_ref, gate_ref, lng_ref, lnb_ref, wo_ref, o_ref,
                     acc_scr):
    j = pl.program_id(1)

    @pl.when(j == 0)
    def _():
        acc_scr[...] = jnp.zeros_like(acc_scr)

    e = _head_indicator(RW_CW)
    y = y_ref[...]
    mean = _dot_hilo_rhs(y, e) * (1.0 / HS)
    yc = y - mean
    var = _dot(yc * yc, e) * (1.0 / HS)
    yn = yc * lax.rsqrt(var + GN_EPS) * lng_ref[...] + lnb_ref[...]
    yn = (yn + bonus_ref[...]) * gate_ref[...]
    acc_scr[...] += _dot(yn, wo_ref[0])

    @pl.when(j == pl.num_programs(1) - 1)
    def _():
        m = m_ref[0, 0]
        g = g_ref[0]
        o_ref[...] = x_ref[...] + m[2:3] * _rms(acc_scr[...], g[1:2])


def _rwkv_out(src, dst, mod, norm_g, y, bonus, gate, p, layer, j, group_of_tile):
    nj = D // RW_CW
    tile = pl.BlockSpec((TM, RW_CW), lambda i, c: (i, c))
    vec = pl.BlockSpec((1, RW_CW), lambda i, c: (j, c))
    return _activation_call(
        _rwkv_out_kernel,
        grid=(src.ntiles, nj),
        in_specs=[
            pl.BlockSpec((TM, D), lambda i, c: (i + src.first, 0)),
            _mod_spec(layer, group_of_tile, 2),
            _normg_spec(layer, 2),
            tile, tile, tile, vec, vec,
            pl.BlockSpec((1, RW_CW, D), lambda i, c: (j, c, 0)),
        ],
        args=[src.array, mod, norm_g, y, bonus, gate, p["ln_g"], p["ln_b"], p["w_o"]],
        place=(dst.rows, dst.base), block_rows=TM, out_index=lambda i, c: (i + dst.first, 0),
        scratch_shapes=[pltpu.VMEM((TM, D), F32)],
        name="rwkv_out")


def _rope_tables(n):
    rows = n // GRID_W
    row = np.repeat(np.arange(rows), GRID_W)
    col = np.tile(np.arange(GRID_W), rows)
    pos = np.stack([row, col], axis=-1).astype(np.float64)
    quarter = D_ROPE // 4
    inv = ROPE_BASE ** (-np.arange(quarter, dtype=np.float64) / quarter)
    ang = pos[:, :, None] * inv
    cos = np.cos(ang)
    sin = np.sin(ang)
    cos_t = np.concatenate([cos, cos], axis=-1).reshape(n, D_ROPE)
    sin_t = np.concatenate([-sin, sin], axis=-1).reshape(n, D_ROPE)
    return cos_t.astype(np.float32), sin_t.astype(np.float32)


def _rope_swap_perm():
    quarter = D_ROPE // 4
    base = np.arange(D_ROPE)
    return np.where((base % (2 * quarter)) < quarter, base + quarter, base - quarter)


def _mla_proj_kernel(*refs, positional):
    (x_ref, m_ref, g_ref, wdq_ref, gq_ref, wqn_ref, wqr_ref, wqs_ref, wdkv_ref, wkr_ref, wks_ref,
     gkv_ref) = refs[:12]
    pos = 12
    if positional:
        cosq_ref, sinq_ref, cosk_ref, sink_ref = refs[pos:pos + 4]
        pos += 4
    qn_o, qr_o, ckv_o, kr_o = refs[pos:pos + 4]
    m = m_ref[0, 0]
    g = g_ref[0]
    h = (_rms(x_ref[...], g[0:1]) * (1.0 + m[1:2]) + m[0:1]).astype(BF16)
    ql = jnp.dot(h, wdq_ref[0].astype(BF16), preferred_element_type=F32)
    ql = (ql * lax.rsqrt(jnp.mean(ql * ql, axis=-1, keepdims=True) + EPS) * gq_ref[...]).astype(BF16)
    qn_o[...] = jnp.dot(ql, wqn_ref[...].astype(BF16), preferred_element_type=F32)
    qr = jnp.dot(ql, wqr_ref[...].astype(BF16), preferred_element_type=F32)
    ckv = jnp.dot(h, wdkv_ref[...].astype(BF16), preferred_element_type=F32)
    ckv_o[...] = ckv * lax.rsqrt(jnp.mean(ckv * ckv, axis=-1, keepdims=True) + EPS) * gkv_ref[...]
    kr = jnp.dot(h, wkr_ref[...].astype(BF16), preferred_element_type=F32)
    if positional:
        qs = jnp.dot(ql, wqs_ref[...].astype(BF16), preferred_element_type=F32)
        ks = jnp.dot(h, wks_ref[...].astype(BF16), preferred_element_type=F32)
        qr = qr * cosq_ref[...] + qs * sinq_ref[...]
        kr = kr * cosk_ref[...] + ks * sink_ref[...]
    qr_o[...] = qr
    kr_o[...] = kr


def _mla_proj(src, mod, norm_g, p, layer, j, positional, group_of_tile):
    n = src.ntiles * TM
    full = lambda shape: pl.BlockSpec(shape, lambda i: (0,) * len(shape))
    in_specs = [
        pl.BlockSpec((TM, D), lambda i: (i + src.first, 0)),
        _mod_spec(layer, group_of_tile, 1),
        _normg_spec(layer, 1),
        pl.BlockSpec((1, D, Q_RANK), lambda i: (j, 0, 0)),
        pl.BlockSpec((1, Q_RANK), lambda i: (j, 0)),
        full((Q_RANK, MLA_H * D_NOPE)), full((Q_RANK, MLA_H * D_ROPE)), full((Q_RANK, MLA_H * D_ROPE)),
        full((D, KV_RANK)), full((D, D_ROPE)), full((D, D_ROPE)),
        pl.BlockSpec((1, KV_RANK), lambda i: (j, 0)),
    ]
    args = [src.array, mod, norm_g, p["w_dq"], p["g_q"], p["w_uq_nope"], p["w_uq_rope"], p["w_uq_rope_sw"],
            p["w_dkv_c"], p["w_dkv_r"], p["w_dkv_r_sw"], p["g_kv"]]
    if positional:
        cos_t, sin_t = _rope_tables(TM)
        in_specs += [full((TM, MLA_H * D_ROPE)), full((TM, MLA_H * D_ROPE)),
                     full((TM, D_ROPE)), full((TM, D_ROPE))]
        args += [jnp.asarray(np.tile(cos_t, (1, MLA_H))), jnp.asarray(np.tile(sin_t, (1, MLA_H))),
                 jnp.asarray(cos_t), jnp.asarray(sin_t)]
    widths = (MLA_H * D_NOPE, MLA_H * D_ROPE, KV_RANK, D_ROPE)
    return pl.pallas_call(
        functools.partial(_mla_proj_kernel, positional=positional),
        grid=(src.ntiles,),
        in_specs=in_specs,
        out_specs=[pl.BlockSpec((TM, w), lambda i: (i, 0)) for w in widths],
        out_shape=[jax.ShapeDtypeStruct((n, w), F32) for w in widths],
        compiler_params=_cparams(("arbitrary",)),
        name="mla_proj",
    )(*args)


def _mla_attn_kernel(x_ref, m_ref, g_ref, qn_ref, qr_ref, ckv_ref, kr_ref, wuk_ref, wuv_ref, wo_ref,
                     o_ref, kn_scr, vv_scr, oh_scr, *, nb, tq, k_len):
    qi = pl.program_id(1)

    @pl.when(qi == 0)
    def _():
        ckv = ckv_ref[...].astype(BF16)
        kn_scr[...] = jnp.dot(ckv, wuk_ref[0].astype(BF16), preferred_element_type=F32).astype(BF16)
        vv_scr[...] = jnp.dot(ckv, wuv_ref[0].astype(BF16), preferred_element_type=F32).astype(BF16)

    units = [(b, hd) for b in range(nb) for hd in range(MLA_H)]

    def scores(unit):
        b, hd = unit
        qrows = slice(b * tq, (b + 1) * tq)
        krows = slice(b * k_len, (b + 1) * k_len)
        q = jnp.concatenate([qn_ref[qrows, hd * D_NOPE:(hd + 1) * D_NOPE].astype(BF16),
                             qr_ref[qrows, hd * D_ROPE:(hd + 1) * D_ROPE].astype(BF16)], axis=1)
        k = jnp.concatenate([kn_scr[krows, hd * D_NOPE:(hd + 1) * D_NOPE],
                             kr_ref[krows, :].astype(BF16)], axis=1)
        return _dot_nt(q, k) * MLA_SCALE

    pending = [scores(u) for u in units[:ATTN_LOOKAHEAD]]
    for idx, (b, hd) in enumerate(units):
        if idx + ATTN_LOOKAHEAD < len(units):
            pending.append(scores(units[idx + ATTN_LOOKAHEAD]))
        s = pending[idx]
        pexp = jnp.exp(s - jnp.max(s, axis=-1, keepdims=True))
        pv = jnp.dot(pexp.astype(BF16), vv_scr[b * k_len:(b + 1) * k_len, hd * D_V:(hd + 1) * D_V],
                     preferred_element_type=F32)
        oh_scr[b * tq:(b + 1) * tq, hd * D_V:(hd + 1) * D_V] = (
            pv / jnp.sum(pexp, axis=-1, keepdims=True)).astype(BF16)
    o = jnp.dot(oh_scr[...], wo_ref[0].astype(BF16), preferred_element_type=F32)
    m = m_ref[0, 0]
    g = g_ref[0]
    o_ref[...] = x_ref[...] + m[2:3] * _rms(o, g[1:2])


def _mla_attn(src, dst, mod, norm_g, qn, qr, ckv_all, kr_all, p, layer, j, n_seq, q_len, k_len, nb, tq,
              group_of_step):
    nq = q_len // tq
    assert nb == 1 or nq == 1
    rows = nb * tq
    x_first, o_first = src.first * TM // rows, dst.first * TM // rows
    return _activation_call(
        functools.partial(_mla_attn_kernel, nb=nb, tq=tq, k_len=k_len),
        grid=(n_seq // nb, nq),
        in_specs=[
            pl.BlockSpec((rows, D), lambda s, q: (s * nq + q + x_first, 0)),
            pl.BlockSpec((1, 1, N_MOD, D), lambda s, q: (layer, group_of_step(s), 0, 0)),
            pl.BlockSpec((1, 4, D), lambda s, q: (layer, 0, 0)),
            pl.BlockSpec((nb * tq, MLA_H * D_NOPE), lambda s, q: (s * nq + q, 0)),
            pl.BlockSpec((nb * tq, MLA_H * D_ROPE), lambda s, q: (s * nq + q, 0)),
            pl.BlockSpec((nb * k_len, KV_RANK), lambda s, q: (s, 0)),
            pl.BlockSpec((nb * k_len, D_ROPE), lambda s, q: (s, 0)),
            pl.BlockSpec((1, KV_RANK, MLA_H * D_NOPE), lambda s, q: (j, 0, 0)),
            pl.BlockSpec((1, KV_RANK, MLA_H * D_V), lambda s, q: (j, 0, 0)),
            pl.BlockSpec((1, MLA_H * D_V, D), lambda s, q: (j, 0, 0)),
        ],
        args=[src.array, mod, norm_g, qn, qr, ckv_all, kr_all, p["w_uk"], p["w_uv"], p["w_o"]],
        place=(dst.rows, dst.base), block_rows=rows, out_index=lambda s, q: (s * nq + q + o_first, 0),
        scratch_shapes=[pltpu.VMEM((nb * k_len, MLA_H * D_NOPE), BF16),
                        pltpu.VMEM((nb * k_len, MLA_H * D_V), BF16),
                        pltpu.VMEM((nb * tq, MLA_H * D_V), BF16)],
        name="mla_attn")


def kernel(x_prompt, x_sample, state_rwkv, cache_mla_ckv, cache_mla_krope, c, c_ctx, mod_w, mod_b, norm_g,
           mlp_w1, mlp_w2, fft_w_out, conv_w_in, conv_w, conv_w_out, rwkv_mu, rwkv_w_r, rwkv_w_k, rwkv_w_v,
           rwkv_w_o, rwkv_w0, rwkv_w_l1, rwkv_w_l2, rwkv_a0, rwkv_a_l1, rwkv_a_l2, rwkv_g_l1, rwkv_g_l2,
           rwkv_k_k, rwkv_k_a, rwkv_r_k, rwkv_ln_g, rwkv_ln_b, mla_w_dq, mla_g_q, mla_w_uq, mla_w_dkv,
           mla_g_kv, mla_w_uk, mla_w_uv, mla_w_o):
    batch, seq, _ = x_prompt.shape
    dec_batch, dec_seq, _ = x_sample.shape
    past_len = cache_mla_ckv.shape[2]
    assert (batch * seq) % TM == 0 and TM % seq == 0 and dec_seq == TM and seq % CHUNK == 0

    np_tiles = batch * seq // TM
    ns_tiles = dec_batch * dec_seq // TM
    total_rows = (np_tiles + ns_tiles) * TM
    cs = jnp.concatenate([c_ctx[None, :], c, jnp.zeros((8 - 1 - dec_batch, D), F32)], axis=0)
    mod = _modulation(cs, mod_w, mod_b)

    grp_p = lambda i: 0
    grp_s = lambda i: 1 + i
    grp_all = lambda i: jnp.maximum(i - (np_tiles - 1), 0)
    new_rwkv, new_ckv, new_krope = [], [], []
    streams = [(True, seq, batch, grp_p, _Rows(x_prompt.reshape(batch * seq, D), 0, np_tiles)),
               (False, dec_seq, dec_batch, grp_s, _Rows(x_sample.reshape(dec_batch * dec_seq, D), 0, ns_tiles))]

    for i in range(DEPTH):
        kind, j = i % 4, i // 4
        shared = None
        for is_prompt, slen, nseq, grp, src in streams:
            dst = _Dest(total_rows, 0 if is_prompt else np_tiles, shared)
            if kind == 0:
                shared = _fnet(src, dst, mod, norm_g, fft_w_out, i, j, slen, grp)
            elif kind == 1:
                shared = _conv(src, dst, mod, norm_g, conv_w_in, conv_w, conv_w_out, i, j, slen, grp)
            elif kind == 2:
                p = dict(mu=rwkv_mu, w_r=rwkv_w_r, w_k=rwkv_w_k, w_v=rwkv_w_v, w_o=rwkv_w_o, w0=rwkv_w0,
                         w_l1=rwkv_w_l1, w_l2=rwkv_w_l2, a0=rwkv_a0, a_l1=rwkv_a_l1, a_l2=rwkv_a_l2,
                         g_l1=rwkv_g_l1, g_l2=rwkv_g_l2, k_k=rwkv_k_k, k_a=rwkv_k_a,
                         r_k=rwkv_r_k.reshape(-1, D), ln_g=rwkv_ln_g, ln_b=rwkv_ln_b)
                proj = _rwkv_proj(src, mod, norm_g, p, i, j, slen, grp)
                if is_prompt:
                    y, s_fin = _rwkv_scan(proj, None, nseq, slen, True)
                    new_rwkv.append(s_fin)
                else:
                    (y,) = _rwkv_scan(proj, state_rwkv[:, j], nseq, slen, False)
                shared = _rwkv_out(src, dst, mod, norm_g, y, proj[10], proj[9], p, i, j, grp)
            else:
                perm = _rope_swap_perm()
                w_uq = mla_w_uq[j].reshape(Q_RANK, MLA_H, D_NOPE + D_ROPE)
                w_uq_rope = w_uq[:, :, D_NOPE:]
                w_dkv_r = mla_w_dkv[j][:, KV_RANK:]
                p = dict(w_dq=mla_w_dq, g_q=mla_g_q, g_kv=mla_g_kv, w_uk=mla_w_uk, w_uv=mla_w_uv, w_o=mla_w_o,
                         w_uq_nope=w_uq[:, :, :D_NOPE].reshape(Q_RANK, MLA_H * D_NOPE),
                         w_uq_rope=w_uq_rope.reshape(Q_RANK, MLA_H * D_ROPE),
                         w_uq_rope_sw=w_uq_rope[:, :, perm].reshape(Q_RANK, MLA_H * D_ROPE),
                         w_dkv_c=mla_w_dkv[j][:, :KV_RANK], w_dkv_r=w_dkv_r, w_dkv_r_sw=w_dkv_r[:, perm])
                qn, qr, ckv, kr = _mla_proj(src, mod, norm_g, p, i, j, not is_prompt, grp)
                if is_prompt:
                    new_ckv.append(ckv.reshape(batch, seq, KV_RANK))
                    new_krope.append(kr.reshape(batch, seq, D_ROPE))
                    shared = _mla_attn(src, dst, mod, norm_g, qn, qr, ckv, kr, p, i, j, nseq, slen, slen,
                                       TM // slen, slen, lambda s: 0)
                else:
                    klen = past_len + slen
                    ckv_all = jnp.concatenate([cache_mla_ckv[:, j], ckv.reshape(nseq, slen, KV_RANK)], axis=1)
                    kr_all = jnp.concatenate([cache_mla_krope[:, j], kr.reshape(nseq, slen, D_ROPE)], axis=1)
                    shared = _mla_attn(src, dst, mod, norm_g, qn, qr, ckv_all.reshape(nseq * klen, KV_RANK),
                                       kr_all.reshape(nseq * klen, D_ROPE), p, i, j, nseq, slen, klen, 1, 256,
                                       lambda s: 1 + s)
        if i < DEPTH - 1:
            both = _mlp(_Rows(shared, 0, np_tiles + ns_tiles), _Dest(total_rows, 0, None), mod, norm_g,
                        mlp_w1, mlp_w2, i, grp_all)
            streams = [st[:4] + (_Rows(both, 0 if st[0] else np_tiles, st[4].ntiles),) for st in streams]
        else:
            y_prompt, y_sample = [
                _mlp(_Rows(shared, 0 if st[0] else np_tiles, st[4].ntiles), _Dest(st[4].ntiles * TM, 0, None),
                     mod, norm_g, mlp_w1, mlp_w2, i, st[3]) for st in streams]

    return (y_prompt.reshape(batch, seq, D), y_sample.reshape(dec_batch, dec_seq, D),
            jnp.stack(new_rwkv, axis=1), jnp.stack(new_ckv, axis=1), jnp.stack(new_krope, axis=1))
```

```python
import functools
from typing import NamedTuple, Optional

import numpy as np
import jax
import jax.numpy as jnp
from jax import lax
from jax.experimental import pallas as pl
from jax.experimental.pallas import tpu as pltpu

D = 1024
DEPTH = 4
N_MOD = 6
D_FF = 4 * D
EPS = 1e-6
GRID_W = 64
FFT_GROUPS = 8
FFT_GW = D // FFT_GROUPS
HS = 64
NH = D // HS
LORA_W = 64
LORA_A = 64
LORA_G = 128
GN_EPS = 64e-5
MLA_H = 8
D_NOPE = 128
D_ROPE = 64
D_V = 128
KV_RANK = 256
Q_RANK = 384
ROPE_BASE = 10000.0
MLA_SCALE = (D_NOPE + D_ROPE) ** -0.5

F32 = jnp.float32
BF16 = jnp.bfloat16

TM = 1024
LANES = 128
CHUNK = 64
SCAN_UNROLL = 4
SCAN_PAIRS = 2
ATTN_LOOKAHEAD = 2
VMEM_LIMIT = 56 * 1024 * 1024


def _cparams(sem):
    return pltpu.CompilerParams(dimension_semantics=sem, vmem_limit_bytes=VMEM_LIMIT)


def _dot(a, b):
    return jnp.dot(a.astype(BF16), b.astype(BF16), preferred_element_type=F32)


def _dot_nt(a, b):
    return lax.dot_general(a.astype(BF16), b.astype(BF16), (((1,), (1,)), ((), ())),
                           preferred_element_type=F32)


def _dot_tn(a, b):
    return lax.dot_general(a.astype(BF16), b.astype(BF16), (((0,), (0,)), ((), ())),
                           preferred_element_type=F32)


def _block_diag2(a, b):
    za = jnp.zeros((a.shape[0], b.shape[1]), a.dtype)
    zb = jnp.zeros((b.shape[0], a.shape[1]), a.dtype)
    return jnp.concatenate([jnp.concatenate([a, za], axis=1), jnp.concatenate([zb, b], axis=1)], axis=0)


def _cumsum_rows(x, reverse):
    n = x.shape[0]
    idx = lax.broadcasted_iota(jnp.int32, x.shape, 0)
    s = 1
    while s < n:
        if reverse:
            x = x + jnp.where(idx < n - s, pltpu.roll(x, n - s, 0), 0.0)
        else:
            x = x + jnp.where(idx >= s, pltpu.roll(x, s, 0), 0.0)
        s *= 2
    return x


def _dot_hilo_rhs(x, e):
    hi = x.astype(BF16)
    lo = (x - hi.astype(F32)).astype(BF16)
    return jnp.dot(hi, e, preferred_element_type=F32) + jnp.dot(lo, e, preferred_element_type=F32)


def _rms(x, g):
    return x * lax.rsqrt(jnp.mean(x * x, axis=-1, keepdims=True) + EPS) * g


def _sigmoid(x):
    return 1.0 / (1.0 + jnp.exp(-x))


def _softplus(x):
    return jnp.maximum(x, 0.0) + jnp.log(1.0 + jnp.exp(-jnp.abs(x)))


def _head_indicator(n):
    r = lax.broadcasted_iota(jnp.int32, (n, n), 0) // HS
    c = lax.broadcasted_iota(jnp.int32, (n, n), 1) // HS
    return jnp.where(r == c, 1.0, 0.0).astype(BF16)


def _shift_rows(z, seq_len):
    n = z.shape[0]
    pos = lax.broadcasted_iota(jnp.int32, z.shape, 0) % seq_len
    prev = jnp.where(pos == 0, 0.0, pltpu.roll(z, 1, 0))
    nxt = jnp.where(pos == seq_len - 1, 0.0, pltpu.roll(z, n - 1, 0))
    return prev, nxt


MOD_TN = 1536


def _mod_kernel(cs_ref, w_ref, b_ref, o_ref):
    cs = cs_ref[...]
    s = cs * _sigmoid(cs)
    o_ref[0] = _dot(s, w_ref[0]) + b_ref[0]


def _modulation(cs, mod_w, mod_b):
    nj = (N_MOD * D) // MOD_TN
    out = pl.pallas_call(
        _mod_kernel,
        grid=(DEPTH, nj),
        in_specs=[
            pl.BlockSpec((8, D), lambda l, j: (0, 0)),
            pl.BlockSpec((1, D, MOD_TN), lambda l, j: (l, 0, j)),
            pl.BlockSpec((1, 1, MOD_TN), lambda l, j: (l, 0, j)),
        ],
        out_specs=pl.BlockSpec((1, 8, MOD_TN), lambda l, j: (l, 0, j)),
        out_shape=jax.ShapeDtypeStruct((DEPTH, 8, N_MOD * D), F32),
        compiler_params=_cparams(("arbitrary", "arbitrary")),
        name="modulation",
    )(cs, mod_w, mod_b.reshape(DEPTH, 1, N_MOD * D))
    return out.reshape(DEPTH, 8, N_MOD, D)


class _Rows(NamedTuple):
    array: jax.Array
    first: int
    ntiles: int


class _Dest(NamedTuple):
    rows: int
    first: int
    base: Optional[jax.Array]


def _without_ref(kernel, pos):
    def call(*refs):
        return kernel(*refs[:pos], *refs[pos + 1:])
    return call


def _activation_call(kernel, *, grid, in_specs, args, place, block_rows, out_index, scratch_shapes, name):
    out_rows, base = place
    aliases = {}
    if base is not None:
        pos = len(in_specs)
        in_specs = list(in_specs) + [pl.BlockSpec(memory_space=pl.ANY)]
        args = list(args) + [base]
        aliases = {pos: 0}
        kernel = _without_ref(kernel, pos)
    return pl.pallas_call(
        kernel, grid=grid, in_specs=in_specs,
        out_specs=pl.BlockSpec((block_rows, D), out_index),
        out_shape=jax.ShapeDtypeStruct((out_rows, D), F32),
        scratch_shapes=scratch_shapes, input_output_aliases=aliases,
        compiler_params=_cparams(("arbitrary",) * len(grid)), name=name)(*args)


def _mod_spec(layer, group_of_tile, ngrid):
    if ngrid == 1:
        return pl.BlockSpec((1, 1, N_MOD, D), lambda i: (layer, group_of_tile(i), 0, 0))
    return pl.BlockSpec((1, 1, N_MOD, D), lambda i, j: (layer, group_of_tile(i), 0, 0))


def _normg_spec(layer, ngrid):
    if ngrid == 1:
        return pl.BlockSpec((1, 4, D), lambda i: (layer, 0, 0))
    return pl.BlockSpec((1, 4, D), lambda i, j: (layer, 0, 0))


MLP_FC = 1024


MLP_GROUP = 2


def _mlp_kernel(x_ref, m_ref, g_ref, w1_ref, w2_ref, o_ref, h_scr, acc_scr):
    j = pl.program_id(1)
    t = pl.program_id(2)
    m = m_ref[0, 0]
    g = g_ref[0]

    @pl.when(j == 0)
    def _():
        h = _rms(x_ref[...], g[2:3]) * (1.0 + m[4:5]) + m[3:4]
        h_scr[t] = h.astype(BF16)
        acc_scr[t] = jnp.zeros((TM, D), F32)

    a = jnp.dot(h_scr[t], w1_ref[0].astype(BF16), preferred_element_type=F32)
    a = jnp.maximum(a, 0.0)
    a = a * a
    acc_scr[t] += _dot(a, w2_ref[0])

    @pl.when(j == pl.num_programs(1) - 1)
    def _():
        o_ref[...] = x_ref[...] + m[5:6] * _rms(acc_scr[t], g[3:4])


def _mlp(src, dst, mod, norm_g, w1, w2, layer, group_of_tile):
    nj = D_FF // MLP_FC
    assert src.ntiles % MLP_GROUP == 0 and nj > 1
    last_t = MLP_GROUP - 1

    def x_index(p, j, t):
        return (MLP_GROUP * p + jnp.where((j == 0) | (j == nj - 1), t, last_t) + src.first, 0)

    def out_index(p, j, t):
        return (MLP_GROUP * p + jnp.where(j == nj - 1, t, 0) + dst.first, 0)

    return _activation_call(
        _mlp_kernel,
        grid=(src.ntiles // MLP_GROUP, nj, MLP_GROUP),
        in_specs=[
            pl.BlockSpec((TM, D), x_index),
            pl.BlockSpec((1, 1, N_MOD, D), lambda p, j, t: (layer, group_of_tile(MLP_GROUP * p + t), 0, 0)),
            pl.BlockSpec((1, 4, D), lambda p, j, t: (layer, 0, 0)),
            pl.BlockSpec((1, D, MLP_FC), lambda p, j, t: (layer, 0, j)),
            pl.BlockSpec((1, MLP_FC, D), lambda p, j, t: (layer, j, 0)),
        ],
        args=[src.array, mod, norm_g, w1, w2],
        place=(dst.rows, dst.base), block_rows=TM, out_index=out_index,
        scratch_shapes=[pltpu.VMEM((MLP_GROUP, TM, D), BF16), pltpu.VMEM((MLP_GROUP, TM, D), F32)],
        name="mlp")


def _dft_mats(n):
    idx = np.arange(n, dtype=np.int64)
    ang = (2.0 * np.pi / n) * ((idx[:, None] * idx[None, :]) % n).astype(np.float64)
    scale = 1.0 / np.sqrt(n)
    return np.cos(ang) * scale, np.sin(ang) * scale


def _fnet_kernel(x_ref, m_ref, g_ref, cs_ref, cn_ref, sn_ref, w_ref, o_ref, p_scr, q_scr, f_scr,
                 *, seq_len):
    m = m_ref[0, 0]
    g = g_ref[0]
    x = x_ref[...]
    h = (_rms(x, g[0:1]) * (1.0 + m[1:2]) + m[0:1]).astype(BF16)
    cs = cs_ref[...].astype(BF16)
    for gi in range(FFT_GROUPS):
        pq = jnp.dot(h[:, gi * FFT_GW:(gi + 1) * FFT_GW], cs, preferred_element_type=F32)
        p_scr[:, gi * FFT_GW:(gi + 1) * FFT_GW] = pq[:, :FFT_GW].astype(BF16)
        q_scr[:, gi * FFT_GW:(gi + 1) * FFT_GW] = pq[:, FFT_GW:].astype(BF16)
    cn = cn_ref[...].astype(BF16)
    sn = sn_ref[...].astype(BF16)
    for s in range(TM // seq_len):
        rows = slice(s * seq_len, (s + 1) * seq_len)
        f = (jnp.dot(cn, p_scr[rows, :], preferred_element_type=F32)
             - jnp.dot(sn, q_scr[rows, :], preferred_element_type=F32))
        f_scr[rows, :] = f.astype(BF16)
    o = jnp.dot(f_scr[...], w_ref[0].astype(BF16), preferred_element_type=F32)
    o_ref[...] = x + m[2:3] * _rms(o, g[1:2])


def _fnet(src, dst, mod, norm_g, w_out, layer, j, seq_len, group_of_tile):
    cg, sg = _dft_mats(FFT_GW)
    cs = jnp.asarray(np.concatenate([cg, sg], axis=1), F32)
    cn_np, sn_np = _dft_mats(seq_len)
    cn = jnp.asarray(cn_np, F32)
    sn = jnp.asarray(sn_np, F32)
    return _activation_call(
        functools.partial(_fnet_kernel, seq_len=seq_len),
        grid=(src.ntiles,),
        in_specs=[
            pl.BlockSpec((TM, D), lambda i: (i + src.first, 0)),
            _mod_spec(layer, group_of_tile, 1),
            _normg_spec(layer, 1),
            pl.BlockSpec((FFT_GW, 2 * FFT_GW), lambda i: (0, 0)),
            pl.BlockSpec((seq_len, seq_len), lambda i: (0, 0)),
            pl.BlockSpec((seq_len, seq_len), lambda i: (0, 0)),
            pl.BlockSpec((1, D, D), lambda i: (j, 0, 0)),
        ],
        args=[src.array, mod, norm_g, cs, cn, sn, w_out],
        place=(dst.rows, dst.base), block_rows=TM, out_index=lambda i: (i + dst.first, 0),
        scratch_shapes=[pltpu.VMEM((TM, D), BF16), pltpu.VMEM((TM, D), BF16),
                        pltpu.VMEM((TM, D), BF16)],
        name="fourier_mix")


CONV_CW = 512


def _conv_kernel(x_ref, m_ref, g_ref, wb_ref, wc_ref, wu_ref, cw_ref, wo_ref, o_ref, h_scr, acc_scr,
                 *, seq_len):
    j = pl.program_id(1)
    m = m_ref[0, 0]
    g = g_ref[0]

    @pl.when(j == 0)
    def _():
        h = _rms(x_ref[...], g[0:1]) * (1.0 + m[1:2]) + m[0:1]
        h_scr[...] = h.astype(BF16)
        acc_scr[...] = jnp.zeros_like(acc_scr)

    h = h_scr[...]
    hw = CONV_CW // 2
    proj = []
    for s in range(2):
        cols = slice(s * hw, (s + 1) * hw)
        proj.append([jnp.dot(h, w_ref[0, :, cols].astype(BF16), preferred_element_type=F32)
                     for w_ref in (wb_ref, wc_ref, wu_ref)])
    out = None
    for s in range(2):
        cols = slice(s * hw, (s + 1) * hw)
        bg, cg, u = proj[s]
        z = cg * u
        z_prev, z_next = _shift_rows(z, seq_len)
        conv = z_prev * cw_ref[0, 0:1, cols] + z * cw_ref[0, 1:2, cols] + z_next * cw_ref[0, 2:3, cols]
        part = _dot(bg * conv, wo_ref[0, cols, :])
        out = part if out is None else out + part
    acc_scr[...] += out

    @pl.when(j == pl.num_programs(1) - 1)
    def _():
        o_ref[...] = x_ref[...] + m[2:3] * _rms(acc_scr[...], g[1:2])


def _conv(src, dst, mod, norm_g, w_in, w_conv, w_out, layer, j, seq_len, group_of_tile):
    nj = D // CONV_CW
    return _activation_call(
        functools.partial(_conv_kernel, seq_len=seq_len),
        grid=(src.ntiles, nj),
        in_specs=[
            pl.BlockSpec((TM, D), lambda i, c: (i + src.first, 0)),
            _mod_spec(layer, group_of_tile, 2),
            _normg_spec(layer, 2),
            pl.BlockSpec((1, D, CONV_CW), lambda i, c: (j, 0, c)),
            pl.BlockSpec((1, D, CONV_CW), lambda i, c: (j, 0, nj + c)),
            pl.BlockSpec((1, D, CONV_CW), lambda i, c: (j, 0, 2 * nj + c)),
            pl.BlockSpec((1, 3, CONV_CW), lambda i, c: (j, 0, c)),
            pl.BlockSpec((1, CONV_CW, D), lambda i, c: (j, c, 0)),
        ],
        args=[src.array, mod, norm_g, w_in, w_in, w_in, w_conv, w_out],
        place=(dst.rows, dst.base), block_rows=TM, out_index=lambda i, c: (i + dst.first, 0),
        scratch_shapes=[pltpu.VMEM((TM, D), BF16), pltpu.VMEM((TM, D), F32)],
        name="short_conv")


RWP_CW = 256
RW_OUT_DTYPES = (BF16, BF16, BF16, F32, BF16, BF16, F32, BF16, BF16, F32, F32)
RW_CW = 512


def _rwkv_proj_kernel(x_ref, m_ref, g_ref, mu_ref, wr_ref, wk_ref, wv_ref, w0_ref, wl1_ref, wl2_ref,
                      a0_ref, al1_ref, al2_ref, gl1_ref, gl2_ref, kk_ref, ka_ref, rk_ref,
                      r_o, v_o, kk_o, ld0_o, b0_o, kt0_o, ld1_o, b1_o, kt1_o, g_o, bonus_o,
                      xr_scr, xk_scr, xv_scr, tw_scr, ta_scr, sg_scr, *, seq_len):
    j = pl.program_id(1)

    @pl.when(j == 0)
    def _():
        m = m_ref[0, 0]
        g = g_ref[0]
        mu = mu_ref[0]
        h = _rms(x_ref[...], g[0:1]) * (1.0 + m[1:2]) + m[0:1]
        h_prev, h_next = _shift_rows(h, seq_len)
        dx = 0.5 * (h_prev + h_next) - h
        xr_scr[...] = (h + dx * mu[0:1]).astype(BF16)
        xk_scr[...] = (h + dx * mu[2:3]).astype(BF16)
        xv_scr[...] = (h + dx * mu[3:4]).astype(BF16)
        xw = (h + dx * mu[1:2]).astype(BF16)
        xa = (h + dx * mu[4:5]).astype(BF16)
        xg = (h + dx * mu[5:6]).astype(BF16)
        for d in range(2):
            tw_scr[d] = jnp.tanh(_dot(xw, wl1_ref[0, d])).astype(BF16)
            ta_scr[d] = _dot(xa, al1_ref[0, d]).astype(BF16)
        sg_scr[...] = _sigmoid(_dot(xg, gl1_ref[0])).astype(BF16)

    e = _head_indicator(RWP_CW)
    r = jnp.dot(xr_scr[...], wr_ref[0].astype(BF16), preferred_element_type=F32)
    k = jnp.dot(xk_scr[...], wk_ref[0].astype(BF16), preferred_element_type=F32)
    v = jnp.dot(xv_scr[...], wv_ref[0].astype(BF16), preferred_element_type=F32)
    g_o[...] = jnp.dot(sg_scr[...], gl2_ref[0].astype(BF16), preferred_element_type=F32)
    zws = [jnp.dot(tw_scr[d], wl2_ref[0, d].astype(BF16), preferred_element_type=F32) for d in range(2)]
    zas = [jnp.dot(ta_scr[d], al2_ref[0, d].astype(BF16), preferred_element_type=F32) for d in range(2)]
    kk = k * kk_ref[...]
    kk = kk * lax.rsqrt(_dot(kk * kk, e) + 1e-12)
    r_o[...] = r.astype(r_o.dtype)
    v_o[...] = v.astype(v_o.dtype)
    kk_o[...] = kk.astype(kk_o.dtype)
    k_a = ka_ref[...]
    kt_sum = None
    for d, (ld_o, b_o, kt_o) in enumerate(((ld0_o, b0_o, kt0_o), (ld1_o, b1_o, kt1_o))):
        logw = -_softplus(-(w0_ref[0, d:d + 1, :] + zws[d])) - 0.5
        ld_o[...] = -jnp.exp(logw)
        a = _sigmoid(a0_ref[0, d:d + 1, :] + zas[d])
        kt = k * (1.0 + (a - 1.0) * k_a)
        b_o[...] = (kk * a).astype(b_o.dtype)
        kt_o[...] = kt.astype(kt_o.dtype)
        kt_sum = kt if kt_sum is None else kt_sum + kt
    bonus_o[...] = _dot(r * rk_ref[...] * kt_sum, e) * v


def _rwkv_proj(src, mod, norm_g, p, layer, j, seq_len, group_of_tile):
    n = src.ntiles * TM
    nj = D // RWP_CW
    tile = pl.BlockSpec((TM, RWP_CW), lambda i, c: (i, c))
    wcol = lambda: pl.BlockSpec((1, D, RWP_CW), lambda i, c: (j, 0, c))
    vec = lambda: pl.BlockSpec((1, RWP_CW), lambda i, c: (j, c))
    return pl.pallas_call(
        functools.partial(_rwkv_proj_kernel, seq_len=seq_len),
        grid=(src.ntiles, nj),
        in_specs=[
            pl.BlockSpec((TM, D), lambda i, c: (i + src.first, 0)),
            _mod_spec(layer, group_of_tile, 2),
            _normg_spec(layer, 2),
            pl.BlockSpec((1, 6, D), lambda i, c: (j, 0, 0)),
            wcol(), wcol(), wcol(),
            pl.BlockSpec((1, 2, RWP_CW), lambda i, c: (j, 0, c)),
            pl.BlockSpec((1, 2, D, LORA_W), lambda i, c: (j, 0, 0, 0)),
            pl.BlockSpec((1, 2, LORA_W, RWP_CW), lambda i, c: (j, 0, 0, c)),
            pl.BlockSpec((1, 2, RWP_CW), lambda i, c: (j, 0, c)),
            pl.BlockSpec((1, 2, D, LORA_A), lambda i, c: (j, 0, 0, 0)),
            pl.BlockSpec((1, 2, LORA_A, RWP_CW), lambda i, c: (j, 0, 0, c)),
            pl.BlockSpec((1, D, LORA_G), lambda i, c: (j, 0, 0)),
            pl.BlockSpec((1, LORA_G, RWP_CW), lambda i, c: (j, 0, c)),
            vec(), vec(), vec(),
        ],
        out_specs=[tile] * len(RW_OUT_DTYPES),
        out_shape=[jax.ShapeDtypeStruct((n, D), dt) for dt in RW_OUT_DTYPES],
        scratch_shapes=[pltpu.VMEM((TM, D), BF16), pltpu.VMEM((TM, D), BF16), pltpu.VMEM((TM, D), BF16),
                        pltpu.VMEM((2, TM, LORA_W), BF16), pltpu.VMEM((2, TM, LORA_A), BF16),
                        pltpu.VMEM((TM, LORA_G), BF16)],
        compiler_params=_cparams(("arbitrary", "arbitrary")),
        name="rwkv_proj",
    )(src.array, mod, norm_g, p["mu"], p["w_r"], p["w_k"], p["w_v"], p["w0"], p["w_l1"], p["w_l2"],
      p["a0"], p["a_l1"], p["a_l2"], p["g_l1"], p["g_l2"], p["k_k"], p["k_a"], p["r_k"])


def _scan_precompute(units):
    c = CHUNK
    c2 = 2 * c
    row = lax.broadcasted_iota(jnp.int32, (c2, c2), 0)
    col = lax.broadcasted_iota(jnp.int32, (c2, c2), 1)
    head_a = lax.broadcasted_iota(jnp.int32, (c, LANES), 1) < HS
    own_lanes = jnp.concatenate([head_a, jnp.logical_not(head_a)], axis=0)
    eye = jnp.where(row == col, 1.0, 0.0)

    def two_heads(t):
        return jnp.concatenate([jnp.where(head_a, t, 0.0), jnp.where(head_a, 0.0, t)], axis=0)

    def causal(reverse):
        if reverse:
            return (col % c) > (row % c), (col % c) >= (row % c)
        return (col % c) < (row % c), (col % c) <= (row % c)

    masks = {rev: causal(rev) for rev in sorted({u[6] for u in units})}

    cums = [_cumsum_rows(u[0], u[6]) for u in units]
    st = []
    for (ld, kk, beta, kt, r, v, rev), cum in zip(units, cums):
        tot = cum[0:1] if rev else cum[c - 1:c]
        ginv = jnp.exp(-cum)
        tail = jnp.exp(tot - cum)
        st.append(dict(
            rev=rev, etot=jnp.exp(tot),
            a_t=two_heads(-kk * jnp.exp(cum - ld)), r_t=two_heads(r * jnp.exp(cum)),
            bk=jnp.concatenate([two_heads(beta * ginv), two_heads(kt * ginv)], axis=0),
            bkg=jnp.concatenate([two_heads(beta * tail), two_heads(kt * tail)], axis=0),
            v2=jnp.concatenate([v, v], axis=0), vh=two_heads(v)))
    grams = [_dot_nt(jnp.concatenate([s["a_t"], s["r_t"]], axis=0), s["bk"]) for s in st]
    for s, gram in zip(st, grams):
        strict, incl = masks[s["rev"]]
        s["l_ab"] = jnp.where(strict, gram[:c2, :c2], 0.0)
        s["l_ak"] = jnp.where(strict, gram[:c2, c2:], 0.0)
        s["t_rb"] = jnp.where(incl, gram[c2:, :c2], 0.0)
        s["t_rk"] = jnp.where(incl, gram[c2:, c2:], 0.0)
    lvs = [_dot(s["l_ak"], s["v2"]) for s in st]
    minvs = [eye for _ in st]
    b = 1
    while b < c:
        same = (row // (2 * b)) == (col // (2 * b))
        es = []
        for s in st:
            first, second = (col % (2 * b)) < b, (row % (2 * b)) >= b
            if s["rev"]:
                first, second = (row % (2 * b)) < b, (col % (2 * b)) >= b
            es.append(jnp.where(same & first & second, s["l_ab"], 0.0))
        if b == 1:
            minvs = [m + e for m, e in zip(minvs, es)]
        else:
            half = [_dot(m, e) for m, e in zip(minvs, es)]
            minvs = [m + _dot(h, m) for m, h in zip(minvs, half)]
        b *= 2
    mms = [_dot(m, jnp.concatenate([s["a_t"], lv], axis=1)) for m, s, lv in zip(minvs, st, lvs)]
    zero = jnp.zeros((c2, LANES), BF16)
    tts = [_dot(jnp.concatenate([s["t_rb"], s["t_rk"]], axis=1),
                jnp.concatenate([mm.astype(BF16), jnp.concatenate([zero, s["v2"].astype(BF16)], axis=1)], axis=0))
           for s, mm in zip(st, mms)]
    ps = [_dot_tn(mm[:, :LANES], s["bkg"][:c2]) for s, mm in zip(st, mms)]
    qs = [_dot_tn(jnp.concatenate([jnp.where(own_lanes, mm[:, LANES:], 0.0), s["vh"]], axis=0), s["bkg"])
          for s, mm in zip(st, mms)]
    out = []
    for s, p, q, tt in zip(st, ps, qs, tts):
        reff = s["r_t"] + tt[:, :LANES]
        out.append(dict(etot=s["etot"], p=p, q=q, reff=reff[:c] + reff[c:],
                        y0=jnp.where(head_a, tt[:c, LANES:], tt[c:, LANES:])))
    return out


def _rwkv_scan_kernel(*refs, seq_len, has_init, want_final):
    r_ref, v_ref, kk_ref, ld0_ref, b0_ref, kt0_ref, ld1_ref, b1_ref, kt1_ref = refs[:9]
    pos = 9
    if has_init:
        s0_ref = refs[pos]
        pos += 1
    y_ref = refs[pos]
    pos += 1
    if want_final:
        sf_ref = refs[pos]
        pos += 1
    g_scr, yb_scr = refs[pos:pos + 2]
    nc = seq_len // CHUNK
    ngroups = nc // SCAN_UNROLL
    per_dir = ((ld0_ref, b0_ref, kt0_ref), (ld1_ref, b1_ref, kt1_ref))
    y_dst = (y_ref, yb_scr)
    nchain = 2 * SCAN_PAIRS

    def group(gi, states):
        where, units = [], []
        for u in range(SCAN_UNROLL):
            for pp in range(SCAN_PAIRS):
                lanes = slice(pp * LANES, (pp + 1) * LANES)
                for d in range(2):
                    ld_ref, b_ref, kt_ref = per_dir[d]
                    cidx = gi * SCAN_UNROLL + u
                    if d == 1:
                        cidx = nc - 1 - cidx
                    start = cidx * CHUNK
                    rw = slice(start, start + CHUNK) if isinstance(start, int) else pl.ds(
                        pl.multiple_of(start, CHUNK), CHUNK)
                    where.append((rw, lanes))
                    units.append(tuple(ref[rw, lanes].astype(F32) for ref in
                                       (ld_ref, kk_ref, b_ref, kt_ref, r_ref, v_ref)) + (d == 1,))
        pre = _scan_precompute(units)
        states = list(states)
        for u in range(SCAN_UNROLL):
            cur = pre[u * nchain:(u + 1) * nchain]
            ys = [_dot_nt(cu["reff"], g) for cu, g in zip(cur, states)]
            gp = [_dot(g, cu["p"]) for cu, g in zip(cur, states)]
            states = [states[ch] * cur[ch]["etot"] + gp[ch] + cur[ch]["q"] for ch in range(nchain)]
            for ch in range(nchain):
                rw, lanes = where[u * nchain + ch]
                y_dst[ch % 2][rw, lanes] = ys[ch] + cur[ch]["y0"]
        return states

    init = [_block_diag2(s0_ref[0, ch % 2, 2 * (ch // 2)], s0_ref[0, ch % 2, 2 * (ch // 2) + 1])
            if has_init else jnp.zeros((LANES, LANES), F32) for ch in range(nchain)]
    if ngroups == 1:
        final = group(0, init)
    else:
        for ch in range(nchain):
            g_scr[ch] = init[ch]

        def body(gi, carry):
            new = group(gi, [g_scr[ch] for ch in range(nchain)])
            for ch in range(nchain):
                g_scr[ch] = new[ch]
            return carry

        lax.fori_loop(0, ngroups, body, 0)
        final = [g_scr[ch] for ch in range(nchain)]
    y_ref[...] += yb_scr[...]
    if want_final:
        for ch in range(nchain):
            sf_ref[0, ch % 2, 2 * (ch // 2)] = final[ch][:HS, :HS]
            sf_ref[0, ch % 2, 2 * (ch // 2) + 1] = final[ch][HS:, HS:]


def _rwkv_scan(proj, s_init, n_seq, seq_len, want_final):
    r, v, kk, ld0, b0, kt0, ld1, b1, kt1 = proj[:9]
    n = r.shape[0]
    npair = D // LANES
    width = SCAN_PAIRS * LANES
    blk = pl.BlockSpec((seq_len, width), lambda b, p: (b, p))
    st_spec = pl.BlockSpec((1, 2, 2 * SCAN_PAIRS, HS, HS), lambda b, p: (b, 0, p, 0, 0))
    in_specs = [blk] * 9
    args = [r, v, kk, ld0, b0, kt0, ld1, b1, kt1]
    has_init = s_init is not None
    if has_init:
        in_specs.append(st_spec)
        args.append(s_init)
    out_specs = [blk]
    out_shape = [jax.ShapeDtypeStruct((n, D), F32)]
    if want_final:
        out_specs.append(st_spec)
        out_shape.append(jax.ShapeDtypeStruct((n_seq, 2, NH, HS, HS), F32))
    res = pl.pallas_call(
        functools.partial(_rwkv_scan_kernel, seq_len=seq_len, has_init=has_init, want_final=want_final),
        grid=(n_seq, npair // SCAN_PAIRS),
        in_specs=in_specs,
        out_specs=out_specs,
        out_shape=out_shape,
        scratch_shapes=[pltpu.VMEM((2 * SCAN_PAIRS, LANES, LANES), F32), pltpu.VMEM((seq_len, width), F32)],
        compiler_params=_cparams(("arbitrary", "arbitrary")),
        name="rwkv_scan",
    )(*args)
    return res


def _rwkv_out_kernel(x_ref, m_ref, g_ref, y_ref, bonus_ref, gate_ref, lng_ref, lnb_ref, wo_ref, o_ref,
                     acc_scr):
    j = pl.program_id(1)

    @pl.when(j == 0)
    def _():
        acc_scr[...] = jnp.zeros_like(acc_scr)

    e = _head_indicator(RW_CW)
    y = y_ref[...]
    mean = _dot_hilo_rhs(y, e) * (1.0 / HS)
    yc = y - mean
    var = _dot(yc * yc, e) * (1.0 / HS)
    yn = yc * lax.rsqrt(var + GN_EPS) * lng_ref[...] + lnb_ref[...]
    yn = (yn + bonus_ref[...]) * gate_ref[...]
    acc_scr[...] += _dot(yn, wo_ref[0])

    @pl.when(j == pl.num_programs(1) - 1)
    def _():
        m = m_ref[0, 0]
        g = g_ref[0]
        o_ref[...] = x_ref[...] + m[2:3] * _rms(acc_scr[...], g[1:2])


def _rwkv_out(src, dst, mod, norm_g, y, bonus, gate, p, layer, j, group_of_tile):
    nj = D // RW_CW
    tile = pl.BlockSpec((TM, RW_CW), lambda i, c: (i, c))
    vec = pl.BlockSpec((1, RW_CW), lambda i, c: (j, c))
    return _activation_call(
        _rwkv_out_kernel,
        grid=(src.ntiles, nj),
        in_specs=[
            pl.BlockSpec((TM, D), lambda i, c: (i + src.first, 0)),
            _mod_spec(layer, group_of_tile, 2),
            _normg_spec(layer, 2),
            tile, tile, tile, vec, vec,
            pl.BlockSpec((1, RW_CW, D), lambda i, c: (j, c, 0)),
        ],
        args=[src.array, mod, norm_g, y, bonus, gate, p["ln_g"], p["ln_b"], p["w_o"]],
        place=(dst.rows, dst.base), block_rows=TM, out_index=lambda i, c: (i + dst.first, 0),
        scratch_shapes=[pltpu.VMEM((TM, D), F32)],
        name="rwkv_out")


def _rope_tables(n):
    rows = n // GRID_W
    row = np.repeat(np.arange(rows), GRID_W)
    col = np.tile(np.arange(GRID_W), rows)
    pos = np.stack([row, col], axis=-1).astype(np.float64)
    quarter = D_ROPE // 4
    inv = ROPE_BASE ** (-np.arange(quarter, dtype=np.float64) / quarter)
    ang = pos[:, :, None] * inv
    cos = np.cos(ang)
    sin = np.sin(ang)
    cos_t = np.concatenate([cos, cos], axis=-1).reshape(n, D_ROPE)
    sin_t = np.concatenate([-sin, sin], axis=-1).reshape(n, D_ROPE)
    return cos_t.astype(np.float32), sin_t.astype(np.float32)


def _rope_swap_perm():
    quarter = D_ROPE // 4
    base = np.arange(D_ROPE)
    return np.where((base % (2 * quarter)) < quarter, base + quarter, base - quarter)


def _mla_proj_kernel(*refs, positional):
    (x_ref, m_ref, g_ref, wdq_ref, gq_ref, wqn_ref, wqr_ref, wqs_ref, wdkv_ref, wkr_ref, wks_ref,
     gkv_ref) = refs[:12]
    pos = 12
    if positional:
        cosq_ref, sinq_ref, cosk_ref, sink_ref = refs[pos:pos + 4]
        pos += 4
    qn_o, qr_o, ckv_o, kr_o = refs[pos:pos + 4]
    m = m_ref[0, 0]
    g = g_ref[0]
    h = (_rms(x_ref[...], g[0:1]) * (1.0 + m[1:2]) + m[0:1]).astype(BF16)
    ql = jnp.dot(h, wdq_ref[0].astype(BF16), preferred_element_type=F32)
    ql = (ql * lax.rsqrt(jnp.mean(ql * ql, axis=-1, keepdims=True) + EPS) * gq_ref[...]).astype(BF16)
    qn_o[...] = jnp.dot(ql, wqn_ref[...].astype(BF16), preferred_element_type=F32)
    qr = jnp.dot(ql, wqr_ref[...].astype(BF16), preferred_element_type=F32)
    ckv = jnp.dot(h, wdkv_ref[...].astype(BF16), preferred_element_type=F32)
    ckv_o[...] = ckv * lax.rsqrt(jnp.mean(ckv * ckv, axis=-1, keepdims=True) + EPS) * gkv_ref[...]
    kr = jnp.dot(h, wkr_ref[...].astype(BF16), preferred_element_type=F32)
    if positional:
        qs = jnp.dot(ql, wqs_ref[...].astype(BF16), preferred_element_type=F32)
        ks = jnp.dot(h, wks_ref[...].astype(BF16), preferred_element_type=F32)
        qr = qr * cosq_ref[...] + qs * sinq_ref[...]
        kr = kr * cosk_ref[...] + ks * sink_ref[...]
    qr_o[...] = qr
    kr_o[...] = kr


def _mla_proj(src, mod, norm_g, p, layer, j, positional, group_of_tile):
    n = src.ntiles * TM
    full = lambda shape: pl.BlockSpec(shape, lambda i: (0,) * len(shape))
    in_specs = [
        pl.BlockSpec((TM, D), lambda i: (i + src.first, 0)),
        _mod_spec(layer, group_of_tile, 1),
        _normg_spec(layer, 1),
        pl.BlockSpec((1, D, Q_RANK), lambda i: (j, 0, 0)),
        pl.BlockSpec((1, Q_RANK), lambda i: (j, 0)),
        full((Q_RANK, MLA_H * D_NOPE)), full((Q_RANK, MLA_H * D_ROPE)), full((Q_RANK, MLA_H * D_ROPE)),
        full((D, KV_RANK)), full((D, D_ROPE)), full((D, D_ROPE)),
        pl.BlockSpec((1, KV_RANK), lambda i: (j, 0)),
    ]
    args = [src.array, mod, norm_g, p["w_dq"], p["g_q"], p["w_uq_nope"], p["w_uq_rope"], p["w_uq_rope_sw"],
            p["w_dkv_c"], p["w_dkv_r"], p["w_dkv_r_sw"], p["g_kv"]]
    if positional:
        cos_t, sin_t = _rope_tables(TM)
        in_specs += [full((TM, MLA_H * D_ROPE)), full((TM, MLA_H * D_ROPE)),
                     full((TM, D_ROPE)), full((TM, D_ROPE))]
        args += [jnp.asarray(np.tile(cos_t, (1, MLA_H))), jnp.asarray(np.tile(sin_t, (1, MLA_H))),
                 jnp.asarray(cos_t), jnp.asarray(sin_t)]
    widths = (MLA_H * D_NOPE, MLA_H * D_ROPE, KV_RANK, D_ROPE)
    return pl.pallas_call(
        functools.partial(_mla_proj_kernel, positional=positional),
        grid=(src.ntiles,),
        in_specs=in_specs,
        out_specs=[pl.BlockSpec((TM, w), lambda i: (i, 0)) for w in widths],
        out_shape=[jax.ShapeDtypeStruct((n, w), F32) for w in widths],
        compiler_params=_cparams(("arbitrary",)),
        name="mla_proj",
    )(*args)


def _mla_attn_kernel(x_ref, m_ref, g_ref, qn_ref, qr_ref, ckv_ref, kr_ref, wuk_ref, wuv_ref, wo_ref,
                     o_ref, kn_scr, vv_scr, oh_scr, *, nb, tq, k_len):
    qi = pl.program_id(1)

    @pl.when(qi == 0)
    def _():
        ckv = ckv_ref[...].astype(BF16)
        kn_scr[...] = jnp.dot(ckv, wuk_ref[0].astype(BF16), preferred_element_type=F32).astype(BF16)
        vv_scr[...] = jnp.dot(ckv, wuv_ref[0].astype(BF16), preferred_element_type=F32).astype(BF16)

    units = [(b, hd) for b in range(nb) for hd in range(MLA_H)]

    def scores(unit):
        b, hd = unit
        qrows = slice(b * tq, (b + 1) * tq)
        krows = slice(b * k_len, (b + 1) * k_len)
        q = jnp.concatenate([qn_ref[qrows, hd * D_NOPE:(hd + 1) * D_NOPE].astype(BF16),
                             qr_ref[qrows, hd * D_ROPE:(hd + 1) * D_ROPE].astype(BF16)], axis=1)
        k = jnp.concatenate([kn_scr[krows, hd * D_NOPE:(hd + 1) * D_NOPE],
                             kr_ref[krows, :].astype(BF16)], axis=1)
        return _dot_nt(q, k) * MLA_SCALE

    pending = [scores(u) for u in units[:ATTN_LOOKAHEAD]]
    for idx, (b, hd) in enumerate(units):
        if idx + ATTN_LOOKAHEAD < len(units):
            pending.append(scores(units[idx + ATTN_LOOKAHEAD]))
        s = pending[idx]
        pexp = jnp.exp(s - jnp.max(s, axis=-1, keepdims=True))
        pv = jnp.dot(pexp.astype(BF16), vv_scr[b * k_len:(b + 1) * k_len, hd * D_V:(hd + 1) * D_V],
                     preferred_element_type=F32)
        oh_scr[b * tq:(b + 1) * tq, hd * D_V:(hd + 1) * D_V] = (
            pv / jnp.sum(pexp, axis=-1, keepdims=True)).astype(BF16)
    o = jnp.dot(oh_scr[...], wo_ref[0].astype(BF16), preferred_element_type=F32)
    m = m_ref[0, 0]
    g = g_ref[0]
    o_ref[...] = x_ref[...] + m[2:3] * _rms(o, g[1:2])


def _mla_attn(src, dst, mod, norm_g, qn, qr, ckv_all, kr_all, p, layer, j, n_seq, q_len, k_len, nb, tq,
              group_of_step):
    nq = q_len // tq
    assert nb == 1 or nq == 1
    rows = nb * tq
    x_first, o_first = src.first * TM // rows, dst.first * TM // rows
    return _activation_call(
        functools.partial(_mla_attn_kernel, nb=nb, tq=tq, k_len=k_len),
        grid=(n_seq // nb, nq),
        in_specs=[
            pl.BlockSpec((rows, D), lambda s, q: (s * nq + q + x_first, 0)),
            pl.BlockSpec((1, 1, N_MOD, D), lambda s, q: (layer, group_of_step(s), 0, 0)),
            pl.BlockSpec((1, 4, D), lambda s, q: (layer, 0, 0)),
            pl.BlockSpec((nb * tq, MLA_H * D_NOPE), lambda s, q: (s * nq + q, 0)),
            pl.BlockSpec((nb * tq, MLA_H * D_ROPE), lambda s, q: (s * nq + q, 0)),
            pl.BlockSpec((nb * k_len, KV_RANK), lambda s, q: (s, 0)),
            pl.BlockSpec((nb * k_len, D_ROPE), lambda s, q: (s, 0)),
            pl.BlockSpec((1, KV_RANK, MLA_H * D_NOPE), lambda s, q: (j, 0, 0)),
            pl.BlockSpec((1, KV_RANK, MLA_H * D_V), lambda s, q: (j, 0, 0)),
            pl.BlockSpec((1, MLA_H * D_V, D), lambda s, q: (j, 0, 0)),
        ],
        args=[src.array, mod, norm_g, qn, qr, ckv_all, kr_all, p["w_uk"], p["w_uv"], p["w_o"]],
        place=(dst.rows, dst.base), block_rows=rows, out_index=lambda s, q: (s * nq + q + o_first, 0),
        scratch_shapes=[pltpu.VMEM((nb * k_len, MLA_H * D_NOPE), BF16),
                        pltpu.VMEM((nb * k_len, MLA_H * D_V), BF16),
                        pltpu.VMEM((nb * tq, MLA_H * D_V), BF16)],
        name="mla_attn")


def kernel(x_prompt, x_sample, state_rwkv, cache_mla_ckv, cache_mla_krope, c, c_ctx, mod_w, mod_b, norm_g,
           mlp_w1, mlp_w2, fft_w_out, conv_w_in, conv_w, conv_w_out, rwkv_mu, rwkv_w_r, rwkv_w_k, rwkv_w_v,
           rwkv_w_o, rwkv_w0, rwkv_w_l1, rwkv_w_l2, rwkv_a0, rwkv_a_l1, rwkv_a_l2, rwkv_g_l1, rwkv_g_l2,
           rwkv_k_k, rwkv_k_a, rwkv_r_k, rwkv_ln_g, rwkv_ln_b, mla_w_dq, mla_g_q, mla_w_uq, mla_w_dkv,
           mla_g_kv, mla_w_uk, mla_w_uv, mla_w_o):
    batch, seq, _ = x_prompt.shape
    dec_batch, dec_seq, _ = x_sample.shape
    past_len = cache_mla_ckv.shape[2]
    assert (batch * seq) % TM == 0 and TM % seq == 0 and dec_seq == TM and seq % CHUNK == 0

    np_tiles = batch * seq // TM
    ns_tiles = dec_batch * dec_seq // TM
    total_rows = (np_tiles + ns_tiles) * TM
    cs = jnp.concatenate([c_ctx[None, :], c, jnp.zeros((8 - 1 - dec_batch, D), F32)], axis=0)
    mod = _modulation(cs, mod_w, mod_b)

    grp_p = lambda i: 0
    grp_s = lambda i: 1 + i
    grp_all = lambda i: jnp.maximum(i - (np_tiles - 1), 0)
    new_rwkv, new_ckv, new_krope = [], [], []
    streams = [(True, seq, batch, grp_p, _Rows(x_prompt.reshape(batch * seq, D), 0, np_tiles)),
               (False, dec_seq, dec_batch, grp_s, _Rows(x_sample.reshape(dec_batch * dec_seq, D), 0, ns_tiles))]

    for i in range(DEPTH):
        kind, j = i % 4, i // 4
        shared = None
        for is_prompt, slen, nseq, grp, src in streams:
            dst = _Dest(total_rows, 0 if is_prompt else np_tiles, shared)
            if kind == 0:
                shared = _fnet(src, dst, mod, norm_g, fft_w_out, i, j, slen, grp)
            elif kind == 1:
                shared = _conv(src, dst, mod, norm_g, conv_w_in, conv_w, conv_w_out, i, j, slen, grp)
            elif kind == 2:
                p = dict(mu=rwkv_mu, w_r=rwkv_w_r, w_k=rwkv_w_k, w_v=rwkv_w_v, w_o=rwkv_w_o, w0=rwkv_w0,
                         w_l1=rwkv_w_l1, w_l2=rwkv_w_l2, a0=rwkv_a0, a_l1=rwkv_a_l1, a_l2=rwkv_a_l2,
                         g_l1=rwkv_g_l1, g_l2=rwkv_g_l2, k_k=rwkv_k_k, k_a=rwkv_k_a,
                         r_k=rwkv_r_k.reshape(-1, D), ln_g=rwkv_ln_g, ln_b=rwkv_ln_b)
                proj = _rwkv_proj(src, mod, norm_g, p, i, j, slen, grp)
                if is_prompt:
                    y, s_fin = _rwkv_scan(proj, None, nseq, slen, True)
                    new_rwkv.append(s_fin)
                else:
                    (y,) = _rwkv_scan(proj, state_rwkv[:, j], nseq, slen, False)
                shared = _rwkv_out(src, dst, mod, norm_g, y, proj[10], proj[9], p, i, j, grp)
            else:
                perm = _rope_swap_perm()
                w_uq = mla_w_uq[j].reshape(Q_RANK, MLA_H, D_NOPE + D_ROPE)
                w_uq_rope = w_uq[:, :, D_NOPE:]
                w_dkv_r = mla_w_dkv[j][:, KV_RANK:]
                p = dict(w_dq=mla_w_dq, g_q=mla_g_q, g_kv=mla_g_kv, w_uk=mla_w_uk, w_uv=mla_w_uv, w_o=mla_w_o,
                         w_uq_nope=w_uq[:, :, :D_NOPE].reshape(Q_RANK, MLA_H * D_NOPE),
                         w_uq_rope=w_uq_rope.reshape(Q_RANK, MLA_H * D_ROPE),
                         w_uq_rope_sw=w_uq_rope[:, :, perm].reshape(Q_RANK, MLA_H * D_ROPE),
                         w_dkv_c=mla_w_dkv[j][:, :KV_RANK], w_dkv_r=w_dkv_r, w_dkv_r_sw=w_dkv_r[:, perm])
                qn, qr, ckv, kr = _mla_proj(src, mod, norm_g, p, i, j, not is_prompt, grp)
                if is_prompt:
                    new_ckv.append(ckv.reshape(batch, seq, KV_RANK))
                    new_krope.append(kr.reshape(batch, seq, D_ROPE))
                    shared = _mla_attn(src, dst, mod, norm_g, qn, qr, ckv, kr, p, i, j, nseq, slen, slen,
                                       TM // slen, slen, lambda s: 0)
                else:
                    klen = past_len + slen
                    ckv_all = jnp.concatenate([cache_mla_ckv[:, j], ckv.reshape(nseq, slen, KV_RANK)], axis=1)
                    kr_all = jnp.concatenate([cache_mla_krope[:, j], kr.reshape(nseq, slen, D_ROPE)], axis=1)
                    shared = _mla_attn(src, dst, mod, norm_g, qn, qr, ckv_all.reshape(nseq * klen, KV_RANK),
                                       kr_all.reshape(nseq * klen, D_ROPE), p, i, j, nseq, slen, klen, 1, 256,
                                       lambda s: 1 + s)
        if i < DEPTH - 1:
            both = _mlp(_Rows(shared, 0, np_tiles + ns_tiles), _Dest(total_rows, 0, None), mod, norm_g,
                        mlp_w1, mlp_w2, i, grp_all)
            streams = [st[:4] + (_Rows(both, 0 if st[0] else np_tiles, st[4].ntiles),) for st in streams]
        else:
            y_prompt, y_sample = [
                _mlp(_Rows(shared, 0 if st[0] else np_tiles, st[4].ntiles), _Dest(st[4].ntiles * TM, 0, None),
                     mod, norm_g, mlp_w1, mlp_w2, i, st[3]) for st in streams]

    return (y_prompt.reshape(batch, seq, D), y_sample.reshape(dec_batch, dec_seq, D),
            jnp.stack(new_rwkv, axis=1), jnp.stack(new_ckv, axis=1), jnp.stack(new_krope, axis=1))
```

```python
import functools
from typing import NamedTuple

import numpy as np
import jax
import jax.numpy as jnp
from jax import lax
from jax.experimental import pallas as pl
from jax.experimental.pallas import tpu as pltpu

D = 1024
DEPTH = 4
N_MOD = 6
D_FF = 4 * D
EPS = 1e-6
GRID_W = 64
FFT_GROUPS = 8
FFT_GW = D // FFT_GROUPS
HS = 64
NH = D // HS
LORA_W = 64
LORA_A = 64
LORA_G = 128
GN_EPS = 64e-5
MLA_H = 8
D_NOPE = 128
D_ROPE = 64
D_V = 128
KV_RANK = 256
Q_RANK = 384
ROPE_BASE = 10000.0
MLA_SCALE = (D_NOPE + D_ROPE) ** -0.5

F32 = jnp.float32
BF16 = jnp.bfloat16

TM = 1024
LANES = 128
CHUNK = 64
SCAN_UNROLL = 4
SCAN_PAIRS = 2
ATTN_LOOKAHEAD = 2
VMEM_LIMIT = 56 * 1024 * 1024


def _cparams(sem):
    return pltpu.CompilerParams(dimension_semantics=sem, vmem_limit_bytes=VMEM_LIMIT)


def _dot(a, b):
    return jnp.dot(a.astype(BF16), b.astype(BF16), preferred_element_type=F32)


def _dot_nt(a, b):
    return lax.dot_general(a.astype(BF16), b.astype(BF16), (((1,), (1,)), ((), ())),
                           preferred_element_type=F32)


def _dot_tn(a, b):
    return lax.dot_general(a.astype(BF16), b.astype(BF16), (((0,), (0,)), ((), ())),
                           preferred_element_type=F32)


def _block_diag2(a, b):
    za = jnp.zeros((a.shape[0], b.shape[1]), a.dtype)
    zb = jnp.zeros((b.shape[0], a.shape[1]), a.dtype)
    return jnp.concatenate([jnp.concatenate([a, za], axis=1), jnp.concatenate([zb, b], axis=1)], axis=0)


def _cumsum_rows(x, reverse):
    n = x.shape[0]
    idx = lax.broadcasted_iota(jnp.int32, x.shape, 0)
    s = 1
    while s < n:
        if reverse:
            x = x + jnp.where(idx < n - s, pltpu.roll(x, n - s, 0), 0.0)
        else:
            x = x + jnp.where(idx >= s, pltpu.roll(x, s, 0), 0.0)
        s *= 2
    return x


def _dot_hilo_rhs(x, e):
    hi = x.astype(BF16)
    lo = (x - hi.astype(F32)).astype(BF16)
    return jnp.dot(hi, e, preferred_element_type=F32) + jnp.dot(lo, e, preferred_element_type=F32)


def _rms(x, g):
    return x * lax.rsqrt(jnp.mean(x * x, axis=-1, keepdims=True) + EPS) * g


def _sigmoid(x):
    return 1.0 / (1.0 + jnp.exp(-x))


def _softplus(x):
    return jnp.maximum(x, 0.0) + jnp.log(1.0 + jnp.exp(-jnp.abs(x)))


def _head_indicator(n):
    r = lax.broadcasted_iota(jnp.int32, (n, n), 0) // HS
    c = lax.broadcasted_iota(jnp.int32, (n, n), 1) // HS
    return jnp.where(r == c, 1.0, 0.0).astype(BF16)


def _shift_rows(z, seq_len):
    n = z.shape[0]
    pos = lax.broadcasted_iota(jnp.int32, z.shape, 0) % seq_len
    prev = jnp.where(pos == 0, 0.0, pltpu.roll(z, 1, 0))
    nxt = jnp.where(pos == seq_len - 1, 0.0, pltpu.roll(z, n - 1, 0))
    return prev, nxt


MOD_TN = 1536


def _mod_kernel(cs_ref, w_ref, b_ref, o_ref):
    cs = cs_ref[...]
    s = cs * _sigmoid(cs)
    o_ref[0] = _dot(s, w_ref[0]) + b_ref[0]


def _modulation(cs, mod_w, mod_b):
    nj = (N_MOD * D) // MOD_TN
    out = pl.pallas_call(
        _mod_kernel,
        grid=(DEPTH, nj),
        in_specs=[
            pl.BlockSpec((8, D), lambda l, j: (0, 0)),
            pl.BlockSpec((1, D, MOD_TN), lambda l, j: (l, 0, j)),
            pl.BlockSpec((1, 1, MOD_TN), lambda l, j: (l, 0, j)),
        ],
        out_specs=pl.BlockSpec((1, 8, MOD_TN), lambda l, j: (l, 0, j)),
        out_shape=jax.ShapeDtypeStruct((DEPTH, 8, N_MOD * D), F32),
        compiler_params=_cparams(("arbitrary", "arbitrary")),
        name="modulation",
    )(cs, mod_w, mod_b.reshape(DEPTH, 1, N_MOD * D))
    return out.reshape(DEPTH, 8, N_MOD, D)


class _Rows(NamedTuple):
    array: jax.Array
    first: int
    ntiles: int


class _Dest(NamedTuple):
    rows: int
    first: int
    inplace: bool


def _activation_call(kernel, *, grid, in_specs, args, dst, block_rows, out_index, scratch_shapes, name):
    assert not dst.inplace or args[0].shape[0] == dst.rows
    return pl.pallas_call(
        kernel, grid=grid, in_specs=in_specs,
        out_specs=pl.BlockSpec((block_rows, D), out_index),
        out_shape=jax.ShapeDtypeStruct((dst.rows, D), F32),
        scratch_shapes=scratch_shapes, input_output_aliases={0: 0} if dst.inplace else {},
        compiler_params=_cparams(("arbitrary",) * len(grid)), name=name)(*args)


def _mod_spec(layer, group_of_tile, ngrid):
    if ngrid == 1:
        return pl.BlockSpec((1, 1, N_MOD, D), lambda i: (layer, group_of_tile(i), 0, 0))
    return pl.BlockSpec((1, 1, N_MOD, D), lambda i, j: (layer, group_of_tile(i), 0, 0))


def _normg_spec(layer, ngrid):
    if ngrid == 1:
        return pl.BlockSpec((1, 4, D), lambda i: (layer, 0, 0))
    return pl.BlockSpec((1, 4, D), lambda i, j: (layer, 0, 0))


MLP_FC = 1024


def _mlp_kernel(*refs, n_first):
    if n_first is None:
        (x_ref,), rest = refs[:1], refs[1:]
        read_x = lambda: x_ref[...]
    else:
        (xa_ref, xb_ref), rest = refs[:2], refs[2:]
        read_x = lambda: jnp.where(pl.program_id(0) < n_first, xa_ref[...], xb_ref[...])
    m_ref, g_ref, w1_ref, w2_ref, o_ref, h_scr, acc_scr = rest
    j = pl.program_id(1)
    m = m_ref[0, 0]
    g = g_ref[0]

    @pl.when(j == 0)
    def _():
        h = _rms(read_x(), g[2:3]) * (1.0 + m[4:5]) + m[3:4]
        h_scr[...] = h.astype(BF16)
        acc_scr[...] = jnp.zeros_like(acc_scr)

    a = jnp.dot(h_scr[...], w1_ref[0].astype(BF16), preferred_element_type=F32)
    a = jnp.maximum(a, 0.0)
    a = a * a
    acc_scr[...] += _dot(a, w2_ref[0])

    @pl.when(j == pl.num_programs(1) - 1)
    def _():
        o_ref[...] = read_x() + m[5:6] * _rms(acc_scr[...], g[3:4])


def _mlp(srcs, dst, mod, norm_g, w1, w2, layer, group_of_tile):
    nj = D_FF // MLP_FC
    if len(srcs) == 1:
        (src,) = srcs
        ntiles, n_first = src.ntiles, None
        x_specs = [pl.BlockSpec((TM, D), lambda i, j: (i + src.first, 0))]
    else:
        sa, sb = srcs
        ntiles, n_first = sa.ntiles + sb.ntiles, sa.ntiles
        x_specs = [pl.BlockSpec((TM, D), lambda i, j: (jnp.minimum(i, sa.ntiles - 1) + sa.first, 0)),
                   pl.BlockSpec((TM, D), lambda i, j: (jnp.maximum(i - sa.ntiles, 0) + sb.first, 0))]
    return _activation_call(
        functools.partial(_mlp_kernel, n_first=n_first),
        grid=(ntiles, nj),
        in_specs=x_specs + [
            _mod_spec(layer, group_of_tile, 2),
            _normg_spec(layer, 2),
            pl.BlockSpec((1, D, MLP_FC), lambda i, j: (layer, 0, j)),
            pl.BlockSpec((1, MLP_FC, D), lambda i, j: (layer, j, 0)),
        ],
        args=[s.array for s in srcs] + [mod, norm_g, w1, w2],
        dst=dst, block_rows=TM, out_index=lambda i, j: (i + dst.first, 0),
        scratch_shapes=[pltpu.VMEM((TM, D), BF16), pltpu.VMEM((TM, D), F32)],
        name="mlp")


def _dft_mats(n):
    idx = np.arange(n, dtype=np.int64)
    ang = (2.0 * np.pi / n) * ((idx[:, None] * idx[None, :]) % n).astype(np.float64)
    scale = 1.0 / np.sqrt(n)
    return np.cos(ang) * scale, np.sin(ang) * scale


def _fnet_kernel(x_ref, m_ref, g_ref, cs_ref, cn_ref, sn_ref, w_ref, o_ref, p_scr, q_scr, f_scr,
                 *, seq_len):
    m = m_ref[0, 0]
    g = g_ref[0]
    x = x_ref[...]
    h = (_rms(x, g[0:1]) * (1.0 + m[1:2]) + m[0:1]).astype(BF16)
    cs = cs_ref[...].astype(BF16)
    for gi in range(FFT_GROUPS):
        pq = jnp.dot(h[:, gi * FFT_GW:(gi + 1) * FFT_GW], cs, preferred_element_type=F32)
        p_scr[:, gi * FFT_GW:(gi + 1) * FFT_GW] = pq[:, :FFT_GW].astype(BF16)
        q_scr[:, gi * FFT_GW:(gi + 1) * FFT_GW] = pq[:, FFT_GW:].astype(BF16)
    cn = cn_ref[...].astype(BF16)
    sn = sn_ref[...].astype(BF16)
    for s in range(TM // seq_len):
        rows = slice(s * seq_len, (s + 1) * seq_len)
        f = (jnp.dot(cn, p_scr[rows, :], preferred_element_type=F32)
             - jnp.dot(sn, q_scr[rows, :], preferred_element_type=F32))
        f_scr[rows, :] = f.astype(BF16)
    o = jnp.dot(f_scr[...], w_ref[0].astype(BF16), preferred_element_type=F32)
    o_ref[...] = x + m[2:3] * _rms(o, g[1:2])


def _fnet(src, dst, mod, norm_g, w_out, layer, j, seq_len, group_of_tile):
    cg, sg = _dft_mats(FFT_GW)
    cs = jnp.asarray(np.concatenate([cg, sg], axis=1), F32)
    cn_np, sn_np = _dft_mats(seq_len)
    cn = jnp.asarray(cn_np, F32)
    sn = jnp.asarray(sn_np, F32)
    return _activation_call(
        functools.partial(_fnet_kernel, seq_len=seq_len),
        grid=(src.ntiles,),
        in_specs=[
            pl.BlockSpec((TM, D), lambda i: (i + src.first, 0)),
            _mod_spec(layer, group_of_tile, 1),
            _normg_spec(layer, 1),
            pl.BlockSpec((FFT_GW, 2 * FFT_GW), lambda i: (0, 0)),
            pl.BlockSpec((seq_len, seq_len), lambda i: (0, 0)),
            pl.BlockSpec((seq_len, seq_len), lambda i: (0, 0)),
            pl.BlockSpec((1, D, D), lambda i: (j, 0, 0)),
        ],
        args=[src.array, mod, norm_g, cs, cn, sn, w_out],
        dst=dst, block_rows=TM, out_index=lambda i: (i + dst.first, 0),
        scratch_shapes=[pltpu.VMEM((TM, D), BF16), pltpu.VMEM((TM, D), BF16),
                        pltpu.VMEM((TM, D), BF16)],
        name="fourier_mix")


CONV_CW = 512


def _conv_kernel(x_ref, m_ref, g_ref, wb_ref, wc_ref, wu_ref, cw_ref, wo_ref, o_ref, h_scr, acc_scr,
                 *, seq_len):
    j = pl.program_id(1)
    m = m_ref[0, 0]
    g = g_ref[0]

    @pl.when(j == 0)
    def _():
        h = _rms(x_ref[...], g[0:1]) * (1.0 + m[1:2]) + m[0:1]
        h_scr[...] = h.astype(BF16)
        acc_scr[...] = jnp.zeros_like(acc_scr)

    h = h_scr[...]
    hw = CONV_CW // 2
    proj = []
    for s in range(2):
        cols = slice(s * hw, (s + 1) * hw)
        proj.append([jnp.dot(h, w_ref[0, :, cols].astype(BF16), preferred_element_type=F32)
                     for w_ref in (wb_ref, wc_ref, wu_ref)])
    out = None
    for s in range(2):
        cols = slice(s * hw, (s + 1) * hw)
        bg, cg, u = proj[s]
        z = cg * u
        z_prev, z_next = _shift_rows(z, seq_len)
        conv = z_prev * cw_ref[0, 0:1, cols] + z * cw_ref[0, 1:2, cols] + z_next * cw_ref[0, 2:3, cols]
        part = _dot(bg * conv, wo_ref[0, cols, :])
        out = part if out is None else out + part
    acc_scr[...] += out

    @pl.when(j == pl.num_programs(1) - 1)
    def _():
        o_ref[...] = x_ref[...] + m[2:3] * _rms(acc_scr[...], g[1:2])


def _conv(src, dst, mod, norm_g, w_in, w_conv, w_out, layer, j, seq_len, group_of_tile):
    nj = D // CONV_CW
    return _activation_call(
        functools.partial(_conv_kernel, seq_len=seq_len),
        grid=(src.ntiles, nj),
        in_specs=[
            pl.BlockSpec((TM, D), lambda i, c: (i + src.first, 0)),
            _mod_spec(layer, group_of_tile, 2),
            _normg_spec(layer, 2),
            pl.BlockSpec((1, D, CONV_CW), lambda i, c: (j, 0, c)),
            pl.BlockSpec((1, D, CONV_CW), lambda i, c: (j, 0, nj + c)),
            pl.BlockSpec((1, D, CONV_CW), lambda i, c: (j, 0, 2 * nj + c)),
            pl.BlockSpec((1, 3, CONV_CW), lambda i, c: (j, 0, c)),
            pl.BlockSpec((1, CONV_CW, D), lambda i, c: (j, c, 0)),
        ],
        args=[src.array, mod, norm_g, w_in, w_in, w_in, w_conv, w_out],
        dst=dst, block_rows=TM, out_index=lambda i, c: (i + dst.first, 0),
        scratch_shapes=[pltpu.VMEM((TM, D), BF16), pltpu.VMEM((TM, D), F32)],
        name="short_conv")


RWP_CW = 256
RW_OUT_DTYPES = (BF16, BF16, BF16, F32, BF16, BF16, F32, BF16, BF16, F32, F32)
RW_CW = 512


def _rwkv_proj_kernel(x_ref, m_ref, g_ref, mu_ref, wr_ref, wk_ref, wv_ref, w0_ref, wl1_ref, wl2_ref,
                      a0_ref, al1_ref, al2_ref, gl1_ref, gl2_ref, kk_ref, ka_ref, rk_ref,
                      r_o, v_o, kk_o, ld0_o, b0_o, kt0_o, ld1_o, b1_o, kt1_o, g_o, bonus_o,
                      xr_scr, xk_scr, xv_scr, tw_scr, ta_scr, sg_scr, *, seq_len):
    j = pl.program_id(1)

    @pl.when(j == 0)
    def _():
        m = m_ref[0, 0]
        g = g_ref[0]
        mu = mu_ref[0]
        h = _rms(x_ref[...], g[0:1]) * (1.0 + m[1:2]) + m[0:1]
        h_prev, h_next = _shift_rows(h, seq_len)
        dx = 0.5 * (h_prev + h_next) - h
        xr_scr[...] = (h + dx * mu[0:1]).astype(BF16)
        xk_scr[...] = (h + dx * mu[2:3]).astype(BF16)
        xv_scr[...] = (h + dx * mu[3:4]).astype(BF16)
        xw = (h + dx * mu[1:2]).astype(BF16)
        xa = (h + dx * mu[4:5]).astype(BF16)
        xg = (h + dx * mu[5:6]).astype(BF16)
        for d in range(2):
            tw_scr[d] = jnp.tanh(_dot(xw, wl1_ref[0, d])).astype(BF16)
            ta_scr[d] = _dot(xa, al1_ref[0, d]).astype(BF16)
        sg_scr[...] = _sigmoid(_dot(xg, gl1_ref[0])).astype(BF16)

    e = _head_indicator(RWP_CW)
    r = jnp.dot(xr_scr[...], wr_ref[0].astype(BF16), preferred_element_type=F32)
    k = jnp.dot(xk_scr[...], wk_ref[0].astype(BF16), preferred_element_type=F32)
    v = jnp.dot(xv_scr[...], wv_ref[0].astype(BF16), preferred_element_type=F32)
    g_o[...] = jnp.dot(sg_scr[...], gl2_ref[0].astype(BF16), preferred_element_type=F32)
    zws = [jnp.dot(tw_scr[d], wl2_ref[0, d].astype(BF16), preferred_element_type=F32) for d in range(2)]
    zas = [jnp.dot(ta_scr[d], al2_ref[0, d].astype(BF16), preferred_element_type=F32) for d in range(2)]
    kk = k * kk_ref[...]
    kk = kk * lax.rsqrt(_dot(kk * kk, e) + 1e-12)
    r_o[...] = r.astype(r_o.dtype)
    v_o[...] = v.astype(v_o.dtype)
    kk_o[...] = kk.astype(kk_o.dtype)
    k_a = ka_ref[...]
    kt_sum = None
    for d, (ld_o, b_o, kt_o) in enumerate(((ld0_o, b0_o, kt0_o), (ld1_o, b1_o, kt1_o))):
        logw = -_softplus(-(w0_ref[0, d:d + 1, :] + zws[d])) - 0.5
        ld_o[...] = -jnp.exp(logw)
        a = _sigmoid(a0_ref[0, d:d + 1, :] + zas[d])
        kt = k * (1.0 + (a - 1.0) * k_a)
        b_o[...] = (kk * a).astype(b_o.dtype)
        kt_o[...] = kt.astype(kt_o.dtype)
        kt_sum = kt if kt_sum is None else kt_sum + kt
    bonus_o[...] = _dot(r * rk_ref[...] * kt_sum, e) * v


def _rwkv_proj(src, mod, norm_g, p, layer, j, seq_len, group_of_tile):
    n = src.ntiles * TM
    nj = D // RWP_CW
    tile = pl.BlockSpec((TM, RWP_CW), lambda i, c: (i, c))
    wcol = lambda: pl.BlockSpec((1, D, RWP_CW), lambda i, c: (j, 0, c))
    vec = lambda: pl.BlockSpec((1, RWP_CW), lambda i, c: (j, c))
    return pl.pallas_call(
        functools.partial(_rwkv_proj_kernel, seq_len=seq_len),
        grid=(src.ntiles, nj),
        in_specs=[
            pl.BlockSpec((TM, D), lambda i, c: (i + src.first, 0)),
            _mod_spec(layer, group_of_tile, 2),
            _normg_spec(layer, 2),
            pl.BlockSpec((1, 6, D), lambda i, c: (j, 0, 0)),
            wcol(), wcol(), wcol(),
            pl.BlockSpec((1, 2, RWP_CW), lambda i, c: (j, 0, c)),
            pl.BlockSpec((1, 2, D, LORA_W), lambda i, c: (j, 0, 0, 0)),
            pl.BlockSpec((1, 2, LORA_W, RWP_CW), lambda i, c: (j, 0, 0, c)),
            pl.BlockSpec((1, 2, RWP_CW), lambda i, c: (j, 0, c)),
            pl.BlockSpec((1, 2, D, LORA_A), lambda i, c: (j, 0, 0, 0)),
            pl.BlockSpec((1, 2, LORA_A, RWP_CW), lambda i, c: (j, 0, 0, c)),
            pl.BlockSpec((1, D, LORA_G), lambda i, c: (j, 0, 0)),
            pl.BlockSpec((1, LORA_G, RWP_CW), lambda i, c: (j, 0, c)),
            vec(), vec(), vec(),
        ],
        out_specs=[tile] * len(RW_OUT_DTYPES),
        out_shape=[jax.ShapeDtypeStruct((n, D), dt) for dt in RW_OUT_DTYPES],
        scratch_shapes=[pltpu.VMEM((TM, D), BF16), pltpu.VMEM((TM, D), BF16), pltpu.VMEM((TM, D), BF16),
                        pltpu.VMEM((2, TM, LORA_W), BF16), pltpu.VMEM((2, TM, LORA_A), BF16),
                        pltpu.VMEM((TM, LORA_G), BF16)],
        compiler_params=_cparams(("arbitrary", "arbitrary")),
        name="rwkv_proj",
    )(src.array, mod, norm_g, p["mu"], p["w_r"], p["w_k"], p["w_v"], p["w0"], p["w_l1"], p["w_l2"],
      p["a0"], p["a_l1"], p["a_l2"], p["g_l1"], p["g_l2"], p["k_k"], p["k_a"], p["r_k"])


def _scan_precompute(units):
    c = CHUNK
    c2 = 2 * c
    row = lax.broadcasted_iota(jnp.int32, (c2, c2), 0)
    col = lax.broadcasted_iota(jnp.int32, (c2, c2), 1)
    head_a = lax.broadcasted_iota(jnp.int32, (c, LANES), 1) < HS
    own_lanes = jnp.concatenate([head_a, jnp.logical_not(head_a)], axis=0)
    eye = jnp.where(row == col, 1.0, 0.0)

    def two_heads(t):
        return jnp.concatenate([jnp.where(head_a, t, 0.0), jnp.where(head_a, 0.0, t)], axis=0)

    def causal(reverse):
        if reverse:
            return (col % c) > (row % c), (col % c) >= (row % c)
        return (col % c) < (row % c), (col % c) <= (row % c)

    masks = {rev: causal(rev) for rev in sorted({u[6] for u in units})}

    cums = [_cumsum_rows(u[0], u[6]) for u in units]
    st = []
    for (ld, kk, beta, kt, r, v, rev), cum in zip(units, cums):
        tot = cum[0:1] if rev else cum[c - 1:c]
        ginv = jnp.exp(-cum)
        tail = jnp.exp(tot - cum)
        st.append(dict(
            rev=rev, etot=jnp.exp(tot),
            a_t=two_heads(-kk * jnp.exp(cum - ld)), r_t=two_heads(r * jnp.exp(cum)),
            bk=jnp.concatenate([two_heads(beta * ginv), two_heads(kt * ginv)], axis=0),
            bkg=jnp.concatenate([two_heads(beta * tail), two_heads(kt * tail)], axis=0),
            v2=jnp.concatenate([v, v], axis=0), vh=two_heads(v)))
    grams = [_dot_nt(jnp.concatenate([s["a_t"], s["r_t"]], axis=0), s["bk"]) for s in st]
    for s, gram in zip(st, grams):
        strict, incl = masks[s["rev"]]
        s["l_ab"] = jnp.where(strict, gram[:c2, :c2], 0.0)
        s["l_ak"] = jnp.where(strict, gram[:c2, c2:], 0.0)
        s["t_rb"] = jnp.where(incl, gram[c2:, :c2], 0.0)
        s["t_rk"] = jnp.where(incl, gram[c2:, c2:], 0.0)
    lvs = [_dot(s["l_ak"], s["v2"]) for s in st]
    minvs = [eye for _ in st]
    b = 1
    while b < c:
        same = (row // (2 * b)) == (col // (2 * b))
        es = []
        for s in st:
            first, second = (col % (2 * b)) < b, (row % (2 * b)) >= b
            if s["rev"]:
                first, second = (row % (2 * b)) < b, (col % (2 * b)) >= b
            es.append(jnp.where(same & first & second, s["l_ab"], 0.0))
        if b == 1:
            minvs = [m + e for m, e in zip(minvs, es)]
        else:
            half = [_dot(m, e) for m, e in zip(minvs, es)]
            minvs = [m + _dot(h, m) for m, h in zip(minvs, half)]
        b *= 2
    mms = [_dot(m, jnp.concatenate([s["a_t"], lv], axis=1)) for m, s, lv in zip(minvs, st, lvs)]
    zero = jnp.zeros((c2, LANES), BF16)
    tts = [_dot(jnp.concatenate([s["t_rb"], s["t_rk"]], axis=1),
                jnp.concatenate([mm.astype(BF16), jnp.concatenate([zero, s["v2"].astype(BF16)], axis=1)], axis=0))
           for s, mm in zip(st, mms)]
    ps = [_dot_tn(mm[:, :LANES], s["bkg"][:c2]) for s, mm in zip(st, mms)]
    qs = [_dot_tn(jnp.concatenate([jnp.where(own_lanes, mm[:, LANES:], 0.0), s["vh"]], axis=0), s["bkg"])
          for s, mm in zip(st, mms)]
    out = []
    for s, p, q, tt in zip(st, ps, qs, tts):
        reff = s["r_t"] + tt[:, :LANES]
        out.append(dict(etot=s["etot"], p=p, q=q, reff=reff[:c] + reff[c:],
                        y0=jnp.where(head_a, tt[:c, LANES:], tt[c:, LANES:])))
    return out


def _rwkv_scan_kernel(*refs, seq_len, has_init, want_final):
    r_ref, v_ref, kk_ref, ld0_ref, b0_ref, kt0_ref, ld1_ref, b1_ref, kt1_ref = refs[:9]
    pos = 9
    if has_init:
        s0_ref = refs[pos]
        pos += 1
    y_ref = refs[pos]
    pos += 1
    if want_final:
        sf_ref = refs[pos]
        pos += 1
    g_scr, yb_scr = refs[pos:pos + 2]
    nc = seq_len // CHUNK
    ngroups = nc // SCAN_UNROLL
    per_dir = ((ld0_ref, b0_ref, kt0_ref), (ld1_ref, b1_ref, kt1_ref))
    y_dst = (y_ref, yb_scr)
    nchain = 2 * SCAN_PAIRS

    def group(gi, states):
        where, units = [], []
        for u in range(SCAN_UNROLL):
            for pp in range(SCAN_PAIRS):
                lanes = slice(pp * LANES, (pp + 1) * LANES)
                for d in range(2):
                    ld_ref, b_ref, kt_ref = per_dir[d]
                    cidx = gi * SCAN_UNROLL + u
                    if d == 1:
                        cidx = nc - 1 - cidx
                    start = cidx * CHUNK
                    rw = slice(start, start + CHUNK) if isinstance(start, int) else pl.ds(
                        pl.multiple_of(start, CHUNK), CHUNK)
                    where.append((rw, lanes))
                    units.append(tuple(ref[rw, lanes].astype(F32) for ref in
                                       (ld_ref, kk_ref, b_ref, kt_ref, r_ref, v_ref)) + (d == 1,))
        pre = _scan_precompute(units)
        states = list(states)
        for u in range(SCAN_UNROLL):
            cur = pre[u * nchain:(u + 1) * nchain]
            ys = [_dot_nt(cu["reff"], g) for cu, g in zip(cur, states)]
            gp = [_dot(g, cu["p"]) for cu, g in zip(cur, states)]
            states = [states[ch] * cur[ch]["etot"] + gp[ch] + cur[ch]["q"] for ch in range(nchain)]
            for ch in range(nchain):
                rw, lanes = where[u * nchain + ch]
                y_dst[ch % 2][rw, lanes] = ys[ch] + cur[ch]["y0"]
        return states

    init = [_block_diag2(s0_ref[0, ch % 2, 2 * (ch // 2)], s0_ref[0, ch % 2, 2 * (ch // 2) + 1])
            if has_init else jnp.zeros((LANES, LANES), F32) for ch in range(nchain)]
    if ngroups == 1:
        final = group(0, init)
    else:
        for ch in range(nchain):
            g_scr[ch] = init[ch]

        def body(gi, carry):
            new = group(gi, [g_scr[ch] for ch in range(nchain)])
            for ch in range(nchain):
                g_scr[ch] = new[ch]
            return carry

        lax.fori_loop(0, ngroups, body, 0)
        final = [g_scr[ch] for ch in range(nchain)]
    y_ref[...] += yb_scr[...]
    if want_final:
        for ch in range(nchain):
            sf_ref[0, ch % 2, 2 * (ch // 2)] = final[ch][:HS, :HS]
            sf_ref[0, ch % 2, 2 * (ch // 2) + 1] = final[ch][HS:, HS:]


def _rwkv_scan(proj, s_init, n_seq, seq_len, want_final):
    r, v, kk, ld0, b0, kt0, ld1, b1, kt1 = proj[:9]
    n = r.shape[0]
    npair = D // LANES
    width = SCAN_PAIRS * LANES
    blk = pl.BlockSpec((seq_len, width), lambda b, p: (b, p))
    st_spec = pl.BlockSpec((1, 2, 2 * SCAN_PAIRS, HS, HS), lambda b, p: (b, 0, p, 0, 0))
    in_specs = [blk] * 9
    args = [r, v, kk, ld0, b0, kt0, ld1, b1, kt1]
    has_init = s_init is not None
    if has_init:
        in_specs.append(st_spec)
        args.append(s_init)
    out_specs = [blk]
    out_shape = [jax.ShapeDtypeStruct((n, D), F32)]
    if want_final:
        out_specs.append(st_spec)
        out_shape.append(jax.ShapeDtypeStruct((n_seq, 2, NH, HS, HS), F32))
    res = pl.pallas_call(
        functools.partial(_rwkv_scan_kernel, seq_len=seq_len, has_init=has_init, want_final=want_final),
        grid=(n_seq, npair // SCAN_PAIRS),
        in_specs=in_specs,
        out_specs=out_specs,
        out_shape=out_shape,
        scratch_shapes=[pltpu.VMEM((2 * SCAN_PAIRS, LANES, LANES), F32), pltpu.VMEM((seq_len, width), F32)],
        compiler_params=_cparams(("arbitrary", "arbitrary")),
        name="rwkv_scan",
    )(*args)
    return res


def _rwkv_out_kernel(x_ref, m_ref, g_ref, y_ref, bonus_ref, gate_ref, lng_ref, lnb_ref, wo_ref, o_ref,
                     acc_scr):
    j = pl.program_id(1)

    @pl.when(j == 0)
    def _():
        acc_scr[...] = jnp.zeros_like(acc_scr)

    e = _head_indicator(RW_CW)
    y = y_ref[...]
    mean = _dot_hilo_rhs(y, e) * (1.0 / HS)
    yc = y - mean
    var = _dot(yc * yc, e) * (1.0 / HS)
    yn = yc * lax.rsqrt(var + GN_EPS) * lng_ref[...] + lnb_ref[...]
    yn = (yn + bonus_ref[...]) * gate_ref[...]
    acc_scr[...] += _dot(yn, wo_ref[0])

    @pl.when(j == pl.num_programs(1) - 1)
    def _():
        m = m_ref[0, 0]
        g = g_ref[0]
        o_ref[...] = x_ref[...] + m[2:3] * _rms(acc_scr[...], g[1:2])


def _rwkv_out(src, dst, mod, norm_g, y, bonus, gate, p, layer, j, group_of_tile):
    nj = D // RW_CW
    tile = pl.BlockSpec((TM, RW_CW), lambda i, c: (i, c))
    vec = pl.BlockSpec((1, RW_CW), lambda i, c: (j, c))
    return _activation_call(
        _rwkv_out_kernel,
        grid=(src.ntiles, nj),
        in_specs=[
            pl.BlockSpec((TM, D), lambda i, c: (i + src.first, 0)),
            _mod_spec(layer, group_of_tile, 2),
            _normg_spec(layer, 2),
            tile, tile, tile, vec, vec,
            pl.BlockSpec((1, RW_CW, D), lambda i, c: (j, c, 0)),
        ],
        args=[src.array, mod, norm_g, y, bonus, gate, p["ln_g"], p["ln_b"], p["w_o"]],
        dst=dst, block_rows=TM, out_index=lambda i, c: (i + dst.first, 0),
        scratch_shapes=[pltpu.VMEM((TM, D), F32)],
        name="rwkv_out")


def _rope_tables(n):
    rows = n // GRID_W
    row = np.repeat(np.arange(rows), GRID_W)
    col = np.tile(np.arange(GRID_W), rows)
    pos = np.stack([row, col], axis=-1).astype(np.float64)
    quarter = D_ROPE // 4
    inv = ROPE_BASE ** (-np.arange(quarter, dtype=np.float64) / quarter)
    ang = pos[:, :, None] * inv
    cos = np.cos(ang)
    sin = np.sin(ang)
    cos_t = np.concatenate([cos, cos], axis=-1).reshape(n, D_ROPE)
    sin_t = np.concatenate([-sin, sin], axis=-1).reshape(n, D_ROPE)
    return cos_t.astype(np.float32), sin_t.astype(np.float32)


def _rope_swap_perm():
    quarter = D_ROPE // 4
    base = np.arange(D_ROPE)
    return np.where((base % (2 * quarter)) < quarter, base + quarter, base - quarter)


def _mla_proj_kernel(*refs, positional):
    (x_ref, m_ref, g_ref, wdq_ref, gq_ref, wqn_ref, wqr_ref, wqs_ref, wdkv_ref, wkr_ref, wks_ref,
     gkv_ref) = refs[:12]
    pos = 12
    if positional:
        cosq_ref, sinq_ref, cosk_ref, sink_ref = refs[pos:pos + 4]
        pos += 4
    qn_o, qr_o, ckv_o, kr_o = refs[pos:pos + 4]
    m = m_ref[0, 0]
    g = g_ref[0]
    h = (_rms(x_ref[...], g[0:1]) * (1.0 + m[1:2]) + m[0:1]).astype(BF16)
    ql = jnp.dot(h, wdq_ref[0].astype(BF16), preferred_element_type=F32)
    ql = (ql * lax.rsqrt(jnp.mean(ql * ql, axis=-1, keepdims=True) + EPS) * gq_ref[...]).astype(BF16)
    qn_o[...] = jnp.dot(ql, wqn_ref[...].astype(BF16), preferred_element_type=F32)
    qr = jnp.dot(ql, wqr_ref[...].astype(BF16), preferred_element_type=F32)
    ckv = jnp.dot(h, wdkv_ref[...].astype(BF16), preferred_element_type=F32)
    ckv_o[...] = ckv * lax.rsqrt(jnp.mean(ckv * ckv, axis=-1, keepdims=True) + EPS) * gkv_ref[...]
    kr = jnp.dot(h, wkr_ref[...].astype(BF16), preferred_element_type=F32)
    if positional:
        qs = jnp.dot(ql, wqs_ref[...].astype(BF16), preferred_element_type=F32)
        ks = jnp.dot(h, wks_ref[...].astype(BF16), preferred_element_type=F32)
        qr = qr * cosq_ref[...] + qs * sinq_ref[...]
        kr = kr * cosk_ref[...] + ks * sink_ref[...]
    qr_o[...] = qr
    kr_o[...] = kr


def _mla_proj(src, mod, norm_g, p, layer, j, positional, group_of_tile):
    n = src.ntiles * TM
    full = lambda shape: pl.BlockSpec(shape, lambda i: (0,) * len(shape))
    in_specs = [
        pl.BlockSpec((TM, D), lambda i: (i + src.first, 0)),
        _mod_spec(layer, group_of_tile, 1),
        _normg_spec(layer, 1),
        pl.BlockSpec((1, D, Q_RANK), lambda i: (j, 0, 0)),
        pl.BlockSpec((1, Q_RANK), lambda i: (j, 0)),
        full((Q_RANK, MLA_H * D_NOPE)), full((Q_RANK, MLA_H * D_ROPE)), full((Q_RANK, MLA_H * D_ROPE)),
        full((D, KV_RANK)), full((D, D_ROPE)), full((D, D_ROPE)),
        pl.BlockSpec((1, KV_RANK), lambda i: (j, 0)),
    ]
    args = [src.array, mod, norm_g, p["w_dq"], p["g_q"], p["w_uq_nope"], p["w_uq_rope"], p["w_uq_rope_sw"],
            p["w_dkv_c"], p["w_dkv_r"], p["w_dkv_r_sw"], p["g_kv"]]
    if positional:
        cos_t, sin_t = _rope_tables(TM)
        in_specs += [full((TM, MLA_H * D_ROPE)), full((TM, MLA_H * D_ROPE)),
                     full((TM, D_ROPE)), full((TM, D_ROPE))]
        args += [jnp.asarray(np.tile(cos_t, (1, MLA_H))), jnp.asarray(np.tile(sin_t, (1, MLA_H))),
                 jnp.asarray(cos_t), jnp.asarray(sin_t)]
    widths = (MLA_H * D_NOPE, MLA_H * D_ROPE, KV_RANK, D_ROPE)
    return pl.pallas_call(
        functools.partial(_mla_proj_kernel, positional=positional),
        grid=(src.ntiles,),
        in_specs=in_specs,
        out_specs=[pl.BlockSpec((TM, w), lambda i: (i, 0)) for w in widths],
        out_shape=[jax.ShapeDtypeStruct((n, w), F32) for w in widths],
        compiler_params=_cparams(("arbitrary",)),
        name="mla_proj",
    )(*args)


def _mla_attn_kernel(x_ref, m_ref, g_ref, qn_ref, qr_ref, ckv_ref, kr_ref, wuk_ref, wuv_ref, wo_ref,
                     o_ref, kn_scr, vv_scr, oh_scr, *, nb, tq, k_len):
    qi = pl.program_id(1)

    @pl.when(qi == 0)
    def _():
        ckv = ckv_ref[...].astype(BF16)
        kn_scr[...] = jnp.dot(ckv, wuk_ref[0].astype(BF16), preferred_element_type=F32).astype(BF16)
        vv_scr[...] = jnp.dot(ckv, wuv_ref[0].astype(BF16), preferred_element_type=F32).astype(BF16)

    units = [(b, hd) for b in range(nb) for hd in range(MLA_H)]

    def scores(unit):
        b, hd = unit
        qrows = slice(b * tq, (b + 1) * tq)
        krows = slice(b * k_len, (b + 1) * k_len)
        q = jnp.concatenate([qn_ref[qrows, hd * D_NOPE:(hd + 1) * D_NOPE].astype(BF16),
                             qr_ref[qrows, hd * D_ROPE:(hd + 1) * D_ROPE].astype(BF16)], axis=1)
        k = jnp.concatenate([kn_scr[krows, hd * D_NOPE:(hd + 1) * D_NOPE],
                             kr_ref[krows, :].astype(BF16)], axis=1)
        return _dot_nt(q, k) * MLA_SCALE

    pending = [scores(u) for u in units[:ATTN_LOOKAHEAD]]
    for idx, (b, hd) in enumerate(units):
        if idx + ATTN_LOOKAHEAD < len(units):
            pending.append(scores(units[idx + ATTN_LOOKAHEAD]))
        s = pending[idx]
        pexp = jnp.exp(s - jnp.max(s, axis=-1, keepdims=True))
        pv = jnp.dot(pexp.astype(BF16), vv_scr[b * k_len:(b + 1) * k_len, hd * D_V:(hd + 1) * D_V],
                     preferred_element_type=F32)
        oh_scr[b * tq:(b + 1) * tq, hd * D_V:(hd + 1) * D_V] = (
            pv / jnp.sum(pexp, axis=-1, keepdims=True)).astype(BF16)
    o = jnp.dot(oh_scr[...], wo_ref[0].astype(BF16), preferred_element_type=F32)
    m = m_ref[0, 0]
    g = g_ref[0]
    o_ref[...] = x_ref[...] + m[2:3] * _rms(o, g[1:2])


def _mla_attn(src, dst, mod, norm_g, qn, qr, ckv_all, kr_all, p, layer, j, n_seq, q_len, k_len, nb, tq,
              group_of_step):
    nq = q_len // tq
    assert nb == 1 or nq == 1
    rows = nb * tq
    x_first, o_first = src.first * TM // rows, dst.first * TM // rows
    return _activation_call(
        functools.partial(_mla_attn_kernel, nb=nb, tq=tq, k_len=k_len),
        grid=(n_seq // nb, nq),
        in_specs=[
            pl.BlockSpec((rows, D), lambda s, q: (s * nq + q + x_first, 0)),
            pl.BlockSpec((1, 1, N_MOD, D), lambda s, q: (layer, group_of_step(s), 0, 0)),
            pl.BlockSpec((1, 4, D), lambda s, q: (layer, 0, 0)),
            pl.BlockSpec((nb * tq, MLA_H * D_NOPE), lambda s, q: (s * nq + q, 0)),
            pl.BlockSpec((nb * tq, MLA_H * D_ROPE), lambda s, q: (s * nq + q, 0)),
            pl.BlockSpec((nb * k_len, KV_RANK), lambda s, q: (s, 0)),
            pl.BlockSpec((nb * k_len, D_ROPE), lambda s, q: (s, 0)),
            pl.BlockSpec((1, KV_RANK, MLA_H * D_NOPE), lambda s, q: (j, 0, 0)),
            pl.BlockSpec((1, KV_RANK, MLA_H * D_V), lambda s, q: (j, 0, 0)),
            pl.BlockSpec((1, MLA_H * D_V, D), lambda s, q: (j, 0, 0)),
        ],
        args=[src.array, mod, norm_g, qn, qr, ckv_all, kr_all, p["w_uk"], p["w_uv"], p["w_o"]],
        dst=dst, block_rows=rows, out_index=lambda s, q: (s * nq + q + o_first, 0),
        scratch_shapes=[pltpu.VMEM((nb * k_len, MLA_H * D_NOPE), BF16),
                        pltpu.VMEM((nb * k_len, MLA_H * D_V), BF16),
                        pltpu.VMEM((nb * tq, MLA_H * D_V), BF16)],
        name="mla_attn")


def kernel(x_prompt, x_sample, state_rwkv, cache_mla_ckv, cache_mla_krope, c, c_ctx, mod_w, mod_b, norm_g,
           mlp_w1, mlp_w2, fft_w_out, conv_w_in, conv_w, conv_w_out, rwkv_mu, rwkv_w_r, rwkv_w_k, rwkv_w_v,
           rwkv_w_o, rwkv_w0, rwkv_w_l1, rwkv_w_l2, rwkv_a0, rwkv_a_l1, rwkv_a_l2, rwkv_g_l1, rwkv_g_l2,
           rwkv_k_k, rwkv_k_a, rwkv_r_k, rwkv_ln_g, rwkv_ln_b, mla_w_dq, mla_g_q, mla_w_uq, mla_w_dkv,
           mla_g_kv, mla_w_uk, mla_w_uv, mla_w_o):
    batch, seq, _ = x_prompt.shape
    dec_batch, dec_seq, _ = x_sample.shape
    past_len = cache_mla_ckv.shape[2]
    assert (batch * seq) % TM == 0 and TM % seq == 0 and dec_seq == TM and seq % CHUNK == 0

    np_tiles = batch * seq // TM
    ns_tiles = dec_batch * dec_seq // TM
    total_rows = (np_tiles + ns_tiles) * TM
    cs = jnp.concatenate([c_ctx[None, :], c, jnp.zeros((8 - 1 - dec_batch, D), F32)], axis=0)
    mod = _modulation(cs, mod_w, mod_b)

    grp_p = lambda i: 0
    grp_s = lambda i: 1 + i
    grp_all = lambda i: jnp.maximum(i - (np_tiles - 1), 0)
    new_rwkv, new_ckv, new_krope = [], [], []
    streams = [(True, seq, batch, grp_p), (False, dec_seq, dec_batch, grp_s)]
    cur = [_Rows(x_prompt.reshape(batch * seq, D), 0, np_tiles),
           _Rows(x_sample.reshape(dec_batch * dec_seq, D), 0, ns_tiles)]

    for i in range(DEPTH):
        kind, j = i % 4, i // 4
        for idx, (is_prompt, slen, nseq, grp) in enumerate(streams):
            src = cur[idx]
            sharing = src.array.shape[0] == total_rows
            dst = _Dest(total_rows, src.first, True) if sharing else _Dest(src.ntiles * TM, 0, False)
            shared = None
            if kind == 0:
                shared = _fnet(src, dst, mod, norm_g, fft_w_out, i, j, slen, grp)
            elif kind == 1:
                shared = _conv(src, dst, mod, norm_g, conv_w_in, conv_w, conv_w_out, i, j, slen, grp)
            elif kind == 2:
                p = dict(mu=rwkv_mu, w_r=rwkv_w_r, w_k=rwkv_w_k, w_v=rwkv_w_v, w_o=rwkv_w_o, w0=rwkv_w0,
                         w_l1=rwkv_w_l1, w_l2=rwkv_w_l2, a0=rwkv_a0, a_l1=rwkv_a_l1, a_l2=rwkv_a_l2,
                         g_l1=rwkv_g_l1, g_l2=rwkv_g_l2, k_k=rwkv_k_k, k_a=rwkv_k_a,
                         r_k=rwkv_r_k.reshape(-1, D), ln_g=rwkv_ln_g, ln_b=rwkv_ln_b)
                proj = _rwkv_proj(src, mod, norm_g, p, i, j, slen, grp)
                if is_prompt:
                    y, s_fin = _rwkv_scan(proj, None, nseq, slen, True)
                    new_rwkv.append(s_fin)
                else:
                    (y,) = _rwkv_scan(proj, state_rwkv[:, j], nseq, slen, False)
                shared = _rwkv_out(src, dst, mod, norm_g, y, proj[10], proj[9], p, i, j, grp)
            else:
                perm = _rope_swap_perm()
                w_uq = mla_w_uq[j].reshape(Q_RANK, MLA_H, D_NOPE + D_ROPE)
                w_uq_rope = w_uq[:, :, D_NOPE:]
                w_dkv_r = mla_w_dkv[j][:, KV_RANK:]
                p = dict(w_dq=mla_w_dq, g_q=mla_g_q, g_kv=mla_g_kv, w_uk=mla_w_uk, w_uv=mla_w_uv, w_o=mla_w_o,
                         w_uq_nope=w_uq[:, :, :D_NOPE].reshape(Q_RANK, MLA_H * D_NOPE),
                         w_uq_rope=w_uq_rope.reshape(Q_RANK, MLA_H * D_ROPE),
                         w_uq_rope_sw=w_uq_rope[:, :, perm].reshape(Q_RANK, MLA_H * D_ROPE),
                         w_dkv_c=mla_w_dkv[j][:, :KV_RANK], w_dkv_r=w_dkv_r, w_dkv_r_sw=w_dkv_r[:, perm])
                qn, qr, ckv, kr = _mla_proj(src, mod, norm_g, p, i, j, not is_prompt, grp)
                if is_prompt:
                    new_ckv.append(ckv.reshape(batch, seq, KV_RANK))
                    new_krope.append(kr.reshape(batch, seq, D_ROPE))
                    shared = _mla_attn(src, dst, mod, norm_g, qn, qr, ckv, kr, p, i, j, nseq, slen, slen,
                                       TM // slen, slen, lambda s: 0)
                else:
                    klen = past_len + slen
                    ckv_all = jnp.concatenate([cache_mla_ckv[:, j], ckv.reshape(nseq, slen, KV_RANK)], axis=1)
                    kr_all = jnp.concatenate([cache_mla_krope[:, j], kr.reshape(nseq, slen, D_ROPE)], axis=1)
                    shared = _mla_attn(src, dst, mod, norm_g, qn, qr, ckv_all.reshape(nseq * klen, KV_RANK),
                                       kr_all.reshape(nseq * klen, D_ROPE), p, i, j, nseq, slen, klen, 1, 256,
                                       lambda s: 1 + s)
            cur[idx] = _Rows(shared, dst.first, src.ntiles)
            if sharing:
                cur[1 - idx] = cur[1 - idx]._replace(array=shared)
        if i < DEPTH - 1:
            srcs = [_Rows(cur[0].array, 0, np_tiles + ns_tiles)] if cur[0].array is cur[1].array else cur
            both = _mlp(srcs, _Dest(total_rows, 0, False), mod, norm_g, mlp_w1, mlp_w2, i, grp_all)
            cur = [_Rows(both, 0, np_tiles), _Rows(both, np_tiles, ns_tiles)]
        else:
            y_prompt, y_sample = [
                _mlp([cur[idx]], _Dest(cur[idx].ntiles * TM, 0, False), mod, norm_g, mlp_w1, mlp_w2, i,
                     streams[idx][3]) for idx in range(2)]

    return (y_prompt.reshape(batch, seq, D), y_sample.reshape(dec_batch, dec_seq, D),
            jnp.stack(new_rwkv, axis=1), jnp.stack(new_ckv, axis=1), jnp.stack(new_krope, axis=1))
```

```python
import functools
from typing import NamedTuple

import numpy as np
import jax
import jax.numpy as jnp
from jax import lax
from jax.experimental import pallas as pl
from jax.experimental.pallas import tpu as pltpu

D = 1024
DEPTH = 4
N_MOD = 6
D_FF = 4 * D
EPS = 1e-6
GRID_W = 64
FFT_GROUPS = 8
FFT_GW = D // FFT_GROUPS
HS = 64
NH = D // HS
LORA_W = 64
LORA_A = 64
LORA_G = 128
GN_EPS = 64e-5
MLA_H = 8
D_NOPE = 128
D_ROPE = 64
D_V = 128
KV_RANK = 256
Q_RANK = 384
ROPE_BASE = 10000.0
MLA_SCALE = (D_NOPE + D_ROPE) ** -0.5

F32 = jnp.float32
BF16 = jnp.bfloat16

TM = 1024
LANES = 128
CHUNK = 64
SCAN_UNROLL = 4
SCAN_PAIRS = 2
ATTN_LOOKAHEAD = 2
VMEM_LIMIT = 58 * 1024 * 1024


def _cparams(sem):
    return pltpu.CompilerParams(dimension_semantics=sem, vmem_limit_bytes=VMEM_LIMIT)


def _dot(a, b):
    return jnp.dot(a.astype(BF16), b.astype(BF16), preferred_element_type=F32)


def _dot_nt(a, b):
    return lax.dot_general(a.astype(BF16), b.astype(BF16), (((1,), (1,)), ((), ())),
                           preferred_element_type=F32)


def _dot_tn(a, b):
    return lax.dot_general(a.astype(BF16), b.astype(BF16), (((0,), (0,)), ((), ())),
                           preferred_element_type=F32)


def _block_diag2(a, b):
    za = jnp.zeros((a.shape[0], b.shape[1]), a.dtype)
    zb = jnp.zeros((b.shape[0], a.shape[1]), a.dtype)
    return jnp.concatenate([jnp.concatenate([a, za], axis=1), jnp.concatenate([zb, b], axis=1)], axis=0)


def _cumsum_rows(x, reverse):
    n = x.shape[0]
    idx = lax.broadcasted_iota(jnp.int32, x.shape, 0)
    s = 1
    while s < n:
        if reverse:
            x = x + jnp.where(idx < n - s, pltpu.roll(x, n - s, 0), 0.0)
        else:
            x = x + jnp.where(idx >= s, pltpu.roll(x, s, 0), 0.0)
        s *= 2
    return x


def _dot_hilo_rhs(x, e):
    hi = x.astype(BF16)
    lo = (x - hi.astype(F32)).astype(BF16)
    return jnp.dot(hi, e, preferred_element_type=F32) + jnp.dot(lo, e, preferred_element_type=F32)


def _rms(x, g):
    return x * lax.rsqrt(jnp.mean(x * x, axis=-1, keepdims=True) + EPS) * g


def _sigmoid(x):
    return 1.0 / (1.0 + jnp.exp(-x))


def _softplus(x):
    return jnp.maximum(x, 0.0) + jnp.log(1.0 + jnp.exp(-jnp.abs(x)))


def _head_indicator(n):
    r = lax.broadcasted_iota(jnp.int32, (n, n), 0) // HS
    c = lax.broadcasted_iota(jnp.int32, (n, n), 1) // HS
    return jnp.where(r == c, 1.0, 0.0).astype(BF16)


def _shift_rows(z, seq_len):
    n = z.shape[0]
    pos = lax.broadcasted_iota(jnp.int32, z.shape, 0) % seq_len
    prev = jnp.where(pos == 0, 0.0, pltpu.roll(z, 1, 0))
    nxt = jnp.where(pos == seq_len - 1, 0.0, pltpu.roll(z, n - 1, 0))
    return prev, nxt


MOD_TN = 1536


def _mod_kernel(cs_ref, w_ref, b_ref, o_ref):
    cs = cs_ref[...]
    s = cs * _sigmoid(cs)
    o_ref[0] = _dot(s, w_ref[0]) + b_ref[0]


def _modulation(cs, mod_w, mod_b):
    nj = (N_MOD * D) // MOD_TN
    out = pl.pallas_call(
        _mod_kernel,
        grid=(DEPTH, nj),
        in_specs=[
            pl.BlockSpec((8, D), lambda l, j: (0, 0)),
            pl.BlockSpec((1, D, MOD_TN), lambda l, j: (l, 0, j)),
            pl.BlockSpec((1, 1, MOD_TN), lambda l, j: (l, 0, j)),
        ],
        out_specs=pl.BlockSpec((1, 8, MOD_TN), lambda l, j: (l, 0, j)),
        out_shape=jax.ShapeDtypeStruct((DEPTH, 8, N_MOD * D), F32),
        compiler_params=_cparams(("arbitrary", "arbitrary")),
        name="modulation",
    )(cs, mod_w, mod_b.reshape(DEPTH, 1, N_MOD * D))
    return out.reshape(DEPTH, 8, N_MOD, D)


class _Rows(NamedTuple):
    array: jax.Array
    first: int
    ntiles: int


class _Dest(NamedTuple):
    rows: int
    first: int
    inplace: bool


def _activation_call(kernel, *, grid, in_specs, args, dst, block_rows, out_index, scratch_shapes, name):
    assert not dst.inplace or args[0].shape[0] == dst.rows
    return pl.pallas_call(
        kernel, grid=grid, in_specs=in_specs,
        out_specs=pl.BlockSpec((block_rows, D), out_index),
        out_shape=jax.ShapeDtypeStruct((dst.rows, D), F32),
        scratch_shapes=scratch_shapes, input_output_aliases={0: 0} if dst.inplace else {},
        compiler_params=_cparams(("arbitrary",) * len(grid)), name=name)(*args)


def _mod_spec(layer, group_of_tile, ngrid):
    if ngrid == 1:
        return pl.BlockSpec((1, 1, N_MOD, D), lambda i: (layer, group_of_tile(i), 0, 0))
    return pl.BlockSpec((1, 1, N_MOD, D), lambda i, j: (layer, group_of_tile(i), 0, 0))


def _normg_spec(layer, ngrid):
    if ngrid == 1:
        return pl.BlockSpec((1, 4, D), lambda i: (layer, 0, 0))
    return pl.BlockSpec((1, 4, D), lambda i, j: (layer, 0, 0))


MLP_FC = 1024


def _mlp_kernel(*refs, n_first):
    if n_first is None:
        (x_ref,), rest = refs[:1], refs[1:]
        read_x = lambda: x_ref[...]
    else:
        (xa_ref, xb_ref), rest = refs[:2], refs[2:]
        read_x = lambda: jnp.where(pl.program_id(0) < n_first, xa_ref[...], xb_ref[...])
    m_ref, g_ref, w1_ref, w2_ref, o_ref, h_scr, acc_scr = rest
    j = pl.program_id(1)
    m = m_ref[0, 0]
    g = g_ref[0]

    last = pl.num_programs(1) - 1

    def chunk():
        a = jnp.dot(h_scr[...], w1_ref[0].astype(BF16), preferred_element_type=F32)
        a = jnp.maximum(a, 0.0)
        return _dot(a * a, w2_ref[0])

    @pl.when(j == 0)
    def _():
        h = _rms(read_x(), g[2:3]) * (1.0 + m[4:5]) + m[3:4]
        h_scr[...] = h.astype(BF16)
        acc_scr[...] = chunk()

    @pl.when((j > 0) & (j < last))
    def _():
        acc_scr[...] += chunk()

    @pl.when(j == last)
    def _():
        f = acc_scr[...] + chunk()
        o_ref[...] = read_x() + m[5:6] * _rms(f, g[3:4])


def _mlp(srcs, dst, mod, norm_g, w1, w2, layer, group_of_tile):
    nj = D_FF // MLP_FC
    assert nj >= 2
    if len(srcs) == 1:
        (src,) = srcs
        ntiles, n_first = src.ntiles, None
        x_specs = [pl.BlockSpec((TM, D), lambda i, j: (i + src.first, 0))]
    else:
        sa, sb = srcs
        ntiles, n_first = sa.ntiles + sb.ntiles, sa.ntiles
        x_specs = [pl.BlockSpec((TM, D), lambda i, j: (jnp.minimum(i, sa.ntiles - 1) + sa.first, 0)),
                   pl.BlockSpec((TM, D), lambda i, j: (jnp.maximum(i - sa.ntiles, 0) + sb.first, 0))]
    return _activation_call(
        functools.partial(_mlp_kernel, n_first=n_first),
        grid=(ntiles, nj),
        in_specs=x_specs + [
            _mod_spec(layer, group_of_tile, 2),
            _normg_spec(layer, 2),
            pl.BlockSpec((1, D, MLP_FC), lambda i, j: (layer, 0, j)),
            pl.BlockSpec((1, MLP_FC, D), lambda i, j: (layer, j, 0)),
        ],
        args=[s.array for s in srcs] + [mod, norm_g, w1, w2],
        dst=dst, block_rows=TM, out_index=lambda i, j: (i + dst.first, 0),
        scratch_shapes=[pltpu.VMEM((TM, D), BF16), pltpu.VMEM((TM, D), F32)],
        name="mlp")


def _dft_mats(n):
    idx = np.arange(n, dtype=np.int64)
    ang = (2.0 * np.pi / n) * ((idx[:, None] * idx[None, :]) % n).astype(np.float64)
    scale = 1.0 / np.sqrt(n)
    return np.cos(ang) * scale, np.sin(ang) * scale


def _fnet_kernel(x_ref, m_ref, g_ref, cs_ref, cn_ref, sn_ref, w_ref, o_ref, p_scr, q_scr, f_scr,
                 *, seq_len):
    m = m_ref[0, 0]
    g = g_ref[0]
    x = x_ref[...]
    h = (_rms(x, g[0:1]) * (1.0 + m[1:2]) + m[0:1]).astype(BF16)
    cs = cs_ref[...].astype(BF16)
    for gi in range(FFT_GROUPS):
        pq = jnp.dot(h[:, gi * FFT_GW:(gi + 1) * FFT_GW], cs, preferred_element_type=F32)
        p_scr[:, gi * FFT_GW:(gi + 1) * FFT_GW] = pq[:, :FFT_GW].astype(BF16)
        q_scr[:, gi * FFT_GW:(gi + 1) * FFT_GW] = pq[:, FFT_GW:].astype(BF16)
    cn = cn_ref[...].astype(BF16)
    sn = sn_ref[...].astype(BF16)
    for s in range(TM // seq_len):
        rows = slice(s * seq_len, (s + 1) * seq_len)
        f = (jnp.dot(cn, p_scr[rows, :], preferred_element_type=F32)
             - jnp.dot(sn, q_scr[rows, :], preferred_element_type=F32))
        f_scr[rows, :] = f.astype(BF16)
    o = jnp.dot(f_scr[...], w_ref[0].astype(BF16), preferred_element_type=F32)
    o_ref[...] = x + m[2:3] * _rms(o, g[1:2])


def _fnet(src, dst, mod, norm_g, w_out, layer, j, seq_len, group_of_tile):
    cg, sg = _dft_mats(FFT_GW)
    cs = jnp.asarray(np.concatenate([cg, sg], axis=1), F32)
    cn_np, sn_np = _dft_mats(seq_len)
    cn = jnp.asarray(cn_np, F32)
    sn = jnp.asarray(sn_np, F32)
    return _activation_call(
        functools.partial(_fnet_kernel, seq_len=seq_len),
        grid=(src.ntiles,),
        in_specs=[
            pl.BlockSpec((TM, D), lambda i: (i + src.first, 0)),
            _mod_spec(layer, group_of_tile, 1),
            _normg_spec(layer, 1),
            pl.BlockSpec((FFT_GW, 2 * FFT_GW), lambda i: (0, 0)),
            pl.BlockSpec((seq_len, seq_len), lambda i: (0, 0)),
            pl.BlockSpec((seq_len, seq_len), lambda i: (0, 0)),
            pl.BlockSpec((1, D, D), lambda i: (j, 0, 0)),
        ],
        args=[src.array, mod, norm_g, cs, cn, sn, w_out],
        dst=dst, block_rows=TM, out_index=lambda i: (i + dst.first, 0),
        scratch_shapes=[pltpu.VMEM((TM, D), BF16), pltpu.VMEM((TM, D), BF16),
                        pltpu.VMEM((TM, D), BF16)],
        name="fourier_mix")


CONV_CW = 512


def _conv_kernel(x_ref, m_ref, g_ref, wb_ref, wc_ref, wu_ref, cw_ref, wo_ref, o_ref, h_scr, acc_scr,
                 *, seq_len):
    j = pl.program_id(1)
    m = m_ref[0, 0]
    g = g_ref[0]

    def chunk():
        h = h_scr[...]
        hw = CONV_CW // 2
        proj = []
        for s in range(2):
            cols = slice(s * hw, (s + 1) * hw)
            proj.append([jnp.dot(h, w_ref[0, :, cols].astype(BF16), preferred_element_type=F32)
                         for w_ref in (wb_ref, wc_ref, wu_ref)])
        out = None
        for s in range(2):
            cols = slice(s * hw, (s + 1) * hw)
            bg, cg, u = proj[s]
            z = cg * u
            z_prev, z_next = _shift_rows(z, seq_len)
            conv = z_prev * cw_ref[0, 0:1, cols] + z * cw_ref[0, 1:2, cols] + z_next * cw_ref[0, 2:3, cols]
            part = _dot(bg * conv, wo_ref[0, cols, :])
            out = part if out is None else out + part
        return out

    last = pl.num_programs(1) - 1

    @pl.when(j == 0)
    def _():
        h = _rms(x_ref[...], g[0:1]) * (1.0 + m[1:2]) + m[0:1]
        h_scr[...] = h.astype(BF16)
        acc_scr[...] = chunk()

    @pl.when((j > 0) & (j < last))
    def _():
        acc_scr[...] += chunk()

    @pl.when(j == last)
    def _():
        o_ref[...] = x_ref[...] + m[2:3] * _rms(acc_scr[...] + chunk(), g[1:2])


def _conv(src, dst, mod, norm_g, w_in, w_conv, w_out, layer, j, seq_len, group_of_tile):
    nj = D // CONV_CW
    assert nj >= 2
    return _activation_call(
        functools.partial(_conv_kernel, seq_len=seq_len),
        grid=(src.ntiles, nj),
        in_specs=[
            pl.BlockSpec((TM, D), lambda i, c: (i + src.first, 0)),
            _mod_spec(layer, group_of_tile, 2),
            _normg_spec(layer, 2),
            pl.BlockSpec((1, D, CONV_CW), lambda i, c: (j, 0, c)),
            pl.BlockSpec((1, D, CONV_CW), lambda i, c: (j, 0, nj + c)),
            pl.BlockSpec((1, D, CONV_CW), lambda i, c: (j, 0, 2 * nj + c)),
            pl.BlockSpec((1, 3, CONV_CW), lambda i, c: (j, 0, c)),
            pl.BlockSpec((1, CONV_CW, D), lambda i, c: (j, c, 0)),
        ],
        args=[src.array, mod, norm_g, w_in, w_in, w_in, w_conv, w_out],
        dst=dst, block_rows=TM, out_index=lambda i, c: (i + dst.first, 0),
        scratch_shapes=[pltpu.VMEM((TM, D), BF16), pltpu.VMEM((TM, D), F32)],
        name="short_conv")


RWP_CW = 256
RW_OUT_DTYPES = (BF16, BF16, BF16, F32, BF16, BF16, F32, BF16, BF16, F32, F32)
RW_CW = 512


def _rwkv_proj_kernel(x_ref, m_ref, g_ref, mu_ref, wr_ref, wk_ref, wv_ref, w0_ref, wl1_ref, wl2_ref,
                      a0_ref, al1_ref, al2_ref, gl1_ref, gl2_ref, kk_ref, ka_ref, rk_ref,
                      r_o, v_o, kk_o, ld0_o, b0_o, kt0_o, ld1_o, b1_o, kt1_o, g_o, bonus_o,
                      xr_scr, xk_scr, xv_scr, tw_scr, ta_scr, sg_scr, *, seq_len):
    j = pl.program_id(1)

    def chunk():
        e = _head_indicator(RWP_CW)
        r = jnp.dot(xr_scr[...], wr_ref[0].astype(BF16), preferred_element_type=F32)
        k = jnp.dot(xk_scr[...], wk_ref[0].astype(BF16), preferred_element_type=F32)
        v = jnp.dot(xv_scr[...], wv_ref[0].astype(BF16), preferred_element_type=F32)
        g_o[...] = jnp.dot(sg_scr[...], gl2_ref[0].astype(BF16), preferred_element_type=F32)
        zws = [jnp.dot(tw_scr[d], wl2_ref[0, d].astype(BF16), preferred_element_type=F32) for d in range(2)]
        zas = [jnp.dot(ta_scr[d], al2_ref[0, d].astype(BF16), preferred_element_type=F32) for d in range(2)]
        kk = k * kk_ref[...]
        kk = kk * lax.rsqrt(_dot(kk * kk, e) + 1e-12)
        r_o[...] = r.astype(r_o.dtype)
        v_o[...] = v.astype(v_o.dtype)
        kk_o[...] = kk.astype(kk_o.dtype)
        k_a = ka_ref[...]
        kt_sum = None
        for d, (ld_o, b_o, kt_o) in enumerate(((ld0_o, b0_o, kt0_o), (ld1_o, b1_o, kt1_o))):
            logw = -_softplus(-(w0_ref[0, d:d + 1, :] + zws[d])) - 0.5
            ld_o[...] = -jnp.exp(logw)
            a = _sigmoid(a0_ref[0, d:d + 1, :] + zas[d])
            kt = k * (1.0 + (a - 1.0) * k_a)
            b_o[...] = (kk * a).astype(b_o.dtype)
            kt_o[...] = kt.astype(kt_o.dtype)
            kt_sum = kt if kt_sum is None else kt_sum + kt
        bonus_o[...] = _dot(r * rk_ref[...] * kt_sum, e) * v

    @pl.when(j == 0)
    def _():
        m = m_ref[0, 0]
        g = g_ref[0]
        mu = mu_ref[0]
        h = _rms(x_ref[...], g[0:1]) * (1.0 + m[1:2]) + m[0:1]
        h_prev, h_next = _shift_rows(h, seq_len)
        dx = 0.5 * (h_prev + h_next) - h
        xr_scr[...] = (h + dx * mu[0:1]).astype(BF16)
        xk_scr[...] = (h + dx * mu[2:3]).astype(BF16)
        xv_scr[...] = (h + dx * mu[3:4]).astype(BF16)
        xw = (h + dx * mu[1:2]).astype(BF16)
        xa = (h + dx * mu[4:5]).astype(BF16)
        xg = (h + dx * mu[5:6]).astype(BF16)
        for d in range(2):
            tw_scr[d] = jnp.tanh(_dot(xw, wl1_ref[0, d])).astype(BF16)
            ta_scr[d] = _dot(xa, al1_ref[0, d]).astype(BF16)
        sg_scr[...] = _sigmoid(_dot(xg, gl1_ref[0])).astype(BF16)
        chunk()

    @pl.when(j > 0)
    def _():
        chunk()


def _rwkv_proj(src, mod, norm_g, p, layer, j, seq_len, group_of_tile):
    n = src.ntiles * TM
    nj = D // RWP_CW
    tile = pl.BlockSpec((TM, RWP_CW), lambda i, c: (i, c))
    wcol = lambda: pl.BlockSpec((1, D, RWP_CW), lambda i, c: (j, 0, c))
    vec = lambda: pl.BlockSpec((1, RWP_CW), lambda i, c: (j, c))
    return pl.pallas_call(
        functools.partial(_rwkv_proj_kernel, seq_len=seq_len),
        grid=(src.ntiles, nj),
        in_specs=[
            pl.BlockSpec((TM, D), lambda i, c: (i + src.first, 0)),
            _mod_spec(layer, group_of_tile, 2),
            _normg_spec(layer, 2),
            pl.BlockSpec((1, 6, D), lambda i, c: (j, 0, 0)),
            wcol(), wcol(), wcol(),
            pl.BlockSpec((1, 2, RWP_CW), lambda i, c: (j, 0, c)),
            pl.BlockSpec((1, 2, D, LORA_W), lambda i, c: (j, 0, 0, 0)),
            pl.BlockSpec((1, 2, LORA_W, RWP_CW), lambda i, c: (j, 0, 0, c)),
            pl.BlockSpec((1, 2, RWP_CW), lambda i, c: (j, 0, c)),
            pl.BlockSpec((1, 2, D, LORA_A), lambda i, c: (j, 0, 0, 0)),
            pl.BlockSpec((1, 2, LORA_A, RWP_CW), lambda i, c: (j, 0, 0, c)),
            pl.BlockSpec((1, D, LORA_G), lambda i, c: (j, 0, 0)),
            pl.BlockSpec((1, LORA_G, RWP_CW), lambda i, c: (j, 0, c)),
            vec(), vec(), vec(),
        ],
        out_specs=[tile] * len(RW_OUT_DTYPES),
        out_shape=[jax.ShapeDtypeStruct((n, D), dt) for dt in RW_OUT_DTYPES],
        scratch_shapes=[pltpu.VMEM((TM, D), BF16), pltpu.VMEM((TM, D), BF16), pltpu.VMEM((TM, D), BF16),
                        pltpu.VMEM((2, TM, LORA_W), BF16), pltpu.VMEM((2, TM, LORA_A), BF16),
                        pltpu.VMEM((TM, LORA_G), BF16)],
        compiler_params=_cparams(("arbitrary", "arbitrary")),
        name="rwkv_proj",
    )(src.array, mod, norm_g, p["mu"], p["w_r"], p["w_k"], p["w_v"], p["w0"], p["w_l1"], p["w_l2"],
      p["a0"], p["a_l1"], p["a_l2"], p["g_l1"], p["g_l2"], p["k_k"], p["k_a"], p["r_k"])


def _scan_precompute(units):
    c = CHUNK
    c2 = 2 * c
    row = lax.broadcasted_iota(jnp.int32, (c2, c2), 0)
    col = lax.broadcasted_iota(jnp.int32, (c2, c2), 1)
    head_a = lax.broadcasted_iota(jnp.int32, (c, LANES), 1) < HS
    own_lanes = jnp.concatenate([head_a, jnp.logical_not(head_a)], axis=0)
    eye = jnp.where(row == col, 1.0, 0.0)

    def two_heads(t):
        return jnp.concatenate([jnp.where(head_a, t, 0.0), jnp.where(head_a, 0.0, t)], axis=0)

    def causal(reverse):
        if reverse:
            return (col % c) > (row % c), (col % c) >= (row % c)
        return (col % c) < (row % c), (col % c) <= (row % c)

    masks = {rev: causal(rev) for rev in sorted({u[6] for u in units})}

    cums = [_cumsum_rows(u[0], u[6]) for u in units]
    st = []
    for (ld, kk, beta, kt, r, v, rev), cum in zip(units, cums):
        tot = cum[0:1] if rev else cum[c - 1:c]
        ginv = jnp.exp(-cum)
        tail = jnp.exp(tot - cum)
        st.append(dict(
            rev=rev, etot=jnp.exp(tot),
            a_t=two_heads(-kk * jnp.exp(cum - ld)), r_t=two_heads(r * jnp.exp(cum)),
            bk=jnp.concatenate([two_heads(beta * ginv), two_heads(kt * ginv)], axis=0),
            bkg=jnp.concatenate([two_heads(beta * tail), two_heads(kt * tail)], axis=0),
            v2=jnp.concatenate([v, v], axis=0), vh=two_heads(v)))
    grams = [_dot_nt(jnp.concatenate([s["a_t"], s["r_t"]], axis=0), s["bk"]) for s in st]
    for s, gram in zip(st, grams):
        strict, incl = masks[s["rev"]]
        s["l_ab"] = jnp.where(strict, gram[:c2, :c2], 0.0)
        s["l_ak"] = jnp.where(strict, gram[:c2, c2:], 0.0)
        s["t_rb"] = jnp.where(incl, gram[c2:, :c2], 0.0)
        s["t_rk"] = jnp.where(incl, gram[c2:, c2:], 0.0)
    lvs = [_dot(s["l_ak"], s["v2"]) for s in st]
    minvs = [eye for _ in st]
    b = 1
    while b < c:
        same = (row // (2 * b)) == (col // (2 * b))
        es = []
        for s in st:
            first, second = (col % (2 * b)) < b, (row % (2 * b)) >= b
            if s["rev"]:
                first, second = (row % (2 * b)) < b, (col % (2 * b)) >= b
            es.append(jnp.where(same & first & second, s["l_ab"], 0.0))
        if b == 1:
            minvs = [m + e for m, e in zip(minvs, es)]
        else:
            half = [_dot(m, e) for m, e in zip(minvs, es)]
            minvs = [m + _dot(h, m) for m, h in zip(minvs, half)]
        b *= 2
    mms = [_dot(m, jnp.concatenate([s["a_t"], lv], axis=1)) for m, s, lv in zip(minvs, st, lvs)]
    zero = jnp.zeros((c2, LANES), BF16)
    tts = [_dot(jnp.concatenate([s["t_rb"], s["t_rk"]], axis=1),
                jnp.concatenate([mm.astype(BF16), jnp.concatenate([zero, s["v2"].astype(BF16)], axis=1)], axis=0))
           for s, mm in zip(st, mms)]
    ps = [_dot_tn(mm[:, :LANES], s["bkg"][:c2]) for s, mm in zip(st, mms)]
    qs = [_dot_tn(jnp.concatenate([jnp.where(own_lanes, mm[:, LANES:], 0.0), s["vh"]], axis=0), s["bkg"])
          for s, mm in zip(st, mms)]
    out = []
    for s, p, q, tt in zip(st, ps, qs, tts):
        reff = s["r_t"] + tt[:, :LANES]
        out.append(dict(etot=s["etot"], p=p, q=q, reff=reff[:c] + reff[c:],
                        y0=jnp.where(head_a, tt[:c, LANES:], tt[c:, LANES:])))
    return out


def _rwkv_scan_kernel(*refs, seq_len, has_init, want_final):
    r_ref, v_ref, kk_ref, ld0_ref, b0_ref, kt0_ref, ld1_ref, b1_ref, kt1_ref = refs[:9]
    pos = 9
    if has_init:
        s0_ref = refs[pos]
        pos += 1
    y_ref = refs[pos]
    pos += 1
    if want_final:
        sf_ref = refs[pos]
        pos += 1
    g_scr, yb_scr = refs[pos:pos + 2]
    nc = seq_len // CHUNK
    ngroups = nc // SCAN_UNROLL
    per_dir = ((ld0_ref, b0_ref, kt0_ref), (ld1_ref, b1_ref, kt1_ref))
    y_dst = (y_ref, yb_scr)
    nchain = 2 * SCAN_PAIRS

    def group(gi, states):
        where, units = [], []
        for u in range(SCAN_UNROLL):
            for pp in range(SCAN_PAIRS):
                lanes = slice(pp * LANES, (pp + 1) * LANES)
                for d in range(2):
                    ld_ref, b_ref, kt_ref = per_dir[d]
                    cidx = gi * SCAN_UNROLL + u
                    if d == 1:
                        cidx = nc - 1 - cidx
                    start = cidx * CHUNK
                    rw = slice(start, start + CHUNK) if isinstance(start, int) else pl.ds(
                        pl.multiple_of(start, CHUNK), CHUNK)
                    where.append((rw, lanes))
                    units.append(tuple(ref[rw, lanes].astype(F32) for ref in
                                       (ld_ref, kk_ref, b_ref, kt_ref, r_ref, v_ref)) + (d == 1,))
        pre = _scan_precompute(units)
        states = list(states)
        for u in range(SCAN_UNROLL):
            cur = pre[u * nchain:(u + 1) * nchain]
            ys = [_dot_nt(cu["reff"], g) for cu, g in zip(cur, states)]
            gp = [_dot(g, cu["p"]) for cu, g in zip(cur, states)]
            states = [states[ch] * cur[ch]["etot"] + gp[ch] + cur[ch]["q"] for ch in range(nchain)]
            for ch in range(nchain):
                rw, lanes = where[u * nchain + ch]
                y_dst[ch % 2][rw, lanes] = ys[ch] + cur[ch]["y0"]
        return states

    init = [_block_diag2(s0_ref[0, ch % 2, 2 * (ch // 2)], s0_ref[0, ch % 2, 2 * (ch // 2) + 1])
            if has_init else jnp.zeros((LANES, LANES), F32) for ch in range(nchain)]
    if ngroups == 1:
        final = group(0, init)
    else:
        for ch in range(nchain):
            g_scr[ch] = init[ch]

        def body(gi, carry):
            new = group(gi, [g_scr[ch] for ch in range(nchain)])
            for ch in range(nchain):
                g_scr[ch] = new[ch]
            return carry

        lax.fori_loop(0, ngroups, body, 0)
        final = [g_scr[ch] for ch in range(nchain)]
    y_ref[...] += yb_scr[...]
    if want_final:
        for ch in range(nchain):
            sf_ref[0, ch % 2, 2 * (ch // 2)] = final[ch][:HS, :HS]
            sf_ref[0, ch % 2, 2 * (ch // 2) + 1] = final[ch][HS:, HS:]


def _rwkv_scan(proj, s_init, n_seq, seq_len, want_final):
    r, v, kk, ld0, b0, kt0, ld1, b1, kt1 = proj[:9]
    n = r.shape[0]
    npair = D // LANES
    width = SCAN_PAIRS * LANES
    blk = pl.BlockSpec((seq_len, width), lambda b, p: (b, p))
    st_spec = pl.BlockSpec((1, 2, 2 * SCAN_PAIRS, HS, HS), lambda b, p: (b, 0, p, 0, 0))
    in_specs = [blk] * 9
    args = [r, v, kk, ld0, b0, kt0, ld1, b1, kt1]
    has_init = s_init is not None
    if has_init:
        in_specs.append(st_spec)
        args.append(s_init)
    out_specs = [blk]
    out_shape = [jax.ShapeDtypeStruct((n, D), F32)]
    if want_final:
        out_specs.append(st_spec)
        out_shape.append(jax.ShapeDtypeStruct((n_seq, 2, NH, HS, HS), F32))
    res = pl.pallas_call(
        functools.partial(_rwkv_scan_kernel, seq_len=seq_len, has_init=has_init, want_final=want_final),
        grid=(n_seq, npair // SCAN_PAIRS),
        in_specs=in_specs,
        out_specs=out_specs,
        out_shape=out_shape,
        scratch_shapes=[pltpu.VMEM((2 * SCAN_PAIRS, LANES, LANES), F32), pltpu.VMEM((seq_len, width), F32)],
        compiler_params=_cparams(("arbitrary", "arbitrary")),
        name="rwkv_scan",
    )(*args)
    return res


def _rwkv_out_kernel(x_ref, m_ref, g_ref, y_ref, bonus_ref, gate_ref, lng_ref, lnb_ref, wo_ref, o_ref,
                     acc_scr):
    j = pl.program_id(1)
    last = pl.num_programs(1) - 1

    def chunk():
        e = _head_indicator(RW_CW)
        y = y_ref[...]
        mean = _dot_hilo_rhs(y, e) * (1.0 / HS)
        yc = y - mean
        var = _dot(yc * yc, e) * (1.0 / HS)
        yn = yc * lax.rsqrt(var + GN_EPS) * lng_ref[...] + lnb_ref[...]
        yn = (yn + bonus_ref[...]) * gate_ref[...]
        return _dot(yn, wo_ref[0])

    @pl.when(j == 0)
    def _():
        acc_scr[...] = chunk()

    @pl.when((j > 0) & (j < last))
    def _():
        acc_scr[...] += chunk()

    @pl.when(j == last)
    def _():
        m = m_ref[0, 0]
        g = g_ref[0]
        o_ref[...] = x_ref[...] + m[2:3] * _rms(acc_scr[...] + chunk(), g[1:2])


def _rwkv_out(src, dst, mod, norm_g, y, bonus, gate, p, layer, j, group_of_tile):
    nj = D // RW_CW
    assert nj >= 2
    tile = pl.BlockSpec((TM, RW_CW), lambda i, c: (i, c))
    vec = pl.BlockSpec((1, RW_CW), lambda i, c: (j, c))
    return _activation_call(
        _rwkv_out_kernel,
        grid=(src.ntiles, nj),
        in_specs=[
            pl.BlockSpec((TM, D), lambda i, c: (i + src.first, 0)),
            _mod_spec(layer, group_of_tile, 2),
            _normg_spec(layer, 2),
            tile, tile, tile, vec, vec,
            pl.BlockSpec((1, RW_CW, D), lambda i, c: (j, c, 0)),
        ],
        args=[src.array, mod, norm_g, y, bonus, gate, p["ln_g"], p["ln_b"], p["w_o"]],
        dst=dst, block_rows=TM, out_index=lambda i, c: (i + dst.first, 0),
        scratch_shapes=[pltpu.VMEM((TM, D), F32)],
        name="rwkv_out")


def _rope_tables(n):
    rows = n // GRID_W
    row = np.repeat(np.arange(rows), GRID_W)
    col = np.tile(np.arange(GRID_W), rows)
    pos = np.stack([row, col], axis=-1).astype(np.float64)
    quarter = D_ROPE // 4
    inv = ROPE_BASE ** (-np.arange(quarter, dtype=np.float64) / quarter)
    ang = pos[:, :, None] * inv
    cos = np.cos(ang)
    sin = np.sin(ang)
    cos_t = np.concatenate([cos, cos], axis=-1).reshape(n, D_ROPE)
    sin_t = np.concatenate([-sin, sin], axis=-1).reshape(n, D_ROPE)
    return cos_t.astype(np.float32), sin_t.astype(np.float32)


def _rope_swap_perm():
    quarter = D_ROPE // 4
    base = np.arange(D_ROPE)
    return np.where((base % (2 * quarter)) < quarter, base + quarter, base - quarter)


def _mla_proj_kernel(*refs, positional):
    (x_ref, m_ref, g_ref, wdq_ref, gq_ref, wqn_ref, wqr_ref, wqs_ref, wdkv_ref, wkr_ref, wks_ref,
     gkv_ref) = refs[:12]
    pos = 12
    if positional:
        cosq_ref, sinq_ref, cosk_ref, sink_ref = refs[pos:pos + 4]
        pos += 4
    qn_o, qr_o, ckv_o, kr_o = refs[pos:pos + 4]
    m = m_ref[0, 0]
    g = g_ref[0]
    h = (_rms(x_ref[...], g[0:1]) * (1.0 + m[1:2]) + m[0:1]).astype(BF16)
    ql = jnp.dot(h, wdq_ref[0].astype(BF16), preferred_element_type=F32)
    ql = (ql * lax.rsqrt(jnp.mean(ql * ql, axis=-1, keepdims=True) + EPS) * gq_ref[...]).astype(BF16)
    qn_o[...] = jnp.dot(ql, wqn_ref[...].astype(BF16), preferred_element_type=F32)
    qr = jnp.dot(ql, wqr_ref[...].astype(BF16), preferred_element_type=F32)
    ckv = jnp.dot(h, wdkv_ref[...].astype(BF16), preferred_element_type=F32)
    ckv_o[...] = ckv * lax.rsqrt(jnp.mean(ckv * ckv, axis=-1, keepdims=True) + EPS) * gkv_ref[...]
    kr = jnp.dot(h, wkr_ref[...].astype(BF16), preferred_element_type=F32)
    if positional:
        qs = jnp.dot(ql, wqs_ref[...].astype(BF16), preferred_element_type=F32)
        ks = jnp.dot(h, wks_ref[...].astype(BF16), preferred_element_type=F32)
        qr = qr * cosq_ref[...] + qs * sinq_ref[...]
        kr = kr * cosk_ref[...] + ks * sink_ref[...]
    qr_o[...] = qr
    kr_o[...] = kr


def _mla_proj(src, mod, norm_g, p, layer, j, positional, group_of_tile):
    n = src.ntiles * TM
    full = lambda shape: pl.BlockSpec(shape, lambda i: (0,) * len(shape))
    in_specs = [
        pl.BlockSpec((TM, D), lambda i: (i + src.first, 0)),
        _mod_spec(layer, group_of_tile, 1),
        _normg_spec(layer, 1),
        pl.BlockSpec((1, D, Q_RANK), lambda i: (j, 0, 0)),
        pl.BlockSpec((1, Q_RANK), lambda i: (j, 0)),
        full((Q_RANK, MLA_H * D_NOPE)), full((Q_RANK, MLA_H * D_ROPE)), full((Q_RANK, MLA_H * D_ROPE)),
        full((D, KV_RANK)), full((D, D_ROPE)), full((D, D_ROPE)),
        pl.BlockSpec((1, KV_RANK), lambda i: (j, 0)),
    ]
    args = [src.array, mod, norm_g, p["w_dq"], p["g_q"], p["w_uq_nope"], p["w_uq_rope"], p["w_uq_rope_sw"],
            p["w_dkv_c"], p["w_dkv_r"], p["w_dkv_r_sw"], p["g_kv"]]
    if positional:
        cos_t, sin_t = _rope_tables(TM)
        in_specs += [full((TM, MLA_H * D_ROPE)), full((TM, MLA_H * D_ROPE)),
                     full((TM, D_ROPE)), full((TM, D_ROPE))]
        args += [jnp.asarray(np.tile(cos_t, (1, MLA_H))), jnp.asarray(np.tile(sin_t, (1, MLA_H))),
                 jnp.asarray(cos_t), jnp.asarray(sin_t)]
    widths = (MLA_H * D_NOPE, MLA_H * D_ROPE, KV_RANK, D_ROPE)
    return pl.pallas_call(
        functools.partial(_mla_proj_kernel, positional=positional),
        grid=(src.ntiles,),
        in_specs=in_specs,
        out_specs=[pl.BlockSpec((TM, w), lambda i: (i, 0)) for w in widths],
        out_shape=[jax.ShapeDtypeStruct((n, w), F32) for w in widths],
        compiler_params=_cparams(("arbitrary",)),
        name="mla_proj",
    )(*args)


def _mla_attn_kernel(x_ref, m_ref, g_ref, qn_ref, qr_ref, ckv_ref, kr_ref, wuk_ref, wuv_ref, wo_ref,
                     o_ref, kn_scr, vv_scr, oh_scr, *, nb, tq, k_len):
    qi = pl.program_id(1)

    @pl.when(qi == 0)
    def _():
        ckv = ckv_ref[...].astype(BF16)
        kn_scr[...] = jnp.dot(ckv, wuk_ref[0].astype(BF16), preferred_element_type=F32).astype(BF16)
        vv_scr[...] = jnp.dot(ckv, wuv_ref[0].astype(BF16), preferred_element_type=F32).astype(BF16)

    units = [(b, hd) for b in range(nb) for hd in range(MLA_H)]

    def scores(unit):
        b, hd = unit
        qrows = slice(b * tq, (b + 1) * tq)
        krows = slice(b * k_len, (b + 1) * k_len)
        q = jnp.concatenate([qn_ref[qrows, hd * D_NOPE:(hd + 1) * D_NOPE].astype(BF16),
                             qr_ref[qrows, hd * D_ROPE:(hd + 1) * D_ROPE].astype(BF16)], axis=1)
        k = jnp.concatenate([kn_scr[krows, hd * D_NOPE:(hd + 1) * D_NOPE],
                             kr_ref[krows, :].astype(BF16)], axis=1)
        return _dot_nt(q, k) * MLA_SCALE

    pending = [scores(u) for u in units[:ATTN_LOOKAHEAD]]
    for idx, (b, hd) in enumerate(units):
        if idx + ATTN_LOOKAHEAD < len(units):
            pending.append(scores(units[idx + ATTN_LOOKAHEAD]))
        s = pending[idx]
        pexp = jnp.exp(s - jnp.max(s, axis=-1, keepdims=True))
        pv = jnp.dot(pexp.astype(BF16), vv_scr[b * k_len:(b + 1) * k_len, hd * D_V:(hd + 1) * D_V],
                     preferred_element_type=F32)
        oh_scr[b * tq:(b + 1) * tq, hd * D_V:(hd + 1) * D_V] = (
            pv / jnp.sum(pexp, axis=-1, keepdims=True)).astype(BF16)
    o = jnp.dot(oh_scr[...], wo_ref[0].astype(BF16), preferred_element_type=F32)
    m = m_ref[0, 0]
    g = g_ref[0]
    o_ref[...] = x_ref[...] + m[2:3] * _rms(o, g[1:2])


def _mla_attn(src, dst, mod, norm_g, qn, qr, ckv_all, kr_all, p, layer, j, n_seq, q_len, k_len, nb, tq,
              group_of_step):
    nq = q_len // tq
    assert nb == 1 or nq == 1
    rows = nb * tq
    x_first, o_first = src.first * TM // rows, dst.first * TM // rows
    return _activation_call(
        functools.partial(_mla_attn_kernel, nb=nb, tq=tq, k_len=k_len),
        grid=(n_seq // nb, nq),
        in_specs=[
            pl.BlockSpec((rows, D), lambda s, q: (s * nq + q + x_first, 0)),
            pl.BlockSpec((1, 1, N_MOD, D), lambda s, q: (layer, group_of_step(s), 0, 0)),
            pl.BlockSpec((1, 4, D), lambda s, q: (layer, 0, 0)),
            pl.BlockSpec((nb * tq, MLA_H * D_NOPE), lambda s, q: (s * nq + q, 0)),
            pl.BlockSpec((nb * tq, MLA_H * D_ROPE), lambda s, q: (s * nq + q, 0)),
            pl.BlockSpec((nb * k_len, KV_RANK), lambda s, q: (s, 0)),
            pl.BlockSpec((nb * k_len, D_ROPE), lambda s, q: (s, 0)),
            pl.BlockSpec((1, KV_RANK, MLA_H * D_NOPE), lambda s, q: (j, 0, 0)),
            pl.BlockSpec((1, KV_RANK, MLA_H * D_V), lambda s, q: (j, 0, 0)),
            pl.BlockSpec((1, MLA_H * D_V, D), lambda s, q: (j, 0, 0)),
        ],
        args=[src.array, mod, norm_g, qn, qr, ckv_all, kr_all, p["w_uk"], p["w_uv"], p["w_o"]],
        dst=dst, block_rows=rows, out_index=lambda s, q: (s * nq + q + o_first, 0),
        scratch_shapes=[pltpu.VMEM((nb * k_len, MLA_H * D_NOPE), BF16),
                        pltpu.VMEM((nb * k_len, MLA_H * D_V), BF16),
                        pltpu.VMEM((nb * tq, MLA_H * D_V), BF16)],
        name="mla_attn")


def kernel(x_prompt, x_sample, state_rwkv, cache_mla_ckv, cache_mla_krope, c, c_ctx, mod_w, mod_b, norm_g,
           mlp_w1, mlp_w2, fft_w_out, conv_w_in, conv_w, conv_w_out, rwkv_mu, rwkv_w_r, rwkv_w_k, rwkv_w_v,
           rwkv_w_o, rwkv_w0, rwkv_w_l1, rwkv_w_l2, rwkv_a0, rwkv_a_l1, rwkv_a_l2, rwkv_g_l1, rwkv_g_l2,
           rwkv_k_k, rwkv_k_a, rwkv_r_k, rwkv_ln_g, rwkv_ln_b, mla_w_dq, mla_g_q, mla_w_uq, mla_w_dkv,
           mla_g_kv, mla_w_uk, mla_w_uv, mla_w_o):
    batch, seq, _ = x_prompt.shape
    dec_batch, dec_seq, _ = x_sample.shape
    past_len = cache_mla_ckv.shape[2]
    assert (batch * seq) % TM == 0 and TM % seq == 0 and dec_seq == TM and seq % CHUNK == 0

    np_tiles = batch * seq // TM
    ns_tiles = dec_batch * dec_seq // TM
    total_rows = (np_tiles + ns_tiles) * TM
    cs = jnp.concatenate([c_ctx[None, :], c, jnp.zeros((8 - 1 - dec_batch, D), F32)], axis=0)
    mod = _modulation(cs, mod_w, mod_b)

    grp_p = lambda i: 0
    grp_s = lambda i: 1 + i
    grp_all = lambda i: jnp.maximum(i - (np_tiles - 1), 0)
    new_rwkv, new_ckv, new_krope = [], [], []
    streams = [(True, seq, batch, grp_p), (False, dec_seq, dec_batch, grp_s)]
    cur = [_Rows(x_prompt.reshape(batch * seq, D), 0, np_tiles),
           _Rows(x_sample.reshape(dec_batch * dec_seq, D), 0, ns_tiles)]

    for i in range(DEPTH):
        kind, j = i % 4, i // 4
        for idx, (is_prompt, slen, nseq, grp) in enumerate(streams):
            src = cur[idx]
            sharing = src.array.shape[0] == total_rows
            dst = _Dest(total_rows, src.first, True) if sharing else _Dest(src.ntiles * TM, 0, False)
            shared = None
            if kind == 0:
                shared = _fnet(src, dst, mod, norm_g, fft_w_out, i, j, slen, grp)
            elif kind == 1:
                shared = _conv(src, dst, mod, norm_g, conv_w_in, conv_w, conv_w_out, i, j, slen, grp)
            elif kind == 2:
                p = dict(mu=rwkv_mu, w_r=rwkv_w_r, w_k=rwkv_w_k, w_v=rwkv_w_v, w_o=rwkv_w_o, w0=rwkv_w0,
                         w_l1=rwkv_w_l1, w_l2=rwkv_w_l2, a0=rwkv_a0, a_l1=rwkv_a_l1, a_l2=rwkv_a_l2,
                         g_l1=rwkv_g_l1, g_l2=rwkv_g_l2, k_k=rwkv_k_k, k_a=rwkv_k_a,
                         r_k=rwkv_r_k.reshape(-1, D), ln_g=rwkv_ln_g, ln_b=rwkv_ln_b)
                proj = _rwkv_proj(src, mod, norm_g, p, i, j, slen, grp)
                if is_prompt:
                    y, s_fin = _rwkv_scan(proj, None, nseq, slen, True)
                    new_rwkv.append(s_fin)
                else:
                    (y,) = _rwkv_scan(proj, state_rwkv[:, j], nseq, slen, False)
                shared = _rwkv_out(src, dst, mod, norm_g, y, proj[10], proj[9], p, i, j, grp)
            else:
                perm = _rope_swap_perm()
                w_uq = mla_w_uq[j].reshape(Q_RANK, MLA_H, D_NOPE + D_ROPE)
                w_uq_rope = w_uq[:, :, D_NOPE:]
                w_dkv_r = mla_w_dkv[j][:, KV_RANK:]
                p = dict(w_dq=mla_w_dq, g_q=mla_g_q, g_kv=mla_g_kv, w_uk=mla_w_uk, w_uv=mla_w_uv, w_o=mla_w_o,
                         w_uq_nope=w_uq[:, :, :D_NOPE].reshape(Q_RANK, MLA_H * D_NOPE),
                         w_uq_rope=w_uq_rope.reshape(Q_RANK, MLA_H * D_ROPE),
                         w_uq_rope_sw=w_uq_rope[:, :, perm].reshape(Q_RANK, MLA_H * D_ROPE),
                         w_dkv_c=mla_w_dkv[j][:, :KV_RANK], w_dkv_r=w_dkv_r, w_dkv_r_sw=w_dkv_r[:, perm])
                qn, qr, ckv, kr = _mla_proj(src, mod, norm_g, p, i, j, not is_prompt, grp)
                if is_prompt:
                    new_ckv.append(ckv.reshape(batch, seq, KV_RANK))
                    new_krope.append(kr.reshape(batch, seq, D_ROPE))
                    shared = _mla_attn(src, dst, mod, norm_g, qn, qr, ckv, kr, p, i, j, nseq, slen, slen,
                                       TM // slen, slen, lambda s: 0)
                else:
                    klen = past_len + slen
                    ckv_all = jnp.concatenate([cache_mla_ckv[:, j], ckv.reshape(nseq, slen, KV_RANK)], axis=1)
                    kr_all = jnp.concatenate([cache_mla_krope[:, j], kr.reshape(nseq, slen, D_ROPE)], axis=1)
                    shared = _mla_attn(src, dst, mod, norm_g, qn, qr, ckv_all.reshape(nseq * klen, KV_RANK),
                                       kr_all.reshape(nseq * klen, D_ROPE), p, i, j, nseq, slen, klen, 1, 256,
                                       lambda s: 1 + s)
            cur[idx] = _Rows(shared, dst.first, src.ntiles)
            if sharing:
                cur[1 - idx] = cur[1 - idx]._replace(array=shared)
        if i < DEPTH - 1:
            srcs = [_Rows(cur[0].array, 0, np_tiles + ns_tiles)] if cur[0].array is cur[1].array else cur
            both = _mlp(srcs, _Dest(total_rows, 0, False), mod, norm_g, mlp_w1, mlp_w2, i, grp_all)
            cur = [_Rows(both, 0, np_tiles), _Rows(both, np_tiles, ns_tiles)]
        else:
            y_prompt, y_sample = [
                _mlp([cur[idx]], _Dest(cur[idx].ntiles * TM, 0, False), mod, norm_g, mlp_w1, mlp_w2, i,
                     streams[idx][3]) for idx in range(2)]

    return (y_prompt.reshape(batch, seq, D), y_sample.reshape(dec_batch, dec_seq, D),
            jnp.stack(new_rwkv, axis=1), jnp.stack(new_ckv, axis=1), jnp.stack(new_krope, axis=1))
```

```python
import functools
from typing import NamedTuple

import numpy as np
import jax
import jax.numpy as jnp
from jax import lax
from jax.experimental import pallas as pl
from jax.experimental.pallas import tpu as pltpu

D = 1024
DEPTH = 4
N_MOD = 6
D_FF = 4 * D
EPS = 1e-6
GRID_W = 64
FFT_GROUPS = 8
FFT_GW = D // FFT_GROUPS
HS = 64
NH = D // HS
LORA_W = 64
LORA_A = 64
LORA_G = 128
GN_EPS = 64e-5
MLA_H = 8
D_NOPE = 128
D_ROPE = 64
D_V = 128
KV_RANK = 256
Q_RANK = 384
ROPE_BASE = 10000.0
MLA_SCALE = (D_NOPE + D_ROPE) ** -0.5

F32 = jnp.float32
BF16 = jnp.bfloat16

TM = 1024
LANES = 128
CHUNK = 64
SCAN_UNROLL = 4
SCAN_PAIRS = 2
ATTN_LOOKAHEAD = 2
VMEM_LIMIT = 58 * 1024 * 1024


def _cparams(sem):
    return pltpu.CompilerParams(dimension_semantics=sem, vmem_limit_bytes=VMEM_LIMIT)


def _dot(a, b):
    return jnp.dot(a.astype(BF16), b.astype(BF16), preferred_element_type=F32)


def _dot_nt(a, b):
    return lax.dot_general(a.astype(BF16), b.astype(BF16), (((1,), (1,)), ((), ())),
                           preferred_element_type=F32)


def _dot_tn(a, b):
    return lax.dot_general(a.astype(BF16), b.astype(BF16), (((0,), (0,)), ((), ())),
                           preferred_element_type=F32)


def _block_diag2(a, b):
    za = jnp.zeros((a.shape[0], b.shape[1]), a.dtype)
    zb = jnp.zeros((b.shape[0], a.shape[1]), a.dtype)
    return jnp.concatenate([jnp.concatenate([a, za], axis=1), jnp.concatenate([zb, b], axis=1)], axis=0)


def _cumsum_rows(x, reverse):
    n = x.shape[0]
    idx = lax.broadcasted_iota(jnp.int32, x.shape, 0)
    s = 1
    while s < n:
        if reverse:
            x = x + jnp.where(idx < n - s, pltpu.roll(x, n - s, 0), 0.0)
        else:
            x = x + jnp.where(idx >= s, pltpu.roll(x, s, 0), 0.0)
        s *= 2
    return x


def _dot_hilo_rhs(x, e):
    hi = x.astype(BF16)
    lo = (x - hi.astype(F32)).astype(BF16)
    return jnp.dot(hi, e, preferred_element_type=F32) + jnp.dot(lo, e, preferred_element_type=F32)


def _rms(x, g):
    return x * lax.rsqrt(jnp.mean(x * x, axis=-1, keepdims=True) + EPS) * g


def _sigmoid(x):
    return 1.0 / (1.0 + jnp.exp(-x))


def _softplus(x):
    return jnp.maximum(x, 0.0) + jnp.log(1.0 + jnp.exp(-jnp.abs(x)))


def _head_indicator(n):
    r = lax.broadcasted_iota(jnp.int32, (n, n), 0) // HS
    c = lax.broadcasted_iota(jnp.int32, (n, n), 1) // HS
    return jnp.where(r == c, 1.0, 0.0).astype(BF16)


def _shift_rows(z, seq_len):
    n = z.shape[0]
    pos = lax.broadcasted_iota(jnp.int32, z.shape, 0) % seq_len
    prev = jnp.where(pos == 0, 0.0, pltpu.roll(z, 1, 0))
    nxt = jnp.where(pos == seq_len - 1, 0.0, pltpu.roll(z, n - 1, 0))
    return prev, nxt


MOD_TN = 1536
MOD_ROWS = 8


def _mod_block(cs_ref, w_ref, b_ref):
    cs = cs_ref[...]
    return _dot(cs * _sigmoid(cs), w_ref[0]) + b_ref[0]


def _mod_kernel(cs_ref, w_ref, b_ref, o_ref):
    o_ref[...] = _mod_block(cs_ref, w_ref, b_ref)


def _modulation(cs, mod_w, mod_b3, layer):
    nj = (N_MOD * D) // MOD_TN
    out = pl.pallas_call(
        _mod_kernel,
        grid=(nj,),
        in_specs=[
            pl.BlockSpec((MOD_ROWS, D), lambda j: (0, 0)),
            pl.BlockSpec((1, D, MOD_TN), lambda j: (layer, 0, j)),
            pl.BlockSpec((1, 1, MOD_TN), lambda j: (layer, 0, j)),
        ],
        out_specs=pl.BlockSpec((MOD_ROWS, MOD_TN), lambda j: (0, j)),
        out_shape=jax.ShapeDtypeStruct((MOD_ROWS, N_MOD * D), F32),
        compiler_params=_cparams(("arbitrary",)),
        name="modulation",
    )(cs, mod_w, mod_b3)
    return out.reshape(1, MOD_ROWS, N_MOD, D)


class _Rows(NamedTuple):
    array: jax.Array
    first: int
    ntiles: int


class _Dest(NamedTuple):
    rows: int
    first: int
    inplace: bool


def _activation_call(kernel, *, grid, in_specs, args, dst, block_rows, out_index, scratch_shapes, name,
                     extra_out=None):
    assert not dst.inplace or args[0].shape[0] == dst.rows
    out_specs = pl.BlockSpec((block_rows, D), out_index)
    out_shape = jax.ShapeDtypeStruct((dst.rows, D), F32)
    if extra_out is not None:
        out_specs, out_shape = [out_specs, extra_out[0]], [out_shape, extra_out[1]]
    return pl.pallas_call(
        kernel, grid=grid, in_specs=in_specs, out_specs=out_specs, out_shape=out_shape,
        scratch_shapes=scratch_shapes, input_output_aliases={0: 0} if dst.inplace else {},
        compiler_params=_cparams(("arbitrary",) * len(grid)), name=name)(*args)


def _mod_spec(layer, group_of_tile, ngrid):
    del layer
    if ngrid == 1:
        return pl.BlockSpec((1, 1, N_MOD, D), lambda i: (0, group_of_tile(i), 0, 0))
    return pl.BlockSpec((1, 1, N_MOD, D), lambda i, j: (0, group_of_tile(i), 0, 0))


def _normg_spec(layer, ngrid):
    if ngrid == 1:
        return pl.BlockSpec((1, 4, D), lambda i: (layer, 0, 0))
    return pl.BlockSpec((1, 4, D), lambda i, j: (layer, 0, 0))


MLP_FC = 1024


def _mlp_kernel(*refs, n_first, with_next_mod):
    if n_first is None:
        (x_ref,), rest = refs[:1], refs[1:]
        read_x = lambda: x_ref[...]
    else:
        (xa_ref, xb_ref), rest = refs[:2], refs[2:]
        read_x = lambda: jnp.where(pl.program_id(0) < n_first, xa_ref[...], xb_ref[...])
    if with_next_mod:
        m_ref, g_ref, w1_ref, w2_ref, cs_ref, mw_ref, mb_ref, o_ref, mo_ref, h_scr, acc_scr = rest
        mo_ref[...] = _mod_block(cs_ref, mw_ref, mb_ref)
    else:
        m_ref, g_ref, w1_ref, w2_ref, o_ref, h_scr, acc_scr = rest
    j = pl.program_id(1)
    m = m_ref[0, 0]
    g = g_ref[0]

    last = pl.num_programs(1) - 1

    def chunk():
        a = jnp.dot(h_scr[...], w1_ref[0].astype(BF16), preferred_element_type=F32)
        a = jnp.maximum(a, 0.0)
        return _dot(a * a, w2_ref[0])

    @pl.when(j == 0)
    def _():
        h = _rms(read_x(), g[2:3]) * (1.0 + m[4:5]) + m[3:4]
        h_scr[...] = h.astype(BF16)
        acc_scr[...] = chunk()

    @pl.when((j > 0) & (j < last))
    def _():
        acc_scr[...] += chunk()

    @pl.when(j == last)
    def _():
        f = acc_scr[...] + chunk()
        o_ref[...] = read_x() + m[5:6] * _rms(f, g[3:4])


def _mlp(srcs, dst, mod, norm_g, w1, w2, layer, group_of_tile, next_mod=None):
    nj = D_FF // MLP_FC
    assert nj >= 2
    if len(srcs) == 1:
        (src,) = srcs
        ntiles, n_first = src.ntiles, None
        x_specs = [pl.BlockSpec((TM, D), lambda i, j: (i + src.first, 0))]
    else:
        sa, sb = srcs
        ntiles, n_first = sa.ntiles + sb.ntiles, sa.ntiles
        x_specs = [pl.BlockSpec((TM, D), lambda i, j: (jnp.minimum(i, sa.ntiles - 1) + sa.first, 0)),
                   pl.BlockSpec((TM, D), lambda i, j: (jnp.maximum(i - sa.ntiles, 0) + sb.first, 0))]
    in_specs = x_specs + [
        _mod_spec(layer, group_of_tile, 2),
        _normg_spec(layer, 2),
        pl.BlockSpec((1, D, MLP_FC), lambda i, j: (layer, 0, j)),
        pl.BlockSpec((1, MLP_FC, D), lambda i, j: (layer, j, 0)),
    ]
    args = [s.array for s in srcs] + [mod, norm_g, w1, w2]
    extra_out = None
    if next_mod is not None:
        cs, mod_w, mod_b3, nxt = next_mod
        cols = (N_MOD * D) // (ntiles * nj)
        assert cols * ntiles * nj == N_MOD * D and cols % LANES == 0
        in_specs += [pl.BlockSpec((MOD_ROWS, D), lambda i, j: (0, 0)),
                     pl.BlockSpec((1, D, cols), lambda i, j: (nxt, 0, i * nj + j)),
                     pl.BlockSpec((1, 1, cols), lambda i, j: (nxt, 0, i * nj + j))]
        args += [cs, mod_w, mod_b3]
        extra_out = (pl.BlockSpec((MOD_ROWS, cols), lambda i, j: (0, i * nj + j)),
                     jax.ShapeDtypeStruct((MOD_ROWS, N_MOD * D), F32))
    res = _activation_call(
        functools.partial(_mlp_kernel, n_first=n_first, with_next_mod=next_mod is not None),
        grid=(ntiles, nj), in_specs=in_specs, args=args,
        dst=dst, block_rows=TM, out_index=lambda i, j: (i + dst.first, 0),
        scratch_shapes=[pltpu.VMEM((TM, D), BF16), pltpu.VMEM((TM, D), F32)],
        name="mlp", extra_out=extra_out)
    if next_mod is None:
        return res
    return res[0], res[1].reshape(1, MOD_ROWS, N_MOD, D)


def _dft_mats(n):
    idx = np.arange(n, dtype=np.int64)
    ang = (2.0 * np.pi / n) * ((idx[:, None] * idx[None, :]) % n).astype(np.float64)
    scale = 1.0 / np.sqrt(n)
    return np.cos(ang) * scale, np.sin(ang) * scale


def _fnet_kernel(x_ref, m_ref, g_ref, cs_ref, cn_ref, sn_ref, w_ref, o_ref, p_scr, q_scr, f_scr,
                 *, seq_len):
    m = m_ref[0, 0]
    g = g_ref[0]
    x = x_ref[...]
    h = (_rms(x, g[0:1]) * (1.0 + m[1:2]) + m[0:1]).astype(BF16)
    cs = cs_ref[...].astype(BF16)
    for gi in range(FFT_GROUPS):
        pq = jnp.dot(h[:, gi * FFT_GW:(gi + 1) * FFT_GW], cs, preferred_element_type=F32)
        p_scr[:, gi * FFT_GW:(gi + 1) * FFT_GW] = pq[:, :FFT_GW].astype(BF16)
        q_scr[:, gi * FFT_GW:(gi + 1) * FFT_GW] = pq[:, FFT_GW:].astype(BF16)
    cn = cn_ref[...].astype(BF16)
    sn = sn_ref[...].astype(BF16)
    for s in range(TM // seq_len):
        rows = slice(s * seq_len, (s + 1) * seq_len)
        f = (jnp.dot(cn, p_scr[rows, :], preferred_element_type=F32)
             - jnp.dot(sn, q_scr[rows, :], preferred_element_type=F32))
        f_scr[rows, :] = f.astype(BF16)
    o = jnp.dot(f_scr[...], w_ref[0].astype(BF16), preferred_element_type=F32)
    o_ref[...] = x + m[2:3] * _rms(o, g[1:2])


def _fnet(src, dst, mod, norm_g, w_out, layer, j, seq_len, group_of_tile):
    cg, sg = _dft_mats(FFT_GW)
    cs = jnp.asarray(np.concatenate([cg, sg], axis=1), F32)
    cn_np, sn_np = _dft_mats(seq_len)
    cn = jnp.asarray(cn_np, F32)
    sn = jnp.asarray(sn_np, F32)
    return _activation_call(
        functools.partial(_fnet_kernel, seq_len=seq_len),
        grid=(src.ntiles,),
        in_specs=[
            pl.BlockSpec((TM, D), lambda i: (i + src.first, 0)),
            _mod_spec(layer, group_of_tile, 1),
            _normg_spec(layer, 1),
            pl.BlockSpec((FFT_GW, 2 * FFT_GW), lambda i: (0, 0)),
            pl.BlockSpec((seq_len, seq_len), lambda i: (0, 0)),
            pl.BlockSpec((seq_len, seq_len), lambda i: (0, 0)),
            pl.BlockSpec((1, D, D), lambda i: (j, 0, 0)),
        ],
        args=[src.array, mod, norm_g, cs, cn, sn, w_out],
        dst=dst, block_rows=TM, out_index=lambda i: (i + dst.first, 0),
        scratch_shapes=[pltpu.VMEM((TM, D), BF16), pltpu.VMEM((TM, D), BF16),
                        pltpu.VMEM((TM, D), BF16)],
        name="fourier_mix")


CONV_CW = 512


def _conv_kernel(x_ref, m_ref, g_ref, wb_ref, wc_ref, wu_ref, cw_ref, wo_ref, o_ref, h_scr, acc_scr,
                 *, seq_len):
    j = pl.program_id(1)
    m = m_ref[0, 0]
    g = g_ref[0]

    def chunk():
        h = h_scr[...]
        hw = CONV_CW // 2
        proj = []
        for s in range(2):
            cols = slice(s * hw, (s + 1) * hw)
            proj.append([jnp.dot(h, w_ref[0, :, cols].astype(BF16), preferred_element_type=F32)
                         for w_ref in (wb_ref, wc_ref, wu_ref)])
        out = None
        for s in range(2):
            cols = slice(s * hw, (s + 1) * hw)
            bg, cg, u = proj[s]
            z = cg * u
            z_prev, z_next = _shift_rows(z, seq_len)
            conv = z_prev * cw_ref[0, 0:1, cols] + z * cw_ref[0, 1:2, cols] + z_next * cw_ref[0, 2:3, cols]
            part = _dot(bg * conv, wo_ref[0, cols, :])
            out = part if out is None else out + part
        return out

    last = pl.num_programs(1) - 1

    @pl.when(j == 0)
    def _():
        h = _rms(x_ref[...], g[0:1]) * (1.0 + m[1:2]) + m[0:1]
        h_scr[...] = h.astype(BF16)
        acc_scr[...] = chunk()

    @pl.when((j > 0) & (j < last))
    def _():
        acc_scr[...] += chunk()

    @pl.when(j == last)
    def _():
        o_ref[...] = x_ref[...] + m[2:3] * _rms(acc_scr[...] + chunk(), g[1:2])


def _conv(src, dst, mod, norm_g, w_in, w_conv, w_out, layer, j, seq_len, group_of_tile):
    nj = D // CONV_CW
    assert nj >= 2
    return _activation_call(
        functools.partial(_conv_kernel, seq_len=seq_len),
        grid=(src.ntiles, nj),
        in_specs=[
            pl.BlockSpec((TM, D), lambda i, c: (i + src.first, 0)),
            _mod_spec(layer, group_of_tile, 2),
            _normg_spec(layer, 2),
            pl.BlockSpec((1, D, CONV_CW), lambda i, c: (j, 0, c)),
            pl.BlockSpec((1, D, CONV_CW), lambda i, c: (j, 0, nj + c)),
            pl.BlockSpec((1, D, CONV_CW), lambda i, c: (j, 0, 2 * nj + c)),
            pl.BlockSpec((1, 3, CONV_CW), lambda i, c: (j, 0, c)),
            pl.BlockSpec((1, CONV_CW, D), lambda i, c: (j, c, 0)),
        ],
        args=[src.array, mod, norm_g, w_in, w_in, w_in, w_conv, w_out],
        dst=dst, block_rows=TM, out_index=lambda i, c: (i + dst.first, 0),
        scratch_shapes=[pltpu.VMEM((TM, D), BF16), pltpu.VMEM((TM, D), F32)],
        name="short_conv")


RWP_CW = 256
RW_OUT_DTYPES = (BF16, BF16, BF16, F32, BF16, BF16, F32, BF16, BF16, F32, F32)
RW_CW = 512


def _rwkv_proj_kernel(x_ref, m_ref, g_ref, mu_ref, wr_ref, wk_ref, wv_ref, w0_ref, wl1_ref, wl2_ref,
                      a0_ref, al1_ref, al2_ref, gl1_ref, gl2_ref, kk_ref, ka_ref, rk_ref,
                      r_o, v_o, kk_o, ld0_o, b0_o, kt0_o, ld1_o, b1_o, kt1_o, g_o, bonus_o,
                      xr_scr, xk_scr, xv_scr, tw_scr, ta_scr, sg_scr, *, seq_len):
    j = pl.program_id(1)

    def chunk():
        e = _head_indicator(RWP_CW)
        r = jnp.dot(xr_scr[...], wr_ref[0].astype(BF16), preferred_element_type=F32)
        k = jnp.dot(xk_scr[...], wk_ref[0].astype(BF16), preferred_element_type=F32)
        v = jnp.dot(xv_scr[...], wv_ref[0].astype(BF16), preferred_element_type=F32)
        g_o[...] = jnp.dot(sg_scr[...], gl2_ref[0].astype(BF16), preferred_element_type=F32)
        zws = [jnp.dot(tw_scr[d], wl2_ref[0, d].astype(BF16), preferred_element_type=F32) for d in range(2)]
        zas = [jnp.dot(ta_scr[d], al2_ref[0, d].astype(BF16), preferred_element_type=F32) for d in range(2)]
        kk = k * kk_ref[...]
        kk = kk * lax.rsqrt(_dot(kk * kk, e) + 1e-12)
        r_o[...] = r.astype(r_o.dtype)
        v_o[...] = v.astype(v_o.dtype)
        kk_o[...] = kk.astype(kk_o.dtype)
        k_a = ka_ref[...]
        kt_sum = None
        for d, (ld_o, b_o, kt_o) in enumerate(((ld0_o, b0_o, kt0_o), (ld1_o, b1_o, kt1_o))):
            logw = -_softplus(-(w0_ref[0, d:d + 1, :] + zws[d])) - 0.5
            ld_o[...] = -jnp.exp(logw)
            a = _sigmoid(a0_ref[0, d:d + 1, :] + zas[d])
            kt = k * (1.0 + (a - 1.0) * k_a)
            b_o[...] = (kk * a).astype(b_o.dtype)
            kt_o[...] = kt.astype(kt_o.dtype)
            kt_sum = kt if kt_sum is None else kt_sum + kt
        bonus_o[...] = _dot(r * rk_ref[...] * kt_sum, e) * v

    @pl.when(j == 0)
    def _():
        m = m_ref[0, 0]
        g = g_ref[0]
        mu = mu_ref[0]
        h = _rms(x_ref[...], g[0:1]) * (1.0 + m[1:2]) + m[0:1]
        h_prev, h_next = _shift_rows(h, seq_len)
        dx = 0.5 * (h_prev + h_next) - h
        xr_scr[...] = (h + dx * mu[0:1]).astype(BF16)
        xk_scr[...] = (h + dx * mu[2:3]).astype(BF16)
        xv_scr[...] = (h + dx * mu[3:4]).astype(BF16)
        xw = (h + dx * mu[1:2]).astype(BF16)
        xa = (h + dx * mu[4:5]).astype(BF16)
        xg = (h + dx * mu[5:6]).astype(BF16)
        for d in range(2):
            tw_scr[d] = jnp.tanh(_dot(xw, wl1_ref[0, d])).astype(BF16)
            ta_scr[d] = _dot(xa, al1_ref[0, d]).astype(BF16)
        sg_scr[...] = _sigmoid(_dot(xg, gl1_ref[0])).astype(BF16)
        chunk()

    @pl.when(j > 0)
    def _():
        chunk()


def _rwkv_proj(src, mod, norm_g, p, layer, j, seq_len, group_of_tile):
    n = src.ntiles * TM
    nj = D // RWP_CW
    tile = pl.BlockSpec((TM, RWP_CW), lambda i, c: (i, c))
    wcol = lambda: pl.BlockSpec((1, D, RWP_CW), lambda i, c: (j, 0, c))
    vec = lambda: pl.BlockSpec((1, RWP_CW), lambda i, c: (j, c))
    return pl.pallas_call(
        functools.partial(_rwkv_proj_kernel, seq_len=seq_len),
        grid=(src.ntiles, nj),
        in_specs=[
            pl.BlockSpec((TM, D), lambda i, c: (i + src.first, 0)),
            _mod_spec(layer, group_of_tile, 2),
            _normg_spec(layer, 2),
            pl.BlockSpec((1, 6, D), lambda i, c: (j, 0, 0)),
            wcol(), wcol(), wcol(),
            pl.BlockSpec((1, 2, RWP_CW), lambda i, c: (j, 0, c)),
            pl.BlockSpec((1, 2, D, LORA_W), lambda i, c: (j, 0, 0, 0)),
            pl.BlockSpec((1, 2, LORA_W, RWP_CW), lambda i, c: (j, 0, 0, c)),
            pl.BlockSpec((1, 2, RWP_CW), lambda i, c: (j, 0, c)),
            pl.BlockSpec((1, 2, D, LORA_A), lambda i, c: (j, 0, 0, 0)),
            pl.BlockSpec((1, 2, LORA_A, RWP_CW), lambda i, c: (j, 0, 0, c)),
            pl.BlockSpec((1, D, LORA_G), lambda i, c: (j, 0, 0)),
            pl.BlockSpec((1, LORA_G, RWP_CW), lambda i, c: (j, 0, c)),
            vec(), vec(), vec(),
        ],
        out_specs=[tile] * len(RW_OUT_DTYPES),
        out_shape=[jax.ShapeDtypeStruct((n, D), dt) for dt in RW_OUT_DTYPES],
        scratch_shapes=[pltpu.VMEM((TM, D), BF16), pltpu.VMEM((TM, D), BF16), pltpu.VMEM((TM, D), BF16),
                        pltpu.VMEM((2, TM, LORA_W), BF16), pltpu.VMEM((2, TM, LORA_A), BF16),
                        pltpu.VMEM((TM, LORA_G), BF16)],
        compiler_params=_cparams(("arbitrary", "arbitrary")),
        name="rwkv_proj",
    )(src.array, mod, norm_g, p["mu"], p["w_r"], p["w_k"], p["w_v"], p["w0"], p["w_l1"], p["w_l2"],
      p["a0"], p["a_l1"], p["a_l2"], p["g_l1"], p["g_l2"], p["k_k"], p["k_a"], p["r_k"])


def _scan_precompute(units):
    c = CHUNK
    c2 = 2 * c
    row = lax.broadcasted_iota(jnp.int32, (c2, c2), 0)
    col = lax.broadcasted_iota(jnp.int32, (c2, c2), 1)
    head_a = lax.broadcasted_iota(jnp.int32, (c, LANES), 1) < HS
    own_lanes = jnp.concatenate([head_a, jnp.logical_not(head_a)], axis=0)
    eye = jnp.where(row == col, 1.0, 0.0)

    def two_heads(t):
        return jnp.concatenate([jnp.where(head_a, t, 0.0), jnp.where(head_a, 0.0, t)], axis=0)

    def causal(reverse):
        if reverse:
            return (col % c) > (row % c), (col % c) >= (row % c)
        return (col % c) < (row % c), (col % c) <= (row % c)

    masks = {rev: causal(rev) for rev in sorted({u[6] for u in units})}

    cums = [_cumsum_rows(u[0], u[6]) for u in units]
    st = []
    for (ld, kk, beta, kt, r, v, rev), cum in zip(units, cums):
        tot = cum[0:1] if rev else cum[c - 1:c]
        ginv = jnp.exp(-cum)
        tail = jnp.exp(tot - cum)
        st.append(dict(
            rev=rev, etot=jnp.exp(tot),
            a_t=two_heads(-kk * jnp.exp(cum - ld)), r_t=two_heads(r * jnp.exp(cum)),
            bk=jnp.concatenate([two_heads(beta * ginv), two_heads(kt * ginv)], axis=0),
            bkg=jnp.concatenate([two_heads(beta * tail), two_heads(kt * tail)], axis=0),
            v2=jnp.concatenate([v, v], axis=0), vh=two_heads(v)))
    grams = [_dot_nt(jnp.concatenate([s["a_t"], s["r_t"]], axis=0), s["bk"]) for s in st]
    for s, gram in zip(st, grams):
        strict, incl = masks[s["rev"]]
        s["l_ab"] = jnp.where(strict, gram[:c2, :c2], 0.0)
        s["l_ak"] = jnp.where(strict, gram[:c2, c2:], 0.0)
        s["t_rb"] = jnp.where(incl, gram[c2:, :c2], 0.0)
        s["t_rk"] = jnp.where(incl, gram[c2:, c2:], 0.0)
    lvs = [_dot(s["l_ak"], s["v2"]) for s in st]
    minvs = [eye for _ in st]
    b = 1
    while b < c:
        same = (row // (2 * b)) == (col // (2 * b))
        es = []
        for s in st:
            first, second = (col % (2 * b)) < b, (row % (2 * b)) >= b
            if s["rev"]:
                first, second = (row % (2 * b)) < b, (col % (2 * b)) >= b
            es.append(jnp.where(same & first & second, s["l_ab"], 0.0))
        if b == 1:
            minvs = [m + e for m, e in zip(minvs, es)]
        else:
            half = [_dot(m, e) for m, e in zip(minvs, es)]
            minvs = [m + _dot(h, m) for m, h in zip(minvs, half)]
        b *= 2
    mms = [_dot(m, jnp.concatenate([s["a_t"], lv], axis=1)) for m, s, lv in zip(minvs, st, lvs)]
    zero = jnp.zeros((c2, LANES), BF16)
    tts = [_dot(jnp.concatenate([s["t_rb"], s["t_rk"]], axis=1),
                jnp.concatenate([mm.astype(BF16), jnp.concatenate([zero, s["v2"].astype(BF16)], axis=1)], axis=0))
           for s, mm in zip(st, mms)]
    ps = [_dot_tn(mm[:, :LANES], s["bkg"][:c2]) for s, mm in zip(st, mms)]
    qs = [_dot_tn(jnp.concatenate([jnp.where(own_lanes, mm[:, LANES:], 0.0), s["vh"]], axis=0), s["bkg"])
          for s, mm in zip(st, mms)]
    out = []
    for s, p, q, tt in zip(st, ps, qs, tts):
        reff = s["r_t"] + tt[:, :LANES]
        out.append(dict(etot=s["etot"], p=p, q=q, reff=reff[:c] + reff[c:],
                        y0=jnp.where(head_a, tt[:c, LANES:], tt[c:, LANES:])))
    return out


def _rwkv_scan_kernel(*refs, seq_len, has_init, want_final):
    r_ref, v_ref, kk_ref, ld0_ref, b0_ref, kt0_ref, ld1_ref, b1_ref, kt1_ref = refs[:9]
    pos = 9
    if has_init:
        s0_ref = refs[pos]
        pos += 1
    y_ref = refs[pos]
    pos += 1
    if want_final:
        sf_ref = refs[pos]
        pos += 1
    g_scr, yb_scr = refs[pos:pos + 2]
    nc = seq_len // CHUNK
    ngroups = nc // SCAN_UNROLL
    per_dir = ((ld0_ref, b0_ref, kt0_ref), (ld1_ref, b1_ref, kt1_ref))
    y_dst = (y_ref, yb_scr)
    nchain = 2 * SCAN_PAIRS

    def group(gi, states):
        where, units = [], []
        for u in range(SCAN_UNROLL):
            for pp in range(SCAN_PAIRS):
                lanes = slice(pp * LANES, (pp + 1) * LANES)
                for d in range(2):
                    ld_ref, b_ref, kt_ref = per_dir[d]
                    cidx = gi * SCAN_UNROLL + u
                    if d == 1:
                        cidx = nc - 1 - cidx
                    start = cidx * CHUNK
                    rw = slice(start, start + CHUNK) if isinstance(start, int) else pl.ds(
                        pl.multiple_of(start, CHUNK), CHUNK)
                    where.append((rw, lanes))
                    units.append(tuple(ref[rw, lanes].astype(F32) for ref in
                                       (ld_ref, kk_ref, b_ref, kt_ref, r_ref, v_ref)) + (d == 1,))
        pre = _scan_precompute(units)
        states = list(states)
        for u in range(SCAN_UNROLL):
            cur = pre[u * nchain:(u + 1) * nchain]
            ys = [_dot_nt(cu["reff"], g) for cu, g in zip(cur, states)]
            gp = [_dot(g, cu["p"]) for cu, g in zip(cur, states)]
            states = [states[ch] * cur[ch]["etot"] + gp[ch] + cur[ch]["q"] for ch in range(nchain)]
            for ch in range(nchain):
                rw, lanes = where[u * nchain + ch]
                y_dst[ch % 2][rw, lanes] = ys[ch] + cur[ch]["y0"]
        return states

    init = [_block_diag2(s0_ref[0, ch % 2, 2 * (ch // 2)], s0_ref[0, ch % 2, 2 * (ch // 2) + 1])
            if has_init else jnp.zeros((LANES, LANES), F32) for ch in range(nchain)]
    if ngroups == 1:
        final = group(0, init)
    else:
        for ch in range(nchain):
            g_scr[ch] = init[ch]

        def body(gi, carry):
            new = group(gi, [g_scr[ch] for ch in range(nchain)])
            for ch in range(nchain):
                g_scr[ch] = new[ch]
            return carry

        lax.fori_loop(0, ngroups, body, 0)
        final = [g_scr[ch] for ch in range(nchain)]
    y_ref[...] += yb_scr[...]
    if want_final:
        for ch in range(nchain):
            sf_ref[0, ch % 2, 2 * (ch // 2)] = final[ch][:HS, :HS]
            sf_ref[0, ch % 2, 2 * (ch // 2) + 1] = final[ch][HS:, HS:]


def _rwkv_scan(proj, s_init, n_seq, seq_len, want_final):
    r, v, kk, ld0, b0, kt0, ld1, b1, kt1 = proj[:9]
    n = r.shape[0]
    npair = D // LANES
    width = SCAN_PAIRS * LANES
    blk = pl.BlockSpec((seq_len, width), lambda b, p: (b, p))
    st_spec = pl.BlockSpec((1, 2, 2 * SCAN_PAIRS, HS, HS), lambda b, p: (b, 0, p, 0, 0))
    in_specs = [blk] * 9
    args = [r, v, kk, ld0, b0, kt0, ld1, b1, kt1]
    has_init = s_init is not None
    if has_init:
        in_specs.append(st_spec)
        args.append(s_init)
    out_specs = [blk]
    out_shape = [jax.ShapeDtypeStruct((n, D), F32)]
    if want_final:
        out_specs.append(st_spec)
        out_shape.append(jax.ShapeDtypeStruct((n_seq, 2, NH, HS, HS), F32))
    res = pl.pallas_call(
        functools.partial(_rwkv_scan_kernel, seq_len=seq_len, has_init=has_init, want_final=want_final),
        grid=(n_seq, npair // SCAN_PAIRS),
        in_specs=in_specs,
        out_specs=out_specs,
        out_shape=out_shape,
        scratch_shapes=[pltpu.VMEM((2 * SCAN_PAIRS, LANES, LANES), F32), pltpu.VMEM((seq_len, width), F32)],
        compiler_params=_cparams(("arbitrary", "arbitrary")),
        name="rwkv_scan",
    )(*args)
    return res


def _rwkv_out_kernel(x_ref, m_ref, g_ref, y_ref, bonus_ref, gate_ref, lng_ref, lnb_ref, wo_ref, o_ref,
                     acc_scr):
    j = pl.program_id(1)
    last = pl.num_programs(1) - 1

    def chunk():
        e = _head_indicator(RW_CW)
        y = y_ref[...]
        mean = _dot_hilo_rhs(y, e) * (1.0 / HS)
        yc = y - mean
        var = _dot(yc * yc, e) * (1.0 / HS)
        yn = yc * lax.rsqrt(var + GN_EPS) * lng_ref[...] + lnb_ref[...]
        yn = (yn + bonus_ref[...]) * gate_ref[...]
        return _dot(yn, wo_ref[0])

    @pl.when(j == 0)
    def _():
        acc_scr[...] = chunk()

    @pl.when((j > 0) & (j < last))
    def _():
        acc_scr[...] += chunk()

    @pl.when(j == last)
    def _():
        m = m_ref[0, 0]
        g = g_ref[0]
        o_ref[...] = x_ref[...] + m[2:3] * _rms(acc_scr[...] + chunk(), g[1:2])


def _rwkv_out(src, dst, mod, norm_g, y, bonus, gate, p, layer, j, group_of_tile):
    nj = D // RW_CW
    assert nj >= 2
    tile = pl.BlockSpec((TM, RW_CW), lambda i, c: (i, c))
    vec = pl.BlockSpec((1, RW_CW), lambda i, c: (j, c))
    return _activation_call(
        _rwkv_out_kernel,
        grid=(src.ntiles, nj),
        in_specs=[
            pl.BlockSpec((TM, D), lambda i, c: (i + src.first, 0)),
            _mod_spec(layer, group_of_tile, 2),
            _normg_spec(layer, 2),
            tile, tile, tile, vec, vec,
            pl.BlockSpec((1, RW_CW, D), lambda i, c: (j, c, 0)),
        ],
        args=[src.array, mod, norm_g, y, bonus, gate, p["ln_g"], p["ln_b"], p["w_o"]],
        dst=dst, block_rows=TM, out_index=lambda i, c: (i + dst.first, 0),
        scratch_shapes=[pltpu.VMEM((TM, D), F32)],
        name="rwkv_out")


def _rope_tables(n):
    rows = n // GRID_W
    row = np.repeat(np.arange(rows), GRID_W)
    col = np.tile(np.arange(GRID_W), rows)
    pos = np.stack([row, col], axis=-1).astype(np.float64)
    quarter = D_ROPE // 4
    inv = ROPE_BASE ** (-np.arange(quarter, dtype=np.float64) / quarter)
    ang = pos[:, :, None] * inv
    cos = np.cos(ang)
    sin = np.sin(ang)
    cos_t = np.concatenate([cos, cos], axis=-1).reshape(n, D_ROPE)
    sin_t = np.concatenate([-sin, sin], axis=-1).reshape(n, D_ROPE)
    return cos_t.astype(np.float32), sin_t.astype(np.float32)


def _rope_swap_perm():
    quarter = D_ROPE // 4
    base = np.arange(D_ROPE)
    return np.where((base % (2 * quarter)) < quarter, base + quarter, base - quarter)


def _mla_proj_kernel(*refs, positional):
    (x_ref, m_ref, g_ref, wdq_ref, gq_ref, wqn_ref, wqr_ref, wqs_ref, wdkv_ref, wkr_ref, wks_ref,
     gkv_ref) = refs[:12]
    pos = 12
    if positional:
        cosq_ref, sinq_ref, cosk_ref, sink_ref = refs[pos:pos + 4]
        pos += 4
    qn_o, qr_o, ckv_o, kr_o = refs[pos:pos + 4]
    m = m_ref[0, 0]
    g = g_ref[0]
    h = (_rms(x_ref[...], g[0:1]) * (1.0 + m[1:2]) + m[0:1]).astype(BF16)
    ql = jnp.dot(h, wdq_ref[0].astype(BF16), preferred_element_type=F32)
    ql = (ql * lax.rsqrt(jnp.mean(ql * ql, axis=-1, keepdims=True) + EPS) * gq_ref[...]).astype(BF16)
    qn_o[...] = jnp.dot(ql, wqn_ref[...].astype(BF16), preferred_element_type=F32).astype(qn_o.dtype)
    qr = jnp.dot(ql, wqr_ref[...].astype(BF16), preferred_element_type=F32)
    ckv = jnp.dot(h, wdkv_ref[...].astype(BF16), preferred_element_type=F32)
    ckv_o[...] = ckv * lax.rsqrt(jnp.mean(ckv * ckv, axis=-1, keepdims=True) + EPS) * gkv_ref[...]
    kr = jnp.dot(h, wkr_ref[...].astype(BF16), preferred_element_type=F32)
    if positional:
        qs = jnp.dot(ql, wqs_ref[...].astype(BF16), preferred_element_type=F32)
        ks = jnp.dot(h, wks_ref[...].astype(BF16), preferred_element_type=F32)
        qr = qr * cosq_ref[...] + qs * sinq_ref[...]
        kr = kr * cosk_ref[...] + ks * sink_ref[...]
    qr_o[...] = qr.astype(qr_o.dtype)
    kr_o[...] = kr


def _mla_proj(src, mod, norm_g, p, layer, j, positional, group_of_tile):
    n = src.ntiles * TM
    full = lambda shape: pl.BlockSpec(shape, lambda i: (0,) * len(shape))
    in_specs = [
        pl.BlockSpec((TM, D), lambda i: (i + src.first, 0)),
        _mod_spec(layer, group_of_tile, 1),
        _normg_spec(layer, 1),
        pl.BlockSpec((1, D, Q_RANK), lambda i: (j, 0, 0)),
        pl.BlockSpec((1, Q_RANK), lambda i: (j, 0)),
        full((Q_RANK, MLA_H * D_NOPE)), full((Q_RANK, MLA_H * D_ROPE)), full((Q_RANK, MLA_H * D_ROPE)),
        full((D, KV_RANK)), full((D, D_ROPE)), full((D, D_ROPE)),
        pl.BlockSpec((1, KV_RANK), lambda i: (j, 0)),
    ]
    args = [src.array, mod, norm_g, p["w_dq"], p["g_q"], p["w_uq_nope"], p["w_uq_rope"], p["w_uq_rope_sw"],
            p["w_dkv_c"], p["w_dkv_r"], p["w_dkv_r_sw"], p["g_kv"]]
    if positional:
        cos_t, sin_t = _rope_tables(TM)
        in_specs += [full((TM, MLA_H * D_ROPE)), full((TM, MLA_H * D_ROPE)),
                     full((TM, D_ROPE)), full((TM, D_ROPE))]
        args += [jnp.asarray(np.tile(cos_t, (1, MLA_H))), jnp.asarray(np.tile(sin_t, (1, MLA_H))),
                 jnp.asarray(cos_t), jnp.asarray(sin_t)]
    outs = ((MLA_H * D_NOPE, BF16), (MLA_H * D_ROPE, BF16), (KV_RANK, F32), (D_ROPE, F32))
    return pl.pallas_call(
        functools.partial(_mla_proj_kernel, positional=positional),
        grid=(src.ntiles,),
        in_specs=in_specs,
        out_specs=[pl.BlockSpec((TM, w), lambda i: (i, 0)) for w, _ in outs],
        out_shape=[jax.ShapeDtypeStruct((n, w), dt) for w, dt in outs],
        compiler_params=_cparams(("arbitrary",)),
        name="mla_proj",
    )(*args)


def _mla_attn_kernel(x_ref, m_ref, g_ref, qn_ref, qr_ref, ckv_ref, kr_ref, wuk_ref, wuv_ref, wo_ref,
                     o_ref, kn_scr, vv_scr, oh_scr, *, nb, tq, k_len):
    qi = pl.program_id(1)

    @pl.when(qi == 0)
    def _():
        ckv = ckv_ref[...].astype(BF16)
        kn_scr[...] = jnp.dot(ckv, wuk_ref[0].astype(BF16), preferred_element_type=F32).astype(BF16)
        vv_scr[...] = jnp.dot(ckv, wuv_ref[0].astype(BF16), preferred_element_type=F32).astype(BF16)

    units = [(b, hd) for b in range(nb) for hd in range(MLA_H)]

    def scores(unit):
        b, hd = unit
        qrows = slice(b * tq, (b + 1) * tq)
        krows = slice(b * k_len, (b + 1) * k_len)
        q = jnp.concatenate([qn_ref[qrows, hd * D_NOPE:(hd + 1) * D_NOPE].astype(BF16),
                             qr_ref[qrows, hd * D_ROPE:(hd + 1) * D_ROPE].astype(BF16)], axis=1)
        k = jnp.concatenate([kn_scr[krows, hd * D_NOPE:(hd + 1) * D_NOPE],
                             kr_ref[krows, :].astype(BF16)], axis=1)
        return _dot_nt(q, k) * MLA_SCALE

    pending = [scores(u) for u in units[:ATTN_LOOKAHEAD]]
    for idx, (b, hd) in enumerate(units):
        if idx + ATTN_LOOKAHEAD < len(units):
            pending.append(scores(units[idx + ATTN_LOOKAHEAD]))
        s = pending[idx]
        pexp = jnp.exp(s - jnp.max(s, axis=-1, keepdims=True))
        pv = jnp.dot(pexp.astype(BF16), vv_scr[b * k_len:(b + 1) * k_len, hd * D_V:(hd + 1) * D_V],
                     preferred_element_type=F32)
        oh_scr[b * tq:(b + 1) * tq, hd * D_V:(hd + 1) * D_V] = (
            pv / jnp.sum(pexp, axis=-1, keepdims=True)).astype(BF16)
    o = jnp.dot(oh_scr[...], wo_ref[0].astype(BF16), preferred_element_type=F32)
    m = m_ref[0, 0]
    g = g_ref[0]
    o_ref[...] = x_ref[...] + m[2:3] * _rms(o, g[1:2])


def _mla_attn(src, dst, mod, norm_g, qn, qr, ckv_all, kr_all, p, layer, j, n_seq, q_len, k_len, nb, tq,
              group_of_step):
    nq = q_len // tq
    assert nb == 1 or nq == 1
    rows = nb * tq
    x_first, o_first = src.first * TM // rows, dst.first * TM // rows
    return _activation_call(
        functools.partial(_mla_attn_kernel, nb=nb, tq=tq, k_len=k_len),
        grid=(n_seq // nb, nq),
        in_specs=[
            pl.BlockSpec((rows, D), lambda s, q: (s * nq + q + x_first, 0)),
            pl.BlockSpec((1, 1, N_MOD, D), lambda s, q: (0, group_of_step(s), 0, 0)),
            pl.BlockSpec((1, 4, D), lambda s, q: (layer, 0, 0)),
            pl.BlockSpec((nb * tq, MLA_H * D_NOPE), lambda s, q: (s * nq + q, 0)),
            pl.BlockSpec((nb * tq, MLA_H * D_ROPE), lambda s, q: (s * nq + q, 0)),
            pl.BlockSpec((nb * k_len, KV_RANK), lambda s, q: (s, 0)),
            pl.BlockSpec((nb * k_len, D_ROPE), lambda s, q: (s, 0)),
            pl.BlockSpec((1, KV_RANK, MLA_H * D_NOPE), lambda s, q: (j, 0, 0)),
            pl.BlockSpec((1, KV_RANK, MLA_H * D_V), lambda s, q: (j, 0, 0)),
            pl.BlockSpec((1, MLA_H * D_V, D), lambda s, q: (j, 0, 0)),
        ],
        args=[src.array, mod, norm_g, qn, qr, ckv_all, kr_all, p["w_uk"], p["w_uv"], p["w_o"]],
        dst=dst, block_rows=rows, out_index=lambda s, q: (s * nq + q + o_first, 0),
        scratch_shapes=[pltpu.VMEM((nb * k_len, MLA_H * D_NOPE), BF16),
                        pltpu.VMEM((nb * k_len, MLA_H * D_V), BF16),
                        pltpu.VMEM((nb * tq, MLA_H * D_V), BF16)],
        name="mla_attn")


def kernel(x_prompt, x_sample, state_rwkv, cache_mla_ckv, cache_mla_krope, c, c_ctx, mod_w, mod_b, norm_g,
           mlp_w1, mlp_w2, fft_w_out, conv_w_in, conv_w, conv_w_out, rwkv_mu, rwkv_w_r, rwkv_w_k, rwkv_w_v,
           rwkv_w_o, rwkv_w0, rwkv_w_l1, rwkv_w_l2, rwkv_a0, rwkv_a_l1, rwkv_a_l2, rwkv_g_l1, rwkv_g_l2,
           rwkv_k_k, rwkv_k_a, rwkv_r_k, rwkv_ln_g, rwkv_ln_b, mla_w_dq, mla_g_q, mla_w_uq, mla_w_dkv,
           mla_g_kv, mla_w_uk, mla_w_uv, mla_w_o):
    batch, seq, _ = x_prompt.shape
    dec_batch, dec_seq, _ = x_sample.shape
    past_len = cache_mla_ckv.shape[2]
    assert (batch * seq) % TM == 0 and TM % seq == 0 and dec_seq == TM and seq % CHUNK == 0

    np_tiles = batch * seq // TM
    ns_tiles = dec_batch * dec_seq // TM
    total_rows = (np_tiles + ns_tiles) * TM
    cs = jnp.concatenate([c_ctx[None, :], c, jnp.zeros((MOD_ROWS - 1 - dec_batch, D), F32)], axis=0)
    mod_b3 = mod_b.reshape(DEPTH, 1, N_MOD * D)
    mod = _modulation(cs, mod_w, mod_b3, 0)

    grp_p = lambda i: 0
    grp_s = lambda i: 1 + i
    grp_all = lambda i: jnp.maximum(i - (np_tiles - 1), 0)
    new_rwkv, new_ckv, new_krope = [], [], []
    streams = [(True, seq, batch, grp_p), (False, dec_seq, dec_batch, grp_s)]
    cur = [_Rows(x_prompt.reshape(batch * seq, D), 0, np_tiles),
           _Rows(x_sample.reshape(dec_batch * dec_seq, D), 0, ns_tiles)]

    for i in range(DEPTH):
        kind, j = i % 4, i // 4
        for idx, (is_prompt, slen, nseq, grp) in enumerate(streams):
            src = cur[idx]
            sharing = src.array.shape[0] == total_rows
            dst = _Dest(total_rows, src.first, True) if sharing else _Dest(src.ntiles * TM, 0, False)
            shared = None
            if kind == 0:
                shared = _fnet(src, dst, mod, norm_g, fft_w_out, i, j, slen, grp)
            elif kind == 1:
                shared = _conv(src, dst, mod, norm_g, conv_w_in, conv_w, conv_w_out, i, j, slen, grp)
            elif kind == 2:
                p = dict(mu=rwkv_mu, w_r=rwkv_w_r, w_k=rwkv_w_k, w_v=rwkv_w_v, w_o=rwkv_w_o, w0=rwkv_w0,
                         w_l1=rwkv_w_l1, w_l2=rwkv_w_l2, a0=rwkv_a0, a_l1=rwkv_a_l1, a_l2=rwkv_a_l2,
                         g_l1=rwkv_g_l1, g_l2=rwkv_g_l2, k_k=rwkv_k_k, k_a=rwkv_k_a,
                         r_k=rwkv_r_k.reshape(-1, D), ln_g=rwkv_ln_g, ln_b=rwkv_ln_b)
                proj = _rwkv_proj(src, mod, norm_g, p, i, j, slen, grp)
                if is_prompt:
                    y, s_fin = _rwkv_scan(proj, None, nseq, slen, True)
                    new_rwkv.append(s_fin)
                else:
                    (y,) = _rwkv_scan(proj, state_rwkv[:, j], nseq, slen, False)
                shared = _rwkv_out(src, dst, mod, norm_g, y, proj[10], proj[9], p, i, j, grp)
            else:
                perm = _rope_swap_perm()
                w_uq = mla_w_uq[j].reshape(Q_RANK, MLA_H, D_NOPE + D_ROPE)
                w_uq_rope = w_uq[:, :, D_NOPE:]
                w_dkv_r = mla_w_dkv[j][:, KV_RANK:]
                p = dict(w_dq=mla_w_dq, g_q=mla_g_q, g_kv=mla_g_kv, w_uk=mla_w_uk, w_uv=mla_w_uv, w_o=mla_w_o,
                         w_uq_nope=w_uq[:, :, :D_NOPE].reshape(Q_RANK, MLA_H * D_NOPE),
                         w_uq_rope=w_uq_rope.reshape(Q_RANK, MLA_H * D_ROPE),
                         w_uq_rope_sw=w_uq_rope[:, :, perm].reshape(Q_RANK, MLA_H * D_ROPE),
                         w_dkv_c=mla_w_dkv[j][:, :KV_RANK], w_dkv_r=w_dkv_r, w_dkv_r_sw=w_dkv_r[:, perm])
                qn, qr, ckv, kr = _mla_proj(src, mod, norm_g, p, i, j, not is_prompt, grp)
                if is_prompt:
                    new_ckv.append(ckv.reshape(batch, seq, KV_RANK))
                    new_krope.append(kr.reshape(batch, seq, D_ROPE))
                    shared = _mla_attn(src, dst, mod, norm_g, qn, qr, ckv, kr, p, i, j, nseq, slen, slen,
                                       TM // slen, slen, lambda s: 0)
                else:
                    klen = past_len + slen
                    ckv_all = jnp.concatenate([cache_mla_ckv[:, j], ckv.reshape(nseq, slen, KV_RANK)], axis=1)
                    kr_all = jnp.concatenate([cache_mla_krope[:, j], kr.reshape(nseq, slen, D_ROPE)], axis=1)
                    shared = _mla_attn(src, dst, mod, norm_g, qn, qr, ckv_all.reshape(nseq * klen, KV_RANK),
                                       kr_all.reshape(nseq * klen, D_ROPE), p, i, j, nseq, slen, klen, 1, 256,
                                       lambda s: 1 + s)
            cur[idx] = _Rows(shared, dst.first, src.ntiles)
            if sharing:
                cur[1 - idx] = cur[1 - idx]._replace(array=shared)
        if i < DEPTH - 1:
            srcs = [_Rows(cur[0].array, 0, np_tiles + ns_tiles)] if cur[0].array is cur[1].array else cur
            both, mod = _mlp(srcs, _Dest(total_rows, 0, False), mod, norm_g, mlp_w1, mlp_w2, i, grp_all,
                             next_mod=(cs, mod_w, mod_b3, i + 1))
            cur = [_Rows(both, 0, np_tiles), _Rows(both, np_tiles, ns_tiles)]
        else:
            y_prompt, y_sample = [
                _mlp([cur[idx]], _Dest(cur[idx].ntiles * TM, 0, False), mod, norm_g, mlp_w1, mlp_w2, i,
                     streams[idx][3]) for idx in range(2)]

    return (y_prompt.reshape(batch, seq, D), y_sample.reshape(dec_batch, dec_seq, D),
            jnp.stack(new_rwkv, axis=1), jnp.stack(new_ckv, axis=1), jnp.stack(new_krope, axis=1))
```

```python
import functools
from typing import NamedTuple

import numpy as np
import jax
import jax.numpy as jnp
from jax import lax
from jax.experimental import pallas as pl
from jax.experimental.pallas import tpu as pltpu

D = 1024
DEPTH = 4
N_MOD = 6
D_FF = 4 * D
EPS = 1e-6
GRID_W = 64
FFT_GROUPS = 8
FFT_GW = D // FFT_GROUPS
HS = 64
NH = D // HS
LORA_W = 64
LORA_A = 64
LORA_G = 128
GN_EPS = 64e-5
MLA_H = 8
D_NOPE = 128
D_ROPE = 64
D_V = 128
KV_RANK = 256
Q_RANK = 384
ROPE_BASE = 10000.0
MLA_SCALE = (D_NOPE + D_ROPE) ** -0.5

F32 = jnp.float32
BF16 = jnp.bfloat16

TM = 1024
LANES = 128
CHUNK = 64
SCAN_UNROLL = 4
SCAN_PAIRS = 2
ATTN_LOOKAHEAD = 2
VMEM_LIMIT = 58 * 1024 * 1024


def _cparams(sem):
    return pltpu.CompilerParams(dimension_semantics=sem, vmem_limit_bytes=VMEM_LIMIT)


def _dot(a, b):
    return jnp.dot(a.astype(BF16), b.astype(BF16), preferred_element_type=F32)


def _dot_nt(a, b):
    return lax.dot_general(a.astype(BF16), b.astype(BF16), (((1,), (1,)), ((), ())),
                           preferred_element_type=F32)


def _dot_tn(a, b):
    return lax.dot_general(a.astype(BF16), b.astype(BF16), (((0,), (0,)), ((), ())),
                           preferred_element_type=F32)


def _block_diag2(a, b):
    za = jnp.zeros((a.shape[0], b.shape[1]), a.dtype)
    zb = jnp.zeros((b.shape[0], a.shape[1]), a.dtype)
    return jnp.concatenate([jnp.concatenate([a, za], axis=1), jnp.concatenate([zb, b], axis=1)], axis=0)


def _cumsum_rows(x, reverse):
    n = x.shape[0]
    idx = lax.broadcasted_iota(jnp.int32, x.shape, 0)
    s = 1
    while s < n:
        if reverse:
            x = x + jnp.where(idx < n - s, pltpu.roll(x, n - s, 0), 0.0)
        else:
            x = x + jnp.where(idx >= s, pltpu.roll(x, s, 0), 0.0)
        s *= 2
    return x


def _dot_hilo_rhs(x, e):
    hi = x.astype(BF16)
    lo = (x - hi.astype(F32)).astype(BF16)
    return jnp.dot(hi, e, preferred_element_type=F32) + jnp.dot(lo, e, preferred_element_type=F32)


def _rms(x, g):
    return x * lax.rsqrt(jnp.mean(x * x, axis=-1, keepdims=True) + EPS) * g


def _sigmoid(x):
    return 1.0 / (1.0 + jnp.exp(-x))


def _softplus(x):
    return jnp.maximum(x, 0.0) + jnp.log(1.0 + jnp.exp(-jnp.abs(x)))


def _head_indicator(n):
    r = lax.broadcasted_iota(jnp.int32, (n, n), 0) // HS
    c = lax.broadcasted_iota(jnp.int32, (n, n), 1) // HS
    return jnp.where(r == c, 1.0, 0.0).astype(BF16)


def _shift_rows(z, seq_len):
    n = z.shape[0]
    pos = lax.broadcasted_iota(jnp.int32, z.shape, 0) % seq_len
    prev = jnp.where(pos == 0, 0.0, pltpu.roll(z, 1, 0))
    nxt = jnp.where(pos == seq_len - 1, 0.0, pltpu.roll(z, n - 1, 0))
    return prev, nxt


MOD_TN = 1536
MOD_ROWS = 8


def _mod_block(cs_ref, w_ref, b_ref):
    cs = cs_ref[...]
    return _dot(cs * _sigmoid(cs), w_ref[0]) + b_ref[0]


def _mod_kernel(cs_ref, w_ref, b_ref, o_ref):
    o_ref[...] = _mod_block(cs_ref, w_ref, b_ref)


def _modulation(cs, mod_w, mod_b3, layer):
    nj = (N_MOD * D) // MOD_TN
    out = pl.pallas_call(
        _mod_kernel,
        grid=(nj,),
        in_specs=[
            pl.BlockSpec((MOD_ROWS, D), lambda j: (0, 0)),
            pl.BlockSpec((1, D, MOD_TN), lambda j: (layer, 0, j)),
            pl.BlockSpec((1, 1, MOD_TN), lambda j: (layer, 0, j)),
        ],
        out_specs=pl.BlockSpec((MOD_ROWS, MOD_TN), lambda j: (0, j)),
        out_shape=jax.ShapeDtypeStruct((MOD_ROWS, N_MOD * D), F32),
        compiler_params=_cparams(("arbitrary",)),
        name="modulation",
    )(cs, mod_w, mod_b3)
    return out.reshape(1, MOD_ROWS, N_MOD, D)


class _Rows(NamedTuple):
    array: jax.Array
    first: int
    ntiles: int


class _Dest(NamedTuple):
    rows: int
    first: int
    inplace: bool


def _activation_call(kernel, *, grid, in_specs, args, dst, block_rows, out_index, scratch_shapes, name,
                     extra_out=None):
    assert not dst.inplace or args[0].shape[0] == dst.rows
    out_specs = pl.BlockSpec((block_rows, D), out_index)
    out_shape = jax.ShapeDtypeStruct((dst.rows, D), F32)
    if extra_out is not None:
        out_specs, out_shape = [out_specs, extra_out[0]], [out_shape, extra_out[1]]
    return pl.pallas_call(
        kernel, grid=grid, in_specs=in_specs, out_specs=out_specs, out_shape=out_shape,
        scratch_shapes=scratch_shapes, input_output_aliases={0: 0} if dst.inplace else {},
        compiler_params=_cparams(("arbitrary",) * len(grid)), name=name)(*args)


def _mod_spec(layer, group_of_tile, ngrid):
    del layer
    if ngrid == 1:
        return pl.BlockSpec((1, 1, N_MOD, D), lambda i: (0, group_of_tile(i), 0, 0))
    return pl.BlockSpec((1, 1, N_MOD, D), lambda i, j: (0, group_of_tile(i), 0, 0))


def _normg_spec(layer, ngrid):
    if ngrid == 1:
        return pl.BlockSpec((1, 4, D), lambda i: (layer, 0, 0))
    return pl.BlockSpec((1, 4, D), lambda i, j: (layer, 0, 0))


MLP_FC = 1024


def _mlp_kernel(*refs, n_first, with_next_mod):
    if n_first is None:
        (x_ref,), rest = refs[:1], refs[1:]
        read_x = lambda: x_ref[...]
    else:
        (xa_ref, xb_ref), rest = refs[:2], refs[2:]
        read_x = lambda: jnp.where(pl.program_id(0) < n_first, xa_ref[...], xb_ref[...])
    if with_next_mod:
        m_ref, g_ref, w1_ref, w2_ref, cs_ref, mw_ref, mb_ref, o_ref, mo_ref, h_scr, acc_scr = rest
    else:
        m_ref, g_ref, w1_ref, w2_ref, o_ref, h_scr, acc_scr = rest
    j = pl.program_id(1)
    m = m_ref[0, 0]
    g = g_ref[0]

    last = pl.num_programs(1) - 1

    def chunk():
        a = jnp.dot(h_scr[...], w1_ref[0].astype(BF16), preferred_element_type=F32)
        a = jnp.maximum(a, 0.0)
        upd = _dot(a * a, w2_ref[0])
        if with_next_mod:
            mo_ref[...] = _mod_block(cs_ref, mw_ref, mb_ref)
        return upd

    @pl.when(j == 0)
    def _():
        h = _rms(read_x(), g[2:3]) * (1.0 + m[4:5]) + m[3:4]
        h_scr[...] = h.astype(BF16)
        acc_scr[...] = chunk()

    @pl.when((j > 0) & (j < last))
    def _():
        acc_scr[...] += chunk()

    @pl.when(j == last)
    def _():
        f = acc_scr[...] + chunk()
        o_ref[...] = read_x() + m[5:6] * _rms(f, g[3:4])


def _mlp(srcs, dst, mod, norm_g, w1, w2, layer, group_of_tile, next_mod=None):
    nj = D_FF // MLP_FC
    assert nj >= 2
    if len(srcs) == 1:
        (src,) = srcs
        ntiles, n_first = src.ntiles, None
        x_specs = [pl.BlockSpec((TM, D), lambda i, j: (i + src.first, 0))]
    else:
        sa, sb = srcs
        ntiles, n_first = sa.ntiles + sb.ntiles, sa.ntiles
        x_specs = [pl.BlockSpec((TM, D), lambda i, j: (jnp.minimum(i, sa.ntiles - 1) + sa.first, 0)),
                   pl.BlockSpec((TM, D), lambda i, j: (jnp.maximum(i - sa.ntiles, 0) + sb.first, 0))]
    in_specs = x_specs + [
        _mod_spec(layer, group_of_tile, 2),
        _normg_spec(layer, 2),
        pl.BlockSpec((1, D, MLP_FC), lambda i, j: (layer, 0, j)),
        pl.BlockSpec((1, MLP_FC, D), lambda i, j: (layer, j, 0)),
    ]
    args = [s.array for s in srcs] + [mod, norm_g, w1, w2]
    extra_out = None
    if next_mod is not None:
        cs, mod_w, mod_b3, nxt = next_mod
        cols = (N_MOD * D) // (ntiles * nj)
        assert cols * ntiles * nj == N_MOD * D and cols % LANES == 0
        in_specs += [pl.BlockSpec((MOD_ROWS, D), lambda i, j: (0, 0)),
                     pl.BlockSpec((1, D, cols), lambda i, j: (nxt, 0, i * nj + j)),
                     pl.BlockSpec((1, 1, cols), lambda i, j: (nxt, 0, i * nj + j))]
        args += [cs, mod_w, mod_b3]
        extra_out = (pl.BlockSpec((MOD_ROWS, cols), lambda i, j: (0, i * nj + j)),
                     jax.ShapeDtypeStruct((MOD_ROWS, N_MOD * D), F32))
    res = _activation_call(
        functools.partial(_mlp_kernel, n_first=n_first, with_next_mod=next_mod is not None),
        grid=(ntiles, nj), in_specs=in_specs, args=args,
        dst=dst, block_rows=TM, out_index=lambda i, j: (i + dst.first, 0),
        scratch_shapes=[pltpu.VMEM((TM, D), BF16), pltpu.VMEM((TM, D), F32)],
        name="mlp", extra_out=extra_out)
    if next_mod is None:
        return res
    return res[0], res[1].reshape(1, MOD_ROWS, N_MOD, D)


def _dft_mats(n):
    idx = np.arange(n, dtype=np.int64)
    ang = (2.0 * np.pi / n) * ((idx[:, None] * idx[None, :]) % n).astype(np.float64)
    scale = 1.0 / np.sqrt(n)
    return np.cos(ang) * scale, np.sin(ang) * scale


def _fnet_kernel(x_ref, m_ref, g_ref, cs_ref, cn_ref, sn_ref, w_ref, o_ref, p_scr, q_scr, f_scr,
                 *, seq_len):
    m = m_ref[0, 0]
    g = g_ref[0]
    x = x_ref[...]
    h = (_rms(x, g[0:1]) * (1.0 + m[1:2]) + m[0:1]).astype(BF16)
    cs = cs_ref[...].astype(BF16)
    for gi in range(FFT_GROUPS):
        pq = jnp.dot(h[:, gi * FFT_GW:(gi + 1) * FFT_GW], cs, preferred_element_type=F32)
        p_scr[:, gi * FFT_GW:(gi + 1) * FFT_GW] = pq[:, :FFT_GW].astype(BF16)
        q_scr[:, gi * FFT_GW:(gi + 1) * FFT_GW] = pq[:, FFT_GW:].astype(BF16)
    cn = cn_ref[...].astype(BF16)
    sn = sn_ref[...].astype(BF16)
    for s in range(TM // seq_len):
        rows = slice(s * seq_len, (s + 1) * seq_len)
        f = (jnp.dot(cn, p_scr[rows, :], preferred_element_type=F32)
             - jnp.dot(sn, q_scr[rows, :], preferred_element_type=F32))
        f_scr[rows, :] = f.astype(BF16)
    o = jnp.dot(f_scr[...], w_ref[0].astype(BF16), preferred_element_type=F32)
    o_ref[...] = x + m[2:3] * _rms(o, g[1:2])


def _fnet(src, dst, mod, norm_g, w_out, layer, j, seq_len, group_of_tile):
    cg, sg = _dft_mats(FFT_GW)
    cs = jnp.asarray(np.concatenate([cg, sg], axis=1), F32)
    cn_np, sn_np = _dft_mats(seq_len)
    cn = jnp.asarray(cn_np, F32)
    sn = jnp.asarray(sn_np, F32)
    return _activation_call(
        functools.partial(_fnet_kernel, seq_len=seq_len),
        grid=(src.ntiles,),
        in_specs=[
            pl.BlockSpec((TM, D), lambda i: (i + src.first, 0)),
            _mod_spec(layer, group_of_tile, 1),
            _normg_spec(layer, 1),
            pl.BlockSpec((FFT_GW, 2 * FFT_GW), lambda i: (0, 0)),
            pl.BlockSpec((seq_len, seq_len), lambda i: (0, 0)),
            pl.BlockSpec((seq_len, seq_len), lambda i: (0, 0)),
            pl.BlockSpec((1, D, D), lambda i: (j, 0, 0)),
        ],
        args=[src.array, mod, norm_g, cs, cn, sn, w_out],
        dst=dst, block_rows=TM, out_index=lambda i: (i + dst.first, 0),
        scratch_shapes=[pltpu.VMEM((TM, D), BF16), pltpu.VMEM((TM, D), BF16),
                        pltpu.VMEM((TM, D), BF16)],
        name="fourier_mix")


CONV_CW = 512


def _conv_kernel(x_ref, m_ref, g_ref, wb_ref, wc_ref, wu_ref, cw_ref, wo_ref, o_ref, h_scr, acc_scr,
                 *, seq_len):
    j = pl.program_id(1)
    m = m_ref[0, 0]
    g = g_ref[0]

    def chunk():
        h = h_scr[...]
        hw = CONV_CW // 2
        proj = []
        for s in range(2):
            cols = slice(s * hw, (s + 1) * hw)
            proj.append([jnp.dot(h, w_ref[0, :, cols].astype(BF16), preferred_element_type=F32)
                         for w_ref in (wb_ref, wc_ref, wu_ref)])
        out = None
        for s in range(2):
            cols = slice(s * hw, (s + 1) * hw)
            bg, cg, u = proj[s]
            z = cg * u
            z_prev, z_next = _shift_rows(z, seq_len)
            conv = z_prev * cw_ref[0, 0:1, cols] + z * cw_ref[0, 1:2, cols] + z_next * cw_ref[0, 2:3, cols]
            part = _dot(bg * conv, wo_ref[0, cols, :])
            out = part if out is None else out + part
        return out

    last = pl.num_programs(1) - 1

    @pl.when(j == 0)
    def _():
        h = _rms(x_ref[...], g[0:1]) * (1.0 + m[1:2]) + m[0:1]
        h_scr[...] = h.astype(BF16)
        acc_scr[...] = chunk()

    @pl.when((j > 0) & (j < last))
    def _():
        acc_scr[...] += chunk()

    @pl.when(j == last)
    def _():
        o_ref[...] = x_ref[...] + m[2:3] * _rms(acc_scr[...] + chunk(), g[1:2])


def _conv(src, dst, mod, norm_g, w_in, w_conv, w_out, layer, j, seq_len, group_of_tile):
    nj = D // CONV_CW
    assert nj >= 2
    return _activation_call(
        functools.partial(_conv_kernel, seq_len=seq_len),
        grid=(src.ntiles, nj),
        in_specs=[
            pl.BlockSpec((TM, D), lambda i, c: (i + src.first, 0)),
            _mod_spec(layer, group_of_tile, 2),
            _normg_spec(layer, 2),
            pl.BlockSpec((1, D, CONV_CW), lambda i, c: (j, 0, c)),
            pl.BlockSpec((1, D, CONV_CW), lambda i, c: (j, 0, nj + c)),
            pl.BlockSpec((1, D, CONV_CW), lambda i, c: (j, 0, 2 * nj + c)),
            pl.BlockSpec((1, 3, CONV_CW), lambda i, c: (j, 0, c)),
            pl.BlockSpec((1, CONV_CW, D), lambda i, c: (j, c, 0)),
        ],
        args=[src.array, mod, norm_g, w_in, w_in, w_in, w_conv, w_out],
        dst=dst, block_rows=TM, out_index=lambda i, c: (i + dst.first, 0),
        scratch_shapes=[pltpu.VMEM((TM, D), BF16), pltpu.VMEM((TM, D), F32)],
        name="short_conv")


RWP_CW = 256
RW_OUT_DTYPES = (BF16, BF16, BF16, F32, BF16, BF16, F32, BF16, BF16, F32, F32)
RW_CW = 512


def _rwkv_proj_kernel(x_ref, m_ref, g_ref, mu_ref, wr_ref, wk_ref, wv_ref, w0_ref, wl1_ref, wl2_ref,
                      a0_ref, al1_ref, al2_ref, gl1_ref, gl2_ref, kk_ref, ka_ref, rk_ref,
                      r_o, v_o, kk_o, ld0_o, b0_o, kt0_o, ld1_o, b1_o, kt1_o, g_o, bonus_o,
                      xr_scr, xk_scr, xv_scr, tw_scr, ta_scr, sg_scr, *, seq_len):
    j = pl.program_id(1)

    def chunk():
        e = _head_indicator(RWP_CW)
        r = jnp.dot(xr_scr[...], wr_ref[0].astype(BF16), preferred_element_type=F32)
        k = jnp.dot(xk_scr[...], wk_ref[0].astype(BF16), preferred_element_type=F32)
        v = jnp.dot(xv_scr[...], wv_ref[0].astype(BF16), preferred_element_type=F32)
        g_o[...] = jnp.dot(sg_scr[...], gl2_ref[0].astype(BF16), preferred_element_type=F32)
        zws = [jnp.dot(tw_scr[d], wl2_ref[0, d].astype(BF16), preferred_element_type=F32) for d in range(2)]
        zas = [jnp.dot(ta_scr[d], al2_ref[0, d].astype(BF16), preferred_element_type=F32) for d in range(2)]
        kk = k * kk_ref[...]
        kk = kk * lax.rsqrt(_dot(kk * kk, e) + 1e-12)
        r_o[...] = r.astype(r_o.dtype)
        v_o[...] = v.astype(v_o.dtype)
        kk_o[...] = kk.astype(kk_o.dtype)
        k_a = ka_ref[...]
        kt_sum = None
        for d, (ld_o, b_o, kt_o) in enumerate(((ld0_o, b0_o, kt0_o), (ld1_o, b1_o, kt1_o))):
            logw = -_softplus(-(w0_ref[0, d:d + 1, :] + zws[d])) - 0.5
            ld_o[...] = -jnp.exp(logw)
            a = _sigmoid(a0_ref[0, d:d + 1, :] + zas[d])
            kt = k * (1.0 + (a - 1.0) * k_a)
            b_o[...] = (kk * a).astype(b_o.dtype)
            kt_o[...] = kt.astype(kt_o.dtype)
            kt_sum = kt if kt_sum is None else kt_sum + kt
        bonus_o[...] = _dot(r * rk_ref[...] * kt_sum, e) * v

    @pl.when(j == 0)
    def _():
        m = m_ref[0, 0]
        g = g_ref[0]
        mu = mu_ref[0]
        h = _rms(x_ref[...], g[0:1]) * (1.0 + m[1:2]) + m[0:1]
        h_prev, h_next = _shift_rows(h, seq_len)
        dx = 0.5 * (h_prev + h_next) - h
        xr_scr[...] = (h + dx * mu[0:1]).astype(BF16)
        xk_scr[...] = (h + dx * mu[2:3]).astype(BF16)
        xv_scr[...] = (h + dx * mu[3:4]).astype(BF16)
        xw = (h + dx * mu[1:2]).astype(BF16)
        xa = (h + dx * mu[4:5]).astype(BF16)
        xg = (h + dx * mu[5:6]).astype(BF16)
        for d in range(2):
            tw_scr[d] = jnp.tanh(_dot(xw, wl1_ref[0, d])).astype(BF16)
            ta_scr[d] = _dot(xa, al1_ref[0, d]).astype(BF16)
        sg_scr[...] = _sigmoid(_dot(xg, gl1_ref[0])).astype(BF16)
        chunk()

    @pl.when(j > 0)
    def _():
        chunk()


def _rwkv_proj(src, mod, norm_g, p, layer, j, seq_len, group_of_tile):
    n = src.ntiles * TM
    nj = D // RWP_CW
    tile = pl.BlockSpec((TM, RWP_CW), lambda i, c: (i, c))
    wcol = lambda: pl.BlockSpec((1, D, RWP_CW), lambda i, c: (j, 0, c))
    vec = lambda: pl.BlockSpec((1, RWP_CW), lambda i, c: (j, c))
    return pl.pallas_call(
        functools.partial(_rwkv_proj_kernel, seq_len=seq_len),
        grid=(src.ntiles, nj),
        in_specs=[
            pl.BlockSpec((TM, D), lambda i, c: (i + src.first, 0)),
            _mod_spec(layer, group_of_tile, 2),
            _normg_spec(layer, 2),
            pl.BlockSpec((1, 6, D), lambda i, c: (j, 0, 0)),
            wcol(), wcol(), wcol(),
            pl.BlockSpec((1, 2, RWP_CW), lambda i, c: (j, 0, c)),
            pl.BlockSpec((1, 2, D, LORA_W), lambda i, c: (j, 0, 0, 0)),
            pl.BlockSpec((1, 2, LORA_W, RWP_CW), lambda i, c: (j, 0, 0, c)),
            pl.BlockSpec((1, 2, RWP_CW), lambda i, c: (j, 0, c)),
            pl.BlockSpec((1, 2, D, LORA_A), lambda i, c: (j, 0, 0, 0)),
            pl.BlockSpec((1, 2, LORA_A, RWP_CW), lambda i, c: (j, 0, 0, c)),
            pl.BlockSpec((1, D, LORA_G), lambda i, c: (j, 0, 0)),
            pl.BlockSpec((1, LORA_G, RWP_CW), lambda i, c: (j, 0, c)),
            vec(), vec(), vec(),
        ],
        out_specs=[tile] * len(RW_OUT_DTYPES),
        out_shape=[jax.ShapeDtypeStruct((n, D), dt) for dt in RW_OUT_DTYPES],
        scratch_shapes=[pltpu.VMEM((TM, D), BF16), pltpu.VMEM((TM, D), BF16), pltpu.VMEM((TM, D), BF16),
                        pltpu.VMEM((2, TM, LORA_W), BF16), pltpu.VMEM((2, TM, LORA_A), BF16),
                        pltpu.VMEM((TM, LORA_G), BF16)],
        compiler_params=_cparams(("arbitrary", "arbitrary")),
        name="rwkv_proj",
    )(src.array, mod, norm_g, p["mu"], p["w_r"], p["w_k"], p["w_v"], p["w0"], p["w_l1"], p["w_l2"],
      p["a0"], p["a_l1"], p["a_l2"], p["g_l1"], p["g_l2"], p["k_k"], p["k_a"], p["r_k"])


def _scan_precompute(units):
    c = CHUNK
    c2 = 2 * c
    row = lax.broadcasted_iota(jnp.int32, (c2, c2), 0)
    col = lax.broadcasted_iota(jnp.int32, (c2, c2), 1)
    head_a = lax.broadcasted_iota(jnp.int32, (c, LANES), 1) < HS
    own_lanes = jnp.concatenate([head_a, jnp.logical_not(head_a)], axis=0)
    eye = jnp.where(row == col, 1.0, 0.0)

    def two_heads(t):
        return jnp.concatenate([jnp.where(head_a, t, 0.0), jnp.where(head_a, 0.0, t)], axis=0)

    def causal(reverse):
        if reverse:
            return (col % c) > (row % c), (col % c) >= (row % c)
        return (col % c) < (row % c), (col % c) <= (row % c)

    masks = {rev: causal(rev) for rev in sorted({u[6] for u in units})}

    cums = [_cumsum_rows(u[0], u[6]) for u in units]
    st = []
    for (ld, kk, beta, kt, r, v, rev), cum in zip(units, cums):
        tot = cum[0:1] if rev else cum[c - 1:c]
        ginv = jnp.exp(-cum)
        tail = jnp.exp(tot - cum)
        st.append(dict(
            rev=rev, etot=jnp.exp(tot),
            a_t=two_heads(-kk * jnp.exp(cum - ld)), r_t=two_heads(r * jnp.exp(cum)),
            bk=jnp.concatenate([two_heads(beta * ginv), two_heads(kt * ginv)], axis=0),
            bkg=jnp.concatenate([two_heads(beta * tail), two_heads(kt * tail)], axis=0),
            v2=jnp.concatenate([v, v], axis=0), vh=two_heads(v)))
    grams = [_dot_nt(jnp.concatenate([s["a_t"], s["r_t"]], axis=0), s["bk"]) for s in st]
    for s, gram in zip(st, grams):
        strict, incl = masks[s["rev"]]
        s["l_ab"] = jnp.where(strict, gram[:c2, :c2], 0.0)
        s["l_ak"] = jnp.where(strict, gram[:c2, c2:], 0.0)
        s["t_rb"] = jnp.where(incl, gram[c2:, :c2], 0.0)
        s["t_rk"] = jnp.where(incl, gram[c2:, c2:], 0.0)
    lvs = [_dot(s["l_ak"], s["v2"]) for s in st]
    minvs = [eye for _ in st]
    b = 1
    while b < c:
        same = (row // (2 * b)) == (col // (2 * b))
        es = []
        for s in st:
            first, second = (col % (2 * b)) < b, (row % (2 * b)) >= b
            if s["rev"]:
                first, second = (row % (2 * b)) < b, (col % (2 * b)) >= b
            es.append(jnp.where(same & first & second, s["l_ab"], 0.0))
        if b == 1:
            minvs = [m + e for m, e in zip(minvs, es)]
        else:
            half = [_dot(m, e) for m, e in zip(minvs, es)]
            minvs = [m + _dot(h, m) for m, h in zip(minvs, half)]
        b *= 2
    mms = [_dot(m, jnp.concatenate([s["a_t"], lv], axis=1)) for m, s, lv in zip(minvs, st, lvs)]
    zero = jnp.zeros((c2, LANES), BF16)
    tts = [_dot(jnp.concatenate([s["t_rb"], s["t_rk"]], axis=1),
                jnp.concatenate([mm.astype(BF16), jnp.concatenate([zero, s["v2"].astype(BF16)], axis=1)], axis=0))
           for s, mm in zip(st, mms)]
    ps = [_dot_tn(mm[:, :LANES], s["bkg"][:c2]) for s, mm in zip(st, mms)]
    qs = [_dot_tn(jnp.concatenate([jnp.where(own_lanes, mm[:, LANES:], 0.0), s["vh"]], axis=0), s["bkg"])
          for s, mm in zip(st, mms)]
    out = []
    for s, p, q, tt in zip(st, ps, qs, tts):
        reff = s["r_t"] + tt[:, :LANES]
        out.append(dict(etot=s["etot"], p=p, q=q, reff=reff[:c] + reff[c:],
                        y0=jnp.where(head_a, tt[:c, LANES:], tt[c:, LANES:])))
    return out


def _rwkv_scan_kernel(*refs, seq_len, has_init, want_final):
    r_ref, v_ref, kk_ref, ld0_ref, b0_ref, kt0_ref, ld1_ref, b1_ref, kt1_ref = refs[:9]
    pos = 9
    if has_init:
        s0_ref = refs[pos]
        pos += 1
    y_ref = refs[pos]
    pos += 1
    if want_final:
        sf_ref = refs[pos]
        pos += 1
    g_scr, yb_scr = refs[pos:pos + 2]
    nc = seq_len // CHUNK
    ngroups = nc // SCAN_UNROLL
    per_dir = ((ld0_ref, b0_ref, kt0_ref), (ld1_ref, b1_ref, kt1_ref))
    y_dst = (y_ref, yb_scr)
    nchain = 2 * SCAN_PAIRS

    def group(gi, states):
        where, units = [], []
        for u in range(SCAN_UNROLL):
            for pp in range(SCAN_PAIRS):
                lanes = slice(pp * LANES, (pp + 1) * LANES)
                for d in range(2):
                    ld_ref, b_ref, kt_ref = per_dir[d]
                    cidx = gi * SCAN_UNROLL + u
                    if d == 1:
                        cidx = nc - 1 - cidx
                    start = cidx * CHUNK
                    rw = slice(start, start + CHUNK) if isinstance(start, int) else pl.ds(
                        pl.multiple_of(start, CHUNK), CHUNK)
                    where.append((rw, lanes))
                    units.append(tuple(ref[rw, lanes].astype(F32) for ref in
                                       (ld_ref, kk_ref, b_ref, kt_ref, r_ref, v_ref)) + (d == 1,))
        pre = _scan_precompute(units)
        states = list(states)
        for u in range(SCAN_UNROLL):
            cur = pre[u * nchain:(u + 1) * nchain]
            ys = [_dot_nt(cu["reff"], g) for cu, g in zip(cur, states)]
            gp = [_dot(g, cu["p"]) for cu, g in zip(cur, states)]
            states = [states[ch] * cur[ch]["etot"] + gp[ch] + cur[ch]["q"] for ch in range(nchain)]
            for ch in range(nchain):
                rw, lanes = where[u * nchain + ch]
                y_dst[ch % 2][rw, lanes] = ys[ch] + cur[ch]["y0"]
        return states

    init = [_block_diag2(s0_ref[0, ch % 2, 2 * (ch // 2)], s0_ref[0, ch % 2, 2 * (ch // 2) + 1])
            if has_init else jnp.zeros((LANES, LANES), F32) for ch in range(nchain)]
    if ngroups == 1:
        final = group(0, init)
    else:
        for ch in range(nchain):
            g_scr[ch] = init[ch]

        def body(gi, carry):
            new = group(gi, [g_scr[ch] for ch in range(nchain)])
            for ch in range(nchain):
                g_scr[ch] = new[ch]
            return carry

        lax.fori_loop(0, ngroups, body, 0)
        final = [g_scr[ch] for ch in range(nchain)]
    y_ref[...] += yb_scr[...]
    if want_final:
        for ch in range(nchain):
            sf_ref[0, ch % 2, 2 * (ch // 2)] = final[ch][:HS, :HS]
            sf_ref[0, ch % 2, 2 * (ch // 2) + 1] = final[ch][HS:, HS:]


def _rwkv_scan(proj, s_init, n_seq, seq_len, want_final):
    r, v, kk, ld0, b0, kt0, ld1, b1, kt1 = proj[:9]
    n = r.shape[0]
    npair = D // LANES
    width = SCAN_PAIRS * LANES
    blk = pl.BlockSpec((seq_len, width), lambda b, p: (b, p))
    st_spec = pl.BlockSpec((1, 2, 2 * SCAN_PAIRS, HS, HS), lambda b, p: (b, 0, p, 0, 0))
    in_specs = [blk] * 9
    args = [r, v, kk, ld0, b0, kt0, ld1, b1, kt1]
    has_init = s_init is not None
    if has_init:
        in_specs.append(st_spec)
        args.append(s_init)
    out_specs = [blk]
    out_shape = [jax.ShapeDtypeStruct((n, D), F32)]
    if want_final:
        out_specs.append(st_spec)
        out_shape.append(jax.ShapeDtypeStruct((n_seq, 2, NH, HS, HS), F32))
    res = pl.pallas_call(
        functools.partial(_rwkv_scan_kernel, seq_len=seq_len, has_init=has_init, want_final=want_final),
        grid=(n_seq, npair // SCAN_PAIRS),
        in_specs=in_specs,
        out_specs=out_specs,
        out_shape=out_shape,
        scratch_shapes=[pltpu.VMEM((2 * SCAN_PAIRS, LANES, LANES), F32), pltpu.VMEM((seq_len, width), F32)],
        compiler_params=_cparams(("arbitrary", "arbitrary")),
        name="rwkv_scan",
    )(*args)
    return res


def _rwkv_out_kernel(x_ref, m_ref, g_ref, y_ref, bonus_ref, gate_ref, lng_ref, lnb_ref, wo_ref, o_ref,
                     acc_scr):
    j = pl.program_id(1)
    last = pl.num_programs(1) - 1

    def chunk():
        e = _head_indicator(RW_CW)
        y = y_ref[...]
        mean = _dot_hilo_rhs(y, e) * (1.0 / HS)
        yc = y - mean
        var = _dot(yc * yc, e) * (1.0 / HS)
        yn = yc * lax.rsqrt(var + GN_EPS) * lng_ref[...] + lnb_ref[...]
        yn = (yn + bonus_ref[...]) * gate_ref[...]
        return _dot(yn, wo_ref[0])

    @pl.when(j == 0)
    def _():
        acc_scr[...] = chunk()

    @pl.when((j > 0) & (j < last))
    def _():
        acc_scr[...] += chunk()

    @pl.when(j == last)
    def _():
        m = m_ref[0, 0]
        g = g_ref[0]
        o_ref[...] = x_ref[...] + m[2:3] * _rms(acc_scr[...] + chunk(), g[1:2])


def _rwkv_out(src, dst, mod, norm_g, y, bonus, gate, p, layer, j, group_of_tile):
    nj = D // RW_CW
    assert nj >= 2
    tile = pl.BlockSpec((TM, RW_CW), lambda i, c: (i, c))
    vec = pl.BlockSpec((1, RW_CW), lambda i, c: (j, c))
    return _activation_call(
        _rwkv_out_kernel,
        grid=(src.ntiles, nj),
        in_specs=[
            pl.BlockSpec((TM, D), lambda i, c: (i + src.first, 0)),
            _mod_spec(layer, group_of_tile, 2),
            _normg_spec(layer, 2),
            tile, tile, tile, vec, vec,
            pl.BlockSpec((1, RW_CW, D), lambda i, c: (j, c, 0)),
        ],
        args=[src.array, mod, norm_g, y, bonus, gate, p["ln_g"], p["ln_b"], p["w_o"]],
        dst=dst, block_rows=TM, out_index=lambda i, c: (i + dst.first, 0),
        scratch_shapes=[pltpu.VMEM((TM, D), F32)],
        name="rwkv_out")


def _rope_tables(n):
    rows = n // GRID_W
    row = np.repeat(np.arange(rows), GRID_W)
    col = np.tile(np.arange(GRID_W), rows)
    pos = np.stack([row, col], axis=-1).astype(np.float64)
    quarter = D_ROPE // 4
    inv = ROPE_BASE ** (-np.arange(quarter, dtype=np.float64) / quarter)
    ang = pos[:, :, None] * inv
    cos = np.cos(ang)
    sin = np.sin(ang)
    cos_t = np.concatenate([cos, cos], axis=-1).reshape(n, D_ROPE)
    sin_t = np.concatenate([-sin, sin], axis=-1).reshape(n, D_ROPE)
    return cos_t.astype(np.float32), sin_t.astype(np.float32)


def _rope_swap_perm():
    quarter = D_ROPE // 4
    base = np.arange(D_ROPE)
    return np.where((base % (2 * quarter)) < quarter, base + quarter, base - quarter)


def _mla_proj_kernel(*refs, positional):
    (x_ref, m_ref, g_ref, wdq_ref, gq_ref, wqn_ref, wqr_ref, wqs_ref, wdkv_ref, wkr_ref, wks_ref,
     gkv_ref) = refs[:12]
    pos = 12
    if positional:
        cosq_ref, sinq_ref, cosk_ref, sink_ref = refs[pos:pos + 4]
        pos += 4
    qn_o, qr_o, ckv_o, kr_o = refs[pos:pos + 4]
    m = m_ref[0, 0]
    g = g_ref[0]
    h = (_rms(x_ref[...], g[0:1]) * (1.0 + m[1:2]) + m[0:1]).astype(BF16)
    ql = jnp.dot(h, wdq_ref[0].astype(BF16), preferred_element_type=F32)
    ql = (ql * lax.rsqrt(jnp.mean(ql * ql, axis=-1, keepdims=True) + EPS) * gq_ref[...]).astype(BF16)
    qn_o[...] = jnp.dot(ql, wqn_ref[...].astype(BF16), preferred_element_type=F32).astype(qn_o.dtype)
    qr = jnp.dot(ql, wqr_ref[...].astype(BF16), preferred_element_type=F32)
    ckv = jnp.dot(h, wdkv_ref[...].astype(BF16), preferred_element_type=F32)
    ckv_o[...] = ckv * lax.rsqrt(jnp.mean(ckv * ckv, axis=-1, keepdims=True) + EPS) * gkv_ref[...]
    kr = jnp.dot(h, wkr_ref[...].astype(BF16), preferred_element_type=F32)
    if positional:
        qs = jnp.dot(ql, wqs_ref[...].astype(BF16), preferred_element_type=F32)
        ks = jnp.dot(h, wks_ref[...].astype(BF16), preferred_element_type=F32)
        qr = qr * cosq_ref[...] + qs * sinq_ref[...]
        kr = kr * cosk_ref[...] + ks * sink_ref[...]
    qr_o[...] = qr.astype(qr_o.dtype)
    kr_o[...] = kr


def _mla_proj(src, mod, norm_g, p, layer, j, positional, group_of_tile):
    n = src.ntiles * TM
    full = lambda shape: pl.BlockSpec(shape, lambda i: (0,) * len(shape))
    in_specs = [
        pl.BlockSpec((TM, D), lambda i: (i + src.first, 0)),
        _mod_spec(layer, group_of_tile, 1),
        _normg_spec(layer, 1),
        pl.BlockSpec((1, D, Q_RANK), lambda i: (j, 0, 0)),
        pl.BlockSpec((1, Q_RANK), lambda i: (j, 0)),
        full((Q_RANK, MLA_H * D_NOPE)), full((Q_RANK, MLA_H * D_ROPE)), full((Q_RANK, MLA_H * D_ROPE)),
        full((D, KV_RANK)), full((D, D_ROPE)), full((D, D_ROPE)),
        pl.BlockSpec((1, KV_RANK), lambda i: (j, 0)),
    ]
    args = [src.array, mod, norm_g, p["w_dq"], p["g_q"], p["w_uq_nope"], p["w_uq_rope"], p["w_uq_rope_sw"],
            p["w_dkv_c"], p["w_dkv_r"], p["w_dkv_r_sw"], p["g_kv"]]
    if positional:
        cos_t, sin_t = _rope_tables(TM)
        in_specs += [full((TM, MLA_H * D_ROPE)), full((TM, MLA_H * D_ROPE)),
                     full((TM, D_ROPE)), full((TM, D_ROPE))]
        args += [jnp.asarray(np.tile(cos_t, (1, MLA_H))), jnp.asarray(np.tile(sin_t, (1, MLA_H))),
                 jnp.asarray(cos_t), jnp.asarray(sin_t)]
    outs = ((MLA_H * D_NOPE, BF16), (MLA_H * D_ROPE, BF16), (KV_RANK, F32), (D_ROPE, F32))
    return pl.pallas_call(
        functools.partial(_mla_proj_kernel, positional=positional),
        grid=(src.ntiles,),
        in_specs=in_specs,
        out_specs=[pl.BlockSpec((TM, w), lambda i: (i, 0)) for w, _ in outs],
        out_shape=[jax.ShapeDtypeStruct((n, w), dt) for w, dt in outs],
        compiler_params=_cparams(("arbitrary",)),
        name="mla_proj",
    )(*args)


def _mla_attn_kernel(x_ref, m_ref, g_ref, qn_ref, qr_ref, ckv_ref, kr_ref, wuk_ref, wuv_ref, wo_ref,
                     o_ref, kn_scr, vv_scr, oh_scr, *, nb, tq, k_len):
    qi = pl.program_id(1)

    @pl.when(qi == 0)
    def _():
        ckv = ckv_ref[...].astype(BF16)
        kn_scr[...] = jnp.dot(ckv, wuk_ref[0].astype(BF16), preferred_element_type=F32).astype(BF16)
        vv_scr[...] = jnp.dot(ckv, wuv_ref[0].astype(BF16), preferred_element_type=F32).astype(BF16)

    units = [(b, hd) for b in range(nb) for hd in range(MLA_H)]

    def scores(unit):
        b, hd = unit
        qrows = slice(b * tq, (b + 1) * tq)
        krows = slice(b * k_len, (b + 1) * k_len)
        q = jnp.concatenate([qn_ref[qrows, hd * D_NOPE:(hd + 1) * D_NOPE].astype(BF16),
                             qr_ref[qrows, hd * D_ROPE:(hd + 1) * D_ROPE].astype(BF16)], axis=1)
        k = jnp.concatenate([kn_scr[krows, hd * D_NOPE:(hd + 1) * D_NOPE],
                             kr_ref[krows, :].astype(BF16)], axis=1)
        return _dot_nt(q, k) * MLA_SCALE

    pending = [scores(u) for u in units[:ATTN_LOOKAHEAD]]
    for idx, (b, hd) in enumerate(units):
        if idx + ATTN_LOOKAHEAD < len(units):
            pending.append(scores(units[idx + ATTN_LOOKAHEAD]))
        s = pending[idx]
        pexp = jnp.exp(s - jnp.max(s, axis=-1, keepdims=True))
        pv = jnp.dot(pexp.astype(BF16), vv_scr[b * k_len:(b + 1) * k_len, hd * D_V:(hd + 1) * D_V],
                     preferred_element_type=F32)
        oh_scr[b * tq:(b + 1) * tq, hd * D_V:(hd + 1) * D_V] = (
            pv / jnp.sum(pexp, axis=-1, keepdims=True)).astype(BF16)
    o = jnp.dot(oh_scr[...], wo_ref[0].astype(BF16), preferred_element_type=F32)
    m = m_ref[0, 0]
    g = g_ref[0]
    o_ref[...] = x_ref[...] + m[2:3] * _rms(o, g[1:2])


def _mla_attn(src, dst, mod, norm_g, qn, qr, ckv_all, kr_all, p, layer, j, n_seq, q_len, k_len, nb, tq,
              group_of_step):
    nq = q_len // tq
    assert nb == 1 or nq == 1
    rows = nb * tq
    x_first, o_first = src.first * TM // rows, dst.first * TM // rows
    return _activation_call(
        functools.partial(_mla_attn_kernel, nb=nb, tq=tq, k_len=k_len),
        grid=(n_seq // nb, nq),
        in_specs=[
            pl.BlockSpec((rows, D), lambda s, q: (s * nq + q + x_first, 0)),
            pl.BlockSpec((1, 1, N_MOD, D), lambda s, q: (0, group_of_step(s), 0, 0)),
            pl.BlockSpec((1, 4, D), lambda s, q: (layer, 0, 0)),
            pl.BlockSpec((nb * tq, MLA_H * D_NOPE), lambda s, q: (s * nq + q, 0)),
            pl.BlockSpec((nb * tq, MLA_H * D_ROPE), lambda s, q: (s * nq + q, 0)),
            pl.BlockSpec((nb * k_len, KV_RANK), lambda s, q: (s, 0)),
            pl.BlockSpec((nb * k_len, D_ROPE), lambda s, q: (s, 0)),
            pl.BlockSpec((1, KV_RANK, MLA_H * D_NOPE), lambda s, q: (j, 0, 0)),
            pl.BlockSpec((1, KV_RANK, MLA_H * D_V), lambda s, q: (j, 0, 0)),
            pl.BlockSpec((1, MLA_H * D_V, D), lambda s, q: (j, 0, 0)),
        ],
        args=[src.array, mod, norm_g, qn, qr, ckv_all, kr_all, p["w_uk"], p["w_uv"], p["w_o"]],
        dst=dst, block_rows=rows, out_index=lambda s, q: (s * nq + q + o_first, 0),
        scratch_shapes=[pltpu.VMEM((nb * k_len, MLA_H * D_NOPE), BF16),
                        pltpu.VMEM((nb * k_len, MLA_H * D_V), BF16),
                        pltpu.VMEM((nb * tq, MLA_H * D_V), BF16)],
        name="mla_attn")


def kernel(x_prompt, x_sample, state_rwkv, cache_mla_ckv, cache_mla_krope, c, c_ctx, mod_w, mod_b, norm_g,
           mlp_w1, mlp_w2, fft_w_out, conv_w_in, conv_w, conv_w_out, rwkv_mu, rwkv_w_r, rwkv_w_k, rwkv_w_v,
           rwkv_w_o, rwkv_w0, rwkv_w_l1, rwkv_w_l2, rwkv_a0, rwkv_a_l1, rwkv_a_l2, rwkv_g_l1, rwkv_g_l2,
           rwkv_k_k, rwkv_k_a, rwkv_r_k, rwkv_ln_g, rwkv_ln_b, mla_w_dq, mla_g_q, mla_w_uq, mla_w_dkv,
           mla_g_kv, mla_w_uk, mla_w_uv, mla_w_o):
    batch, seq, _ = x_prompt.shape
    dec_batch, dec_seq, _ = x_sample.shape
    past_len = cache_mla_ckv.shape[2]
    assert (batch * seq) % TM == 0 and TM % seq == 0 and dec_seq == TM and seq % CHUNK == 0

    np_tiles = batch * seq // TM
    ns_tiles = dec_batch * dec_seq // TM
    total_rows = (np_tiles + ns_tiles) * TM
    cs = jnp.concatenate([c_ctx[None, :], c, jnp.zeros((MOD_ROWS - 1 - dec_batch, D), F32)], axis=0)
    mod_b3 = mod_b.reshape(DEPTH, 1, N_MOD * D)
    mod = _modulation(cs, mod_w, mod_b3, 0)

    grp_p = lambda i: 0
    grp_s = lambda i: 1 + i
    grp_all = lambda i: jnp.maximum(i - (np_tiles - 1), 0)
    new_rwkv, new_ckv, new_krope = [], [], []
    streams = [(True, seq, batch, grp_p), (False, dec_seq, dec_batch, grp_s)]
    cur = [_Rows(x_prompt.reshape(batch * seq, D), 0, np_tiles),
           _Rows(x_sample.reshape(dec_batch * dec_seq, D), 0, ns_tiles)]

    for i in range(DEPTH):
        kind, j = i % 4, i // 4
        for idx, (is_prompt, slen, nseq, grp) in enumerate(streams):
            src = cur[idx]
            sharing = src.array.shape[0] == total_rows
            dst = _Dest(total_rows, src.first, True) if sharing else _Dest(src.ntiles * TM, 0, False)
            shared = None
            if kind == 0:
                shared = _fnet(src, dst, mod, norm_g, fft_w_out, i, j, slen, grp)
            elif kind == 1:
                shared = _conv(src, dst, mod, norm_g, conv_w_in, conv_w, conv_w_out, i, j, slen, grp)
            elif kind == 2:
                p = dict(mu=rwkv_mu, w_r=rwkv_w_r, w_k=rwkv_w_k, w_v=rwkv_w_v, w_o=rwkv_w_o, w0=rwkv_w0,
                         w_l1=rwkv_w_l1, w_l2=rwkv_w_l2, a0=rwkv_a0, a_l1=rwkv_a_l1, a_l2=rwkv_a_l2,
                         g_l1=rwkv_g_l1, g_l2=rwkv_g_l2, k_k=rwkv_k_k, k_a=rwkv_k_a,
                         r_k=rwkv_r_k.reshape(-1, D), ln_g=rwkv_ln_g, ln_b=rwkv_ln_b)
                proj = _rwkv_proj(src, mod, norm_g, p, i, j, slen, grp)
                if is_prompt:
                    y, s_fin = _rwkv_scan(proj, None, nseq, slen, True)
                    new_rwkv.append(s_fin)
                else:
                    (y,) = _rwkv_scan(proj, state_rwkv[:, j], nseq, slen, False)
                shared = _rwkv_out(src, dst, mod, norm_g, y, proj[10], proj[9], p, i, j, grp)
            else:
                perm = _rope_swap_perm()
                w_uq = mla_w_uq[j].reshape(Q_RANK, MLA_H, D_NOPE + D_ROPE)
                w_uq_rope = w_uq[:, :, D_NOPE:]
                w_dkv_r = mla_w_dkv[j][:, KV_RANK:]
                p = dict(w_dq=mla_w_dq, g_q=mla_g_q, g_kv=mla_g_kv, w_uk=mla_w_uk, w_uv=mla_w_uv, w_o=mla_w_o,
                         w_uq_nope=w_uq[:, :, :D_NOPE].reshape(Q_RANK, MLA_H * D_NOPE),
                         w_uq_rope=w_uq_rope.reshape(Q_RANK, MLA_H * D_ROPE),
                         w_uq_rope_sw=w_uq_rope[:, :, perm].reshape(Q_RANK, MLA_H * D_ROPE),
                         w_dkv_c=mla_w_dkv[j][:, :KV_RANK], w_dkv_r=w_dkv_r, w_dkv_r_sw=w_dkv_r[:, perm])
                qn, qr, ckv, kr = _mla_proj(src, mod, norm_g, p, i, j, not is_prompt, grp)
                if is_prompt:
                    new_ckv.append(ckv.reshape(batch, seq, KV_RANK))
                    new_krope.append(kr.reshape(batch, seq, D_ROPE))
                    shared = _mla_attn(src, dst, mod, norm_g, qn, qr, ckv, kr, p, i, j, nseq, slen, slen,
                                       TM // slen, slen, lambda s: 0)
                else:
                    klen = past_len + slen
                    ckv_all = jnp.concatenate([cache_mla_ckv[:, j], ckv.reshape(nseq, slen, KV_RANK)], axis=1)
                    kr_all = jnp.concatenate([cache_mla_krope[:, j], kr.reshape(nseq, slen, D_ROPE)], axis=1)
                    shared = _mla_attn(src, dst, mod, norm_g, qn, qr, ckv_all.reshape(nseq * klen, KV_RANK),
                                       kr_all.reshape(nseq * klen, D_ROPE), p, i, j, nseq, slen, klen, 1, 256,
                                       lambda s: 1 + s)
            cur[idx] = _Rows(shared, dst.first, src.ntiles)
            if sharing:
                cur[1 - idx] = cur[1 - idx]._replace(array=shared)
        if i < DEPTH - 1:
            srcs = [_Rows(cur[0].array, 0, np_tiles + ns_tiles)] if cur[0].array is cur[1].array else cur
            both, mod = _mlp(srcs, _Dest(total_rows, 0, False), mod, norm_g, mlp_w1, mlp_w2, i, grp_all,
                             next_mod=(cs, mod_w, mod_b3, i + 1))
            cur = [_Rows(both, 0, np_tiles), _Rows(both, np_tiles, ns_tiles)]
        else:
            y_prompt, y_sample = [
                _mlp([cur[idx]], _Dest(cur[idx].ntiles * TM, 0, False), mod, norm_g, mlp_w1, mlp_w2, i,
                     streams[idx][3]) for idx in range(2)]

    return (y_prompt.reshape(batch, seq, D), y_sample.reshape(dec_batch, dec_seq, D),
            jnp.stack(new_rwkv, axis=1), jnp.stack(new_ckv, axis=1), jnp.stack(new_krope, axis=1))
```

```python
import functools
from typing import NamedTuple

import numpy as np
import jax
import jax.numpy as jnp
from jax import lax
from jax.experimental import pallas as pl
from jax.experimental.pallas import tpu as pltpu

D = 1024
DEPTH = 4
N_MOD = 6
D_FF = 4 * D
EPS = 1e-6
GRID_W = 64
FFT_GROUPS = 8
FFT_GW = D // FFT_GROUPS
HS = 64
NH = D // HS
LORA_W = 64
LORA_A = 64
LORA_G = 128
GN_EPS = 64e-5
MLA_H = 8
D_NOPE = 128
D_ROPE = 64
D_V = 128
KV_RANK = 256
Q_RANK = 384
ROPE_BASE = 10000.0
MLA_SCALE = (D_NOPE + D_ROPE) ** -0.5

F32 = jnp.float32
BF16 = jnp.bfloat16

TM = 1024
LANES = 128
CHUNK = 64
SCAN_UNROLL = 4
SCAN_PAIRS = 2
ATTN_LOOKAHEAD = 2
VMEM_LIMIT = 58 * 1024 * 1024


def _cparams(sem):
    return pltpu.CompilerParams(dimension_semantics=sem, vmem_limit_bytes=VMEM_LIMIT)


def _dot(a, b):
    return jnp.dot(a.astype(BF16), b.astype(BF16), preferred_element_type=F32)


def _dot_nt(a, b):
    return lax.dot_general(a.astype(BF16), b.astype(BF16), (((1,), (1,)), ((), ())),
                           preferred_element_type=F32)


def _dot_tn(a, b):
    return lax.dot_general(a.astype(BF16), b.astype(BF16), (((0,), (0,)), ((), ())),
                           preferred_element_type=F32)


def _block_diag2(a, b):
    za = jnp.zeros((a.shape[0], b.shape[1]), a.dtype)
    zb = jnp.zeros((b.shape[0], a.shape[1]), a.dtype)
    return jnp.concatenate([jnp.concatenate([a, za], axis=1), jnp.concatenate([zb, b], axis=1)], axis=0)


def _cumsum_rows(x, reverse):
    n = x.shape[0]
    idx = lax.broadcasted_iota(jnp.int32, x.shape, 0)
    s = 1
    while s < n:
        if reverse:
            x = x + jnp.where(idx < n - s, pltpu.roll(x, n - s, 0), 0.0)
        else:
            x = x + jnp.where(idx >= s, pltpu.roll(x, s, 0), 0.0)
        s *= 2
    return x


def _dot_hilo_rhs(x, e):
    hi = x.astype(BF16)
    lo = (x - hi.astype(F32)).astype(BF16)
    return jnp.dot(hi, e, preferred_element_type=F32) + jnp.dot(lo, e, preferred_element_type=F32)


def _rms(x, g):
    return x * lax.rsqrt(jnp.mean(x * x, axis=-1, keepdims=True) + EPS) * g


def _sigmoid(x):
    return 1.0 / (1.0 + jnp.exp(-x))


def _softplus(x):
    return jnp.maximum(x, 0.0) + jnp.log(1.0 + jnp.exp(-jnp.abs(x)))


def _head_indicator(n):
    r = lax.broadcasted_iota(jnp.int32, (n, n), 0) // HS
    c = lax.broadcasted_iota(jnp.int32, (n, n), 1) // HS
    return jnp.where(r == c, 1.0, 0.0).astype(BF16)


def _shift_rows(z, seq_len):
    n = z.shape[0]
    pos = lax.broadcasted_iota(jnp.int32, z.shape, 0) % seq_len
    prev = jnp.where(pos == 0, 0.0, pltpu.roll(z, 1, 0))
    nxt = jnp.where(pos == seq_len - 1, 0.0, pltpu.roll(z, n - 1, 0))
    return prev, nxt


MOD_TN = 1536
MOD_ROWS = 8


def _mod_block(cs_ref, w_ref, b_ref):
    cs = cs_ref[...]
    return _dot(cs * _sigmoid(cs), w_ref[0]) + b_ref[0]


def _mod_kernel(cs_ref, w_ref, b_ref, o_ref):
    o_ref[...] = _mod_block(cs_ref, w_ref, b_ref)


def _modulation(cs, mod_w, mod_b3, layer):
    nj = (N_MOD * D) // MOD_TN
    out = pl.pallas_call(
        _mod_kernel,
        grid=(nj,),
        in_specs=[
            pl.BlockSpec((MOD_ROWS, D), lambda j: (0, 0)),
            pl.BlockSpec((1, D, MOD_TN), lambda j: (layer, 0, j)),
            pl.BlockSpec((1, 1, MOD_TN), lambda j: (layer, 0, j)),
        ],
        out_specs=pl.BlockSpec((MOD_ROWS, MOD_TN), lambda j: (0, j)),
        out_shape=jax.ShapeDtypeStruct((MOD_ROWS, N_MOD * D), F32),
        compiler_params=_cparams(("arbitrary",)),
        name="modulation",
    )(cs, mod_w, mod_b3)
    return out.reshape(1, MOD_ROWS, N_MOD, D)


class _Rows(NamedTuple):
    array: jax.Array
    first: int
    ntiles: int


class _Dest(NamedTuple):
    rows: int
    first: int
    inplace: bool


def _activation_call(kernel, *, grid, in_specs, args, dst, block_rows, out_index, scratch_shapes, name,
                     extra_out=None):
    assert not dst.inplace or args[0].shape[0] == dst.rows
    out_specs = pl.BlockSpec((block_rows, D), out_index)
    out_shape = jax.ShapeDtypeStruct((dst.rows, D), F32)
    if extra_out is not None:
        out_specs, out_shape = [out_specs, extra_out[0]], [out_shape, extra_out[1]]
    return pl.pallas_call(
        kernel, grid=grid, in_specs=in_specs, out_specs=out_specs, out_shape=out_shape,
        scratch_shapes=scratch_shapes, input_output_aliases={0: 0} if dst.inplace else {},
        compiler_params=_cparams(("arbitrary",) * len(grid)), name=name)(*args)


def _mod_spec(layer, group_of_tile, ngrid):
    del layer
    if ngrid == 1:
        return pl.BlockSpec((1, 1, N_MOD, D), lambda i: (0, group_of_tile(i), 0, 0))
    return pl.BlockSpec((1, 1, N_MOD, D), lambda i, j: (0, group_of_tile(i), 0, 0))


def _normg_spec(layer, ngrid):
    if ngrid == 1:
        return pl.BlockSpec((1, 4, D), lambda i: (layer, 0, 0))
    return pl.BlockSpec((1, 4, D), lambda i, j: (layer, 0, 0))


MLP_FC = 1024


def _mlp_kernel(*refs, n_first, with_next_mod):
    if n_first is None:
        (x_ref,), rest = refs[:1], refs[1:]
        read_x = lambda: x_ref[...]
    else:
        (xa_ref, xb_ref), rest = refs[:2], refs[2:]
        read_x = lambda: jnp.where(pl.program_id(0) < n_first, xa_ref[...], xb_ref[...])
    if with_next_mod:
        m_ref, g_ref, w1_ref, w2_ref, cs_ref, mw_ref, mb_ref, o_ref, mo_ref, h_scr, acc_scr = rest
        mo_ref[...] = _mod_block(cs_ref, mw_ref, mb_ref)
    else:
        m_ref, g_ref, w1_ref, w2_ref, o_ref, h_scr, acc_scr = rest
    j = pl.program_id(1)
    m = m_ref[0, 0]
    g = g_ref[0]

    last = pl.num_programs(1) - 1

    def chunk():
        a = jnp.dot(h_scr[...], w1_ref[0].astype(BF16), preferred_element_type=F32)
        a = jnp.maximum(a, 0.0)
        return _dot(a * a, w2_ref[0])

    @pl.when(j == 0)
    def _():
        h = _rms(read_x(), g[2:3]) * (1.0 + m[4:5]) + m[3:4]
        h_scr[...] = h.astype(BF16)
        acc_scr[...] = chunk()

    @pl.when((j > 0) & (j < last))
    def _():
        acc_scr[...] += chunk()

    @pl.when(j == last)
    def _():
        f = acc_scr[...] + chunk()
        o_ref[...] = read_x() + m[5:6] * _rms(f, g[3:4])


def _mlp(srcs, dst, mod, norm_g, w1, w2, layer, group_of_tile, next_mod=None):
    nj = D_FF // MLP_FC
    assert nj >= 2
    if len(srcs) == 1:
        (src,) = srcs
        ntiles, n_first = src.ntiles, None
        x_specs = [pl.BlockSpec((TM, D), lambda i, j: (i + src.first, 0))]
    else:
        sa, sb = srcs
        ntiles, n_first = sa.ntiles + sb.ntiles, sa.ntiles
        x_specs = [pl.BlockSpec((TM, D), lambda i, j: (jnp.minimum(i, sa.ntiles - 1) + sa.first, 0)),
                   pl.BlockSpec((TM, D), lambda i, j: (jnp.maximum(i - sa.ntiles, 0) + sb.first, 0))]
    in_specs = x_specs + [
        _mod_spec(layer, group_of_tile, 2),
        _normg_spec(layer, 2),
        pl.BlockSpec((1, D, MLP_FC), lambda i, j: (layer, 0, j)),
        pl.BlockSpec((1, MLP_FC, D), lambda i, j: (layer, j, 0)),
    ]
    args = [s.array for s in srcs] + [mod, norm_g, w1, w2]
    extra_out = None
    if next_mod is not None:
        cs, mod_w, mod_b3, nxt = next_mod
        cols = (N_MOD * D) // (ntiles * nj)
        assert cols * ntiles * nj == N_MOD * D and cols % LANES == 0
        in_specs += [pl.BlockSpec((MOD_ROWS, D), lambda i, j: (0, 0)),
                     pl.BlockSpec((1, D, cols), lambda i, j: (nxt, 0, i * nj + j)),
                     pl.BlockSpec((1, 1, cols), lambda i, j: (nxt, 0, i * nj + j))]
        args += [cs, mod_w, mod_b3]
        extra_out = (pl.BlockSpec((MOD_ROWS, cols), lambda i, j: (0, i * nj + j)),
                     jax.ShapeDtypeStruct((MOD_ROWS, N_MOD * D), F32))
    res = _activation_call(
        functools.partial(_mlp_kernel, n_first=n_first, with_next_mod=next_mod is not None),
        grid=(ntiles, nj), in_specs=in_specs, args=args,
        dst=dst, block_rows=TM, out_index=lambda i, j: (i + dst.first, 0),
        scratch_shapes=[pltpu.VMEM((TM, D), BF16), pltpu.VMEM((TM, D), F32)],
        name="mlp", extra_out=extra_out)
    if next_mod is None:
        return res
    return res[0], res[1].reshape(1, MOD_ROWS, N_MOD, D)


def _dft_mats(n):
    idx = np.arange(n, dtype=np.int64)
    ang = (2.0 * np.pi / n) * ((idx[:, None] * idx[None, :]) % n).astype(np.float64)
    scale = 1.0 / np.sqrt(n)
    return np.cos(ang) * scale, np.sin(ang) * scale


def _fnet_kernel(x_ref, m_ref, g_ref, cs_ref, cn_ref, sn_ref, w_ref, o_ref, p_scr, q_scr, f_scr,
                 *, seq_len):
    m = m_ref[0, 0]
    g = g_ref[0]
    x = x_ref[...]
    h = (_rms(x, g[0:1]) * (1.0 + m[1:2]) + m[0:1]).astype(BF16)
    cs = cs_ref[...].astype(BF16)
    for gi in range(FFT_GROUPS):
        pq = jnp.dot(h[:, gi * FFT_GW:(gi + 1) * FFT_GW], cs, preferred_element_type=F32)
        p_scr[:, gi * FFT_GW:(gi + 1) * FFT_GW] = pq[:, :FFT_GW].astype(BF16)
        q_scr[:, gi * FFT_GW:(gi + 1) * FFT_GW] = pq[:, FFT_GW:].astype(BF16)
    cn = cn_ref[...].astype(BF16)
    sn = sn_ref[...].astype(BF16)
    for s in range(TM // seq_len):
        rows = slice(s * seq_len, (s + 1) * seq_len)
        f = (jnp.dot(cn, p_scr[rows, :], preferred_element_type=F32)
             - jnp.dot(sn, q_scr[rows, :], preferred_element_type=F32))
        f_scr[rows, :] = f.astype(BF16)
    o = jnp.dot(f_scr[...], w_ref[0].astype(BF16), preferred_element_type=F32)
    o_ref[...] = x + m[2:3] * _rms(o, g[1:2])


def _fnet(src, dst, mod, norm_g, w_out, layer, j, seq_len, group_of_tile):
    cg, sg = _dft_mats(FFT_GW)
    cs = jnp.asarray(np.concatenate([cg, sg], axis=1), F32)
    cn_np, sn_np = _dft_mats(seq_len)
    cn = jnp.asarray(cn_np, F32)
    sn = jnp.asarray(sn_np, F32)
    return _activation_call(
        functools.partial(_fnet_kernel, seq_len=seq_len),
        grid=(src.ntiles,),
        in_specs=[
            pl.BlockSpec((TM, D), lambda i: (i + src.first, 0)),
            _mod_spec(layer, group_of_tile, 1),
            _normg_spec(layer, 1),
            pl.BlockSpec((FFT_GW, 2 * FFT_GW), lambda i: (0, 0)),
            pl.BlockSpec((seq_len, seq_len), lambda i: (0, 0)),
            pl.BlockSpec((seq_len, seq_len), lambda i: (0, 0)),
            pl.BlockSpec((1, D, D), lambda i: (j, 0, 0)),
        ],
        args=[src.array, mod, norm_g, cs, cn, sn, w_out],
        dst=dst, block_rows=TM, out_index=lambda i: (i + dst.first, 0),
        scratch_shapes=[pltpu.VMEM((TM, D), BF16), pltpu.VMEM((TM, D), BF16),
                        pltpu.VMEM((TM, D), BF16)],
        name="fourier_mix")


CONV_CW = 512


def _conv_kernel(x_ref, m_ref, g_ref, wb_ref, wc_ref, wu_ref, cw_ref, wo_ref, o_ref, h_scr, acc_scr,
                 *, seq_len):
    j = pl.program_id(1)
    m = m_ref[0, 0]
    g = g_ref[0]

    def chunk():
        h = h_scr[...]
        hw = CONV_CW // 2
        proj = []
        for s in range(2):
            cols = slice(s * hw, (s + 1) * hw)
            proj.append([jnp.dot(h, w_ref[0, :, cols].astype(BF16), preferred_element_type=F32)
                         for w_ref in (wb_ref, wc_ref, wu_ref)])
        out = None
        for s in range(2):
            cols = slice(s * hw, (s + 1) * hw)
            bg, cg, u = proj[s]
            z = cg * u
            z_prev, z_next = _shift_rows(z, seq_len)
            conv = z_prev * cw_ref[0, 0:1, cols] + z * cw_ref[0, 1:2, cols] + z_next * cw_ref[0, 2:3, cols]
            part = _dot(bg * conv, wo_ref[0, cols, :])
            out = part if out is None else out + part
        return out

    last = pl.num_programs(1) - 1

    @pl.when(j == 0)
    def _():
        h = _rms(x_ref[...], g[0:1]) * (1.0 + m[1:2]) + m[0:1]
        h_scr[...] = h.astype(BF16)
        acc_scr[...] = chunk()

    @pl.when((j > 0) & (j < last))
    def _():
        acc_scr[...] += chunk()

    @pl.when(j == last)
    def _():
        o_ref[...] = x_ref[...] + m[2:3] * _rms(acc_scr[...] + chunk(), g[1:2])


def _conv(src, dst, mod, norm_g, w_in, w_conv, w_out, layer, j, seq_len, group_of_tile):
    nj = D // CONV_CW
    assert nj >= 2
    return _activation_call(
        functools.partial(_conv_kernel, seq_len=seq_len),
        grid=(src.ntiles, nj),
        in_specs=[
            pl.BlockSpec((TM, D), lambda i, c: (i + src.first, 0)),
            _mod_spec(layer, group_of_tile, 2),
            _normg_spec(layer, 2),
            pl.BlockSpec((1, D, CONV_CW), lambda i, c: (j, 0, c)),
            pl.BlockSpec((1, D, CONV_CW), lambda i, c: (j, 0, nj + c)),
            pl.BlockSpec((1, D, CONV_CW), lambda i, c: (j, 0, 2 * nj + c)),
            pl.BlockSpec((1, 3, CONV_CW), lambda i, c: (j, 0, c)),
            pl.BlockSpec((1, CONV_CW, D), lambda i, c: (j, c, 0)),
        ],
        args=[src.array, mod, norm_g, w_in, w_in, w_in, w_conv, w_out],
        dst=dst, block_rows=TM, out_index=lambda i, c: (i + dst.first, 0),
        scratch_shapes=[pltpu.VMEM((TM, D), BF16), pltpu.VMEM((TM, D), F32)],
        name="short_conv")


RWP_CW = 256
RW_OUT_DTYPES = (BF16, BF16, BF16, F32, BF16, BF16, F32, BF16, BF16, BF16, BF16)
RW_CW = 512


def _rwkv_proj_kernel(x_ref, m_ref, g_ref, mu_ref, wr_ref, wk_ref, wv_ref, w0_ref, wl1_ref, wl2_ref,
                      a0_ref, al1_ref, al2_ref, gl1_ref, gl2_ref, kk_ref, ka_ref, rk_ref,
                      r_o, v_o, kk_o, ld0_o, b0_o, kt0_o, ld1_o, b1_o, kt1_o, g_o, bonus_o,
                      xr_scr, xk_scr, xv_scr, tw_scr, ta_scr, sg_scr, *, seq_len):
    j = pl.program_id(1)

    def chunk():
        e = _head_indicator(RWP_CW)
        r = jnp.dot(xr_scr[...], wr_ref[0].astype(BF16), preferred_element_type=F32)
        k = jnp.dot(xk_scr[...], wk_ref[0].astype(BF16), preferred_element_type=F32)
        v = jnp.dot(xv_scr[...], wv_ref[0].astype(BF16), preferred_element_type=F32)
        g_o[...] = jnp.dot(sg_scr[...], gl2_ref[0].astype(BF16), preferred_element_type=F32).astype(g_o.dtype)
        zws = [jnp.dot(tw_scr[d], wl2_ref[0, d].astype(BF16), preferred_element_type=F32) for d in range(2)]
        zas = [jnp.dot(ta_scr[d], al2_ref[0, d].astype(BF16), preferred_element_type=F32) for d in range(2)]
        kk = k * kk_ref[...]
        kk = kk * lax.rsqrt(_dot(kk * kk, e) + 1e-12)
        r_o[...] = r.astype(r_o.dtype)
        v_o[...] = v.astype(v_o.dtype)
        kk_o[...] = kk.astype(kk_o.dtype)
        k_a = ka_ref[...]
        kt_sum = None
        for d, (ld_o, b_o, kt_o) in enumerate(((ld0_o, b0_o, kt0_o), (ld1_o, b1_o, kt1_o))):
            logw = -_softplus(-(w0_ref[0, d:d + 1, :] + zws[d])) - 0.5
            ld_o[...] = -jnp.exp(logw)
            a = _sigmoid(a0_ref[0, d:d + 1, :] + zas[d])
            kt = k * (1.0 + (a - 1.0) * k_a)
            b_o[...] = (kk * a).astype(b_o.dtype)
            kt_o[...] = kt.astype(kt_o.dtype)
            kt_sum = kt if kt_sum is None else kt_sum + kt
        bonus_o[...] = (_dot(r * rk_ref[...] * kt_sum, e) * v).astype(bonus_o.dtype)

    @pl.when(j == 0)
    def _():
        m = m_ref[0, 0]
        g = g_ref[0]
        mu = mu_ref[0]
        h = _rms(x_ref[...], g[0:1]) * (1.0 + m[1:2]) + m[0:1]
        h_prev, h_next = _shift_rows(h, seq_len)
        dx = 0.5 * (h_prev + h_next) - h
        xr_scr[...] = (h + dx * mu[0:1]).astype(BF16)
        xk_scr[...] = (h + dx * mu[2:3]).astype(BF16)
        xv_scr[...] = (h + dx * mu[3:4]).astype(BF16)
        xw = (h + dx * mu[1:2]).astype(BF16)
        xa = (h + dx * mu[4:5]).astype(BF16)
        xg = (h + dx * mu[5:6]).astype(BF16)
        for d in range(2):
            tw_scr[d] = jnp.tanh(_dot(xw, wl1_ref[0, d])).astype(BF16)
            ta_scr[d] = _dot(xa, al1_ref[0, d]).astype(BF16)
        sg_scr[...] = _sigmoid(_dot(xg, gl1_ref[0])).astype(BF16)
        chunk()

    @pl.when(j > 0)
    def _():
        chunk()


def _rwkv_proj(src, mod, norm_g, p, layer, j, seq_len, group_of_tile):
    n = src.ntiles * TM
    nj = D // RWP_CW
    tile = pl.BlockSpec((TM, RWP_CW), lambda i, c: (i, c))
    wcol = lambda: pl.BlockSpec((1, D, RWP_CW), lambda i, c: (j, 0, c))
    vec = lambda: pl.BlockSpec((1, RWP_CW), lambda i, c: (j, c))
    return pl.pallas_call(
        functools.partial(_rwkv_proj_kernel, seq_len=seq_len),
        grid=(src.ntiles, nj),
        in_specs=[
            pl.BlockSpec((TM, D), lambda i, c: (i + src.first, 0)),
            _mod_spec(layer, group_of_tile, 2),
            _normg_spec(layer, 2),
            pl.BlockSpec((1, 6, D), lambda i, c: (j, 0, 0)),
            wcol(), wcol(), wcol(),
            pl.BlockSpec((1, 2, RWP_CW), lambda i, c: (j, 0, c)),
            pl.BlockSpec((1, 2, D, LORA_W), lambda i, c: (j, 0, 0, 0)),
            pl.BlockSpec((1, 2, LORA_W, RWP_CW), lambda i, c: (j, 0, 0, c)),
            pl.BlockSpec((1, 2, RWP_CW), lambda i, c: (j, 0, c)),
            pl.BlockSpec((1, 2, D, LORA_A), lambda i, c: (j, 0, 0, 0)),
            pl.BlockSpec((1, 2, LORA_A, RWP_CW), lambda i, c: (j, 0, 0, c)),
            pl.BlockSpec((1, D, LORA_G), lambda i, c: (j, 0, 0)),
            pl.BlockSpec((1, LORA_G, RWP_CW), lambda i, c: (j, 0, c)),
            vec(), vec(), vec(),
        ],
        out_specs=[tile] * len(RW_OUT_DTYPES),
        out_shape=[jax.ShapeDtypeStruct((n, D), dt) for dt in RW_OUT_DTYPES],
        scratch_shapes=[pltpu.VMEM((TM, D), BF16), pltpu.VMEM((TM, D), BF16), pltpu.VMEM((TM, D), BF16),
                        pltpu.VMEM((2, TM, LORA_W), BF16), pltpu.VMEM((2, TM, LORA_A), BF16),
                        pltpu.VMEM((TM, LORA_G), BF16)],
        compiler_params=_cparams(("arbitrary", "arbitrary")),
        name="rwkv_proj",
    )(src.array, mod, norm_g, p["mu"], p["w_r"], p["w_k"], p["w_v"], p["w0"], p["w_l1"], p["w_l2"],
      p["a0"], p["a_l1"], p["a_l2"], p["g_l1"], p["g_l2"], p["k_k"], p["k_a"], p["r_k"])


def _scan_precompute(units):
    c = CHUNK
    c2 = 2 * c
    row = lax.broadcasted_iota(jnp.int32, (c2, c2), 0)
    col = lax.broadcasted_iota(jnp.int32, (c2, c2), 1)
    head_a = lax.broadcasted_iota(jnp.int32, (c, LANES), 1) < HS
    own_lanes = jnp.concatenate([head_a, jnp.logical_not(head_a)], axis=0)
    eye = jnp.where(row == col, 1.0, 0.0)

    def two_heads(t):
        return jnp.concatenate([jnp.where(head_a, t, 0.0), jnp.where(head_a, 0.0, t)], axis=0)

    def causal(reverse):
        if reverse:
            return (col % c) > (row % c), (col % c) >= (row % c)
        return (col % c) < (row % c), (col % c) <= (row % c)

    masks = {rev: causal(rev) for rev in sorted({u[6] for u in units})}

    cums = [_cumsum_rows(u[0], u[6]) for u in units]
    st = []
    for (ld, kk, beta, kt, r, v, rev), cum in zip(units, cums):
        tot = cum[0:1] if rev else cum[c - 1:c]
        ginv = jnp.exp(-cum)
        tail = jnp.exp(tot - cum)
        st.append(dict(
            rev=rev, etot=jnp.exp(tot),
            a_t=two_heads(-kk * jnp.exp(cum - ld)), r_t=two_heads(r * jnp.exp(cum)),
            bk=jnp.concatenate([two_heads(beta * ginv), two_heads(kt * ginv)], axis=0),
            bkg=jnp.concatenate([two_heads(beta * tail), two_heads(kt * tail)], axis=0),
            v2=jnp.concatenate([v, v], axis=0), vh=two_heads(v)))
    grams = [_dot_nt(jnp.concatenate([s["a_t"], s["r_t"]], axis=0), s["bk"]) for s in st]
    for s, gram in zip(st, grams):
        strict, incl = masks[s["rev"]]
        s["l_ab"] = jnp.where(strict, gram[:c2, :c2], 0.0)
        s["l_ak"] = jnp.where(strict, gram[:c2, c2:], 0.0)
        s["t_rb"] = jnp.where(incl, gram[c2:, :c2], 0.0)
        s["t_rk"] = jnp.where(incl, gram[c2:, c2:], 0.0)
    lvs = [_dot(s["l_ak"], s["v2"]) for s in st]
    minvs = [eye for _ in st]
    b = 1
    while b < c:
        same = (row // (2 * b)) == (col // (2 * b))
        es = []
        for s in st:
            first, second = (col % (2 * b)) < b, (row % (2 * b)) >= b
            if s["rev"]:
                first, second = (row % (2 * b)) < b, (col % (2 * b)) >= b
            es.append(jnp.where(same & first & second, s["l_ab"], 0.0))
        if b == 1:
            minvs = [m + e for m, e in zip(minvs, es)]
        else:
            half = [_dot(m, e) for m, e in zip(minvs, es)]
            minvs = [m + _dot(h, m) for m, h in zip(minvs, half)]
        b *= 2
    mms = [_dot(m, jnp.concatenate([s["a_t"], lv], axis=1)) for m, s, lv in zip(minvs, st, lvs)]
    zero = jnp.zeros((c2, LANES), BF16)
    tts = [_dot(jnp.concatenate([s["t_rb"], s["t_rk"]], axis=1),
                jnp.concatenate([mm.astype(BF16), jnp.concatenate([zero, s["v2"].astype(BF16)], axis=1)], axis=0))
           for s, mm in zip(st, mms)]
    ps = [_dot_tn(mm[:, :LANES], s["bkg"][:c2]) for s, mm in zip(st, mms)]
    qs = [_dot_tn(jnp.concatenate([jnp.where(own_lanes, mm[:, LANES:], 0.0), s["vh"]], axis=0), s["bkg"])
          for s, mm in zip(st, mms)]
    out = []
    for s, p, q, tt in zip(st, ps, qs, tts):
        reff = s["r_t"] + tt[:, :LANES]
        out.append(dict(etot=s["etot"], p=p, q=q, reff=reff[:c] + reff[c:],
                        y0=jnp.where(head_a, tt[:c, LANES:], tt[c:, LANES:])))
    return out


def _rwkv_scan_kernel(*refs, seq_len, has_init, want_final):
    r_ref, v_ref, kk_ref, ld0_ref, b0_ref, kt0_ref, ld1_ref, b1_ref, kt1_ref = refs[:9]
    pos = 9
    if has_init:
        s0_ref = refs[pos]
        pos += 1
    y_ref = refs[pos]
    pos += 1
    if want_final:
        sf_ref = refs[pos]
        pos += 1
    g_scr, yf_scr, yb_scr = refs[pos:pos + 3]
    nc = seq_len // CHUNK
    ngroups = nc // SCAN_UNROLL
    per_dir = ((ld0_ref, b0_ref, kt0_ref), (ld1_ref, b1_ref, kt1_ref))
    y_dst = (yf_scr, yb_scr)
    nchain = 2 * SCAN_PAIRS

    def group(gi, states):
        where, units = [], []
        for u in range(SCAN_UNROLL):
            for pp in range(SCAN_PAIRS):
                lanes = slice(pp * LANES, (pp + 1) * LANES)
                for d in range(2):
                    ld_ref, b_ref, kt_ref = per_dir[d]
                    cidx = gi * SCAN_UNROLL + u
                    if d == 1:
                        cidx = nc - 1 - cidx
                    start = cidx * CHUNK
                    rw = slice(start, start + CHUNK) if isinstance(start, int) else pl.ds(
                        pl.multiple_of(start, CHUNK), CHUNK)
                    where.append((rw, lanes))
                    units.append(tuple(ref[rw, lanes].astype(F32) for ref in
                                       (ld_ref, kk_ref, b_ref, kt_ref, r_ref, v_ref)) + (d == 1,))
        pre = _scan_precompute(units)
        states = list(states)
        for u in range(SCAN_UNROLL):
            cur = pre[u * nchain:(u + 1) * nchain]
            ys = [_dot_nt(cu["reff"], g) for cu, g in zip(cur, states)]
            gp = [_dot(g, cu["p"]) for cu, g in zip(cur, states)]
            states = [states[ch] * cur[ch]["etot"] + gp[ch] + cur[ch]["q"] for ch in range(nchain)]
            for ch in range(nchain):
                rw, lanes = where[u * nchain + ch]
                y_dst[ch % 2][rw, lanes] = ys[ch] + cur[ch]["y0"]
        return states

    init = [_block_diag2(s0_ref[0, ch % 2, 2 * (ch // 2)], s0_ref[0, ch % 2, 2 * (ch // 2) + 1])
            if has_init else jnp.zeros((LANES, LANES), F32) for ch in range(nchain)]
    if ngroups == 1:
        final = group(0, init)
    else:
        for ch in range(nchain):
            g_scr[ch] = init[ch]

        def body(gi, carry):
            new = group(gi, [g_scr[ch] for ch in range(nchain)])
            for ch in range(nchain):
                g_scr[ch] = new[ch]
            return carry

        lax.fori_loop(0, ngroups, body, 0)
        final = [g_scr[ch] for ch in range(nchain)]
    y_ref[...] = (yf_scr[...] + yb_scr[...]).astype(y_ref.dtype)
    if want_final:
        for ch in range(nchain):
            sf_ref[0, ch % 2, 2 * (ch // 2)] = final[ch][:HS, :HS]
            sf_ref[0, ch % 2, 2 * (ch // 2) + 1] = final[ch][HS:, HS:]


def _rwkv_scan(proj, s_init, n_seq, seq_len, want_final):
    r, v, kk, ld0, b0, kt0, ld1, b1, kt1 = proj[:9]
    n = r.shape[0]
    npair = D // LANES
    width = SCAN_PAIRS * LANES
    blk = pl.BlockSpec((seq_len, width), lambda b, p: (b, p))
    st_spec = pl.BlockSpec((1, 2, 2 * SCAN_PAIRS, HS, HS), lambda b, p: (b, 0, p, 0, 0))
    in_specs = [blk] * 9
    args = [r, v, kk, ld0, b0, kt0, ld1, b1, kt1]
    has_init = s_init is not None
    if has_init:
        in_specs.append(st_spec)
        args.append(s_init)
    out_specs = [blk]
    out_shape = [jax.ShapeDtypeStruct((n, D), BF16)]
    if want_final:
        out_specs.append(st_spec)
        out_shape.append(jax.ShapeDtypeStruct((n_seq, 2, NH, HS, HS), F32))
    res = pl.pallas_call(
        functools.partial(_rwkv_scan_kernel, seq_len=seq_len, has_init=has_init, want_final=want_final),
        grid=(n_seq, npair // SCAN_PAIRS),
        in_specs=in_specs,
        out_specs=out_specs,
        out_shape=out_shape,
        scratch_shapes=[pltpu.VMEM((2 * SCAN_PAIRS, LANES, LANES), F32), pltpu.VMEM((seq_len, width), F32),
                        pltpu.VMEM((seq_len, width), F32)],
        compiler_params=_cparams(("arbitrary", "arbitrary")),
        name="rwkv_scan",
    )(*args)
    return res


def _rwkv_out_kernel(x_ref, m_ref, g_ref, y_ref, bonus_ref, gate_ref, lng_ref, lnb_ref, wo_ref, o_ref,
                     acc_scr):
    j = pl.program_id(1)
    last = pl.num_programs(1) - 1

    def chunk():
        e = _head_indicator(RW_CW)
        y = y_ref[...].astype(F32)
        mean = _dot_hilo_rhs(y, e) * (1.0 / HS)
        yc = y - mean
        var = _dot(yc * yc, e) * (1.0 / HS)
        yn = yc * lax.rsqrt(var + GN_EPS) * lng_ref[...] + lnb_ref[...]
        yn = (yn + bonus_ref[...].astype(F32)) * gate_ref[...].astype(F32)
        return _dot(yn, wo_ref[0])

    @pl.when(j == 0)
    def _():
        acc_scr[...] = chunk()

    @pl.when((j > 0) & (j < last))
    def _():
        acc_scr[...] += chunk()

    @pl.when(j == last)
    def _():
        m = m_ref[0, 0]
        g = g_ref[0]
        o_ref[...] = x_ref[...] + m[2:3] * _rms(acc_scr[...] + chunk(), g[1:2])


def _rwkv_out(src, dst, mod, norm_g, y, bonus, gate, p, layer, j, group_of_tile):
    nj = D // RW_CW
    assert nj >= 2
    tile = pl.BlockSpec((TM, RW_CW), lambda i, c: (i, c))
    vec = pl.BlockSpec((1, RW_CW), lambda i, c: (j, c))
    return _activation_call(
        _rwkv_out_kernel,
        grid=(src.ntiles, nj),
        in_specs=[
            pl.BlockSpec((TM, D), lambda i, c: (i + src.first, 0)),
            _mod_spec(layer, group_of_tile, 2),
            _normg_spec(layer, 2),
            tile, tile, tile, vec, vec,
            pl.BlockSpec((1, RW_CW, D), lambda i, c: (j, c, 0)),
        ],
        args=[src.array, mod, norm_g, y, bonus, gate, p["ln_g"], p["ln_b"], p["w_o"]],
        dst=dst, block_rows=TM, out_index=lambda i, c: (i + dst.first, 0),
        scratch_shapes=[pltpu.VMEM((TM, D), F32)],
        name="rwkv_out")


def _rope_tables(n):
    rows = n // GRID_W
    row = np.repeat(np.arange(rows), GRID_W)
    col = np.tile(np.arange(GRID_W), rows)
    pos = np.stack([row, col], axis=-1).astype(np.float64)
    quarter = D_ROPE // 4
    inv = ROPE_BASE ** (-np.arange(quarter, dtype=np.float64) / quarter)
    ang = pos[:, :, None] * inv
    cos = np.cos(ang)
    sin = np.sin(ang)
    cos_t = np.concatenate([cos, cos], axis=-1).reshape(n, D_ROPE)
    sin_t = np.concatenate([-sin, sin], axis=-1).reshape(n, D_ROPE)
    return cos_t.astype(np.float32), sin_t.astype(np.float32)


def _rope_swap_perm():
    quarter = D_ROPE // 4
    base = np.arange(D_ROPE)
    return np.where((base % (2 * quarter)) < quarter, base + quarter, base - quarter)


def _mla_proj_kernel(*refs, positional):
    (x_ref, m_ref, g_ref, wdq_ref, gq_ref, wqn_ref, wqr_ref, wqs_ref, wdkv_ref, wkr_ref, wks_ref,
     gkv_ref) = refs[:12]
    pos = 12
    if positional:
        cosq_ref, sinq_ref, cosk_ref, sink_ref = refs[pos:pos + 4]
        pos += 4
    qn_o, qr_o, ckv_o, kr_o = refs[pos:pos + 4]
    m = m_ref[0, 0]
    g = g_ref[0]
    h = (_rms(x_ref[...], g[0:1]) * (1.0 + m[1:2]) + m[0:1]).astype(BF16)
    ql = jnp.dot(h, wdq_ref[0].astype(BF16), preferred_element_type=F32)
    ql = (ql * lax.rsqrt(jnp.mean(ql * ql, axis=-1, keepdims=True) + EPS) * gq_ref[...]).astype(BF16)
    qn_o[...] = jnp.dot(ql, wqn_ref[...].astype(BF16), preferred_element_type=F32).astype(qn_o.dtype)
    qr = jnp.dot(ql, wqr_ref[...].astype(BF16), preferred_element_type=F32)
    ckv = jnp.dot(h, wdkv_ref[...].astype(BF16), preferred_element_type=F32)
    ckv_o[...] = ckv * lax.rsqrt(jnp.mean(ckv * ckv, axis=-1, keepdims=True) + EPS) * gkv_ref[...]
    kr = jnp.dot(h, wkr_ref[...].astype(BF16), preferred_element_type=F32)
    if positional:
        qs = jnp.dot(ql, wqs_ref[...].astype(BF16), preferred_element_type=F32)
        ks = jnp.dot(h, wks_ref[...].astype(BF16), preferred_element_type=F32)
        qr = qr * cosq_ref[...] + qs * sinq_ref[...]
        kr = kr * cosk_ref[...] + ks * sink_ref[...]
    qr_o[...] = qr.astype(qr_o.dtype)
    kr_o[...] = kr


def _mla_proj(src, mod, norm_g, p, layer, j, positional, group_of_tile):
    n = src.ntiles * TM
    full = lambda shape: pl.BlockSpec(shape, lambda i: (0,) * len(shape))
    in_specs = [
        pl.BlockSpec((TM, D), lambda i: (i + src.first, 0)),
        _mod_spec(layer, group_of_tile, 1),
        _normg_spec(layer, 1),
        pl.BlockSpec((1, D, Q_RANK), lambda i: (j, 0, 0)),
        pl.BlockSpec((1, Q_RANK), lambda i: (j, 0)),
        full((Q_RANK, MLA_H * D_NOPE)), full((Q_RANK, MLA_H * D_ROPE)), full((Q_RANK, MLA_H * D_ROPE)),
        full((D, KV_RANK)), full((D, D_ROPE)), full((D, D_ROPE)),
        pl.BlockSpec((1, KV_RANK), lambda i: (j, 0)),
    ]
    args = [src.array, mod, norm_g, p["w_dq"], p["g_q"], p["w_uq_nope"], p["w_uq_rope"], p["w_uq_rope_sw"],
            p["w_dkv_c"], p["w_dkv_r"], p["w_dkv_r_sw"], p["g_kv"]]
    if positional:
        cos_t, sin_t = _rope_tables(TM)
        in_specs += [full((TM, MLA_H * D_ROPE)), full((TM, MLA_H * D_ROPE)),
                     full((TM, D_ROPE)), full((TM, D_ROPE))]
        args += [jnp.asarray(np.tile(cos_t, (1, MLA_H))), jnp.asarray(np.tile(sin_t, (1, MLA_H))),
                 jnp.asarray(cos_t), jnp.asarray(sin_t)]
    outs = ((MLA_H * D_NOPE, BF16), (MLA_H * D_ROPE, BF16), (KV_RANK, F32), (D_ROPE, F32))
    return pl.pallas_call(
        functools.partial(_mla_proj_kernel, positional=positional),
        grid=(src.ntiles,),
        in_specs=in_specs,
        out_specs=[pl.BlockSpec((TM, w), lambda i: (i, 0)) for w, _ in outs],
        out_shape=[jax.ShapeDtypeStruct((n, w), dt) for w, dt in outs],
        compiler_params=_cparams(("arbitrary",)),
        name="mla_proj",
    )(*args)


def _mla_attn_kernel(x_ref, m_ref, g_ref, qn_ref, qr_ref, ckv_ref, kr_ref, wuk_ref, wuv_ref, wo_ref,
                     o_ref, kn_scr, vv_scr, oh_scr, *, nb, tq, k_len):
    qi = pl.program_id(1)

    @pl.when(qi == 0)
    def _():
        ckv = ckv_ref[...].astype(BF16)
        kn_scr[...] = jnp.dot(ckv, wuk_ref[0].astype(BF16), preferred_element_type=F32).astype(BF16)
        vv_scr[...] = jnp.dot(ckv, wuv_ref[0].astype(BF16), preferred_element_type=F32).astype(BF16)

    units = [(b, hd) for b in range(nb) for hd in range(MLA_H)]

    def scores(unit):
        b, hd = unit
        qrows = slice(b * tq, (b + 1) * tq)
        krows = slice(b * k_len, (b + 1) * k_len)
        q = jnp.concatenate([qn_ref[qrows, hd * D_NOPE:(hd + 1) * D_NOPE].astype(BF16),
                             qr_ref[qrows, hd * D_ROPE:(hd + 1) * D_ROPE].astype(BF16)], axis=1)
        k = jnp.concatenate([kn_scr[krows, hd * D_NOPE:(hd + 1) * D_NOPE],
                             kr_ref[krows, :].astype(BF16)], axis=1)
        return _dot_nt(q, k) * MLA_SCALE

    pending = [scores(u) for u in units[:ATTN_LOOKAHEAD]]
    for idx, (b, hd) in enumerate(units):
        if idx + ATTN_LOOKAHEAD < len(units):
            pending.append(scores(units[idx + ATTN_LOOKAHEAD]))
        s = pending[idx]
        pexp = jnp.exp(s - jnp.max(s, axis=-1, keepdims=True))
        pv = jnp.dot(pexp.astype(BF16), vv_scr[b * k_len:(b + 1) * k_len, hd * D_V:(hd + 1) * D_V],
                     preferred_element_type=F32)
        oh_scr[b * tq:(b + 1) * tq, hd * D_V:(hd + 1) * D_V] = (
            pv / jnp.sum(pexp, axis=-1, keepdims=True)).astype(BF16)
    o = jnp.dot(oh_scr[...], wo_ref[0].astype(BF16), preferred_element_type=F32)
    m = m_ref[0, 0]
    g = g_ref[0]
    o_ref[...] = x_ref[...] + m[2:3] * _rms(o, g[1:2])


def _mla_attn(src, dst, mod, norm_g, qn, qr, ckv_all, kr_all, p, layer, j, n_seq, q_len, k_len, nb, tq,
              group_of_step):
    nq = q_len // tq
    assert nb == 1 or nq == 1
    rows = nb * tq
    x_first, o_first = src.first * TM // rows, dst.first * TM // rows
    return _activation_call(
        functools.partial(_mla_attn_kernel, nb=nb, tq=tq, k_len=k_len),
        grid=(n_seq // nb, nq),
        in_specs=[
            pl.BlockSpec((rows, D), lambda s, q: (s * nq + q + x_first, 0)),
            pl.BlockSpec((1, 1, N_MOD, D), lambda s, q: (0, group_of_step(s), 0, 0)),
            pl.BlockSpec((1, 4, D), lambda s, q: (layer, 0, 0)),
            pl.BlockSpec((nb * tq, MLA_H * D_NOPE), lambda s, q: (s * nq + q, 0)),
            pl.BlockSpec((nb * tq, MLA_H * D_ROPE), lambda s, q: (s * nq + q, 0)),
            pl.BlockSpec((nb * k_len, KV_RANK), lambda s, q: (s, 0)),
            pl.BlockSpec((nb * k_len, D_ROPE), lambda s, q: (s, 0)),
            pl.BlockSpec((1, KV_RANK, MLA_H * D_NOPE), lambda s, q: (j, 0, 0)),
            pl.BlockSpec((1, KV_RANK, MLA_H * D_V), lambda s, q: (j, 0, 0)),
            pl.BlockSpec((1, MLA_H * D_V, D), lambda s, q: (j, 0, 0)),
        ],
        args=[src.array, mod, norm_g, qn, qr, ckv_all, kr_all, p["w_uk"], p["w_uv"], p["w_o"]],
        dst=dst, block_rows=rows, out_index=lambda s, q: (s * nq + q + o_first, 0),
        scratch_shapes=[pltpu.VMEM((nb * k_len, MLA_H * D_NOPE), BF16),
                        pltpu.VMEM((nb * k_len, MLA_H * D_V), BF16),
                        pltpu.VMEM((nb * tq, MLA_H * D_V), BF16)],
        name="mla_attn")


def kernel(x_prompt, x_sample, state_rwkv, cache_mla_ckv, cache_mla_krope, c, c_ctx, mod_w, mod_b, norm_g,
           mlp_w1, mlp_w2, fft_w_out, conv_w_in, conv_w, conv_w_out, rwkv_mu, rwkv_w_r, rwkv_w_k, rwkv_w_v,
           rwkv_w_o, rwkv_w0, rwkv_w_l1, rwkv_w_l2, rwkv_a0, rwkv_a_l1, rwkv_a_l2, rwkv_g_l1, rwkv_g_l2,
           rwkv_k_k, rwkv_k_a, rwkv_r_k, rwkv_ln_g, rwkv_ln_b, mla_w_dq, mla_g_q, mla_w_uq, mla_w_dkv,
           mla_g_kv, mla_w_uk, mla_w_uv, mla_w_o):
    batch, seq, _ = x_prompt.shape
    dec_batch, dec_seq, _ = x_sample.shape
    past_len = cache_mla_ckv.shape[2]
    assert (batch * seq) % TM == 0 and TM % seq == 0 and dec_seq == TM and seq % CHUNK == 0

    np_tiles = batch * seq // TM
    ns_tiles = dec_batch * dec_seq // TM
    total_rows = (np_tiles + ns_tiles) * TM
    cs = jnp.concatenate([c_ctx[None, :], c, jnp.zeros((MOD_ROWS - 1 - dec_batch, D), F32)], axis=0)
    mod_b3 = mod_b.reshape(DEPTH, 1, N_MOD * D)
    mod = _modulation(cs, mod_w, mod_b3, 0)

    grp_p = lambda i: 0
    grp_s = lambda i: 1 + i
    grp_all = lambda i: jnp.maximum(i - (np_tiles - 1), 0)
    new_rwkv, new_ckv, new_krope = [], [], []
    streams = [(True, seq, batch, grp_p), (False, dec_seq, dec_batch, grp_s)]
    cur = [_Rows(x_prompt.reshape(batch * seq, D), 0, np_tiles),
           _Rows(x_sample.reshape(dec_batch * dec_seq, D), 0, ns_tiles)]

    for i in range(DEPTH):
        kind, j = i % 4, i // 4
        for idx, (is_prompt, slen, nseq, grp) in enumerate(streams):
            src = cur[idx]
            sharing = src.array.shape[0] == total_rows
            dst = _Dest(total_rows, src.first, True) if sharing else _Dest(src.ntiles * TM, 0, False)
            shared = None
            if kind == 0:
                shared = _fnet(src, dst, mod, norm_g, fft_w_out, i, j, slen, grp)
            elif kind == 1:
                shared = _conv(src, dst, mod, norm_g, conv_w_in, conv_w, conv_w_out, i, j, slen, grp)
            elif kind == 2:
                p = dict(mu=rwkv_mu, w_r=rwkv_w_r, w_k=rwkv_w_k, w_v=rwkv_w_v, w_o=rwkv_w_o, w0=rwkv_w0,
                         w_l1=rwkv_w_l1, w_l2=rwkv_w_l2, a0=rwkv_a0, a_l1=rwkv_a_l1, a_l2=rwkv_a_l2,
                         g_l1=rwkv_g_l1, g_l2=rwkv_g_l2, k_k=rwkv_k_k, k_a=rwkv_k_a,
                         r_k=rwkv_r_k.reshape(-1, D), ln_g=rwkv_ln_g, ln_b=rwkv_ln_b)
                proj = _rwkv_proj(src, mod, norm_g, p, i, j, slen, grp)
                if is_prompt:
                    y, s_fin = _rwkv_scan(proj, None, nseq, slen, True)
                    new_rwkv.append(s_fin)
                else:
                    (y,) = _rwkv_scan(proj, state_rwkv[:, j], nseq, slen, False)
                shared = _rwkv_out(src, dst, mod, norm_g, y, proj[10], proj[9], p, i, j, grp)
            else:
                perm = _rope_swap_perm()
                w_uq = mla_w_uq[j].reshape(Q_RANK, MLA_H, D_NOPE + D_ROPE)
                w_uq_rope = w_uq[:, :, D_NOPE:]
                w_dkv_r = mla_w_dkv[j][:, KV_RANK:]
                p = dict(w_dq=mla_w_dq, g_q=mla_g_q, g_kv=mla_g_kv, w_uk=mla_w_uk, w_uv=mla_w_uv, w_o=mla_w_o,
                         w_uq_nope=w_uq[:, :, :D_NOPE].reshape(Q_RANK, MLA_H * D_NOPE),
                         w_uq_rope=w_uq_rope.reshape(Q_RANK, MLA_H * D_ROPE),
                         w_uq_rope_sw=w_uq_rope[:, :, perm].reshape(Q_RANK, MLA_H * D_ROPE),
                         w_dkv_c=mla_w_dkv[j][:, :KV_RANK], w_dkv_r=w_dkv_r, w_dkv_r_sw=w_dkv_r[:, perm])
                qn, qr, ckv, kr = _mla_proj(src, mod, norm_g, p, i, j, not is_prompt, grp)
                if is_prompt:
                    new_ckv.append(ckv.reshape(batch, seq, KV_RANK))
                    new_krope.append(kr.reshape(batch, seq, D_ROPE))
                    shared = _mla_attn(src, dst, mod, norm_g, qn, qr, ckv, kr, p, i, j, nseq, slen, slen,
                                       TM // slen, slen, lambda s: 0)
                else:
                    klen = past_len + slen
                    ckv_all = jnp.concatenate([cache_mla_ckv[:, j], ckv.reshape(nseq, slen, KV_RANK)], axis=1)
                    kr_all = jnp.concatenate([cache_mla_krope[:, j], kr.reshape(nseq, slen, D_ROPE)], axis=1)
                    shared = _mla_attn(src, dst, mod, norm_g, qn, qr, ckv_all.reshape(nseq * klen, KV_RANK),
                                       kr_all.reshape(nseq * klen, D_ROPE), p, i, j, nseq, slen, klen, 1, 256,
                                       lambda s: 1 + s)
            cur[idx] = _Rows(shared, dst.first, src.ntiles)
            if sharing:
                cur[1 - idx] = cur[1 - idx]._replace(array=shared)
        if i < DEPTH - 1:
            srcs = [_Rows(cur[0].array, 0, np_tiles + ns_tiles)] if cur[0].array is cur[1].array else cur
            both, mod = _mlp(srcs, _Dest(total_rows, 0, False), mod, norm_g, mlp_w1, mlp_w2, i, grp_all,
                             next_mod=(cs, mod_w, mod_b3, i + 1))
            cur = [_Rows(both, 0, np_tiles), _Rows(both, np_tiles, ns_tiles)]
        else:
            y_prompt, y_sample = [
                _mlp([cur[idx]], _Dest(cur[idx].ntiles * TM, 0, False), mod, norm_g, mlp_w1, mlp_w2, i,
                     streams[idx][3]) for idx in range(2)]

    return (y_prompt.reshape(batch, seq, D), y_sample.reshape(dec_batch, dec_seq, D),
            jnp.stack(new_rwkv, axis=1), jnp.stack(new_ckv, axis=1), jnp.stack(new_krope, axis=1))
```

```python
import functools
from typing import NamedTuple

import numpy as np
import jax
import jax.numpy as jnp
from jax import lax
from jax.experimental import pallas as pl
from jax.experimental.pallas import tpu as pltpu

D = 1024
DEPTH = 4
N_MOD = 6
D_FF = 4 * D
EPS = 1e-6
GRID_W = 64
FFT_GROUPS = 8
FFT_GW = D // FFT_GROUPS
HS = 64
NH = D // HS
LORA_W = 64
LORA_A = 64
LORA_G = 128
GN_EPS = 64e-5
DECAY_SCALE = float(np.exp(-0.5))
MLA_H = 8
D_NOPE = 128
D_ROPE = 64
D_V = 128
KV_RANK = 256
Q_RANK = 384
ROPE_BASE = 10000.0
MLA_SCALE = (D_NOPE + D_ROPE) ** -0.5

F32 = jnp.float32
BF16 = jnp.bfloat16

TM = 1024
LANES = 128
CHUNK = 64
SCAN_UNROLL = 4
SCAN_PAIRS = 2
ATTN_LOOKAHEAD = 2
VMEM_LIMIT = 58 * 1024 * 1024


def _cparams(sem):
    return pltpu.CompilerParams(dimension_semantics=sem, vmem_limit_bytes=VMEM_LIMIT)


def _dot(a, b):
    return jnp.dot(a.astype(BF16), b.astype(BF16), preferred_element_type=F32)


def _dot_nt(a, b):
    return lax.dot_general(a.astype(BF16), b.astype(BF16), (((1,), (1,)), ((), ())),
                           preferred_element_type=F32)


def _dot_tn(a, b):
    return lax.dot_general(a.astype(BF16), b.astype(BF16), (((0,), (0,)), ((), ())),
                           preferred_element_type=F32)


def _block_diag2(a, b):
    za = jnp.zeros((a.shape[0], b.shape[1]), a.dtype)
    zb = jnp.zeros((b.shape[0], a.shape[1]), a.dtype)
    return jnp.concatenate([jnp.concatenate([a, za], axis=1), jnp.concatenate([zb, b], axis=1)], axis=0)


def _cumsum_rows(x, reverse):
    n = x.shape[0]
    idx = lax.broadcasted_iota(jnp.int32, x.shape, 0)
    s = 1
    while s < n:
        if reverse:
            x = x + jnp.where(idx < n - s, pltpu.roll(x, n - s, 0), 0.0)
        else:
            x = x + jnp.where(idx >= s, pltpu.roll(x, s, 0), 0.0)
        s *= 2
    return x


def _rms(x, g):
    return x * lax.rsqrt(jnp.mean(x * x, axis=-1, keepdims=True) + EPS) * g


def _sigmoid(x):
    return 1.0 / (1.0 + jnp.exp(-x))


def _head_indicator(n):
    r = lax.broadcasted_iota(jnp.int32, (n, n), 0) // HS
    c = lax.broadcasted_iota(jnp.int32, (n, n), 1) // HS
    return jnp.where(r == c, 1.0, 0.0).astype(BF16)


def _shift_rows(z, seq_len):
    n, w = z.shape
    pos = lax.broadcasted_iota(jnp.int32, (n, LANES), 0) % seq_len
    keep_prev = jnp.concatenate([jnp.where(pos == 0, 0.0, 1.0)] * (w // LANES), axis=1)
    keep_next = jnp.concatenate([jnp.where(pos == seq_len - 1, 0.0, 1.0)] * (w // LANES), axis=1)
    return pltpu.roll(z, 1, 0) * keep_prev, pltpu.roll(z, n - 1, 0) * keep_next


MOD_TN = 1536
MOD_ROWS = 8


def _mod_block(cs_ref, w_ref, b_ref):
    cs = cs_ref[...]
    return _dot(cs * _sigmoid(cs), w_ref[0]) + b_ref[0]


def _mod_kernel(cs_ref, w_ref, b_ref, o_ref):
    o_ref[...] = _mod_block(cs_ref, w_ref, b_ref)


def _modulation(cs, mod_w, mod_b3, layer):
    nj = (N_MOD * D) // MOD_TN
    out = pl.pallas_call(
        _mod_kernel,
        grid=(nj,),
        in_specs=[
            pl.BlockSpec((MOD_ROWS, D), lambda j: (0, 0)),
            pl.BlockSpec((1, D, MOD_TN), lambda j: (layer, 0, j)),
            pl.BlockSpec((1, 1, MOD_TN), lambda j: (layer, 0, j)),
        ],
        out_specs=pl.BlockSpec((MOD_ROWS, MOD_TN), lambda j: (0, j)),
        out_shape=jax.ShapeDtypeStruct((MOD_ROWS, N_MOD * D), F32),
        compiler_params=_cparams(("arbitrary",)),
        name="modulation",
    )(cs, mod_w, mod_b3)
    return out.reshape(1, MOD_ROWS, N_MOD, D)


class _Rows(NamedTuple):
    array: jax.Array
    first: int
    ntiles: int


class _Dest(NamedTuple):
    rows: int
    first: int
    inplace: bool


def _activation_call(kernel, *, grid, in_specs, args, dst, block_rows, out_index, scratch_shapes, name,
                     extra_out=None):
    assert not dst.inplace or args[0].shape[0] == dst.rows
    out_specs = pl.BlockSpec((block_rows, D), out_index)
    out_shape = jax.ShapeDtypeStruct((dst.rows, D), F32)
    if extra_out is not None:
        out_specs, out_shape = [out_specs, extra_out[0]], [out_shape, extra_out[1]]
    return pl.pallas_call(
        kernel, grid=grid, in_specs=in_specs, out_specs=out_specs, out_shape=out_shape,
        scratch_shapes=scratch_shapes, input_output_aliases={0: 0} if dst.inplace else {},
        compiler_params=_cparams(("arbitrary",) * len(grid)), name=name)(*args)


def _mod_spec(layer, group_of_tile, ngrid):
    del layer
    if ngrid == 1:
        return pl.BlockSpec((1, 1, N_MOD, D), lambda i: (0, group_of_tile(i), 0, 0))
    return pl.BlockSpec((1, 1, N_MOD, D), lambda i, j: (0, group_of_tile(i), 0, 0))


def _normg_spec(layer, ngrid):
    if ngrid == 1:
        return pl.BlockSpec((1, 4, D), lambda i: (layer, 0, 0))
    return pl.BlockSpec((1, 4, D), lambda i, j: (layer, 0, 0))


MLP_FC = 1024


def _mlp_kernel(*refs, n_first, with_next_mod):
    if n_first is None:
        (x_ref,), rest = refs[:1], refs[1:]
        read_x = lambda: x_ref[...]
    else:
        (xa_ref, xb_ref), rest = refs[:2], refs[2:]
        read_x = lambda: jnp.where(pl.program_id(0) < n_first, xa_ref[...], xb_ref[...])
    if with_next_mod:
        m_ref, g_ref, w1_ref, w2_ref, cs_ref, mw_ref, mb_ref, o_ref, mo_ref, h_scr, acc_scr = rest
        mo_ref[...] = _mod_block(cs_ref, mw_ref, mb_ref)
    else:
        m_ref, g_ref, w1_ref, w2_ref, o_ref, h_scr, acc_scr = rest
    j = pl.program_id(1)
    m = m_ref[0, 0]
    g = g_ref[0]

    last = pl.num_programs(1) - 1

    def chunk():
        a = jnp.dot(h_scr[...], w1_ref[0].astype(BF16), preferred_element_type=F32)
        a = jnp.maximum(a, 0.0)
        return _dot(a * a, w2_ref[0])

    @pl.when(j == 0)
    def _():
        h = _rms(read_x(), g[2:3]) * (1.0 + m[4:5]) + m[3:4]
        h_scr[...] = h.astype(BF16)
        acc_scr[...] = chunk()

    @pl.when((j > 0) & (j < last))
    def _():
        acc_scr[...] += chunk()

    @pl.when(j == last)
    def _():
        f = acc_scr[...] + chunk()
        o_ref[...] = read_x() + m[5:6] * _rms(f, g[3:4])


def _mlp(srcs, dst, mod, norm_g, w1, w2, layer, group_of_tile, next_mod=None):
    nj = D_FF // MLP_FC
    assert nj >= 2
    if len(srcs) == 1:
        (src,) = srcs
        ntiles, n_first = src.ntiles, None
        x_specs = [pl.BlockSpec((TM, D), lambda i, j: (i + src.first, 0))]
    else:
        sa, sb = srcs
        ntiles, n_first = sa.ntiles + sb.ntiles, sa.ntiles
        x_specs = [pl.BlockSpec((TM, D), lambda i, j: (jnp.minimum(i, sa.ntiles - 1) + sa.first, 0)),
                   pl.BlockSpec((TM, D), lambda i, j: (jnp.maximum(i - sa.ntiles, 0) + sb.first, 0))]
    in_specs = x_specs + [
        _mod_spec(layer, group_of_tile, 2),
        _normg_spec(layer, 2),
        pl.BlockSpec((1, D, MLP_FC), lambda i, j: (layer, 0, j)),
        pl.BlockSpec((1, MLP_FC, D), lambda i, j: (layer, j, 0)),
    ]
    args = [s.array for s in srcs] + [mod, norm_g, w1, w2]
    extra_out = None
    if next_mod is not None:
        cs, mod_w, mod_b3, nxt = next_mod
        cols = (N_MOD * D) // (ntiles * nj)
        assert cols * ntiles * nj == N_MOD * D and cols % LANES == 0
        in_specs += [pl.BlockSpec((MOD_ROWS, D), lambda i, j: (0, 0)),
                     pl.BlockSpec((1, D, cols), lambda i, j: (nxt, 0, i * nj + j)),
                     pl.BlockSpec((1, 1, cols), lambda i, j: (nxt, 0, i * nj + j))]
        args += [cs, mod_w, mod_b3]
        extra_out = (pl.BlockSpec((MOD_ROWS, cols), lambda i, j: (0, i * nj + j)),
                     jax.ShapeDtypeStruct((MOD_ROWS, N_MOD * D), F32))
    res = _activation_call(
        functools.partial(_mlp_kernel, n_first=n_first, with_next_mod=next_mod is not None),
        grid=(ntiles, nj), in_specs=in_specs, args=args,
        dst=dst, block_rows=TM, out_index=lambda i, j: (i + dst.first, 0),
        scratch_shapes=[pltpu.VMEM((TM, D), BF16), pltpu.VMEM((TM, D), F32)],
        name="mlp", extra_out=extra_out)
    if next_mod is None:
        return res
    return res[0], res[1].reshape(1, MOD_ROWS, N_MOD, D)


def _dft_mats(n):
    idx = np.arange(n, dtype=np.int64)
    ang = (2.0 * np.pi / n) * ((idx[:, None] * idx[None, :]) % n).astype(np.float64)
    scale = 1.0 / np.sqrt(n)
    return np.cos(ang) * scale, np.sin(ang) * scale


def _fnet_kernel(x_ref, m_ref, g_ref, cs_ref, cn_ref, sn_ref, w_ref, o_ref, p_scr, q_scr, f_scr,
                 *, seq_len):
    m = m_ref[0, 0]
    g = g_ref[0]
    x = x_ref[...]
    h = (_rms(x, g[0:1]) * (1.0 + m[1:2]) + m[0:1]).astype(BF16)
    cs = cs_ref[...].astype(BF16)
    for gi in range(FFT_GROUPS):
        pq = jnp.dot(h[:, gi * FFT_GW:(gi + 1) * FFT_GW], cs, preferred_element_type=F32)
        p_scr[:, gi * FFT_GW:(gi + 1) * FFT_GW] = pq[:, :FFT_GW].astype(BF16)
        q_scr[:, gi * FFT_GW:(gi + 1) * FFT_GW] = pq[:, FFT_GW:].astype(BF16)
    cn = cn_ref[...].astype(BF16)
    sn = sn_ref[...].astype(BF16)
    for s in range(TM // seq_len):
        rows = slice(s * seq_len, (s + 1) * seq_len)
        f = (jnp.dot(cn, p_scr[rows, :], preferred_element_type=F32)
             - jnp.dot(sn, q_scr[rows, :], preferred_element_type=F32))
        f_scr[rows, :] = f.astype(BF16)
    o = jnp.dot(f_scr[...], w_ref[0].astype(BF16), preferred_element_type=F32)
    o_ref[...] = x + m[2:3] * _rms(o, g[1:2])


def _fnet(src, dst, mod, norm_g, w_out, layer, j, seq_len, group_of_tile):
    cg, sg = _dft_mats(FFT_GW)
    cs = jnp.asarray(np.concatenate([cg, sg], axis=1), F32)
    cn_np, sn_np = _dft_mats(seq_len)
    cn = jnp.asarray(cn_np, F32)
    sn = jnp.asarray(sn_np, F32)
    return _activation_call(
        functools.partial(_fnet_kernel, seq_len=seq_len),
        grid=(src.ntiles,),
        in_specs=[
            pl.BlockSpec((TM, D), lambda i: (i + src.first, 0)),
            _mod_spec(layer, group_of_tile, 1),
            _normg_spec(layer, 1),
            pl.BlockSpec((FFT_GW, 2 * FFT_GW), lambda i: (0, 0)),
            pl.BlockSpec((seq_len, seq_len), lambda i: (0, 0)),
            pl.BlockSpec((seq_len, seq_len), lambda i: (0, 0)),
            pl.BlockSpec((1, D, D), lambda i: (j, 0, 0)),
        ],
        args=[src.array, mod, norm_g, cs, cn, sn, w_out],
        dst=dst, block_rows=TM, out_index=lambda i: (i + dst.first, 0),
        scratch_shapes=[pltpu.VMEM((TM, D), BF16), pltpu.VMEM((TM, D), BF16),
                        pltpu.VMEM((TM, D), BF16)],
        name="fourier_mix")


CONV_CW = 512


def _conv_kernel(x_ref, m_ref, g_ref, wb_ref, wc_ref, wu_ref, cw_ref, wo_ref, o_ref, h_scr, acc_scr,
                 *, seq_len):
    j = pl.program_id(1)
    m = m_ref[0, 0]
    g = g_ref[0]

    def chunk():
        h = h_scr[...]
        hw = CONV_CW // 2
        proj = []
        for s in range(2):
            cols = slice(s * hw, (s + 1) * hw)
            proj.append([jnp.dot(h, w_ref[0, :, cols].astype(BF16), preferred_element_type=F32)
                         for w_ref in (wb_ref, wc_ref, wu_ref)])
        out = None
        for s in range(2):
            cols = slice(s * hw, (s + 1) * hw)
            bg, cg, u = proj[s]
            z = cg * u
            z_prev, z_next = _shift_rows(z, seq_len)
            conv = z_prev * cw_ref[0, 0:1, cols] + z * cw_ref[0, 1:2, cols] + z_next * cw_ref[0, 2:3, cols]
            part = _dot(bg * conv, wo_ref[0, cols, :])
            out = part if out is None else out + part
        return out

    last = pl.num_programs(1) - 1

    @pl.when(j == 0)
    def _():
        h = _rms(x_ref[...], g[0:1]) * (1.0 + m[1:2]) + m[0:1]
        h_scr[...] = h.astype(BF16)
        acc_scr[...] = chunk()

    @pl.when((j > 0) & (j < last))
    def _():
        acc_scr[...] += chunk()

    @pl.when(j == last)
    def _():
        o_ref[...] = x_ref[...] + m[2:3] * _rms(acc_scr[...] + chunk(), g[1:2])


def _conv(src, dst, mod, norm_g, w_in, w_conv, w_out, layer, j, seq_len, group_of_tile):
    nj = D // CONV_CW
    assert nj >= 2
    return _activation_call(
        functools.partial(_conv_kernel, seq_len=seq_len),
        grid=(src.ntiles, nj),
        in_specs=[
            pl.BlockSpec((TM, D), lambda i, c: (i + src.first, 0)),
            _mod_spec(layer, group_of_tile, 2),
            _normg_spec(layer, 2),
            pl.BlockSpec((1, D, CONV_CW), lambda i, c: (j, 0, c)),
            pl.BlockSpec((1, D, CONV_CW), lambda i, c: (j, 0, nj + c)),
            pl.BlockSpec((1, D, CONV_CW), lambda i, c: (j, 0, 2 * nj + c)),
            pl.BlockSpec((1, 3, CONV_CW), lambda i, c: (j, 0, c)),
            pl.BlockSpec((1, CONV_CW, D), lambda i, c: (j, c, 0)),
        ],
        args=[src.array, mod, norm_g, w_in, w_in, w_in, w_conv, w_out],
        dst=dst, block_rows=TM, out_index=lambda i, c: (i + dst.first, 0),
        scratch_shapes=[pltpu.VMEM((TM, D), BF16), pltpu.VMEM((TM, D), F32)],
        name="short_conv")


RWP_CW = 256
RW_OUT_DTYPES = (BF16, BF16, BF16, F32, BF16, BF16, F32, BF16, BF16, BF16, BF16)
RW_CW = 512


def _rwkv_proj_kernel(x_ref, m_ref, g_ref, mu_ref, wr_ref, wk_ref, wv_ref, w0_ref, wl1_ref, wl2_ref,
                      a0_ref, al1_ref, al2_ref, gl1_ref, gl2_ref, kk_ref, ka_ref, rk_ref,
                      r_o, v_o, kk_o, ld0_o, b0_o, kt0_o, ld1_o, b1_o, kt1_o, g_o, bonus_o,
                      xr_scr, xk_scr, xv_scr, tw_scr, ta_scr, sg_scr, *, seq_len):
    j = pl.program_id(1)

    def chunk():
        e = _head_indicator(RWP_CW)
        r = jnp.dot(xr_scr[...], wr_ref[0].astype(BF16), preferred_element_type=F32)
        k = jnp.dot(xk_scr[...], wk_ref[0].astype(BF16), preferred_element_type=F32)
        v = jnp.dot(xv_scr[...], wv_ref[0].astype(BF16), preferred_element_type=F32)
        g_o[...] = jnp.dot(sg_scr[...], gl2_ref[0].astype(BF16), preferred_element_type=F32).astype(g_o.dtype)
        zws = [jnp.dot(tw_scr[d], wl2_ref[0, d].astype(BF16), preferred_element_type=F32) for d in range(2)]
        zas = [jnp.dot(ta_scr[d], al2_ref[0, d].astype(BF16), preferred_element_type=F32) for d in range(2)]
        kk = k * kk_ref[...]
        kk = kk * lax.rsqrt(_dot(kk * kk, e) + 1e-12)
        r_o[...] = r.astype(r_o.dtype)
        v_o[...] = v.astype(v_o.dtype)
        kk_o[...] = kk.astype(kk_o.dtype)
        k_scaled = k * ka_ref[...]
        kt_sum = None
        for d, (ld_o, b_o, kt_o) in enumerate(((ld0_o, b0_o, kt0_o), (ld1_o, b1_o, kt1_o))):
            ld_o[...] = -DECAY_SCALE * _sigmoid(w0_ref[0, d:d + 1, :] + zws[d])
            a = _sigmoid(a0_ref[0, d:d + 1, :] + zas[d])
            kt = k + k_scaled * (a - 1.0)
            b_o[...] = (kk * a).astype(b_o.dtype)
            kt_o[...] = kt.astype(kt_o.dtype)
            kt_sum = kt if kt_sum is None else kt_sum + kt
        bonus_o[...] = (_dot(r * rk_ref[...] * kt_sum, e) * v).astype(bonus_o.dtype)

    @pl.when(j == 0)
    def _():
        m = m_ref[0, 0]
        g = g_ref[0]
        mu = mu_ref[0]
        h = _rms(x_ref[...], g[0:1]) * (1.0 + m[1:2]) + m[0:1]
        h_prev, h_next = _shift_rows(h, seq_len)
        dx = 0.5 * (h_prev + h_next) - h
        xr_scr[...] = (h + dx * mu[0:1]).astype(BF16)
        xk_scr[...] = (h + dx * mu[2:3]).astype(BF16)
        xv_scr[...] = (h + dx * mu[3:4]).astype(BF16)
        xw = (h + dx * mu[1:2]).astype(BF16)
        xa = (h + dx * mu[4:5]).astype(BF16)
        xg = (h + dx * mu[5:6]).astype(BF16)
        for d in range(2):
            tw_scr[d] = jnp.tanh(_dot(xw, wl1_ref[0, d])).astype(BF16)
            ta_scr[d] = _dot(xa, al1_ref[0, d]).astype(BF16)
        sg_scr[...] = _sigmoid(_dot(xg, gl1_ref[0])).astype(BF16)
        chunk()

    @pl.when(j > 0)
    def _():
        chunk()


def _rwkv_proj(src, mod, norm_g, p, layer, j, seq_len, group_of_tile):
    n = src.ntiles * TM
    nj = D // RWP_CW
    tile = pl.BlockSpec((TM, RWP_CW), lambda i, c: (i, c))
    wcol = lambda: pl.BlockSpec((1, D, RWP_CW), lambda i, c: (j, 0, c))
    vec = lambda: pl.BlockSpec((1, RWP_CW), lambda i, c: (j, c))
    return pl.pallas_call(
        functools.partial(_rwkv_proj_kernel, seq_len=seq_len),
        grid=(src.ntiles, nj),
        in_specs=[
            pl.BlockSpec((TM, D), lambda i, c: (i + src.first, 0)),
            _mod_spec(layer, group_of_tile, 2),
            _normg_spec(layer, 2),
            pl.BlockSpec((1, 6, D), lambda i, c: (j, 0, 0)),
            wcol(), wcol(), wcol(),
            pl.BlockSpec((1, 2, RWP_CW), lambda i, c: (j, 0, c)),
            pl.BlockSpec((1, 2, D, LORA_W), lambda i, c: (j, 0, 0, 0)),
            pl.BlockSpec((1, 2, LORA_W, RWP_CW), lambda i, c: (j, 0, 0, c)),
            pl.BlockSpec((1, 2, RWP_CW), lambda i, c: (j, 0, c)),
            pl.BlockSpec((1, 2, D, LORA_A), lambda i, c: (j, 0, 0, 0)),
            pl.BlockSpec((1, 2, LORA_A, RWP_CW), lambda i, c: (j, 0, 0, c)),
            pl.BlockSpec((1, D, LORA_G), lambda i, c: (j, 0, 0)),
            pl.BlockSpec((1, LORA_G, RWP_CW), lambda i, c: (j, 0, c)),
            vec(), vec(), vec(),
        ],
        out_specs=[tile] * len(RW_OUT_DTYPES),
        out_shape=[jax.ShapeDtypeStruct((n, D), dt) for dt in RW_OUT_DTYPES],
        scratch_shapes=[pltpu.VMEM((TM, D), BF16), pltpu.VMEM((TM, D), BF16), pltpu.VMEM((TM, D), BF16),
                        pltpu.VMEM((2, TM, LORA_W), BF16), pltpu.VMEM((2, TM, LORA_A), BF16),
                        pltpu.VMEM((TM, LORA_G), BF16)],
        compiler_params=_cparams(("arbitrary", "arbitrary")),
        name="rwkv_proj",
    )(src.array, mod, norm_g, p["mu"], p["w_r"], p["w_k"], p["w_v"], p["w0"], p["w_l1"], p["w_l2"],
      p["a0"], p["a_l1"], p["a_l2"], p["g_l1"], p["g_l2"], p["k_k"], p["k_a"], p["r_k"])


def _scan_precompute(units):
    c = CHUNK
    c2 = 2 * c
    row = lax.broadcasted_iota(jnp.int32, (c2, c2), 0)
    col = lax.broadcasted_iota(jnp.int32, (c2, c2), 1)
    head_a = lax.broadcasted_iota(jnp.int32, (c, LANES), 1) < HS
    own_lanes = jnp.concatenate([head_a, jnp.logical_not(head_a)], axis=0)
    eye = jnp.where(row == col, 1.0, 0.0)

    def two_heads(t):
        return jnp.concatenate([jnp.where(head_a, t, 0.0), jnp.where(head_a, 0.0, t)], axis=0)

    def causal(reverse):
        if reverse:
            return (col % c) > (row % c), (col % c) >= (row % c)
        return (col % c) < (row % c), (col % c) <= (row % c)

    masks = {rev: causal(rev) for rev in sorted({u[6] for u in units})}

    cums = [_cumsum_rows(u[0], u[6]) for u in units]
    st = []
    for (ld, kk, beta, kt, r, v, rev), cum in zip(units, cums):
        tot = cum[0:1] if rev else cum[c - 1:c]
        ginv = jnp.exp(-cum)
        tail = jnp.exp(tot - cum)
        st.append(dict(
            rev=rev, etot=jnp.exp(tot),
            a_t=two_heads(-kk * jnp.exp(cum - ld)), r_t=two_heads(r * jnp.exp(cum)),
            bk=jnp.concatenate([two_heads(beta * ginv), two_heads(kt * ginv)], axis=0),
            bkg=jnp.concatenate([two_heads(beta * tail), two_heads(kt * tail)], axis=0),
            v2=jnp.concatenate([v, v], axis=0), vh=two_heads(v)))
    grams = [_dot_nt(jnp.concatenate([s["a_t"], s["r_t"]], axis=0), s["bk"]) for s in st]
    for s, gram in zip(st, grams):
        strict, incl = masks[s["rev"]]
        s["l_ab"] = jnp.where(strict, gram[:c2, :c2], 0.0)
        s["l_ak"] = jnp.where(strict, gram[:c2, c2:], 0.0)
        s["t_rb"] = jnp.where(incl, gram[c2:, :c2], 0.0)
        s["t_rk"] = jnp.where(incl, gram[c2:, c2:], 0.0)
    lvs = [_dot(s["l_ak"], s["v2"]) for s in st]
    minvs = [eye for _ in st]
    b = 1
    while b < c:
        same = (row // (2 * b)) == (col // (2 * b))
        es = []
        for s in st:
            first, second = (col % (2 * b)) < b, (row % (2 * b)) >= b
            if s["rev"]:
                first, second = (row % (2 * b)) < b, (col % (2 * b)) >= b
            es.append(jnp.where(same & first & second, s["l_ab"], 0.0))
        if b == 1:
            minvs = [m + e for m, e in zip(minvs, es)]
        else:
            half = [_dot(m, e) for m, e in zip(minvs, es)]
            minvs = [m + _dot(h, m) for m, h in zip(minvs, half)]
        b *= 2
    mms = [_dot(m, jnp.concatenate([s["a_t"], lv], axis=1)) for m, s, lv in zip(minvs, st, lvs)]
    zero = jnp.zeros((c2, LANES), BF16)
    tts = [_dot(jnp.concatenate([s["t_rb"], s["t_rk"]], axis=1),
                jnp.concatenate([mm.astype(BF16), jnp.concatenate([zero, s["v2"].astype(BF16)], axis=1)], axis=0))
           for s, mm in zip(st, mms)]
    ps = [_dot_tn(mm[:, :LANES], s["bkg"][:c2]) for s, mm in zip(st, mms)]
    qs = [_dot_tn(jnp.concatenate([jnp.where(own_lanes, mm[:, LANES:], 0.0), s["vh"]], axis=0), s["bkg"])
          for s, mm in zip(st, mms)]
    out = []
    for s, p, q, tt in zip(st, ps, qs, tts):
        reff = s["r_t"] + tt[:, :LANES]
        out.append(dict(etot=s["etot"], p=p, q=q, reff=reff[:c] + reff[c:],
                        y0=jnp.where(head_a, tt[:c, LANES:], tt[c:, LANES:])))
    return out


def _rwkv_scan_kernel(*refs, seq_len, has_init, want_final):
    r_ref, v_ref, kk_ref, ld0_ref, b0_ref, kt0_ref, ld1_ref, b1_ref, kt1_ref = refs[:9]
    pos = 9
    if has_init:
        s0_ref = refs[pos]
        pos += 1
    y_ref = refs[pos]
    pos += 1
    if want_final:
        sf_ref = refs[pos]
        pos += 1
    g_scr, yf_scr, yb_scr = refs[pos:pos + 3]
    nc = seq_len // CHUNK
    ngroups = nc // SCAN_UNROLL
    per_dir = ((ld0_ref, b0_ref, kt0_ref), (ld1_ref, b1_ref, kt1_ref))
    y_dst = (yf_scr, yb_scr)
    nchain = 2 * SCAN_PAIRS

    def group(gi, states):
        where, units = [], []
        for u in range(SCAN_UNROLL):
            for pp in range(SCAN_PAIRS):
                lanes = slice(pp * LANES, (pp + 1) * LANES)
                for d in range(2):
                    ld_ref, b_ref, kt_ref = per_dir[d]
                    cidx = gi * SCAN_UNROLL + u
                    if d == 1:
                        cidx = nc - 1 - cidx
                    start = cidx * CHUNK
                    rw = slice(start, start + CHUNK) if isinstance(start, int) else pl.ds(
                        pl.multiple_of(start, CHUNK), CHUNK)
                    where.append((rw, lanes))
                    units.append(tuple(ref[rw, lanes].astype(F32) for ref in
                                       (ld_ref, kk_ref, b_ref, kt_ref, r_ref, v_ref)) + (d == 1,))
        pre = _scan_precompute(units)
        states = list(states)
        for u in range(SCAN_UNROLL):
            cur = pre[u * nchain:(u + 1) * nchain]
            ys = [_dot_nt(cu["reff"], g) for cu, g in zip(cur, states)]
            gp = [_dot(g, cu["p"]) for cu, g in zip(cur, states)]
            states = [states[ch] * cur[ch]["etot"] + gp[ch] + cur[ch]["q"] for ch in range(nchain)]
            for ch in range(nchain):
                rw, lanes = where[u * nchain + ch]
                y_dst[ch % 2][rw, lanes] = ys[ch] + cur[ch]["y0"]
        return states

    init = [_block_diag2(s0_ref[0, ch % 2, 2 * (ch // 2)], s0_ref[0, ch % 2, 2 * (ch // 2) + 1])
            if has_init else jnp.zeros((LANES, LANES), F32) for ch in range(nchain)]
    if ngroups == 1:
        final = group(0, init)
    else:
        for ch in range(nchain):
            g_scr[ch] = init[ch]

        def body(gi, carry):
            new = group(gi, [g_scr[ch] for ch in range(nchain)])
            for ch in range(nchain):
                g_scr[ch] = new[ch]
            return carry

        lax.fori_loop(0, ngroups, body, 0)
        final = [g_scr[ch] for ch in range(nchain)]
    y_ref[...] = (yf_scr[...] + yb_scr[...]).astype(y_ref.dtype)
    if want_final:
        for ch in range(nchain):
            sf_ref[0, ch % 2, 2 * (ch // 2)] = final[ch][:HS, :HS]
            sf_ref[0, ch % 2, 2 * (ch // 2) + 1] = final[ch][HS:, HS:]


def _rwkv_scan(proj, s_init, n_seq, seq_len, want_final):
    r, v, kk, ld0, b0, kt0, ld1, b1, kt1 = proj[:9]
    n = r.shape[0]
    npair = D // LANES
    width = SCAN_PAIRS * LANES
    blk = pl.BlockSpec((seq_len, width), lambda b, p: (b, p))
    st_spec = pl.BlockSpec((1, 2, 2 * SCAN_PAIRS, HS, HS), lambda b, p: (b, 0, p, 0, 0))
    in_specs = [blk] * 9
    args = [r, v, kk, ld0, b0, kt0, ld1, b1, kt1]
    has_init = s_init is not None
    if has_init:
        in_specs.append(st_spec)
        args.append(s_init)
    out_specs = [blk]
    out_shape = [jax.ShapeDtypeStruct((n, D), BF16)]
    if want_final:
        out_specs.append(st_spec)
        out_shape.append(jax.ShapeDtypeStruct((n_seq, 2, NH, HS, HS), F32))
    res = pl.pallas_call(
        functools.partial(_rwkv_scan_kernel, seq_len=seq_len, has_init=has_init, want_final=want_final),
        grid=(n_seq, npair // SCAN_PAIRS),
        in_specs=in_specs,
        out_specs=out_specs,
        out_shape=out_shape,
        scratch_shapes=[pltpu.VMEM((2 * SCAN_PAIRS, LANES, LANES), F32), pltpu.VMEM((seq_len, width), F32),
                        pltpu.VMEM((seq_len, width), F32)],
        compiler_params=_cparams(("arbitrary", "arbitrary")),
        name="rwkv_scan",
    )(*args)
    return res


def _rwkv_out_kernel(x_ref, m_ref, g_ref, y_ref, bonus_ref, gate_ref, lng_ref, lnb_ref, wo_ref, o_ref,
                     acc_scr):
    j = pl.program_id(1)
    last = pl.num_programs(1) - 1

    def chunk():
        e = _head_indicator(RW_CW)
        mean = jnp.dot(y_ref[...], e, preferred_element_type=F32) * (1.0 / HS)
        yc = y_ref[...].astype(F32) - mean
        var = _dot(yc * yc, e) * (1.0 / HS)
        yn = yc * lax.rsqrt(var + GN_EPS) * lng_ref[...] + lnb_ref[...]
        yn = (yn + bonus_ref[...].astype(F32)) * gate_ref[...].astype(F32)
        return _dot(yn, wo_ref[0])

    @pl.when(j == 0)
    def _():
        acc_scr[...] = chunk()

    @pl.when((j > 0) & (j < last))
    def _():
        acc_scr[...] += chunk()

    @pl.when(j == last)
    def _():
        m = m_ref[0, 0]
        g = g_ref[0]
        o_ref[...] = x_ref[...] + m[2:3] * _rms(acc_scr[...] + chunk(), g[1:2])


def _rwkv_out(src, dst, mod, norm_g, y, bonus, gate, p, layer, j, group_of_tile):
    nj = D // RW_CW
    assert nj >= 2
    tile = pl.BlockSpec((TM, RW_CW), lambda i, c: (i, c))
    vec = pl.BlockSpec((1, RW_CW), lambda i, c: (j, c))
    return _activation_call(
        _rwkv_out_kernel,
        grid=(src.ntiles, nj),
        in_specs=[
            pl.BlockSpec((TM, D), lambda i, c: (i + src.first, 0)),
            _mod_spec(layer, group_of_tile, 2),
            _normg_spec(layer, 2),
            tile, tile, tile, vec, vec,
            pl.BlockSpec((1, RW_CW, D), lambda i, c: (j, c, 0)),
        ],
        args=[src.array, mod, norm_g, y, bonus, gate, p["ln_g"], p["ln_b"], p["w_o"]],
        dst=dst, block_rows=TM, out_index=lambda i, c: (i + dst.first, 0),
        scratch_shapes=[pltpu.VMEM((TM, D), F32)],
        name="rwkv_out")


def _rope_tables(n):
    rows = n // GRID_W
    row = np.repeat(np.arange(rows), GRID_W)
    col = np.tile(np.arange(GRID_W), rows)
    pos = np.stack([row, col], axis=-1).astype(np.float64)
    quarter = D_ROPE // 4
    inv = ROPE_BASE ** (-np.arange(quarter, dtype=np.float64) / quarter)
    ang = pos[:, :, None] * inv
    cos = np.cos(ang)
    sin = np.sin(ang)
    cos_t = np.concatenate([cos, cos], axis=-1).reshape(n, D_ROPE)
    sin_t = np.concatenate([-sin, sin], axis=-1).reshape(n, D_ROPE)
    return cos_t.astype(np.float32), sin_t.astype(np.float32)


def _rope_swap_perm():
    quarter = D_ROPE // 4
    base = np.arange(D_ROPE)
    return np.where((base % (2 * quarter)) < quarter, base + quarter, base - quarter)


def _mla_proj_kernel(*refs, positional):
    (x_ref, m_ref, g_ref, wdq_ref, gq_ref, wqn_ref, wqr_ref, wqs_ref, wdkv_ref, wkr_ref, wks_ref,
     gkv_ref) = refs[:12]
    pos = 12
    if positional:
        cosq_ref, sinq_ref, cosk_ref, sink_ref = refs[pos:pos + 4]
        pos += 4
    qn_o, qr_o, ckv_o, kr_o = refs[pos:pos + 4]
    m = m_ref[0, 0]
    g = g_ref[0]
    h = (_rms(x_ref[...], g[0:1]) * (1.0 + m[1:2]) + m[0:1]).astype(BF16)
    ql = jnp.dot(h, wdq_ref[0].astype(BF16), preferred_element_type=F32)
    ql = (ql * lax.rsqrt(jnp.mean(ql * ql, axis=-1, keepdims=True) + EPS) * gq_ref[...]).astype(BF16)
    qn_o[...] = jnp.dot(ql, wqn_ref[...].astype(BF16), preferred_element_type=F32).astype(qn_o.dtype)
    qr = jnp.dot(ql, wqr_ref[...].astype(BF16), preferred_element_type=F32)
    ckv = jnp.dot(h, wdkv_ref[...].astype(BF16), preferred_element_type=F32)
    ckv_o[...] = ckv * lax.rsqrt(jnp.mean(ckv * ckv, axis=-1, keepdims=True) + EPS) * gkv_ref[...]
    kr = jnp.dot(h, wkr_ref[...].astype(BF16), preferred_element_type=F32)
    if positional:
        qs = jnp.dot(ql, wqs_ref[...].astype(BF16), preferred_element_type=F32)
        ks = jnp.dot(h, wks_ref[...].astype(BF16), preferred_element_type=F32)
        qr = qr * cosq_ref[...] + qs * sinq_ref[...]
        kr = kr * cosk_ref[...] + ks * sink_ref[...]
    qr_o[...] = qr.astype(qr_o.dtype)
    kr_o[...] = kr


def _mla_proj(src, mod, norm_g, p, layer, j, positional, group_of_tile):
    n = src.ntiles * TM
    full = lambda shape: pl.BlockSpec(shape, lambda i: (0,) * len(shape))
    in_specs = [
        pl.BlockSpec((TM, D), lambda i: (i + src.first, 0)),
        _mod_spec(layer, group_of_tile, 1),
        _normg_spec(layer, 1),
        pl.BlockSpec((1, D, Q_RANK), lambda i: (j, 0, 0)),
        pl.BlockSpec((1, Q_RANK), lambda i: (j, 0)),
        full((Q_RANK, MLA_H * D_NOPE)), full((Q_RANK, MLA_H * D_ROPE)), full((Q_RANK, MLA_H * D_ROPE)),
        full((D, KV_RANK)), full((D, D_ROPE)), full((D, D_ROPE)),
        pl.BlockSpec((1, KV_RANK), lambda i: (j, 0)),
    ]
    args = [src.array, mod, norm_g, p["w_dq"], p["g_q"], p["w_uq_nope"], p["w_uq_rope"], p["w_uq_rope_sw"],
            p["w_dkv_c"], p["w_dkv_r"], p["w_dkv_r_sw"], p["g_kv"]]
    if positional:
        cos_t, sin_t = _rope_tables(TM)
        in_specs += [full((TM, MLA_H * D_ROPE)), full((TM, MLA_H * D_ROPE)),
                     full((TM, D_ROPE)), full((TM, D_ROPE))]
        args += [jnp.asarray(np.tile(cos_t, (1, MLA_H))), jnp.asarray(np.tile(sin_t, (1, MLA_H))),
                 jnp.asarray(cos_t), jnp.asarray(sin_t)]
    outs = ((MLA_H * D_NOPE, BF16), (MLA_H * D_ROPE, BF16), (KV_RANK, F32), (D_ROPE, F32))
    return pl.pallas_call(
        functools.partial(_mla_proj_kernel, positional=positional),
        grid=(src.ntiles,),
        in_specs=in_specs,
        out_specs=[pl.BlockSpec((TM, w), lambda i: (i, 0)) for w, _ in outs],
        out_shape=[jax.ShapeDtypeStruct((n, w), dt) for w, dt in outs],
        compiler_params=_cparams(("arbitrary",)),
        name="mla_proj",
    )(*args)


def _mla_attn_kernel(x_ref, m_ref, g_ref, qn_ref, qr_ref, ckv_ref, kr_ref, wuk_ref, wuv_ref, wo_ref,
                     o_ref, kn_scr, vv_scr, oh_scr, *, nb, tq, k_len):
    qi = pl.program_id(1)

    @pl.when(qi == 0)
    def _():
        ckv = ckv_ref[...].astype(BF16)
        kn_scr[...] = jnp.dot(ckv, wuk_ref[0].astype(BF16), preferred_element_type=F32).astype(BF16)
        vv_scr[...] = jnp.dot(ckv, wuv_ref[0].astype(BF16), preferred_element_type=F32).astype(BF16)

    units = [(b, hd) for b in range(nb) for hd in range(MLA_H)]

    def scores(unit):
        b, hd = unit
        qrows = slice(b * tq, (b + 1) * tq)
        krows = slice(b * k_len, (b + 1) * k_len)
        q = jnp.concatenate([qn_ref[qrows, hd * D_NOPE:(hd + 1) * D_NOPE].astype(BF16),
                             qr_ref[qrows, hd * D_ROPE:(hd + 1) * D_ROPE].astype(BF16)], axis=1)
        k = jnp.concatenate([kn_scr[krows, hd * D_NOPE:(hd + 1) * D_NOPE],
                             kr_ref[krows, :].astype(BF16)], axis=1)
        return _dot_nt(q, k) * MLA_SCALE

    pending = [scores(u) for u in units[:ATTN_LOOKAHEAD]]
    for idx, (b, hd) in enumerate(units):
        if idx + ATTN_LOOKAHEAD < len(units):
            pending.append(scores(units[idx + ATTN_LOOKAHEAD]))
        s = pending[idx]
        pexp = jnp.exp(s - jnp.max(s, axis=-1, keepdims=True))
        pv = jnp.dot(pexp.astype(BF16), vv_scr[b * k_len:(b + 1) * k_len, hd * D_V:(hd + 1) * D_V],
                     preferred_element_type=F32)
        oh_scr[b * tq:(b + 1) * tq, hd * D_V:(hd + 1) * D_V] = (
            pv / jnp.sum(pexp, axis=-1, keepdims=True)).astype(BF16)
    o = jnp.dot(oh_scr[...], wo_ref[0].astype(BF16), preferred_element_type=F32)
    m = m_ref[0, 0]
    g = g_ref[0]
    o_ref[...] = x_ref[...] + m[2:3] * _rms(o, g[1:2])


def _mla_attn(src, dst, mod, norm_g, qn, qr, ckv_all, kr_all, p, layer, j, n_seq, q_len, k_len, nb, tq,
              group_of_step):
    nq = q_len // tq
    assert nb == 1 or nq == 1
    rows = nb * tq
    x_first, o_first = src.first * TM // rows, dst.first * TM // rows
    return _activation_call(
        functools.partial(_mla_attn_kernel, nb=nb, tq=tq, k_len=k_len),
        grid=(n_seq // nb, nq),
        in_specs=[
            pl.BlockSpec((rows, D), lambda s, q: (s * nq + q + x_first, 0)),
            pl.BlockSpec((1, 1, N_MOD, D), lambda s, q: (0, group_of_step(s), 0, 0)),
            pl.BlockSpec((1, 4, D), lambda s, q: (layer, 0, 0)),
            pl.BlockSpec((nb * tq, MLA_H * D_NOPE), lambda s, q: (s * nq + q, 0)),
            pl.BlockSpec((nb * tq, MLA_H * D_ROPE), lambda s, q: (s * nq + q, 0)),
            pl.BlockSpec((nb * k_len, KV_RANK), lambda s, q: (s, 0)),
            pl.BlockSpec((nb * k_len, D_ROPE), lambda s, q: (s, 0)),
            pl.BlockSpec((1, KV_RANK, MLA_H * D_NOPE), lambda s, q: (j, 0, 0)),
            pl.BlockSpec((1, KV_RANK, MLA_H * D_V), lambda s, q: (j, 0, 0)),
            pl.BlockSpec((1, MLA_H * D_V, D), lambda s, q: (j, 0, 0)),
        ],
        args=[src.array, mod, norm_g, qn, qr, ckv_all, kr_all, p["w_uk"], p["w_uv"], p["w_o"]],
        dst=dst, block_rows=rows, out_index=lambda s, q: (s * nq + q + o_first, 0),
        scratch_shapes=[pltpu.VMEM((nb * k_len, MLA_H * D_NOPE), BF16),
                        pltpu.VMEM((nb * k_len, MLA_H * D_V), BF16),
                        pltpu.VMEM((nb * tq, MLA_H * D_V), BF16)],
        name="mla_attn")


def kernel(x_prompt, x_sample, state_rwkv, cache_mla_ckv, cache_mla_krope, c, c_ctx, mod_w, mod_b, norm_g,
           mlp_w1, mlp_w2, fft_w_out, conv_w_in, conv_w, conv_w_out, rwkv_mu, rwkv_w_r, rwkv_w_k, rwkv_w_v,
           rwkv_w_o, rwkv_w0, rwkv_w_l1, rwkv_w_l2, rwkv_a0, rwkv_a_l1, rwkv_a_l2, rwkv_g_l1, rwkv_g_l2,
           rwkv_k_k, rwkv_k_a, rwkv_r_k, rwkv_ln_g, rwkv_ln_b, mla_w_dq, mla_g_q, mla_w_uq, mla_w_dkv,
           mla_g_kv, mla_w_uk, mla_w_uv, mla_w_o):
    batch, seq, _ = x_prompt.shape
    dec_batch, dec_seq, _ = x_sample.shape
    past_len = cache_mla_ckv.shape[2]
    assert (batch * seq) % TM == 0 and TM % seq == 0 and dec_seq == TM and seq % CHUNK == 0

    np_tiles = batch * seq // TM
    ns_tiles = dec_batch * dec_seq // TM
    total_rows = (np_tiles + ns_tiles) * TM
    cs = jnp.concatenate([c_ctx[None, :], c, jnp.zeros((MOD_ROWS - 1 - dec_batch, D), F32)], axis=0)
    mod_b3 = mod_b.reshape(DEPTH, 1, N_MOD * D)
    mod = _modulation(cs, mod_w, mod_b3, 0)

    grp_p = lambda i: 0
    grp_s = lambda i: 1 + i
    grp_all = lambda i: jnp.maximum(i - (np_tiles - 1), 0)
    new_rwkv, new_ckv, new_krope = [], [], []
    streams = [(True, seq, batch, grp_p), (False, dec_seq, dec_batch, grp_s)]
    cur = [_Rows(x_prompt.reshape(batch * seq, D), 0, np_tiles),
           _Rows(x_sample.reshape(dec_batch * dec_seq, D), 0, ns_tiles)]

    for i in range(DEPTH):
        kind, j = i % 4, i // 4
        for idx, (is_prompt, slen, nseq, grp) in enumerate(streams):
            src = cur[idx]
            sharing = src.array.shape[0] == total_rows
            dst = _Dest(total_rows, src.first, True) if sharing else _Dest(src.ntiles * TM, 0, False)
            shared = None
            if kind == 0:
                shared = _fnet(src, dst, mod, norm_g, fft_w_out, i, j, slen, grp)
            elif kind == 1:
                shared = _conv(src, dst, mod, norm_g, conv_w_in, conv_w, conv_w_out, i, j, slen, grp)
            elif kind == 2:
                p = dict(mu=rwkv_mu, w_r=rwkv_w_r, w_k=rwkv_w_k, w_v=rwkv_w_v, w_o=rwkv_w_o, w0=rwkv_w0,
                         w_l1=rwkv_w_l1, w_l2=rwkv_w_l2, a0=rwkv_a0, a_l1=rwkv_a_l1, a_l2=rwkv_a_l2,
                         g_l1=rwkv_g_l1, g_l2=rwkv_g_l2, k_k=rwkv_k_k, k_a=rwkv_k_a,
                         r_k=rwkv_r_k.reshape(-1, D), ln_g=rwkv_ln_g, ln_b=rwkv_ln_b)
                proj = _rwkv_proj(src, mod, norm_g, p, i, j, slen, grp)
                if is_prompt:
                    y, s_fin = _rwkv_scan(proj, None, nseq, slen, True)
                    new_rwkv.append(s_fin)
                else:
                    (y,) = _rwkv_scan(proj, state_rwkv[:, j], nseq, slen, False)
                shared = _rwkv_out(src, dst, mod, norm_g, y, proj[10], proj[9], p, i, j, grp)
            else:
                perm = _rope_swap_perm()
                w_uq = mla_w_uq[j].reshape(Q_RANK, MLA_H, D_NOPE + D_ROPE)
                w_uq_rope = w_uq[:, :, D_NOPE:]
                w_dkv_r = mla_w_dkv[j][:, KV_RANK:]
                p = dict(w_dq=mla_w_dq, g_q=mla_g_q, g_kv=mla_g_kv, w_uk=mla_w_uk, w_uv=mla_w_uv, w_o=mla_w_o,
                         w_uq_nope=w_uq[:, :, :D_NOPE].reshape(Q_RANK, MLA_H * D_NOPE),
                         w_uq_rope=w_uq_rope.reshape(Q_RANK, MLA_H * D_ROPE),
                         w_uq_rope_sw=w_uq_rope[:, :, perm].reshape(Q_RANK, MLA_H * D_ROPE),
                         w_dkv_c=mla_w_dkv[j][:, :KV_RANK], w_dkv_r=w_dkv_r, w_dkv_r_sw=w_dkv_r[:, perm])
                qn, qr, ckv, kr = _mla_proj(src, mod, norm_g, p, i, j, not is_prompt, grp)
                if is_prompt:
                    new_ckv.append(ckv.reshape(batch, seq, KV_RANK))
                    new_krope.append(kr.reshape(batch, seq, D_ROPE))
                    shared = _mla_attn(src, dst, mod, norm_g, qn, qr, ckv, kr, p, i, j, nseq, slen, slen,
                                       TM // slen, slen, lambda s: 0)
                else:
                    klen = past_len + slen
                    ckv_all = jnp.concatenate([cache_mla_ckv[:, j], ckv.reshape(nseq, slen, KV_RANK)], axis=1)
                    kr_all = jnp.concatenate([cache_mla_krope[:, j], kr.reshape(nseq, slen, D_ROPE)], axis=1)
                    shared = _mla_attn(src, dst, mod, norm_g, qn, qr, ckv_all.reshape(nseq * klen, KV_RANK),
                                       kr_all.reshape(nseq * klen, D_ROPE), p, i, j, nseq, slen, klen, 1, 256,
                                       lambda s: 1 + s)
            cur[idx] = _Rows(shared, dst.first, src.ntiles)
            if sharing:
                cur[1 - idx] = cur[1 - idx]._replace(array=shared)
        if i < DEPTH - 1:
            srcs = [_Rows(cur[0].array, 0, np_tiles + ns_tiles)] if cur[0].array is cur[1].array else cur
            both, mod = _mlp(srcs, _Dest(total_rows, 0, False), mod, norm_g, mlp_w1, mlp_w2, i, grp_all,
                             next_mod=(cs, mod_w, mod_b3, i + 1))
            cur = [_Rows(both, 0, np_tiles), _Rows(both, np_tiles, ns_tiles)]
        else:
            y_prompt, y_sample = [
                _mlp([cur[idx]], _Dest(cur[idx].ntiles * TM, 0, False), mod, norm_g, mlp_w1, mlp_w2, i,
                     streams[idx][3]) for idx in range(2)]

    return (y_prompt.reshape(batch, seq, D), y_sample.reshape(dec_batch, dec_seq, D),
            jnp.stack(new_rwkv, axis=1), jnp.stack(new_ckv, axis=1), jnp.stack(new_krope, axis=1))
```

```python
import functools
from typing import NamedTuple

import numpy as np
import jax
import jax.numpy as jnp
from jax import lax
from jax.experimental import pallas as pl
from jax.experimental.pallas import tpu as pltpu

D = 1024
DEPTH = 4
N_MOD = 6
D_FF = 4 * D
EPS = 1e-6
GRID_W = 64
FFT_GROUPS = 8
FFT_GW = D // FFT_GROUPS
HS = 64
NH = D // HS
LORA_W = 64
LORA_A = 64
LORA_G = 128
GN_EPS = 64e-5
DECAY_SCALE = float(np.exp(-0.5))
MLA_H = 8
D_NOPE = 128
D_ROPE = 64
D_V = 128
KV_RANK = 256
Q_RANK = 384
ROPE_BASE = 10000.0
MLA_SCALE = (D_NOPE + D_ROPE) ** -0.5

F32 = jnp.float32
BF16 = jnp.bfloat16

TM = 1024
LANES = 128
CHUNK = 64
SCAN_UNROLL = 4
SCAN_PAIRS = 2
ATTN_LOOKAHEAD = 2
VMEM_LIMIT = 58 * 1024 * 1024


def _cparams(sem):
    return pltpu.CompilerParams(dimension_semantics=sem, vmem_limit_bytes=VMEM_LIMIT)


def _dot(a, b):
    return jnp.dot(a.astype(BF16), b.astype(BF16), preferred_element_type=F32)


def _dot_nt(a, b):
    return lax.dot_general(a.astype(BF16), b.astype(BF16), (((1,), (1,)), ((), ())),
                           preferred_element_type=F32)


def _dot_tn(a, b):
    return lax.dot_general(a.astype(BF16), b.astype(BF16), (((0,), (0,)), ((), ())),
                           preferred_element_type=F32)


def _block_diag2(a, b):
    za = jnp.zeros((a.shape[0], b.shape[1]), a.dtype)
    zb = jnp.zeros((b.shape[0], a.shape[1]), a.dtype)
    return jnp.concatenate([jnp.concatenate([a, za], axis=1), jnp.concatenate([zb, b], axis=1)], axis=0)


def _cumsum_rows(x, reverse):
    n = x.shape[0]
    idx = lax.broadcasted_iota(jnp.int32, x.shape, 0)
    s = 1
    while s < n:
        if reverse:
            x = x + jnp.where(idx < n - s, pltpu.roll(x, n - s, 0), 0.0)
        else:
            x = x + jnp.where(idx >= s, pltpu.roll(x, s, 0), 0.0)
        s *= 2
    return x


def _rms(x, g):
    return x * lax.rsqrt(jnp.mean(x * x, axis=-1, keepdims=True) + EPS) * g


def _sigmoid(x):
    return 1.0 / (1.0 + jnp.exp(-x))


def _head_indicator(n):
    r = lax.broadcasted_iota(jnp.int32, (n, n), 0) // HS
    c = lax.broadcasted_iota(jnp.int32, (n, n), 1) // HS
    return jnp.where(r == c, 1.0, 0.0).astype(BF16)


def _shift_rows(z, seq_len):
    n, w = z.shape
    pos = lax.broadcasted_iota(jnp.int32, (n, LANES), 0) % seq_len
    keep_prev = jnp.concatenate([jnp.where(pos == 0, 0.0, 1.0)] * (w // LANES), axis=1)
    keep_next = jnp.concatenate([jnp.where(pos == seq_len - 1, 0.0, 1.0)] * (w // LANES), axis=1)
    return pltpu.roll(z, 1, 0) * keep_prev, pltpu.roll(z, n - 1, 0) * keep_next


MOD_TN = 1536
MOD_ROWS = 8


def _mod_block(cs_ref, w_ref, b_ref):
    cs = cs_ref[...]
    return _dot(cs * _sigmoid(cs), w_ref[0]) + b_ref[0]


def _mod_kernel(cs_ref, w_ref, b_ref, o_ref):
    o_ref[...] = _mod_block(cs_ref, w_ref, b_ref)


def _modulation(cs, mod_w, mod_b3, layer):
    nj = (N_MOD * D) // MOD_TN
    out = pl.pallas_call(
        _mod_kernel,
        grid=(nj,),
        in_specs=[
            pl.BlockSpec((MOD_ROWS, D), lambda j: (0, 0)),
            pl.BlockSpec((1, D, MOD_TN), lambda j: (layer, 0, j)),
            pl.BlockSpec((1, 1, MOD_TN), lambda j: (layer, 0, j)),
        ],
        out_specs=pl.BlockSpec((MOD_ROWS, MOD_TN), lambda j: (0, j)),
        out_shape=jax.ShapeDtypeStruct((MOD_ROWS, N_MOD * D), F32),
        compiler_params=_cparams(("arbitrary",)),
        name="modulation",
    )(cs, mod_w, mod_b3)
    return out


class _Rows(NamedTuple):
    array: jax.Array
    first: int
    ntiles: int


class _Dest(NamedTuple):
    rows: int
    first: int
    inplace: bool


def _activation_call(kernel, *, grid, in_specs, args, dst, block_rows, out_index, scratch_shapes, name,
                     extra_out=None):
    assert not dst.inplace or args[0].shape[0] == dst.rows
    out_specs = pl.BlockSpec((block_rows, D), out_index)
    out_shape = jax.ShapeDtypeStruct((dst.rows, D), F32)
    if extra_out is not None:
        out_specs, out_shape = [out_specs, extra_out[0]], [out_shape, extra_out[1]]
    return pl.pallas_call(
        kernel, grid=grid, in_specs=in_specs, out_specs=out_specs, out_shape=out_shape,
        scratch_shapes=scratch_shapes, input_output_aliases={0: 0} if dst.inplace else {},
        compiler_params=_cparams(("arbitrary",) * len(grid)), name=name)(*args)


def _mod_spec():
    return pl.BlockSpec((MOD_ROWS, N_MOD * D), lambda *_: (0, 0))


def _mod_rows(m_ref, g):
    row = m_ref[pl.ds(g, 1), :]
    return jnp.concatenate([row[:, k * D:(k + 1) * D] for k in range(N_MOD)], axis=0)


def _normg_spec(layer, ngrid):
    if ngrid == 1:
        return pl.BlockSpec((1, 4, D), lambda i: (layer, 0, 0))
    return pl.BlockSpec((1, 4, D), lambda i, j: (layer, 0, 0))


MLP_FC = 1024


def _mlp_kernel(*refs, n_first, with_next_mod, group_of_tile):
    if n_first is None:
        (x_ref,), rest = refs[:1], refs[1:]
        read_x = lambda: x_ref[...]
    else:
        (xa_ref, xb_ref), rest = refs[:2], refs[2:]
        read_x = lambda: jnp.where(pl.program_id(0) < n_first, xa_ref[...], xb_ref[...])
    if with_next_mod:
        m_ref, g_ref, w1_ref, w2_ref, cs_ref, mw_ref, mb_ref, o_ref, mo_ref, h_scr, acc_scr = rest
        mo_ref[...] = _mod_block(cs_ref, mw_ref, mb_ref)
    else:
        m_ref, g_ref, w1_ref, w2_ref, o_ref, h_scr, acc_scr = rest
    j = pl.program_id(1)
    m = _mod_rows(m_ref, group_of_tile(pl.program_id(0)))
    g = g_ref[0]

    last = pl.num_programs(1) - 1

    def chunk():
        a = jnp.dot(h_scr[...], w1_ref[0].astype(BF16), preferred_element_type=F32)
        a = jnp.maximum(a, 0.0)
        return _dot(a * a, w2_ref[0])

    @pl.when(j == 0)
    def _():
        h = _rms(read_x(), g[2:3]) * (1.0 + m[4:5]) + m[3:4]
        h_scr[...] = h.astype(BF16)
        acc_scr[...] = chunk()

    @pl.when((j > 0) & (j < last))
    def _():
        acc_scr[...] += chunk()

    @pl.when(j == last)
    def _():
        f = acc_scr[...] + chunk()
        o_ref[...] = read_x() + m[5:6] * _rms(f, g[3:4])


def _mlp(srcs, dst, mod, norm_g, w1, w2, layer, group_of_tile, next_mod=None):
    nj = D_FF // MLP_FC
    assert nj >= 2
    if len(srcs) == 1:
        (src,) = srcs
        ntiles, n_first = src.ntiles, None
        x_specs = [pl.BlockSpec((TM, D), lambda i, j: (i + src.first, 0))]
    else:
        sa, sb = srcs
        ntiles, n_first = sa.ntiles + sb.ntiles, sa.ntiles
        x_specs = [pl.BlockSpec((TM, D), lambda i, j: (jnp.minimum(i, sa.ntiles - 1) + sa.first, 0)),
                   pl.BlockSpec((TM, D), lambda i, j: (jnp.maximum(i - sa.ntiles, 0) + sb.first, 0))]
    in_specs = x_specs + [
        _mod_spec(),
        _normg_spec(layer, 2),
        pl.BlockSpec((1, D, MLP_FC), lambda i, j: (layer, 0, j)),
        pl.BlockSpec((1, MLP_FC, D), lambda i, j: (layer, j, 0)),
    ]
    args = [s.array for s in srcs] + [mod, norm_g, w1, w2]
    extra_out = None
    if next_mod is not None:
        cs, mod_w, mod_b3, nxt = next_mod
        cols = (N_MOD * D) // (ntiles * nj)
        assert cols * ntiles * nj == N_MOD * D and cols % LANES == 0
        in_specs += [pl.BlockSpec((MOD_ROWS, D), lambda i, j: (0, 0)),
                     pl.BlockSpec((1, D, cols), lambda i, j: (nxt, 0, i * nj + j)),
                     pl.BlockSpec((1, 1, cols), lambda i, j: (nxt, 0, i * nj + j))]
        args += [cs, mod_w, mod_b3]
        extra_out = (pl.BlockSpec((MOD_ROWS, cols), lambda i, j: (0, i * nj + j)),
                     jax.ShapeDtypeStruct((MOD_ROWS, N_MOD * D), F32))
    res = _activation_call(
        functools.partial(_mlp_kernel, n_first=n_first, with_next_mod=next_mod is not None,
                          group_of_tile=group_of_tile),
        grid=(ntiles, nj), in_specs=in_specs, args=args,
        dst=dst, block_rows=TM, out_index=lambda i, j: (i + dst.first, 0),
        scratch_shapes=[pltpu.VMEM((TM, D), BF16), pltpu.VMEM((TM, D), F32)],
        name="mlp", extra_out=extra_out)
    if next_mod is None:
        return res
    return res[0], res[1]


def _dft_mats(n):
    idx = np.arange(n, dtype=np.int64)
    ang = (2.0 * np.pi / n) * ((idx[:, None] * idx[None, :]) % n).astype(np.float64)
    scale = 1.0 / np.sqrt(n)
    return np.cos(ang) * scale, np.sin(ang) * scale


def _fnet_kernel(x_ref, m_ref, g_ref, cs_ref, cn_ref, sn_ref, w_ref, o_ref, p_scr, q_scr, f_scr,
                 *, seq_len, group_of_tile):
    m = _mod_rows(m_ref, group_of_tile(pl.program_id(0)))
    g = g_ref[0]
    x = x_ref[...]
    h = (_rms(x, g[0:1]) * (1.0 + m[1:2]) + m[0:1]).astype(BF16)
    cs = cs_ref[...].astype(BF16)
    for gi in range(FFT_GROUPS):
        pq = jnp.dot(h[:, gi * FFT_GW:(gi + 1) * FFT_GW], cs, preferred_element_type=F32)
        p_scr[:, gi * FFT_GW:(gi + 1) * FFT_GW] = pq[:, :FFT_GW].astype(BF16)
        q_scr[:, gi * FFT_GW:(gi + 1) * FFT_GW] = pq[:, FFT_GW:].astype(BF16)
    cn = cn_ref[...].astype(BF16)
    sn = sn_ref[...].astype(BF16)
    for s in range(TM // seq_len):
        rows = slice(s * seq_len, (s + 1) * seq_len)
        f = (jnp.dot(cn, p_scr[rows, :], preferred_element_type=F32)
             - jnp.dot(sn, q_scr[rows, :], preferred_element_type=F32))
        f_scr[rows, :] = f.astype(BF16)
    o = jnp.dot(f_scr[...], w_ref[0].astype(BF16), preferred_element_type=F32)
    o_ref[...] = x + m[2:3] * _rms(o, g[1:2])


def _fnet(src, dst, mod, norm_g, w_out, layer, j, seq_len, group_of_tile):
    cg, sg = _dft_mats(FFT_GW)
    cs = jnp.asarray(np.concatenate([cg, sg], axis=1), F32)
    cn_np, sn_np = _dft_mats(seq_len)
    cn = jnp.asarray(cn_np, F32)
    sn = jnp.asarray(sn_np, F32)
    return _activation_call(
        functools.partial(_fnet_kernel, seq_len=seq_len, group_of_tile=group_of_tile),
        grid=(src.ntiles,),
        in_specs=[
            pl.BlockSpec((TM, D), lambda i: (i + src.first, 0)),
            _mod_spec(),
            _normg_spec(layer, 1),
            pl.BlockSpec((FFT_GW, 2 * FFT_GW), lambda i: (0, 0)),
            pl.BlockSpec((seq_len, seq_len), lambda i: (0, 0)),
            pl.BlockSpec((seq_len, seq_len), lambda i: (0, 0)),
            pl.BlockSpec((1, D, D), lambda i: (j, 0, 0)),
        ],
        args=[src.array, mod, norm_g, cs, cn, sn, w_out],
        dst=dst, block_rows=TM, out_index=lambda i: (i + dst.first, 0),
        scratch_shapes=[pltpu.VMEM((TM, D), BF16), pltpu.VMEM((TM, D), BF16),
                        pltpu.VMEM((TM, D), BF16)],
        name="fourier_mix")


CONV_CW = 512


def _conv_kernel(x_ref, m_ref, g_ref, wb_ref, wc_ref, wu_ref, cw_ref, wo_ref, o_ref, h_scr, acc_scr,
                 *, seq_len, group_of_tile):
    j = pl.program_id(1)
    m = _mod_rows(m_ref, group_of_tile(pl.program_id(0)))
    g = g_ref[0]

    def chunk():
        h = h_scr[...]
        hw = CONV_CW // 2
        proj = []
        for s in range(2):
            cols = slice(s * hw, (s + 1) * hw)
            proj.append([jnp.dot(h, w_ref[0, :, cols].astype(BF16), preferred_element_type=F32)
                         for w_ref in (wb_ref, wc_ref, wu_ref)])
        out = None
        for s in range(2):
            cols = slice(s * hw, (s + 1) * hw)
            bg, cg, u = proj[s]
            z = cg * u
            z_prev, z_next = _shift_rows(z, seq_len)
            conv = z_prev * cw_ref[0, 0:1, cols] + z * cw_ref[0, 1:2, cols] + z_next * cw_ref[0, 2:3, cols]
            part = _dot(bg * conv, wo_ref[0, cols, :])
            out = part if out is None else out + part
        return out

    last = pl.num_programs(1) - 1

    @pl.when(j == 0)
    def _():
        h = _rms(x_ref[...], g[0:1]) * (1.0 + m[1:2]) + m[0:1]
        h_scr[...] = h.astype(BF16)
        acc_scr[...] = chunk()

    @pl.when((j > 0) & (j < last))
    def _():
        acc_scr[...] += chunk()

    @pl.when(j == last)
    def _():
        o_ref[...] = x_ref[...] + m[2:3] * _rms(acc_scr[...] + chunk(), g[1:2])


def _conv(src, dst, mod, norm_g, w_in, w_conv, w_out, layer, j, seq_len, group_of_tile):
    nj = D // CONV_CW
    assert nj >= 2
    return _activation_call(
        functools.partial(_conv_kernel, seq_len=seq_len, group_of_tile=group_of_tile),
        grid=(src.ntiles, nj),
        in_specs=[
            pl.BlockSpec((TM, D), lambda i, c: (i + src.first, 0)),
            _mod_spec(),
            _normg_spec(layer, 2),
            pl.BlockSpec((1, D, CONV_CW), lambda i, c: (j, 0, c)),
            pl.BlockSpec((1, D, CONV_CW), lambda i, c: (j, 0, nj + c)),
            pl.BlockSpec((1, D, CONV_CW), lambda i, c: (j, 0, 2 * nj + c)),
            pl.BlockSpec((1, 3, CONV_CW), lambda i, c: (j, 0, c)),
            pl.BlockSpec((1, CONV_CW, D), lambda i, c: (j, c, 0)),
        ],
        args=[src.array, mod, norm_g, w_in, w_in, w_in, w_conv, w_out],
        dst=dst, block_rows=TM, out_index=lambda i, c: (i + dst.first, 0),
        scratch_shapes=[pltpu.VMEM((TM, D), BF16), pltpu.VMEM((TM, D), F32)],
        name="short_conv")


RWP_CW = 256
RW_OUT_DTYPES = (BF16, BF16, BF16, F32, BF16, BF16, F32, BF16, BF16, BF16, BF16)
RW_CW = 512


def _rwkv_proj_kernel(x_ref, m_ref, g_ref, mu_ref, wr_ref, wk_ref, wv_ref, w0_ref, wl1_ref, wl2_ref,
                      a0_ref, al1_ref, al2_ref, gl1_ref, gl2_ref, kk_ref, ka_ref, rk_ref,
                      r_o, v_o, kk_o, ld0_o, b0_o, kt0_o, ld1_o, b1_o, kt1_o, g_o, bonus_o,
                      xr_scr, xk_scr, xv_scr, tw_scr, ta_scr, sg_scr, *, seq_len, group_of_tile):
    j = pl.program_id(1)

    def main_projections():
        return [jnp.dot(x_scr[...], w_ref[0].astype(BF16), preferred_element_type=F32)
                for x_scr, w_ref in ((xr_scr, wr_ref), (xk_scr, wk_ref), (xv_scr, wv_ref))]

    def finish_chunk(r, k, v):
        e = _head_indicator(RWP_CW)
        g_o[...] = jnp.dot(sg_scr[...], gl2_ref[0].astype(BF16), preferred_element_type=F32).astype(g_o.dtype)
        zws = [jnp.dot(tw_scr[d], wl2_ref[0, d].astype(BF16), preferred_element_type=F32) for d in range(2)]
        zas = [jnp.dot(ta_scr[d], al2_ref[0, d].astype(BF16), preferred_element_type=F32) for d in range(2)]
        kk = k * kk_ref[...]
        kk = kk * lax.rsqrt(_dot(kk * kk, e) + 1e-12)
        r_o[...] = r.astype(r_o.dtype)
        v_o[...] = v.astype(v_o.dtype)
        kk_o[...] = kk.astype(kk_o.dtype)
        k_scaled = k * ka_ref[...]
        kt_sum = None
        for d, (ld_o, b_o, kt_o) in enumerate(((ld0_o, b0_o, kt0_o), (ld1_o, b1_o, kt1_o))):
            ld_o[...] = -DECAY_SCALE * _sigmoid(w0_ref[0, d:d + 1, :] + zws[d])
            a = _sigmoid(a0_ref[0, d:d + 1, :] + zas[d])
            kt = k + k_scaled * (a - 1.0)
            b_o[...] = (kk * a).astype(b_o.dtype)
            kt_o[...] = kt.astype(kt_o.dtype)
            kt_sum = kt if kt_sum is None else kt_sum + kt
        bonus_o[...] = (_dot(r * rk_ref[...] * kt_sum, e) * v).astype(bonus_o.dtype)

    @pl.when(j == 0)
    def _():
        m = _mod_rows(m_ref, group_of_tile(pl.program_id(0)))
        g = g_ref[0]
        mu = mu_ref[0]
        h = _rms(x_ref[...], g[0:1]) * (1.0 + m[1:2]) + m[0:1]
        h_prev, h_next = _shift_rows(h, seq_len)
        dx = 0.5 * (h_prev + h_next) - h
        xr_scr[...] = (h + dx * mu[0:1]).astype(BF16)
        xk_scr[...] = (h + dx * mu[2:3]).astype(BF16)
        xv_scr[...] = (h + dx * mu[3:4]).astype(BF16)
        rkv = main_projections()
        xw = (h + dx * mu[1:2]).astype(BF16)
        xa = (h + dx * mu[4:5]).astype(BF16)
        xg = (h + dx * mu[5:6]).astype(BF16)
        for d in range(2):
            tw_scr[d] = jnp.tanh(_dot(xw, wl1_ref[0, d])).astype(BF16)
            ta_scr[d] = _dot(xa, al1_ref[0, d]).astype(BF16)
        sg_scr[...] = _sigmoid(_dot(xg, gl1_ref[0])).astype(BF16)
        finish_chunk(*rkv)

    @pl.when(j > 0)
    def _():
        finish_chunk(*main_projections())


def _rwkv_proj(src, mod, norm_g, p, layer, j, seq_len, group_of_tile):
    n = src.ntiles * TM
    nj = D // RWP_CW
    tile = pl.BlockSpec((TM, RWP_CW), lambda i, c: (i, c))
    wcol = lambda: pl.BlockSpec((1, D, RWP_CW), lambda i, c: (j, 0, c))
    vec = lambda: pl.BlockSpec((1, RWP_CW), lambda i, c: (j, c))
    return pl.pallas_call(
        functools.partial(_rwkv_proj_kernel, seq_len=seq_len, group_of_tile=group_of_tile),
        grid=(src.ntiles, nj),
        in_specs=[
            pl.BlockSpec((TM, D), lambda i, c: (i + src.first, 0)),
            _mod_spec(),
            _normg_spec(layer, 2),
            pl.BlockSpec((1, 6, D), lambda i, c: (j, 0, 0)),
            wcol(), wcol(), wcol(),
            pl.BlockSpec((1, 2, RWP_CW), lambda i, c: (j, 0, c)),
            pl.BlockSpec((1, 2, D, LORA_W), lambda i, c: (j, 0, 0, 0)),
            pl.BlockSpec((1, 2, LORA_W, RWP_CW), lambda i, c: (j, 0, 0, c)),
            pl.BlockSpec((1, 2, RWP_CW), lambda i, c: (j, 0, c)),
            pl.BlockSpec((1, 2, D, LORA_A), lambda i, c: (j, 0, 0, 0)),
            pl.BlockSpec((1, 2, LORA_A, RWP_CW), lambda i, c: (j, 0, 0, c)),
            pl.BlockSpec((1, D, LORA_G), lambda i, c: (j, 0, 0)),
            pl.BlockSpec((1, LORA_G, RWP_CW), lambda i, c: (j, 0, c)),
            vec(), vec(), vec(),
        ],
        out_specs=[tile] * len(RW_OUT_DTYPES),
        out_shape=[jax.ShapeDtypeStruct((n, D), dt) for dt in RW_OUT_DTYPES],
        scratch_shapes=[pltpu.VMEM((TM, D), BF16), pltpu.VMEM((TM, D), BF16), pltpu.VMEM((TM, D), BF16),
                        pltpu.VMEM((2, TM, LORA_W), BF16), pltpu.VMEM((2, TM, LORA_A), BF16),
                        pltpu.VMEM((TM, LORA_G), BF16)],
        compiler_params=_cparams(("arbitrary", "arbitrary")),
        name="rwkv_proj",
    )(src.array, mod, norm_g, p["mu"], p["w_r"], p["w_k"], p["w_v"], p["w0"], p["w_l1"], p["w_l2"],
      p["a0"], p["a_l1"], p["a_l2"], p["g_l1"], p["g_l2"], p["k_k"], p["k_a"], p["r_k"])


def _scan_precompute(units):
    c = CHUNK
    c2 = 2 * c
    row = lax.broadcasted_iota(jnp.int32, (c2, c2), 0)
    col = lax.broadcasted_iota(jnp.int32, (c2, c2), 1)
    head_a = lax.broadcasted_iota(jnp.int32, (c, LANES), 1) < HS
    own_lanes = jnp.concatenate([head_a, jnp.logical_not(head_a)], axis=0)
    eye = jnp.where(row == col, 1.0, 0.0)

    def two_heads(t):
        return jnp.concatenate([jnp.where(head_a, t, 0.0), jnp.where(head_a, 0.0, t)], axis=0)

    def causal(reverse):
        if reverse:
            return (col % c) > (row % c), (col % c) >= (row % c)
        return (col % c) < (row % c), (col % c) <= (row % c)

    masks = {rev: causal(rev) for rev in sorted({u[6] for u in units})}

    cums = [_cumsum_rows(u[0], u[6]) for u in units]
    st = []
    for (ld, kk, beta, kt, r, v, rev), cum in zip(units, cums):
        tot = cum[0:1] if rev else cum[c - 1:c]
        ginv = jnp.exp(-cum)
        tail = jnp.exp(tot - cum)
        st.append(dict(
            rev=rev, etot=jnp.exp(tot),
            a_t=two_heads(-kk * jnp.exp(cum - ld)), r_t=two_heads(r * jnp.exp(cum)),
            bk=jnp.concatenate([two_heads(beta * ginv), two_heads(kt * ginv)], axis=0),
            bkg=jnp.concatenate([two_heads(beta * tail), two_heads(kt * tail)], axis=0),
            v2=jnp.concatenate([v, v], axis=0), vh=two_heads(v)))
    grams = [_dot_nt(jnp.concatenate([s["a_t"], s["r_t"]], axis=0), s["bk"]) for s in st]
    for s, gram in zip(st, grams):
        strict, incl = masks[s["rev"]]
        s["l_ab"] = jnp.where(strict, gram[:c2, :c2], 0.0)
        s["l_ak"] = jnp.where(strict, gram[:c2, c2:], 0.0)
        s["t_rb"] = jnp.where(incl, gram[c2:, :c2], 0.0)
        s["t_rk"] = jnp.where(incl, gram[c2:, c2:], 0.0)
    lvs = [_dot(s["l_ak"], s["v2"]) for s in st]
    minvs = [eye for _ in st]
    b = 1
    while b < c:
        same = (row // (2 * b)) == (col // (2 * b))
        es = []
        for s in st:
            first, second = (col % (2 * b)) < b, (row % (2 * b)) >= b
            if s["rev"]:
                first, second = (row % (2 * b)) < b, (col % (2 * b)) >= b
            es.append(jnp.where(same & first & second, s["l_ab"], 0.0))
        if b == 1:
            minvs = [m + e for m, e in zip(minvs, es)]
        else:
            half = [_dot(m, e) for m, e in zip(minvs, es)]
            minvs = [m + _dot(h, m) for m, h in zip(minvs, half)]
        b *= 2
    mms = [_dot(m, jnp.concatenate([s["a_t"], lv], axis=1)) for m, s, lv in zip(minvs, st, lvs)]
    zero = jnp.zeros((c2, LANES), BF16)
    tts = [_dot(jnp.concatenate([s["t_rb"], s["t_rk"]], axis=1),
                jnp.concatenate([mm.astype(BF16), jnp.concatenate([zero, s["v2"].astype(BF16)], axis=1)], axis=0))
           for s, mm in zip(st, mms)]
    ps = [_dot_tn(mm[:, :LANES], s["bkg"][:c2]) for s, mm in zip(st, mms)]
    qs = [_dot_tn(jnp.concatenate([jnp.where(own_lanes, mm[:, LANES:], 0.0), s["vh"]], axis=0), s["bkg"])
          for s, mm in zip(st, mms)]
    out = []
    for s, p, q, tt in zip(st, ps, qs, tts):
        reff = s["r_t"] + tt[:, :LANES]
        out.append(dict(etot=s["etot"], p=p, q=q, reff=reff[:c] + reff[c:],
                        y0=jnp.where(head_a, tt[:c, LANES:], tt[c:, LANES:])))
    return out


def _rwkv_scan_kernel(*refs, seq_len, has_init, want_final):
    r_ref, v_ref, kk_ref, ld0_ref, b0_ref, kt0_ref, ld1_ref, b1_ref, kt1_ref = refs[:9]
    pos = 9
    if has_init:
        s0_ref = refs[pos]
        pos += 1
    y_ref = refs[pos]
    pos += 1
    if want_final:
        sf_ref = refs[pos]
        pos += 1
    g_scr, yf_scr, yb_scr = refs[pos:pos + 3]
    nc = seq_len // CHUNK
    ngroups = nc // SCAN_UNROLL
    per_dir = ((ld0_ref, b0_ref, kt0_ref), (ld1_ref, b1_ref, kt1_ref))
    y_dst = (yf_scr, yb_scr)
    nchain = 2 * SCAN_PAIRS

    def group(gi, states):
        where, units = [], []
        for u in range(SCAN_UNROLL):
            for pp in range(SCAN_PAIRS):
                lanes = slice(pp * LANES, (pp + 1) * LANES)
                for d in range(2):
                    ld_ref, b_ref, kt_ref = per_dir[d]
                    cidx = gi * SCAN_UNROLL + u
                    if d == 1:
                        cidx = nc - 1 - cidx
                    start = cidx * CHUNK
                    rw = slice(start, start + CHUNK) if isinstance(start, int) else pl.ds(
                        pl.multiple_of(start, CHUNK), CHUNK)
                    where.append((rw, lanes))
                    units.append(tuple(ref[rw, lanes].astype(F32) for ref in
                                       (ld_ref, kk_ref, b_ref, kt_ref, r_ref, v_ref)) + (d == 1,))
        pre = _scan_precompute(units)
        states = list(states)
        for u in range(SCAN_UNROLL):
            cur = pre[u * nchain:(u + 1) * nchain]
            ys = [_dot_nt(cu["reff"], g) for cu, g in zip(cur, states)]
            gp = [_dot(g, cu["p"]) for cu, g in zip(cur, states)]
            states = [states[ch] * cur[ch]["etot"] + gp[ch] + cur[ch]["q"] for ch in range(nchain)]
            for ch in range(nchain):
                rw, lanes = where[u * nchain + ch]
                y_dst[ch % 2][rw, lanes] = ys[ch] + cur[ch]["y0"]
        return states

    init = [_block_diag2(s0_ref[0, ch % 2, 2 * (ch // 2)], s0_ref[0, ch % 2, 2 * (ch // 2) + 1])
            if has_init else jnp.zeros((LANES, LANES), F32) for ch in range(nchain)]
    if ngroups == 1:
        final = group(0, init)
    else:
        for ch in range(nchain):
            g_scr[ch] = init[ch]

        def body(gi, carry):
            new = group(gi, [g_scr[ch] for ch in range(nchain)])
            for ch in range(nchain):
                g_scr[ch] = new[ch]
            return carry

        lax.fori_loop(0, ngroups, body, 0)
        final = [g_scr[ch] for ch in range(nchain)]
    y_ref[...] = (yf_scr[...] + yb_scr[...]).astype(y_ref.dtype)
    if want_final:
        for ch in range(nchain):
            sf_ref[0, ch % 2, 2 * (ch // 2)] = final[ch][:HS, :HS]
            sf_ref[0, ch % 2, 2 * (ch // 2) + 1] = final[ch][HS:, HS:]


def _rwkv_scan(proj, s_init, n_seq, seq_len, want_final):
    r, v, kk, ld0, b0, kt0, ld1, b1, kt1 = proj[:9]
    n = r.shape[0]
    npair = D // LANES
    width = SCAN_PAIRS * LANES
    blk = pl.BlockSpec((seq_len, width), lambda b, p: (b, p))
    st_spec = pl.BlockSpec((1, 2, 2 * SCAN_PAIRS, HS, HS), lambda b, p: (b, 0, p, 0, 0))
    in_specs = [blk] * 9
    args = [r, v, kk, ld0, b0, kt0, ld1, b1, kt1]
    has_init = s_init is not None
    if has_init:
        in_specs.append(st_spec)
        args.append(s_init)
    out_specs = [blk]
    out_shape = [jax.ShapeDtypeStruct((n, D), BF16)]
    if want_final:
        out_specs.append(st_spec)
        out_shape.append(jax.ShapeDtypeStruct((n_seq, 2, NH, HS, HS), F32))
    res = pl.pallas_call(
        functools.partial(_rwkv_scan_kernel, seq_len=seq_len, has_init=has_init, want_final=want_final),
        grid=(n_seq, npair // SCAN_PAIRS),
        in_specs=in_specs,
        out_specs=out_specs,
        out_shape=out_shape,
        scratch_shapes=[pltpu.VMEM((2 * SCAN_PAIRS, LANES, LANES), F32), pltpu.VMEM((seq_len, width), F32),
                        pltpu.VMEM((seq_len, width), F32)],
        compiler_params=_cparams(("arbitrary", "arbitrary")),
        name="rwkv_scan",
    )(*args)
    return res


def _rwkv_out_kernel(x_ref, m_ref, g_ref, y_ref, bonus_ref, gate_ref, lng_ref, lnb_ref, wo_ref, o_ref,
                     acc_scr, *, group_of_tile):
    j = pl.program_id(1)
    last = pl.num_programs(1) - 1

    def chunk():
        e = _head_indicator(RW_CW)
        mean = jnp.dot(y_ref[...], e, preferred_element_type=F32) * (1.0 / HS)
        yc = y_ref[...].astype(F32) - mean
        var = _dot(yc * yc, e) * (1.0 / HS)
        yn = yc * lax.rsqrt(var + GN_EPS) * lng_ref[...] + lnb_ref[...]
        yn = (yn + bonus_ref[...].astype(F32)) * gate_ref[...].astype(F32)
        return _dot(yn, wo_ref[0])

    @pl.when(j == 0)
    def _():
        acc_scr[...] = chunk()

    @pl.when((j > 0) & (j < last))
    def _():
        acc_scr[...] += chunk()

    @pl.when(j == last)
    def _():
        m = _mod_rows(m_ref, group_of_tile(pl.program_id(0)))
        g = g_ref[0]
        o_ref[...] = x_ref[...] + m[2:3] * _rms(acc_scr[...] + chunk(), g[1:2])


def _rwkv_out(src, dst, mod, norm_g, y, bonus, gate, p, layer, j, group_of_tile):
    nj = D // RW_CW
    assert nj >= 2
    tile = pl.BlockSpec((TM, RW_CW), lambda i, c: (i, c))
    vec = pl.BlockSpec((1, RW_CW), lambda i, c: (j, c))
    return _activation_call(
        functools.partial(_rwkv_out_kernel, group_of_tile=group_of_tile),
        grid=(src.ntiles, nj),
        in_specs=[
            pl.BlockSpec((TM, D), lambda i, c: (i + src.first, 0)),
            _mod_spec(),
            _normg_spec(layer, 2),
            tile, tile, tile, vec, vec,
            pl.BlockSpec((1, RW_CW, D), lambda i, c: (j, c, 0)),
        ],
        args=[src.array, mod, norm_g, y, bonus, gate, p["ln_g"], p["ln_b"], p["w_o"]],
        dst=dst, block_rows=TM, out_index=lambda i, c: (i + dst.first, 0),
        scratch_shapes=[pltpu.VMEM((TM, D), F32)],
        name="rwkv_out")


def _rope_tables(n):
    rows = n // GRID_W
    row = np.repeat(np.arange(rows), GRID_W)
    col = np.tile(np.arange(GRID_W), rows)
    pos = np.stack([row, col], axis=-1).astype(np.float64)
    quarter = D_ROPE // 4
    inv = ROPE_BASE ** (-np.arange(quarter, dtype=np.float64) / quarter)
    ang = pos[:, :, None] * inv
    cos = np.cos(ang)
    sin = np.sin(ang)
    cos_t = np.concatenate([cos, cos], axis=-1).reshape(n, D_ROPE)
    sin_t = np.concatenate([-sin, sin], axis=-1).reshape(n, D_ROPE)
    return cos_t.astype(np.float32), sin_t.astype(np.float32)


def _rope_swap_perm():
    quarter = D_ROPE // 4
    base = np.arange(D_ROPE)
    return np.where((base % (2 * quarter)) < quarter, base + quarter, base - quarter)


def _mla_proj_kernel(*refs, positional, group_of_tile):
    (x_ref, m_ref, g_ref, wdq_ref, gq_ref, wqn_ref, wqr_ref, wqs_ref, wdkv_ref, wkr_ref, wks_ref,
     gkv_ref) = refs[:12]
    pos = 12
    if positional:
        cosq_ref, sinq_ref, cosk_ref, sink_ref = refs[pos:pos + 4]
        pos += 4
    qn_o, qr_o, ckv_o, kr_o = refs[pos:pos + 4]
    m = _mod_rows(m_ref, group_of_tile(pl.program_id(0)))
    g = g_ref[0]
    h = (_rms(x_ref[...], g[0:1]) * (1.0 + m[1:2]) + m[0:1]).astype(BF16)
    ql = jnp.dot(h, wdq_ref[0].astype(BF16), preferred_element_type=F32)
    ql = (ql * lax.rsqrt(jnp.mean(ql * ql, axis=-1, keepdims=True) + EPS) * gq_ref[...]).astype(BF16)
    qn_o[...] = jnp.dot(ql, wqn_ref[...].astype(BF16), preferred_element_type=F32).astype(qn_o.dtype)
    qr = jnp.dot(ql, wqr_ref[...].astype(BF16), preferred_element_type=F32)
    ckv = jnp.dot(h, wdkv_ref[...].astype(BF16), preferred_element_type=F32)
    ckv_o[...] = ckv * lax.rsqrt(jnp.mean(ckv * ckv, axis=-1, keepdims=True) + EPS) * gkv_ref[...]
    kr = jnp.dot(h, wkr_ref[...].astype(BF16), preferred_element_type=F32)
    if positional:
        qs = jnp.dot(ql, wqs_ref[...].astype(BF16), preferred_element_type=F32)
        ks = jnp.dot(h, wks_ref[...].astype(BF16), preferred_element_type=F32)
        qr = qr * cosq_ref[...] + qs * sinq_ref[...]
        kr = kr * cosk_ref[...] + ks * sink_ref[...]
    qr_o[...] = qr.astype(qr_o.dtype)
    kr_o[...] = kr


def _mla_proj(src, mod, norm_g, p, layer, j, positional, group_of_tile):
    n = src.ntiles * TM
    full = lambda shape: pl.BlockSpec(shape, lambda i: (0,) * len(shape))
    in_specs = [
        pl.BlockSpec((TM, D), lambda i: (i + src.first, 0)),
        _mod_spec(),
        _normg_spec(layer, 1),
        pl.BlockSpec((1, D, Q_RANK), lambda i: (j, 0, 0)),
        pl.BlockSpec((1, Q_RANK), lambda i: (j, 0)),
        full((Q_RANK, MLA_H * D_NOPE)), full((Q_RANK, MLA_H * D_ROPE)), full((Q_RANK, MLA_H * D_ROPE)),
        full((D, KV_RANK)), full((D, D_ROPE)), full((D, D_ROPE)),
        pl.BlockSpec((1, KV_RANK), lambda i: (j, 0)),
    ]
    args = [src.array, mod, norm_g, p["w_dq"], p["g_q"], p["w_uq_nope"], p["w_uq_rope"], p["w_uq_rope_sw"],
            p["w_dkv_c"], p["w_dkv_r"], p["w_dkv_r_sw"], p["g_kv"]]
    if positional:
        cos_t, sin_t = _rope_tables(TM)
        in_specs += [full((TM, MLA_H * D_ROPE)), full((TM, MLA_H * D_ROPE)),
                     full((TM, D_ROPE)), full((TM, D_ROPE))]
        args += [jnp.asarray(np.tile(cos_t, (1, MLA_H))), jnp.asarray(np.tile(sin_t, (1, MLA_H))),
                 jnp.asarray(cos_t), jnp.asarray(sin_t)]
    outs = ((MLA_H * D_NOPE, BF16), (MLA_H * D_ROPE, BF16), (KV_RANK, F32), (D_ROPE, F32))
    return pl.pallas_call(
        functools.partial(_mla_proj_kernel, positional=positional, group_of_tile=group_of_tile),
        grid=(src.ntiles,),
        in_specs=in_specs,
        out_specs=[pl.BlockSpec((TM, w), lambda i: (i, 0)) for w, _ in outs],
        out_shape=[jax.ShapeDtypeStruct((n, w), dt) for w, dt in outs],
        compiler_params=_cparams(("arbitrary",)),
        name="mla_proj",
    )(*args)


def _mla_attn_kernel(x_ref, m_ref, g_ref, qn_ref, qr_ref, ckv_ref, kr_ref, wuk_ref, wuv_ref, wo_ref,
                     o_ref, kn_scr, vv_scr, oh_scr, *, nb, tq, k_len, group_of_tile):
    qi = pl.program_id(1)

    @pl.when(qi == 0)
    def _():
        ckv = ckv_ref[...].astype(BF16)
        kn_scr[...] = jnp.dot(ckv, wuk_ref[0].astype(BF16), preferred_element_type=F32).astype(BF16)
        vv_scr[...] = jnp.dot(ckv, wuv_ref[0].astype(BF16), preferred_element_type=F32).astype(BF16)

    units = [(b, hd) for b in range(nb) for hd in range(MLA_H)]

    def scores(unit):
        b, hd = unit
        qrows = slice(b * tq, (b + 1) * tq)
        krows = slice(b * k_len, (b + 1) * k_len)
        q = jnp.concatenate([qn_ref[qrows, hd * D_NOPE:(hd + 1) * D_NOPE].astype(BF16),
                             qr_ref[qrows, hd * D_ROPE:(hd + 1) * D_ROPE].astype(BF16)], axis=1)
        k = jnp.concatenate([kn_scr[krows, hd * D_NOPE:(hd + 1) * D_NOPE],
                             kr_ref[krows, :].astype(BF16)], axis=1)
        return _dot_nt(q, k) * MLA_SCALE

    pending = [scores(u) for u in units[:ATTN_LOOKAHEAD]]
    for idx, (b, hd) in enumerate(units):
        if idx + ATTN_LOOKAHEAD < len(units):
            pending.append(scores(units[idx + ATTN_LOOKAHEAD]))
        s = pending[idx]
        pexp = jnp.exp(s - jnp.max(s, axis=-1, keepdims=True))
        pv = jnp.dot(pexp.astype(BF16), vv_scr[b * k_len:(b + 1) * k_len, hd * D_V:(hd + 1) * D_V],
                     preferred_element_type=F32)
        oh_scr[b * tq:(b + 1) * tq, hd * D_V:(hd + 1) * D_V] = (
            pv / jnp.sum(pexp, axis=-1, keepdims=True)).astype(BF16)
    o = jnp.dot(oh_scr[...], wo_ref[0].astype(BF16), preferred_element_type=F32)
    m = _mod_rows(m_ref, group_of_tile(pl.program_id(0)))
    g = g_ref[0]
    o_ref[...] = x_ref[...] + m[2:3] * _rms(o, g[1:2])


def _mla_attn(src, dst, mod, norm_g, qn, qr, ckv_all, kr_all, p, layer, j, n_seq, q_len, k_len, nb, tq,
              group_of_step):
    nq = q_len // tq
    assert nb == 1 or nq == 1
    rows = nb * tq
    x_first, o_first = src.first * TM // rows, dst.first * TM // rows
    return _activation_call(
        functools.partial(_mla_attn_kernel, nb=nb, tq=tq, k_len=k_len, group_of_tile=group_of_step),
        grid=(n_seq // nb, nq),
        in_specs=[
            pl.BlockSpec((rows, D), lambda s, q: (s * nq + q + x_first, 0)),
            _mod_spec(),
            pl.BlockSpec((1, 4, D), lambda s, q: (layer, 0, 0)),
            pl.BlockSpec((nb * tq, MLA_H * D_NOPE), lambda s, q: (s * nq + q, 0)),
            pl.BlockSpec((nb * tq, MLA_H * D_ROPE), lambda s, q: (s * nq + q, 0)),
            pl.BlockSpec((nb * k_len, KV_RANK), lambda s, q: (s, 0)),
            pl.BlockSpec((nb * k_len, D_ROPE), lambda s, q: (s, 0)),
            pl.BlockSpec((1, KV_RANK, MLA_H * D_NOPE), lambda s, q: (j, 0, 0)),
            pl.BlockSpec((1, KV_RANK, MLA_H * D_V), lambda s, q: (j, 0, 0)),
            pl.BlockSpec((1, MLA_H * D_V, D), lambda s, q: (j, 0, 0)),
        ],
        args=[src.array, mod, norm_g, qn, qr, ckv_all, kr_all, p["w_uk"], p["w_uv"], p["w_o"]],
        dst=dst, block_rows=rows, out_index=lambda s, q: (s * nq + q + o_first, 0),
        scratch_shapes=[pltpu.VMEM((nb * k_len, MLA_H * D_NOPE), BF16),
                        pltpu.VMEM((nb * k_len, MLA_H * D_V), BF16),
                        pltpu.VMEM((nb * tq, MLA_H * D_V), BF16)],
        name="mla_attn")


def kernel(x_prompt, x_sample, state_rwkv, cache_mla_ckv, cache_mla_krope, c, c_ctx, mod_w, mod_b, norm_g,
           mlp_w1, mlp_w2, fft_w_out, conv_w_in, conv_w, conv_w_out, rwkv_mu, rwkv_w_r, rwkv_w_k, rwkv_w_v,
           rwkv_w_o, rwkv_w0, rwkv_w_l1, rwkv_w_l2, rwkv_a0, rwkv_a_l1, rwkv_a_l2, rwkv_g_l1, rwkv_g_l2,
           rwkv_k_k, rwkv_k_a, rwkv_r_k, rwkv_ln_g, rwkv_ln_b, mla_w_dq, mla_g_q, mla_w_uq, mla_w_dkv,
           mla_g_kv, mla_w_uk, mla_w_uv, mla_w_o):
    batch, seq, _ = x_prompt.shape
    dec_batch, dec_seq, _ = x_sample.shape
    past_len = cache_mla_ckv.shape[2]
    assert (batch * seq) % TM == 0 and TM % seq == 0 and dec_seq == TM and seq % CHUNK == 0

    np_tiles = batch * seq // TM
    ns_tiles = dec_batch * dec_seq // TM
    total_rows = (np_tiles + ns_tiles) * TM
    cs = jnp.concatenate([c_ctx[None, :], c, jnp.zeros((MOD_ROWS - 1 - dec_batch, D), F32)], axis=0)
    mod_b3 = mod_b.reshape(DEPTH, 1, N_MOD * D)
    mod = _modulation(cs, mod_w, mod_b3, 0)

    grp_p = lambda i: 0
    grp_s = lambda i: 1 + i
    grp_all = lambda i: jnp.maximum(i - (np_tiles - 1), 0)
    new_rwkv, new_ckv, new_krope = [], [], []
    streams = [(True, seq, batch, grp_p), (False, dec_seq, dec_batch, grp_s)]
    cur = [_Rows(x_prompt.reshape(batch * seq, D), 0, np_tiles),
           _Rows(x_sample.reshape(dec_batch * dec_seq, D), 0, ns_tiles)]

    for i in range(DEPTH):
        kind, j = i % 4, i // 4
        for idx, (is_prompt, slen, nseq, grp) in enumerate(streams):
            src = cur[idx]
            sharing = src.array.shape[0] == total_rows
            dst = _Dest(total_rows, src.first, True) if sharing else _Dest(src.ntiles * TM, 0, False)
            shared = None
            if kind == 0:
                shared = _fnet(src, dst, mod, norm_g, fft_w_out, i, j, slen, grp)
            elif kind == 1:
                shared = _conv(src, dst, mod, norm_g, conv_w_in, conv_w, conv_w_out, i, j, slen, grp)
            elif kind == 2:
                p = dict(mu=rwkv_mu, w_r=rwkv_w_r, w_k=rwkv_w_k, w_v=rwkv_w_v, w_o=rwkv_w_o, w0=rwkv_w0,
                         w_l1=rwkv_w_l1, w_l2=rwkv_w_l2, a0=rwkv_a0, a_l1=rwkv_a_l1, a_l2=rwkv_a_l2,
                         g_l1=rwkv_g_l1, g_l2=rwkv_g_l2, k_k=rwkv_k_k, k_a=rwkv_k_a,
                         r_k=rwkv_r_k.reshape(-1, D), ln_g=rwkv_ln_g, ln_b=rwkv_ln_b)
                proj = _rwkv_proj(src, mod, norm_g, p, i, j, slen, grp)
                if is_prompt:
                    y, s_fin = _rwkv_scan(proj, None, nseq, slen, True)
                    new_rwkv.append(s_fin)
                else:
                    (y,) = _rwkv_scan(proj, state_rwkv[:, j], nseq, slen, False)
                shared = _rwkv_out(src, dst, mod, norm_g, y, proj[10], proj[9], p, i, j, grp)
            else:
                perm = _rope_swap_perm()
                w_uq = mla_w_uq[j].reshape(Q_RANK, MLA_H, D_NOPE + D_ROPE)
                w_uq_rope = w_uq[:, :, D_NOPE:]
                w_dkv_r = mla_w_dkv[j][:, KV_RANK:]
                p = dict(w_dq=mla_w_dq, g_q=mla_g_q, g_kv=mla_g_kv, w_uk=mla_w_uk, w_uv=mla_w_uv, w_o=mla_w_o,
                         w_uq_nope=w_uq[:, :, :D_NOPE].reshape(Q_RANK, MLA_H * D_NOPE),
                         w_uq_rope=w_uq_rope.reshape(Q_RANK, MLA_H * D_ROPE),
                         w_uq_rope_sw=w_uq_rope[:, :, perm].reshape(Q_RANK, MLA_H * D_ROPE),
                         w_dkv_c=mla_w_dkv[j][:, :KV_RANK], w_dkv_r=w_dkv_r, w_dkv_r_sw=w_dkv_r[:, perm])
                qn, qr, ckv, kr = _mla_proj(src, mod, norm_g, p, i, j, not is_prompt, grp)
                if is_prompt:
                    new_ckv.append(ckv.reshape(batch, seq, KV_RANK))
                    new_krope.append(kr.reshape(batch, seq, D_ROPE))
                    shared = _mla_attn(src, dst, mod, norm_g, qn, qr, ckv, kr, p, i, j, nseq, slen, slen,
                                       TM // slen, slen, lambda s: 0)
                else:
                    klen = past_len + slen
                    ckv_all = jnp.concatenate([cache_mla_ckv[:, j], ckv.reshape(nseq, slen, KV_RANK)], axis=1)
                    kr_all = jnp.concatenate([cache_mla_krope[:, j], kr.reshape(nseq, slen, D_ROPE)], axis=1)
                    shared = _mla_attn(src, dst, mod, norm_g, qn, qr, ckv_all.reshape(nseq * klen, KV_RANK),
                                       kr_all.reshape(nseq * klen, D_ROPE), p, i, j, nseq, slen, klen, 1, 256,
                                       lambda s: 1 + s)
            cur[idx] = _Rows(shared, dst.first, src.ntiles)
            if sharing:
                cur[1 - idx] = cur[1 - idx]._replace(array=shared)
        if i < DEPTH - 1:
            srcs = [_Rows(cur[0].array, 0, np_tiles + ns_tiles)] if cur[0].array is cur[1].array else cur
            both, mod = _mlp(srcs, _Dest(total_rows, 0, False), mod, norm_g, mlp_w1, mlp_w2, i, grp_all,
                             next_mod=(cs, mod_w, mod_b3, i + 1))
            cur = [_Rows(both, 0, np_tiles), _Rows(both, np_tiles, ns_tiles)]
        else:
            y_prompt, y_sample = [
                _mlp([cur[idx]], _Dest(cur[idx].ntiles * TM, 0, False), mod, norm_g, mlp_w1, mlp_w2, i,
                     streams[idx][3]) for idx in range(2)]

    return (y_prompt.reshape(batch, seq, D), y_sample.reshape(dec_batch, dec_seq, D),
            jnp.stack(new_rwkv, axis=1), jnp.stack(new_ckv, axis=1), jnp.stack(new_krope, axis=1))
```

```python
import functools
from typing import NamedTuple

import numpy as np
import jax
import jax.numpy as jnp
from jax import lax
from jax.experimental import pallas as pl
from jax.experimental.pallas import tpu as pltpu

D = 1024
DEPTH = 4
N_MOD = 6
D_FF = 4 * D
EPS = 1e-6
GRID_W = 64
FFT_GROUPS = 8
FFT_GW = D // FFT_GROUPS
HS = 64
NH = D // HS
LORA_W = 64
LORA_A = 64
LORA_G = 128
GN_EPS = 64e-5
DECAY_SCALE = float(np.exp(-0.5))
MLA_H = 8
D_NOPE = 128
D_ROPE = 64
D_V = 128
KV_RANK = 256
Q_RANK = 384
ROPE_BASE = 10000.0
MLA_SCALE = (D_NOPE + D_ROPE) ** -0.5

F32 = jnp.float32
BF16 = jnp.bfloat16

TM = 1024
LANES = 128
CHUNK = 64
SCAN_UNROLL = 4
SCAN_PAIRS = 2
ATTN_LOOKAHEAD = 2
VMEM_LIMIT = 58 * 1024 * 1024
VMEM_MIB = {"modulation": 20, "fourier_mix": 44, "mlp": 58, "short_conv": 50, "rwkv_proj": 58,
            "rwkv_scan": 32, "rwkv_out": 42, "mla_proj": 40, "mla_attn": 46}


def _cparams(sem, name):
    limit = VMEM_MIB[name] * 1024 * 1024
    assert limit <= VMEM_LIMIT
    return pltpu.CompilerParams(dimension_semantics=sem, vmem_limit_bytes=limit)


def _dot(a, b):
    return jnp.dot(a.astype(BF16), b.astype(BF16), preferred_element_type=F32)


def _dot_nt(a, b):
    return lax.dot_general(a.astype(BF16), b.astype(BF16), (((1,), (1,)), ((), ())),
                           preferred_element_type=F32)


def _dot_tn(a, b):
    return lax.dot_general(a.astype(BF16), b.astype(BF16), (((0,), (0,)), ((), ())),
                           preferred_element_type=F32)


def _block_diag2(a, b):
    za = jnp.zeros((a.shape[0], b.shape[1]), a.dtype)
    zb = jnp.zeros((b.shape[0], a.shape[1]), a.dtype)
    return jnp.concatenate([jnp.concatenate([a, za], axis=1), jnp.concatenate([zb, b], axis=1)], axis=0)


def _cumsum_rows(x, reverse):
    n = x.shape[0]
    idx = lax.broadcasted_iota(jnp.int32, x.shape, 0)
    s = 1
    while s < n:
        if reverse:
            x = x + jnp.where(idx < n - s, pltpu.roll(x, n - s, 0), 0.0)
        else:
            x = x + jnp.where(idx >= s, pltpu.roll(x, s, 0), 0.0)
        s *= 2
    return x


def _rms(x, g):
    return x * lax.rsqrt(jnp.mean(x * x, axis=-1, keepdims=True) + EPS) * g


def _sigmoid(x):
    return 1.0 / (1.0 + jnp.exp(-x))


def _head_indicator(n):
    r = lax.broadcasted_iota(jnp.int32, (n, n), 0) // HS
    c = lax.broadcasted_iota(jnp.int32, (n, n), 1) // HS
    return jnp.where(r == c, 1.0, 0.0).astype(BF16)


def _shift_rows(z, seq_len):
    n, w = z.shape
    pos = lax.broadcasted_iota(jnp.int32, (n, LANES), 0) % seq_len
    keep_prev = jnp.concatenate([jnp.where(pos == 0, 0.0, 1.0)] * (w // LANES), axis=1)
    keep_next = jnp.concatenate([jnp.where(pos == seq_len - 1, 0.0, 1.0)] * (w // LANES), axis=1)
    return pltpu.roll(z, 1, 0) * keep_prev, pltpu.roll(z, n - 1, 0) * keep_next


MOD_TN = 1536
MOD_ROWS = 8


def _mod_block(cs_ref, w_ref, b_ref):
    cs = cs_ref[...]
    return _dot(cs * _sigmoid(cs), w_ref[0]) + b_ref[0]


def _mod_kernel(cs_ref, w_ref, b_ref, o_ref):
    o_ref[...] = _mod_block(cs_ref, w_ref, b_ref)


def _modulation(cs, mod_w, mod_b3, layer):
    nj = (N_MOD * D) // MOD_TN
    out = pl.pallas_call(
        _mod_kernel,
        grid=(nj,),
        in_specs=[
            pl.BlockSpec((MOD_ROWS, D), lambda j: (0, 0)),
            pl.BlockSpec((1, D, MOD_TN), lambda j: (layer, 0, j)),
            pl.BlockSpec((1, 1, MOD_TN), lambda j: (layer, 0, j)),
        ],
        out_specs=pl.BlockSpec((MOD_ROWS, MOD_TN), lambda j: (0, j)),
        out_shape=jax.ShapeDtypeStruct((MOD_ROWS, N_MOD * D), F32),
        compiler_params=_cparams(("arbitrary",), "modulation"),
        name="modulation",
    )(cs, mod_w, mod_b3)
    return out.reshape(1, MOD_ROWS, N_MOD, D)


class _Rows(NamedTuple):
    array: jax.Array
    first: int
    ntiles: int


class _Dest(NamedTuple):
    rows: int
    first: int
    inplace: bool


def _activation_call(kernel, *, grid, in_specs, args, dst, block_rows, out_index, scratch_shapes, name,
                     extra_out=None):
    assert not dst.inplace or args[0].shape[0] == dst.rows
    out_specs = pl.BlockSpec((block_rows, D), out_index)
    out_shape = jax.ShapeDtypeStruct((dst.rows, D), F32)
    if extra_out is not None:
        out_specs, out_shape = [out_specs, extra_out[0]], [out_shape, extra_out[1]]
    return pl.pallas_call(
        kernel, grid=grid, in_specs=in_specs, out_specs=out_specs, out_shape=out_shape,
        scratch_shapes=scratch_shapes, input_output_aliases={0: 0} if dst.inplace else {},
        compiler_params=_cparams(("arbitrary",) * len(grid), name), name=name)(*args)


def _mod_spec(layer, group_of_tile, ngrid):
    del layer
    if ngrid == 1:
        return pl.BlockSpec((1, 1, N_MOD, D), lambda i: (0, group_of_tile(i), 0, 0))
    return pl.BlockSpec((1, 1, N_MOD, D), lambda i, j: (0, group_of_tile(i), 0, 0))


def _normg_spec(layer, ngrid):
    if ngrid == 1:
        return pl.BlockSpec((1, 4, D), lambda i: (layer, 0, 0))
    return pl.BlockSpec((1, 4, D), lambda i, j: (layer, 0, 0))


MLP_FC = 1024


def _mlp_kernel(*refs, n_first, with_next_mod):
    if n_first is None:
        (x_ref,), rest = refs[:1], refs[1:]
        read_x = lambda: x_ref[...]
    else:
        (xa_ref, xb_ref), rest = refs[:2], refs[2:]
        read_x = lambda: jnp.where(pl.program_id(0) < n_first, xa_ref[...], xb_ref[...])
    if with_next_mod:
        m_ref, g_ref, w1_ref, w2_ref, cs_ref, mw_ref, mb_ref, o_ref, mo_ref, h_scr, acc_scr = rest
        mo_ref[...] = _mod_block(cs_ref, mw_ref, mb_ref)
    else:
        m_ref, g_ref, w1_ref, w2_ref, o_ref, h_scr, acc_scr = rest
    j = pl.program_id(1)
    m = m_ref[0, 0]
    g = g_ref[0]

    last = pl.num_programs(1) - 1

    def chunk():
        a = jnp.dot(h_scr[...], w1_ref[0].astype(BF16), preferred_element_type=F32)
        a = jnp.maximum(a, 0.0)
        return _dot(a * a, w2_ref[0])

    @pl.when(j == 0)
    def _():
        h = _rms(read_x(), g[2:3]) * (1.0 + m[4:5]) + m[3:4]
        h_scr[...] = h.astype(BF16)
        acc_scr[...] = chunk()

    @pl.when((j > 0) & (j < last))
    def _():
        acc_scr[...] += chunk()

    @pl.when(j == last)
    def _():
        f = acc_scr[...] + chunk()
        o_ref[...] = read_x() + m[5:6] * _rms(f, g[3:4])


def _mlp(srcs, dst, mod, norm_g, w1, w2, layer, group_of_tile, next_mod=None):
    nj = D_FF // MLP_FC
    assert nj >= 2
    if len(srcs) == 1:
        (src,) = srcs
        ntiles, n_first = src.ntiles, None
        x_specs = [pl.BlockSpec((TM, D), lambda i, j: (i + src.first, 0))]
    else:
        sa, sb = srcs
        ntiles, n_first = sa.ntiles + sb.ntiles, sa.ntiles
        x_specs = [pl.BlockSpec((TM, D), lambda i, j: (jnp.minimum(i, sa.ntiles - 1) + sa.first, 0)),
                   pl.BlockSpec((TM, D), lambda i, j: (jnp.maximum(i - sa.ntiles, 0) + sb.first, 0))]
    in_specs = x_specs + [
        _mod_spec(layer, group_of_tile, 2),
        _normg_spec(layer, 2),
        pl.BlockSpec((1, D, MLP_FC), lambda i, j: (layer, 0, j)),
        pl.BlockSpec((1, MLP_FC, D), lambda i, j: (layer, j, 0)),
    ]
    args = [s.array for s in srcs] + [mod, norm_g, w1, w2]
    extra_out = None
    if next_mod is not None:
        cs, mod_w, mod_b3, nxt = next_mod
        cols = (N_MOD * D) // (ntiles * nj)
        assert cols * ntiles * nj == N_MOD * D and cols % LANES == 0
        in_specs += [pl.BlockSpec((MOD_ROWS, D), lambda i, j: (0, 0)),
                     pl.BlockSpec((1, D, cols), lambda i, j: (nxt, 0, i * nj + j)),
                     pl.BlockSpec((1, 1, cols), lambda i, j: (nxt, 0, i * nj + j))]
        args += [cs, mod_w, mod_b3]
        extra_out = (pl.BlockSpec((MOD_ROWS, cols), lambda i, j: (0, i * nj + j)),
                     jax.ShapeDtypeStruct((MOD_ROWS, N_MOD * D), F32))
    res = _activation_call(
        functools.partial(_mlp_kernel, n_first=n_first, with_next_mod=next_mod is not None),
        grid=(ntiles, nj), in_specs=in_specs, args=args,
        dst=dst, block_rows=TM, out_index=lambda i, j: (i + dst.first, 0),
        scratch_shapes=[pltpu.VMEM((TM, D), BF16), pltpu.VMEM((TM, D), F32)],
        name="mlp", extra_out=extra_out)
    if next_mod is None:
        return res
    return res[0], res[1].reshape(1, MOD_ROWS, N_MOD, D)


def _dft_mats(n):
    idx = np.arange(n, dtype=np.int64)
    ang = (2.0 * np.pi / n) * ((idx[:, None] * idx[None, :]) % n).astype(np.float64)
    scale = 1.0 / np.sqrt(n)
    return np.cos(ang) * scale, np.sin(ang) * scale


def _fnet_kernel(x_ref, m_ref, g_ref, cs_ref, cn_ref, sn_ref, w_ref, o_ref, p_scr, q_scr, f_scr,
                 *, seq_len):
    m = m_ref[0, 0]
    g = g_ref[0]
    x = x_ref[...]
    h = (_rms(x, g[0:1]) * (1.0 + m[1:2]) + m[0:1]).astype(BF16)
    cs = cs_ref[...].astype(BF16)
    for gi in range(FFT_GROUPS):
        pq = jnp.dot(h[:, gi * FFT_GW:(gi + 1) * FFT_GW], cs, preferred_element_type=F32)
        p_scr[:, gi * FFT_GW:(gi + 1) * FFT_GW] = pq[:, :FFT_GW].astype(BF16)
        q_scr[:, gi * FFT_GW:(gi + 1) * FFT_GW] = pq[:, FFT_GW:].astype(BF16)
    cn = cn_ref[...].astype(BF16)
    sn = sn_ref[...].astype(BF16)
    for s in range(TM // seq_len):
        rows = slice(s * seq_len, (s + 1) * seq_len)
        f = (jnp.dot(cn, p_scr[rows, :], preferred_element_type=F32)
             - jnp.dot(sn, q_scr[rows, :], preferred_element_type=F32))
        f_scr[rows, :] = f.astype(BF16)
    o = jnp.dot(f_scr[...], w_ref[0].astype(BF16), preferred_element_type=F32)
    o_ref[...] = x + m[2:3] * _rms(o, g[1:2])


def _fnet(src, dst, mod, norm_g, w_out, layer, j, seq_len, group_of_tile):
    cg, sg = _dft_mats(FFT_GW)
    cs = jnp.asarray(np.concatenate([cg, sg], axis=1), F32)
    cn_np, sn_np = _dft_mats(seq_len)
    cn = jnp.asarray(cn_np, F32)
    sn = jnp.asarray(sn_np, F32)
    return _activation_call(
        functools.partial(_fnet_kernel, seq_len=seq_len),
        grid=(src.ntiles,),
        in_specs=[
            pl.BlockSpec((TM, D), lambda i: (i + src.first, 0)),
            _mod_spec(layer, group_of_tile, 1),
            _normg_spec(layer, 1),
            pl.BlockSpec((FFT_GW, 2 * FFT_GW), lambda i: (0, 0)),
            pl.BlockSpec((seq_len, seq_len), lambda i: (0, 0)),
            pl.BlockSpec((seq_len, seq_len), lambda i: (0, 0)),
            pl.BlockSpec((1, D, D), lambda i: (j, 0, 0)),
        ],
        args=[src.array, mod, norm_g, cs, cn, sn, w_out],
        dst=dst, block_rows=TM, out_index=lambda i: (i + dst.first, 0),
        scratch_shapes=[pltpu.VMEM((TM, D), BF16), pltpu.VMEM((TM, D), BF16),
                        pltpu.VMEM((TM, D), BF16)],
        name="fourier_mix")


CONV_CW = 512


def _conv_kernel(x_ref, m_ref, g_ref, wb_ref, wc_ref, wu_ref, cw_ref, wo_ref, o_ref, h_scr, acc_scr,
                 *, seq_len):
    j = pl.program_id(1)
    m = m_ref[0, 0]
    g = g_ref[0]

    def chunk():
        h = h_scr[...]
        hw = CONV_CW // 2
        proj = []
        for s in range(2):
            cols = slice(s * hw, (s + 1) * hw)
            proj.append([jnp.dot(h, w_ref[0, :, cols].astype(BF16), preferred_element_type=F32)
                         for w_ref in (wb_ref, wc_ref, wu_ref)])
        out = None
        for s in range(2):
            cols = slice(s * hw, (s + 1) * hw)
            bg, cg, u = proj[s]
            z = cg * u
            z_prev, z_next = _shift_rows(z, seq_len)
            conv = z_prev * cw_ref[0, 0:1, cols] + z * cw_ref[0, 1:2, cols] + z_next * cw_ref[0, 2:3, cols]
            part = _dot(bg * conv, wo_ref[0, cols, :])
            out = part if out is None else out + part
        return out

    last = pl.num_programs(1) - 1

    @pl.when(j == 0)
    def _():
        h = _rms(x_ref[...], g[0:1]) * (1.0 + m[1:2]) + m[0:1]
        h_scr[...] = h.astype(BF16)
        acc_scr[...] = chunk()

    @pl.when((j > 0) & (j < last))
    def _():
        acc_scr[...] += chunk()

    @pl.when(j == last)
    def _():
        o_ref[...] = x_ref[...] + m[2:3] * _rms(acc_scr[...] + chunk(), g[1:2])


def _conv(src, dst, mod, norm_g, w_in, w_conv, w_out, layer, j, seq_len, group_of_tile):
    nj = D // CONV_CW
    assert nj >= 2
    return _activation_call(
        functools.partial(_conv_kernel, seq_len=seq_len),
        grid=(src.ntiles, nj),
        in_specs=[
            pl.BlockSpec((TM, D), lambda i, c: (i + src.first, 0)),
            _mod_spec(layer, group_of_tile, 2),
            _normg_spec(layer, 2),
            pl.BlockSpec((1, D, CONV_CW), lambda i, c: (j, 0, c)),
            pl.BlockSpec((1, D, CONV_CW), lambda i, c: (j, 0, nj + c)),
            pl.BlockSpec((1, D, CONV_CW), lambda i, c: (j, 0, 2 * nj + c)),
            pl.BlockSpec((1, 3, CONV_CW), lambda i, c: (j, 0, c)),
            pl.BlockSpec((1, CONV_CW, D), lambda i, c: (j, c, 0)),
        ],
        args=[src.array, mod, norm_g, w_in, w_in, w_in, w_conv, w_out],
        dst=dst, block_rows=TM, out_index=lambda i, c: (i + dst.first, 0),
        scratch_shapes=[pltpu.VMEM((TM, D), BF16), pltpu.VMEM((TM, D), F32)],
        name="short_conv")


RWP_CW = 256
RW_OUT_DTYPES = (BF16, BF16, BF16, F32, BF16, BF16, F32, BF16, BF16, BF16, BF16)
RW_CW = 512


def _rwkv_proj_kernel(x_ref, m_ref, g_ref, mu_ref, wr_ref, wk_ref, wv_ref, w0_ref, wl1_ref, wl2_ref,
                      a0_ref, al1_ref, al2_ref, gl1_ref, gl2_ref, kk_ref, ka_ref, rk_ref,
                      r_o, v_o, kk_o, ld0_o, b0_o, kt0_o, ld1_o, b1_o, kt1_o, g_o, bonus_o,
                      xr_scr, xk_scr, xv_scr, tw_scr, ta_scr, sg_scr, *, seq_len):
    j = pl.program_id(1)

    def chunk():
        e = _head_indicator(RWP_CW)
        r = jnp.dot(xr_scr[...], wr_ref[0].astype(BF16), preferred_element_type=F32)
        k = jnp.dot(xk_scr[...], wk_ref[0].astype(BF16), preferred_element_type=F32)
        v = jnp.dot(xv_scr[...], wv_ref[0].astype(BF16), preferred_element_type=F32)
        g_o[...] = jnp.dot(sg_scr[...], gl2_ref[0].astype(BF16), preferred_element_type=F32).astype(g_o.dtype)
        zws = [jnp.dot(tw_scr[d], wl2_ref[0, d].astype(BF16), preferred_element_type=F32) for d in range(2)]
        zas = [jnp.dot(ta_scr[d], al2_ref[0, d].astype(BF16), preferred_element_type=F32) for d in range(2)]
        kk = k * kk_ref[...]
        kk = kk * lax.rsqrt(_dot(kk * kk, e) + 1e-12)
        r_o[...] = r.astype(r_o.dtype)
        v_o[...] = v.astype(v_o.dtype)
        kk_o[...] = kk.astype(kk_o.dtype)
        k_scaled = k * ka_ref[...]
        kt_sum = None
        for d, (ld_o, b_o, kt_o) in enumerate(((ld0_o, b0_o, kt0_o), (ld1_o, b1_o, kt1_o))):
            ld_o[...] = -DECAY_SCALE * _sigmoid(w0_ref[0, d:d + 1, :] + zws[d])
            a = _sigmoid(a0_ref[0, d:d + 1, :] + zas[d])
            kt = k + k_scaled * (a - 1.0)
            b_o[...] = (kk * a).astype(b_o.dtype)
            kt_o[...] = kt.astype(kt_o.dtype)
            kt_sum = kt if kt_sum is None else kt_sum + kt
        bonus_o[...] = (_dot(r * rk_ref[...] * kt_sum, e) * v).astype(bonus_o.dtype)

    @pl.when(j == 0)
    def _():
        m = m_ref[0, 0]
        g = g_ref[0]
        mu = mu_ref[0]
        h = _rms(x_ref[...], g[0:1]) * (1.0 + m[1:2]) + m[0:1]
        h_prev, h_next = _shift_rows(h, seq_len)
        dx = 0.5 * (h_prev + h_next) - h
        xr_scr[...] = (h + dx * mu[0:1]).astype(BF16)
        xk_scr[...] = (h + dx * mu[2:3]).astype(BF16)
        xv_scr[...] = (h + dx * mu[3:4]).astype(BF16)
        xw = (h + dx * mu[1:2]).astype(BF16)
        xa = (h + dx * mu[4:5]).astype(BF16)
        xg = (h + dx * mu[5:6]).astype(BF16)
        for d in range(2):
            tw_scr[d] = jnp.tanh(_dot(xw, wl1_ref[0, d])).astype(BF16)
            ta_scr[d] = _dot(xa, al1_ref[0, d]).astype(BF16)
        sg_scr[...] = _sigmoid(_dot(xg, gl1_ref[0])).astype(BF16)
        chunk()

    @pl.when(j > 0)
    def _():
        chunk()


def _rwkv_proj(src, mod, norm_g, p, layer, j, seq_len, group_of_tile):
    n = src.ntiles * TM
    nj = D // RWP_CW
    tile = pl.BlockSpec((TM, RWP_CW), lambda i, c: (i, c))
    wcol = lambda: pl.BlockSpec((1, D, RWP_CW), lambda i, c: (j, 0, c))
    vec = lambda: pl.BlockSpec((1, RWP_CW), lambda i, c: (j, c))
    return pl.pallas_call(
        functools.partial(_rwkv_proj_kernel, seq_len=seq_len),
        grid=(src.ntiles, nj),
        in_specs=[
            pl.BlockSpec((TM, D), lambda i, c: (i + src.first, 0)),
            _mod_spec(layer, group_of_tile, 2),
            _normg_spec(layer, 2),
            pl.BlockSpec((1, 6, D), lambda i, c: (j, 0, 0)),
            wcol(), wcol(), wcol(),
            pl.BlockSpec((1, 2, RWP_CW), lambda i, c: (j, 0, c)),
            pl.BlockSpec((1, 2, D, LORA_W), lambda i, c: (j, 0, 0, 0)),
            pl.BlockSpec((1, 2, LORA_W, RWP_CW), lambda i, c: (j, 0, 0, c)),
            pl.BlockSpec((1, 2, RWP_CW), lambda i, c: (j, 0, c)),
            pl.BlockSpec((1, 2, D, LORA_A), lambda i, c: (j, 0, 0, 0)),
            pl.BlockSpec((1, 2, LORA_A, RWP_CW), lambda i, c: (j, 0, 0, c)),
            pl.BlockSpec((1, D, LORA_G), lambda i, c: (j, 0, 0)),
            pl.BlockSpec((1, LORA_G, RWP_CW), lambda i, c: (j, 0, c)),
            vec(), vec(), vec(),
        ],
        out_specs=[tile] * len(RW_OUT_DTYPES),
        out_shape=[jax.ShapeDtypeStruct((n, D), dt) for dt in RW_OUT_DTYPES],
        scratch_shapes=[pltpu.VMEM((TM, D), BF16), pltpu.VMEM((TM, D), BF16), pltpu.VMEM((TM, D), BF16),
                        pltpu.VMEM((2, TM, LORA_W), BF16), pltpu.VMEM((2, TM, LORA_A), BF16),
                        pltpu.VMEM((TM, LORA_G), BF16)],
        compiler_params=_cparams(("arbitrary", "arbitrary"), "rwkv_proj"),
        name="rwkv_proj",
    )(src.array, mod, norm_g, p["mu"], p["w_r"], p["w_k"], p["w_v"], p["w0"], p["w_l1"], p["w_l2"],
      p["a0"], p["a_l1"], p["a_l2"], p["g_l1"], p["g_l2"], p["k_k"], p["k_a"], p["r_k"])


def _scan_precompute(units):
    c = CHUNK
    c2 = 2 * c
    row = lax.broadcasted_iota(jnp.int32, (c2, c2), 0)
    col = lax.broadcasted_iota(jnp.int32, (c2, c2), 1)
    head_a = lax.broadcasted_iota(jnp.int32, (c, LANES), 1) < HS
    own_lanes = jnp.concatenate([head_a, jnp.logical_not(head_a)], axis=0)
    eye = jnp.where(row == col, 1.0, 0.0)

    def two_heads(t):
        return jnp.concatenate([jnp.where(head_a, t, 0.0), jnp.where(head_a, 0.0, t)], axis=0)

    def causal(reverse):
        if reverse:
            return (col % c) > (row % c), (col % c) >= (row % c)
        return (col % c) < (row % c), (col % c) <= (row % c)

    masks = {rev: causal(rev) for rev in sorted({u[6] for u in units})}

    cums = [_cumsum_rows(u[0], u[6]) for u in units]
    st = []
    for (ld, kk, beta, kt, r, v, rev), cum in zip(units, cums):
        tot = cum[0:1] if rev else cum[c - 1:c]
        ginv = jnp.exp(-cum)
        tail = jnp.exp(tot - cum)
        st.append(dict(
            rev=rev, etot=jnp.exp(tot),
            a_t=two_heads(-kk * jnp.exp(cum - ld)), r_t=two_heads(r * jnp.exp(cum)),
            bk=jnp.concatenate([two_heads(beta * ginv), two_heads(kt * ginv)], axis=0),
            bkg=jnp.concatenate([two_heads(beta * tail), two_heads(kt * tail)], axis=0),
            v2=jnp.concatenate([v, v], axis=0), vh=two_heads(v)))
    grams = [_dot_nt(jnp.concatenate([s["a_t"], s["r_t"]], axis=0), s["bk"]) for s in st]
    for s, gram in zip(st, grams):
        strict, incl = masks[s["rev"]]
        s["l_ab"] = jnp.where(strict, gram[:c2, :c2], 0.0)
        s["l_ak"] = jnp.where(strict, gram[:c2, c2:], 0.0)
        s["t_rb"] = jnp.where(incl, gram[c2:, :c2], 0.0)
        s["t_rk"] = jnp.where(incl, gram[c2:, c2:], 0.0)
    lvs = [_dot(s["l_ak"], s["v2"]) for s in st]
    minvs = [eye for _ in st]
    b = 1
    while b < c:
        same = (row // (2 * b)) == (col // (2 * b))
        es = []
        for s in st:
            first, second = (col % (2 * b)) < b, (row % (2 * b)) >= b
            if s["rev"]:
                first, second = (row % (2 * b)) < b, (col % (2 * b)) >= b
            es.append(jnp.where(same & first & second, s["l_ab"], 0.0))
        if b == 1:
            minvs = [m + e for m, e in zip(minvs, es)]
        else:
            half = [_dot(m, e) for m, e in zip(minvs, es)]
            minvs = [m + _dot(h, m) for m, h in zip(minvs, half)]
        b *= 2
    mms = [_dot(m, jnp.concatenate([s["a_t"], lv], axis=1)) for m, s, lv in zip(minvs, st, lvs)]
    zero = jnp.zeros((c2, LANES), BF16)
    tts = [_dot(jnp.concatenate([s["t_rb"], s["t_rk"]], axis=1),
                jnp.concatenate([mm.astype(BF16), jnp.concatenate([zero, s["v2"].astype(BF16)], axis=1)], axis=0))
           for s, mm in zip(st, mms)]
    ps = [_dot_tn(mm[:, :LANES], s["bkg"][:c2]) for s, mm in zip(st, mms)]
    qs = [_dot_tn(jnp.concatenate([jnp.where(own_lanes, mm[:, LANES:], 0.0), s["vh"]], axis=0), s["bkg"])
          for s, mm in zip(st, mms)]
    out = []
    for s, p, q, tt in zip(st, ps, qs, tts):
        reff = s["r_t"] + tt[:, :LANES]
        out.append(dict(etot=s["etot"], p=p, q=q, reff=reff[:c] + reff[c:],
                        y0=jnp.where(head_a, tt[:c, LANES:], tt[c:, LANES:])))
    return out


def _rwkv_scan_kernel(*refs, seq_len, has_init, want_final):
    r_ref, v_ref, kk_ref, ld0_ref, b0_ref, kt0_ref, ld1_ref, b1_ref, kt1_ref = refs[:9]
    pos = 9
    if has_init:
        s0_ref = refs[pos]
        pos += 1
    y_ref = refs[pos]
    pos += 1
    if want_final:
        sf_ref = refs[pos]
        pos += 1
    g_scr, yf_scr, yb_scr = refs[pos:pos + 3]
    nc = seq_len // CHUNK
    ngroups = nc // SCAN_UNROLL
    per_dir = ((ld0_ref, b0_ref, kt0_ref), (ld1_ref, b1_ref, kt1_ref))
    y_dst = (yf_scr, yb_scr)
    nchain = 2 * SCAN_PAIRS

    def group(gi, states):
        where, units = [], []
        for u in range(SCAN_UNROLL):
            for pp in range(SCAN_PAIRS):
                lanes = slice(pp * LANES, (pp + 1) * LANES)
                for d in range(2):
                    ld_ref, b_ref, kt_ref = per_dir[d]
                    cidx = gi * SCAN_UNROLL + u
                    if d == 1:
                        cidx = nc - 1 - cidx
                    start = cidx * CHUNK
                    rw = slice(start, start + CHUNK) if isinstance(start, int) else pl.ds(
                        pl.multiple_of(start, CHUNK), CHUNK)
                    where.append((rw, lanes))
                    units.append(tuple(ref[rw, lanes].astype(F32) for ref in
                                       (ld_ref, kk_ref, b_ref, kt_ref, r_ref, v_ref)) + (d == 1,))
        pre = _scan_precompute(units)
        states = list(states)
        for u in range(SCAN_UNROLL):
            cur = pre[u * nchain:(u + 1) * nchain]
            ys = [_dot_nt(cu["reff"], g) for cu, g in zip(cur, states)]
            gp = [_dot(g, cu["p"]) for cu, g in zip(cur, states)]
            states = [states[ch] * cur[ch]["etot"] + gp[ch] + cur[ch]["q"] for ch in range(nchain)]
            for ch in range(nchain):
                rw, lanes = where[u * nchain + ch]
                y_dst[ch % 2][rw, lanes] = ys[ch] + cur[ch]["y0"]
        return states

    init = [_block_diag2(s0_ref[0, ch % 2, 2 * (ch // 2)], s0_ref[0, ch % 2, 2 * (ch // 2) + 1])
            if has_init else jnp.zeros((LANES, LANES), F32) for ch in range(nchain)]
    if ngroups == 1:
        final = group(0, init)
    else:
        for ch in range(nchain):
            g_scr[ch] = init[ch]

        def body(gi, carry):
            new = group(gi, [g_scr[ch] for ch in range(nchain)])
            for ch in range(nchain):
                g_scr[ch] = new[ch]
            return carry

        lax.fori_loop(0, ngroups, body, 0)
        final = [g_scr[ch] for ch in range(nchain)]
    y_ref[...] = (yf_scr[...] + yb_scr[...]).astype(y_ref.dtype)
    if want_final:
        for ch in range(nchain):
            sf_ref[0, ch % 2, 2 * (ch // 2)] = final[ch][:HS, :HS]
            sf_ref[0, ch % 2, 2 * (ch // 2) + 1] = final[ch][HS:, HS:]


def _rwkv_scan(proj, s_init, n_seq, seq_len, want_final):
    r, v, kk, ld0, b0, kt0, ld1, b1, kt1 = proj[:9]
    n = r.shape[0]
    npair = D // LANES
    width = SCAN_PAIRS * LANES
    blk = pl.BlockSpec((seq_len, width), lambda b, p: (b, p))
    st_spec = pl.BlockSpec((1, 2, 2 * SCAN_PAIRS, HS, HS), lambda b, p: (b, 0, p, 0, 0))
    in_specs = [blk] * 9
    args = [r, v, kk, ld0, b0, kt0, ld1, b1, kt1]
    has_init = s_init is not None
    if has_init:
        in_specs.append(st_spec)
        args.append(s_init)
    out_specs = [blk]
    out_shape = [jax.ShapeDtypeStruct((n, D), BF16)]
    if want_final:
        out_specs.append(st_spec)
        out_shape.append(jax.ShapeDtypeStruct((n_seq, 2, NH, HS, HS), F32))
    res = pl.pallas_call(
        functools.partial(_rwkv_scan_kernel, seq_len=seq_len, has_init=has_init, want_final=want_final),
        grid=(n_seq, npair // SCAN_PAIRS),
        in_specs=in_specs,
        out_specs=out_specs,
        out_shape=out_shape,
        scratch_shapes=[pltpu.VMEM((2 * SCAN_PAIRS, LANES, LANES), F32), pltpu.VMEM((seq_len, width), F32),
                        pltpu.VMEM((seq_len, width), F32)],
        compiler_params=_cparams(("arbitrary", "arbitrary"), "rwkv_scan"),
        name="rwkv_scan",
    )(*args)
    return res


def _rwkv_out_kernel(x_ref, m_ref, g_ref, y_ref, bonus_ref, gate_ref, lng_ref, lnb_ref, wo_ref, o_ref,
                     acc_scr):
    j = pl.program_id(1)
    last = pl.num_programs(1) - 1

    def chunk():
        e = _head_indicator(RW_CW)
        mean = jnp.dot(y_ref[...], e, preferred_element_type=F32) * (1.0 / HS)
        yc = y_ref[...].astype(F32) - mean
        var = _dot(yc * yc, e) * (1.0 / HS)
        yn = yc * lax.rsqrt(var + GN_EPS) * lng_ref[...] + lnb_ref[...]
        yn = (yn + bonus_ref[...].astype(F32)) * gate_ref[...].astype(F32)
        return _dot(yn, wo_ref[0])

    @pl.when(j == 0)
    def _():
        acc_scr[...] = chunk()

    @pl.when((j > 0) & (j < last))
    def _():
        acc_scr[...] += chunk()

    @pl.when(j == last)
    def _():
        m = m_ref[0, 0]
        g = g_ref[0]
        o_ref[...] = x_ref[...] + m[2:3] * _rms(acc_scr[...] + chunk(), g[1:2])


def _rwkv_out(src, dst, mod, norm_g, y, bonus, gate, p, layer, j, group_of_tile):
    nj = D // RW_CW
    assert nj >= 2
    tile = pl.BlockSpec((TM, RW_CW), lambda i, c: (i, c))
    vec = pl.BlockSpec((1, RW_CW), lambda i, c: (j, c))
    return _activation_call(
        _rwkv_out_kernel,
        grid=(src.ntiles, nj),
        in_specs=[
            pl.BlockSpec((TM, D), lambda i, c: (i + src.first, 0)),
            _mod_spec(layer, group_of_tile, 2),
            _normg_spec(layer, 2),
            tile, tile, tile, vec, vec,
            pl.BlockSpec((1, RW_CW, D), lambda i, c: (j, c, 0)),
        ],
        args=[src.array, mod, norm_g, y, bonus, gate, p["ln_g"], p["ln_b"], p["w_o"]],
        dst=dst, block_rows=TM, out_index=lambda i, c: (i + dst.first, 0),
        scratch_shapes=[pltpu.VMEM((TM, D), F32)],
        name="rwkv_out")


def _rope_tables(n):
    rows = n // GRID_W
    row = np.repeat(np.arange(rows), GRID_W)
    col = np.tile(np.arange(GRID_W), rows)
    pos = np.stack([row, col], axis=-1).astype(np.float64)
    quarter = D_ROPE // 4
    inv = ROPE_BASE ** (-np.arange(quarter, dtype=np.float64) / quarter)
    ang = pos[:, :, None] * inv
    cos = np.cos(ang)
    sin = np.sin(ang)
    cos_t = np.concatenate([cos, cos], axis=-1).reshape(n, D_ROPE)
    sin_t = np.concatenate([-sin, sin], axis=-1).reshape(n, D_ROPE)
    return cos_t.astype(np.float32), sin_t.astype(np.float32)


def _rope_swap_perm():
    quarter = D_ROPE // 4
    base = np.arange(D_ROPE)
    return np.where((base % (2 * quarter)) < quarter, base + quarter, base - quarter)


def _mla_proj_kernel(*refs, positional):
    (x_ref, m_ref, g_ref, wdq_ref, gq_ref, wqn_ref, wqr_ref, wqs_ref, wdkv_ref, wkr_ref, wks_ref,
     gkv_ref) = refs[:12]
    pos = 12
    if positional:
        cosq_ref, sinq_ref, cosk_ref, sink_ref = refs[pos:pos + 4]
        pos += 4
    qn_o, qr_o, ckv_o, kr_o = refs[pos:pos + 4]
    m = m_ref[0, 0]
    g = g_ref[0]
    h = (_rms(x_ref[...], g[0:1]) * (1.0 + m[1:2]) + m[0:1]).astype(BF16)
    ql = jnp.dot(h, wdq_ref[0].astype(BF16), preferred_element_type=F32)
    ql = (ql * lax.rsqrt(jnp.mean(ql * ql, axis=-1, keepdims=True) + EPS) * gq_ref[...]).astype(BF16)
    qn_o[...] = jnp.dot(ql, wqn_ref[...].astype(BF16), preferred_element_type=F32).astype(qn_o.dtype)
    qr = jnp.dot(ql, wqr_ref[...].astype(BF16), preferred_element_type=F32)
    ckv = jnp.dot(h, wdkv_ref[...].astype(BF16), preferred_element_type=F32)
    ckv_o[...] = ckv * lax.rsqrt(jnp.mean(ckv * ckv, axis=-1, keepdims=True) + EPS) * gkv_ref[...]
    kr = jnp.dot(h, wkr_ref[...].astype(BF16), preferred_element_type=F32)
    if positional:
        qs = jnp.dot(ql, wqs_ref[...].astype(BF16), preferred_element_type=F32)
        ks = jnp.dot(h, wks_ref[...].astype(BF16), preferred_element_type=F32)
        qr = qr * cosq_ref[...] + qs * sinq_ref[...]
        kr = kr * cosk_ref[...] + ks * sink_ref[...]
    qr_o[...] = qr.astype(qr_o.dtype)
    kr_o[...] = kr


def _mla_proj(src, mod, norm_g, p, layer, j, positional, group_of_tile):
    n = src.ntiles * TM
    full = lambda shape: pl.BlockSpec(shape, lambda i: (0,) * len(shape))
    in_specs = [
        pl.BlockSpec((TM, D), lambda i: (i + src.first, 0)),
        _mod_spec(layer, group_of_tile, 1),
        _normg_spec(layer, 1),
        pl.BlockSpec((1, D, Q_RANK), lambda i: (j, 0, 0)),
        pl.BlockSpec((1, Q_RANK), lambda i: (j, 0)),
        full((Q_RANK, MLA_H * D_NOPE)), full((Q_RANK, MLA_H * D_ROPE)), full((Q_RANK, MLA_H * D_ROPE)),
        full((D, KV_RANK)), full((D, D_ROPE)), full((D, D_ROPE)),
        pl.BlockSpec((1, KV_RANK), lambda i: (j, 0)),
    ]
    args = [src.array, mod, norm_g, p["w_dq"], p["g_q"], p["w_uq_nope"], p["w_uq_rope"], p["w_uq_rope_sw"],
            p["w_dkv_c"], p["w_dkv_r"], p["w_dkv_r_sw"], p["g_kv"]]
    if positional:
        cos_t, sin_t = _rope_tables(TM)
        in_specs += [full((TM, MLA_H * D_ROPE)), full((TM, MLA_H * D_ROPE)),
                     full((TM, D_ROPE)), full((TM, D_ROPE))]
        args += [jnp.asarray(np.tile(cos_t, (1, MLA_H))), jnp.asarray(np.tile(sin_t, (1, MLA_H))),
                 jnp.asarray(cos_t), jnp.asarray(sin_t)]
    outs = ((MLA_H * D_NOPE, BF16), (MLA_H * D_ROPE, BF16), (KV_RANK, F32), (D_ROPE, F32))
    return pl.pallas_call(
        functools.partial(_mla_proj_kernel, positional=positional),
        grid=(src.ntiles,),
        in_specs=in_specs,
        out_specs=[pl.BlockSpec((TM, w), lambda i: (i, 0)) for w, _ in outs],
        out_shape=[jax.ShapeDtypeStruct((n, w), dt) for w, dt in outs],
        compiler_params=_cparams(("arbitrary",), "mla_proj"),
        name="mla_proj",
    )(*args)


def _mla_attn_kernel(x_ref, m_ref, g_ref, qn_ref, qr_ref, ckv_ref, kr_ref, wuk_ref, wuv_ref, wo_ref,
                     o_ref, kn_scr, vv_scr, oh_scr, *, nb, tq, k_len):
    qi = pl.program_id(1)

    @pl.when(qi == 0)
    def _():
        ckv = ckv_ref[...].astype(BF16)
        kn_scr[...] = jnp.dot(ckv, wuk_ref[0].astype(BF16), preferred_element_type=F32).astype(BF16)
        vv_scr[...] = jnp.dot(ckv, wuv_ref[0].astype(BF16), preferred_element_type=F32).astype(BF16)

    units = [(b, hd) for b in range(nb) for hd in range(MLA_H)]

    def scores(unit):
        b, hd = unit
        qrows = slice(b * tq, (b + 1) * tq)
        krows = slice(b * k_len, (b + 1) * k_len)
        q = jnp.concatenate([qn_ref[qrows, hd * D_NOPE:(hd + 1) * D_NOPE].astype(BF16),
                             qr_ref[qrows, hd * D_ROPE:(hd + 1) * D_ROPE].astype(BF16)], axis=1)
        k = jnp.concatenate([kn_scr[krows, hd * D_NOPE:(hd + 1) * D_NOPE],
                             kr_ref[krows, :].astype(BF16)], axis=1)
        return _dot_nt(q, k) * MLA_SCALE

    pending = [scores(u) for u in units[:ATTN_LOOKAHEAD]]
    for idx, (b, hd) in enumerate(units):
        if idx + ATTN_LOOKAHEAD < len(units):
            pending.append(scores(units[idx + ATTN_LOOKAHEAD]))
        s = pending[idx]
        pexp = jnp.exp(s - jnp.max(s, axis=-1, keepdims=True))
        pv = jnp.dot(pexp.astype(BF16), vv_scr[b * k_len:(b + 1) * k_len, hd * D_V:(hd + 1) * D_V],
                     preferred_element_type=F32)
        oh_scr[b * tq:(b + 1) * tq, hd * D_V:(hd + 1) * D_V] = (
            pv / jnp.sum(pexp, axis=-1, keepdims=True)).astype(BF16)
    o = jnp.dot(oh_scr[...], wo_ref[0].astype(BF16), preferred_element_type=F32)
    m = m_ref[0, 0]
    g = g_ref[0]
    o_ref[...] = x_ref[...] + m[2:3] * _rms(o, g[1:2])


def _mla_attn(src, dst, mod, norm_g, qn, qr, ckv_all, kr_all, p, layer, j, n_seq, q_len, k_len, nb, tq,
              group_of_step):
    nq = q_len // tq
    assert nb == 1 or nq == 1
    rows = nb * tq
    x_first, o_first = src.first * TM // rows, dst.first * TM // rows
    return _activation_call(
        functools.partial(_mla_attn_kernel, nb=nb, tq=tq, k_len=k_len),
        grid=(n_seq // nb, nq),
        in_specs=[
            pl.BlockSpec((rows, D), lambda s, q: (s * nq + q + x_first, 0)),
            pl.BlockSpec((1, 1, N_MOD, D), lambda s, q: (0, group_of_step(s), 0, 0)),
            pl.BlockSpec((1, 4, D), lambda s, q: (layer, 0, 0)),
            pl.BlockSpec((nb * tq, MLA_H * D_NOPE), lambda s, q: (s * nq + q, 0)),
            pl.BlockSpec((nb * tq, MLA_H * D_ROPE), lambda s, q: (s * nq + q, 0)),
            pl.BlockSpec((nb * k_len, KV_RANK), lambda s, q: (s, 0)),
            pl.BlockSpec((nb * k_len, D_ROPE), lambda s, q: (s, 0)),
            pl.BlockSpec((1, KV_RANK, MLA_H * D_NOPE), lambda s, q: (j, 0, 0)),
            pl.BlockSpec((1, KV_RANK, MLA_H * D_V), lambda s, q: (j, 0, 0)),
            pl.BlockSpec((1, MLA_H * D_V, D), lambda s, q: (j, 0, 0)),
        ],
        args=[src.array, mod, norm_g, qn, qr, ckv_all, kr_all, p["w_uk"], p["w_uv"], p["w_o"]],
        dst=dst, block_rows=rows, out_index=lambda s, q: (s * nq + q + o_first, 0),
        scratch_shapes=[pltpu.VMEM((nb * k_len, MLA_H * D_NOPE), BF16),
                        pltpu.VMEM((nb * k_len, MLA_H * D_V), BF16),
                        pltpu.VMEM((nb * tq, MLA_H * D_V), BF16)],
        name="mla_attn")


def kernel(x_prompt, x_sample, state_rwkv, cache_mla_ckv, cache_mla_krope, c, c_ctx, mod_w, mod_b, norm_g,
           mlp_w1, mlp_w2, fft_w_out, conv_w_in, conv_w, conv_w_out, rwkv_mu, rwkv_w_r, rwkv_w_k, rwkv_w_v,
           rwkv_w_o, rwkv_w0, rwkv_w_l1, rwkv_w_l2, rwkv_a0, rwkv_a_l1, rwkv_a_l2, rwkv_g_l1, rwkv_g_l2,
           rwkv_k_k, rwkv_k_a, rwkv_r_k, rwkv_ln_g, rwkv_ln_b, mla_w_dq, mla_g_q, mla_w_uq, mla_w_dkv,
           mla_g_kv, mla_w_uk, mla_w_uv, mla_w_o):
    batch, seq, _ = x_prompt.shape
    dec_batch, dec_seq, _ = x_sample.shape
    past_len = cache_mla_ckv.shape[2]
    assert (batch * seq) % TM == 0 and TM % seq == 0 and dec_seq == TM and seq % CHUNK == 0

    np_tiles = batch * seq // TM
    ns_tiles = dec_batch * dec_seq // TM
    total_rows = (np_tiles + ns_tiles) * TM
    cs = jnp.concatenate([c_ctx[None, :], c, jnp.zeros((MOD_ROWS - 1 - dec_batch, D), F32)], axis=0)
    mod_b3 = mod_b.reshape(DEPTH, 1, N_MOD * D)
    mod = _modulation(cs, mod_w, mod_b3, 0)

    grp_p = lambda i: 0
    grp_s = lambda i: 1 + i
    grp_all = lambda i: jnp.maximum(i - (np_tiles - 1), 0)
    new_rwkv, new_ckv, new_krope = [], [], []
    streams = [(True, seq, batch, grp_p), (False, dec_seq, dec_batch, grp_s)]
    cur = [_Rows(x_prompt.reshape(batch * seq, D), 0, np_tiles),
           _Rows(x_sample.reshape(dec_batch * dec_seq, D), 0, ns_tiles)]

    for i in range(DEPTH):
        kind, j = i % 4, i // 4
        for idx, (is_prompt, slen, nseq, grp) in enumerate(streams):
            src = cur[idx]
            sharing = src.array.shape[0] == total_rows
            dst = _Dest(total_rows, src.first, True) if sharing else _Dest(src.ntiles * TM, 0, False)
            shared = None
            if kind == 0:
                shared = _fnet(src, dst, mod, norm_g, fft_w_out, i, j, slen, grp)
            elif kind == 1:
                shared = _conv(src, dst, mod, norm_g, conv_w_in, conv_w, conv_w_out, i, j, slen, grp)
            elif kind == 2:
                p = dict(mu=rwkv_mu, w_r=rwkv_w_r, w_k=rwkv_w_k, w_v=rwkv_w_v, w_o=rwkv_w_o, w0=rwkv_w0,
                         w_l1=rwkv_w_l1, w_l2=rwkv_w_l2, a0=rwkv_a0, a_l1=rwkv_a_l1, a_l2=rwkv_a_l2,
                         g_l1=rwkv_g_l1, g_l2=rwkv_g_l2, k_k=rwkv_k_k, k_a=rwkv_k_a,
                         r_k=rwkv_r_k.reshape(-1, D), ln_g=rwkv_ln_g, ln_b=rwkv_ln_b)
                proj = _rwkv_proj(src, mod, norm_g, p, i, j, slen, grp)
                if is_prompt:
                    y, s_fin = _rwkv_scan(proj, None, nseq, slen, True)
                    new_rwkv.append(s_fin)
                else:
                    (y,) = _rwkv_scan(proj, state_rwkv[:, j], nseq, slen, False)
                shared = _rwkv_out(src, dst, mod, norm_g, y, proj[10], proj[9], p, i, j, grp)
            else:
                perm = _rope_swap_perm()
                w_uq = mla_w_uq[j].reshape(Q_RANK, MLA_H, D_NOPE + D_ROPE)
                w_uq_rope = w_uq[:, :, D_NOPE:]
                w_dkv_r = mla_w_dkv[j][:, KV_RANK:]
                p = dict(w_dq=mla_w_dq, g_q=mla_g_q, g_kv=mla_g_kv, w_uk=mla_w_uk, w_uv=mla_w_uv, w_o=mla_w_o,
                         w_uq_nope=w_uq[:, :, :D_NOPE].reshape(Q_RANK, MLA_H * D_NOPE),
                         w_uq_rope=w_uq_rope.reshape(Q_RANK, MLA_H * D_ROPE),
                         w_uq_rope_sw=w_uq_rope[:, :, perm].reshape(Q_RANK, MLA_H * D_ROPE),
                         w_dkv_c=mla_w_dkv[j][:, :KV_RANK], w_dkv_r=w_dkv_r, w_dkv_r_sw=w_dkv_r[:, perm])
                qn, qr, ckv, kr = _mla_proj(src, mod, norm_g, p, i, j, not is_prompt, grp)
                if is_prompt:
                    new_ckv.append(ckv.reshape(batch, seq, KV_RANK))
                    new_krope.append(kr.reshape(batch, seq, D_ROPE))
                    shared = _mla_attn(src, dst, mod, norm_g, qn, qr, ckv, kr, p, i, j, nseq, slen, slen,
                                       TM // slen, slen, lambda s: 0)
                else:
                    klen = past_len + slen
                    ckv_all = jnp.concatenate([cache_mla_ckv[:, j], ckv.reshape(nseq, slen, KV_RANK)], axis=1)
                    kr_all = jnp.concatenate([cache_mla_krope[:, j], kr.reshape(nseq, slen, D_ROPE)], axis=1)
                    shared = _mla_attn(src, dst, mod, norm_g, qn, qr, ckv_all.reshape(nseq * klen, KV_RANK),
                                       kr_all.reshape(nseq * klen, D_ROPE), p, i, j, nseq, slen, klen, 1, 256,
                                       lambda s: 1 + s)
            cur[idx] = _Rows(shared, dst.first, src.ntiles)
            if sharing:
                cur[1 - idx] = cur[1 - idx]._replace(array=shared)
        if i < DEPTH - 1:
            srcs = [_Rows(cur[0].array, 0, np_tiles + ns_tiles)] if cur[0].array is cur[1].array else cur
            both, mod = _mlp(srcs, _Dest(total_rows, 0, False), mod, norm_g, mlp_w1, mlp_w2, i, grp_all,
                             next_mod=(cs, mod_w, mod_b3, i + 1))
            cur = [_Rows(both, 0, np_tiles), _Rows(both, np_tiles, ns_tiles)]
        else:
            y_prompt, y_sample = [
                _mlp([cur[idx]], _Dest(cur[idx].ntiles * TM, 0, False), mod, norm_g, mlp_w1, mlp_w2, i,
                     streams[idx][3]) for idx in range(2)]

    return (y_prompt.reshape(batch, seq, D), y_sample.reshape(dec_batch, dec_seq, D),
            jnp.stack(new_rwkv, axis=1), jnp.stack(new_ckv, axis=1), jnp.stack(new_krope, axis=1))
```

```python
import functools
from typing import NamedTuple

import numpy as np
import jax
import jax.numpy as jnp
from jax import lax
from jax.experimental import pallas as pl
from jax.experimental.pallas import tpu as pltpu

D = 1024
DEPTH = 4
N_MOD = 6
D_FF = 4 * D
EPS = 1e-6
GRID_W = 64
FFT_GROUPS = 8
FFT_GW = D // FFT_GROUPS
HS = 64
NH = D // HS
LORA_W = 64
LORA_A = 64
LORA_G = 128
GN_EPS = 64e-5
DECAY_SCALE = float(np.exp(-0.5))
MLA_H = 8
D_NOPE = 128
D_ROPE = 64
D_V = 128
KV_RANK = 256
Q_RANK = 384
ROPE_BASE = 10000.0
MLA_SCALE = (D_NOPE + D_ROPE) ** -0.5

F32 = jnp.float32
BF16 = jnp.bfloat16

TM = 1024
LANES = 128
CHUNK = 64
SCAN_UNROLL = 4
SCAN_PAIRS = 2
ATTN_LOOKAHEAD = 2
VMEM_LIMIT = 58 * 1024 * 1024


def _cparams(sem):
    return pltpu.CompilerParams(dimension_semantics=sem, vmem_limit_bytes=VMEM_LIMIT)


def _dot(a, b):
    return jnp.dot(a.astype(BF16), b.astype(BF16), preferred_element_type=F32)


def _dot_nt(a, b):
    return lax.dot_general(a.astype(BF16), b.astype(BF16), (((1,), (1,)), ((), ())),
                           preferred_element_type=F32)


def _dot_tn(a, b):
    return lax.dot_general(a.astype(BF16), b.astype(BF16), (((0,), (0,)), ((), ())),
                           preferred_element_type=F32)


def _block_diag2(a, b):
    za = jnp.zeros((a.shape[0], b.shape[1]), a.dtype)
    zb = jnp.zeros((b.shape[0], a.shape[1]), a.dtype)
    return jnp.concatenate([jnp.concatenate([a, za], axis=1), jnp.concatenate([zb, b], axis=1)], axis=0)


def _cumsum_rows(x, reverse):
    n = x.shape[0]
    idx = lax.broadcasted_iota(jnp.int32, x.shape, 0)
    s = 1
    while s < n:
        if reverse:
            x = x + jnp.where(idx < n - s, pltpu.roll(x, n - s, 0), 0.0)
        else:
            x = x + jnp.where(idx >= s, pltpu.roll(x, s, 0), 0.0)
        s *= 2
    return x


def _rms(x, g):
    return x * lax.rsqrt(jnp.mean(x * x, axis=-1, keepdims=True) + EPS) * g


def _sigmoid(x):
    return 1.0 / (1.0 + jnp.exp(-x))


def _head_indicator(n):
    r = lax.broadcasted_iota(jnp.int32, (n, n), 0) // HS
    c = lax.broadcasted_iota(jnp.int32, (n, n), 1) // HS
    return jnp.where(r == c, 1.0, 0.0).astype(BF16)


def _shift_rows(z, seq_len):
    n, w = z.shape
    pos = lax.broadcasted_iota(jnp.int32, (n, LANES), 0) & (seq_len - 1)
    keep_prev = jnp.concatenate([jnp.where(pos == 0, 0.0, 1.0)] * (w // LANES), axis=1)
    keep_next = jnp.concatenate([jnp.where(pos == seq_len - 1, 0.0, 1.0)] * (w // LANES), axis=1)
    return pltpu.roll(z, 1, 0) * keep_prev, pltpu.roll(z, n - 1, 0) * keep_next


MOD_TN = 1536
MOD_ROWS = 8


def _mod_block(cs_ref, w_ref, b_ref):
    cs = cs_ref[...]
    return _dot(cs * _sigmoid(cs), w_ref[0]) + b_ref[0]


def _mod_kernel(cs_ref, w_ref, b_ref, o_ref):
    o_ref[...] = _mod_block(cs_ref, w_ref, b_ref)


def _modulation(cs, mod_w, mod_b3, layer):
    nj = (N_MOD * D) // MOD_TN
    out = pl.pallas_call(
        _mod_kernel,
        grid=(nj,),
        in_specs=[
            pl.BlockSpec((MOD_ROWS, D), lambda j: (0, 0)),
            pl.BlockSpec((1, D, MOD_TN), lambda j: (layer, 0, j)),
            pl.BlockSpec((1, 1, MOD_TN), lambda j: (layer, 0, j)),
        ],
        out_specs=pl.BlockSpec((MOD_ROWS, MOD_TN), lambda j: (0, j)),
        out_shape=jax.ShapeDtypeStruct((MOD_ROWS, N_MOD * D), F32),
        compiler_params=_cparams(("arbitrary",)),
        name="modulation",
    )(cs, mod_w, mod_b3)
    return out.reshape(1, MOD_ROWS, N_MOD, D)


class _Rows(NamedTuple):
    array: jax.Array
    first: int
    ntiles: int


class _Dest(NamedTuple):
    rows: int
    first: int
    inplace: bool


def _activation_call(kernel, *, grid, in_specs, args, dst, block_rows, out_index, scratch_shapes, name,
                     extra_out=None):
    assert not dst.inplace or args[0].shape[0] == dst.rows
    out_specs = pl.BlockSpec((block_rows, D), out_index)
    out_shape = jax.ShapeDtypeStruct((dst.rows, D), F32)
    if extra_out is not None:
        out_specs, out_shape = [out_specs, extra_out[0]], [out_shape, extra_out[1]]
    return pl.pallas_call(
        kernel, grid=grid, in_specs=in_specs, out_specs=out_specs, out_shape=out_shape,
        scratch_shapes=scratch_shapes, input_output_aliases={0: 0} if dst.inplace else {},
        compiler_params=_cparams(("arbitrary",) * len(grid)), name=name)(*args)


def _mod_spec(layer, group_of_tile, ngrid):
    del layer
    if ngrid == 1:
        return pl.BlockSpec((1, 1, N_MOD, D), lambda i: (0, group_of_tile(i), 0, 0))
    return pl.BlockSpec((1, 1, N_MOD, D), lambda i, j: (0, group_of_tile(i), 0, 0))


def _normg_spec(layer, ngrid):
    if ngrid == 1:
        return pl.BlockSpec((1, 4, D), lambda i: (layer, 0, 0))
    return pl.BlockSpec((1, 4, D), lambda i, j: (layer, 0, 0))


MLP_FC = 1024


def _mlp_kernel(*refs, n_first, with_next_mod):
    if n_first is None:
        (x_ref,), rest = refs[:1], refs[1:]
        read_x = lambda: x_ref[...]
    else:
        (xa_ref, xb_ref), rest = refs[:2], refs[2:]
        read_x = lambda: jnp.where(pl.program_id(0) < n_first, xa_ref[...], xb_ref[...])
    if with_next_mod:
        m_ref, g_ref, w1_ref, w2_ref, cs_ref, mw_ref, mb_ref, o_ref, mo_ref, h_scr, acc_scr = rest
        mo_ref[...] = _mod_block(cs_ref, mw_ref, mb_ref)
    else:
        m_ref, g_ref, w1_ref, w2_ref, o_ref, h_scr, acc_scr = rest
    j = pl.program_id(1)
    m = m_ref[0, 0]
    g = g_ref[0]

    last = pl.num_programs(1) - 1

    def chunk():
        a = jnp.dot(h_scr[...], w1_ref[0].astype(BF16), preferred_element_type=F32)
        a = jnp.maximum(a, 0.0)
        return _dot(a * a, w2_ref[0])

    @pl.when(j == 0)
    def _():
        h = _rms(read_x(), g[2:3]) * (1.0 + m[4:5]) + m[3:4]
        h_scr[...] = h.astype(BF16)
        acc_scr[...] = chunk()

    @pl.when((j > 0) & (j < last))
    def _():
        acc_scr[...] += chunk()

    @pl.when(j == last)
    def _():
        f = acc_scr[...] + chunk()
        o_ref[...] = read_x() + m[5:6] * _rms(f, g[3:4])


def _mlp(srcs, dst, mod, norm_g, w1, w2, layer, group_of_tile, next_mod=None):
    nj = D_FF // MLP_FC
    assert nj >= 2
    if len(srcs) == 1:
        (src,) = srcs
        ntiles, n_first = src.ntiles, None
        x_specs = [pl.BlockSpec((TM, D), lambda i, j: (i + src.first, 0))]
    else:
        sa, sb = srcs
        ntiles, n_first = sa.ntiles + sb.ntiles, sa.ntiles
        x_specs = [pl.BlockSpec((TM, D), lambda i, j: (jnp.minimum(i, sa.ntiles - 1) + sa.first, 0)),
                   pl.BlockSpec((TM, D), lambda i, j: (jnp.maximum(i - sa.ntiles, 0) + sb.first, 0))]
    in_specs = x_specs + [
        _mod_spec(layer, group_of_tile, 2),
        _normg_spec(layer, 2),
        pl.BlockSpec((1, D, MLP_FC), lambda i, j: (layer, 0, j)),
        pl.BlockSpec((1, MLP_FC, D), lambda i, j: (layer, j, 0)),
    ]
    args = [s.array for s in srcs] + [mod, norm_g, w1, w2]
    extra_out = None
    if next_mod is not None:
        cs, mod_w, mod_b3, nxt = next_mod
        cols = (N_MOD * D) // (ntiles * nj)
        assert cols * ntiles * nj == N_MOD * D and cols % LANES == 0
        in_specs += [pl.BlockSpec((MOD_ROWS, D), lambda i, j: (0, 0)),
                     pl.BlockSpec((1, D, cols), lambda i, j: (nxt, 0, i * nj + j)),
                     pl.BlockSpec((1, 1, cols), lambda i, j: (nxt, 0, i * nj + j))]
        args += [cs, mod_w, mod_b3]
        extra_out = (pl.BlockSpec((MOD_ROWS, cols), lambda i, j: (0, i * nj + j)),
                     jax.ShapeDtypeStruct((MOD_ROWS, N_MOD * D), F32))
    res = _activation_call(
        functools.partial(_mlp_kernel, n_first=n_first, with_next_mod=next_mod is not None),
        grid=(ntiles, nj), in_specs=in_specs, args=args,
        dst=dst, block_rows=TM, out_index=lambda i, j: (i + dst.first, 0),
        scratch_shapes=[pltpu.VMEM((TM, D), BF16), pltpu.VMEM((TM, D), F32)],
        name="mlp", extra_out=extra_out)
    if next_mod is None:
        return res
    return res[0], res[1].reshape(1, MOD_ROWS, N_MOD, D)


def _dft_mats(n):
    idx = np.arange(n, dtype=np.int64)
    ang = (2.0 * np.pi / n) * ((idx[:, None] * idx[None, :]) % n).astype(np.float64)
    scale = 1.0 / np.sqrt(n)
    return np.cos(ang) * scale, np.sin(ang) * scale


def _fnet_kernel(x_ref, m_ref, g_ref, cs_ref, cn_ref, sn_ref, w_ref, o_ref, p_scr, q_scr, f_scr,
                 *, seq_len):
    m = m_ref[0, 0]
    g = g_ref[0]
    x = x_ref[...]
    h = (_rms(x, g[0:1]) * (1.0 + m[1:2]) + m[0:1]).astype(BF16)
    cs = cs_ref[...].astype(BF16)
    for gi in range(FFT_GROUPS):
        pq = jnp.dot(h[:, gi * FFT_GW:(gi + 1) * FFT_GW], cs, preferred_element_type=F32)
        p_scr[:, gi * FFT_GW:(gi + 1) * FFT_GW] = pq[:, :FFT_GW].astype(BF16)
        q_scr[:, gi * FFT_GW:(gi + 1) * FFT_GW] = pq[:, FFT_GW:].astype(BF16)
    cn = cn_ref[...].astype(BF16)
    sn = sn_ref[...].astype(BF16)
    for s in range(TM // seq_len):
        rows = slice(s * seq_len, (s + 1) * seq_len)
        f = (jnp.dot(cn, p_scr[rows, :], preferred_element_type=F32)
             - jnp.dot(sn, q_scr[rows, :], preferred_element_type=F32))
        f_scr[rows, :] = f.astype(BF16)
    o = jnp.dot(f_scr[...], w_ref[0].astype(BF16), preferred_element_type=F32)
    o_ref[...] = x + m[2:3] * _rms(o, g[1:2])


def _fnet(src, dst, mod, norm_g, w_out, layer, j, seq_len, group_of_tile):
    cg, sg = _dft_mats(FFT_GW)
    cs = jnp.asarray(np.concatenate([cg, sg], axis=1), F32)
    cn_np, sn_np = _dft_mats(seq_len)
    cn = jnp.asarray(cn_np, F32)
    sn = jnp.asarray(sn_np, F32)
    return _activation_call(
        functools.partial(_fnet_kernel, seq_len=seq_len),
        grid=(src.ntiles,),
        in_specs=[
            pl.BlockSpec((TM, D), lambda i: (i + src.first, 0)),
            _mod_spec(layer, group_of_tile, 1),
            _normg_spec(layer, 1),
            pl.BlockSpec((FFT_GW, 2 * FFT_GW), lambda i: (0, 0)),
            pl.BlockSpec((seq_len, seq_len), lambda i: (0, 0)),
            pl.BlockSpec((seq_len, seq_len), lambda i: (0, 0)),
            pl.BlockSpec((1, D, D), lambda i: (j, 0, 0)),
        ],
        args=[src.array, mod, norm_g, cs, cn, sn, w_out],
        dst=dst, block_rows=TM, out_index=lambda i: (i + dst.first, 0),
        scratch_shapes=[pltpu.VMEM((TM, D), BF16), pltpu.VMEM((TM, D), BF16),
                        pltpu.VMEM((TM, D), BF16)],
        name="fourier_mix")


CONV_CW = 512


def _conv_kernel(x_ref, m_ref, g_ref, wb_ref, wc_ref, wu_ref, cw_ref, wo_ref, o_ref, h_scr, acc_scr,
                 *, seq_lens, n_first):
    j = pl.program_id(1)
    m = m_ref[0, 0]
    g = g_ref[0]
    seq_len = seq_lens[0] if n_first is None else jnp.where(pl.program_id(0) < n_first, *seq_lens)

    def chunk():
        h = h_scr[...]
        hw = CONV_CW // 2
        proj = []
        for s in range(2):
            cols = slice(s * hw, (s + 1) * hw)
            proj.append([jnp.dot(h, w_ref[0, :, cols].astype(BF16), preferred_element_type=F32)
                         for w_ref in (wb_ref, wc_ref, wu_ref)])
        out = None
        for s in range(2):
            cols = slice(s * hw, (s + 1) * hw)
            bg, cg, u = proj[s]
            z = cg * u
            z_prev, z_next = _shift_rows(z, seq_len)
            conv = z_prev * cw_ref[0, 0:1, cols] + z * cw_ref[0, 1:2, cols] + z_next * cw_ref[0, 2:3, cols]
            part = _dot(bg * conv, wo_ref[0, cols, :])
            out = part if out is None else out + part
        return out

    last = pl.num_programs(1) - 1

    @pl.when(j == 0)
    def _():
        h = _rms(x_ref[...], g[0:1]) * (1.0 + m[1:2]) + m[0:1]
        h_scr[...] = h.astype(BF16)
        acc_scr[...] = chunk()

    @pl.when((j > 0) & (j < last))
    def _():
        acc_scr[...] += chunk()

    @pl.when(j == last)
    def _():
        o_ref[...] = x_ref[...] + m[2:3] * _rms(acc_scr[...] + chunk(), g[1:2])


def _conv(src, dst, mod, norm_g, w_in, w_conv, w_out, layer, j, seq_lens, group_of_tile, n_first=None):
    nj = D // CONV_CW
    assert nj >= 2
    assert all(s & (s - 1) == 0 and TM % s == 0 for s in seq_lens)
    return _activation_call(
        functools.partial(_conv_kernel, seq_lens=seq_lens, n_first=n_first),
        grid=(src.ntiles, nj),
        in_specs=[
            pl.BlockSpec((TM, D), lambda i, c: (i + src.first, 0)),
            _mod_spec(layer, group_of_tile, 2),
            _normg_spec(layer, 2),
            pl.BlockSpec((1, D, CONV_CW), lambda i, c: (j, 0, c)),
            pl.BlockSpec((1, D, CONV_CW), lambda i, c: (j, 0, nj + c)),
            pl.BlockSpec((1, D, CONV_CW), lambda i, c: (j, 0, 2 * nj + c)),
            pl.BlockSpec((1, 3, CONV_CW), lambda i, c: (j, 0, c)),
            pl.BlockSpec((1, CONV_CW, D), lambda i, c: (j, c, 0)),
        ],
        args=[src.array, mod, norm_g, w_in, w_in, w_in, w_conv, w_out],
        dst=dst, block_rows=TM, out_index=lambda i, c: (i + dst.first, 0),
        scratch_shapes=[pltpu.VMEM((TM, D), BF16), pltpu.VMEM((TM, D), F32)],
        name="short_conv")


RWP_CW = 256
RW_OUT_DTYPES = (BF16, BF16, BF16, F32, BF16, BF16, F32, BF16, BF16, BF16, BF16)
RW_CW = 512


def _rwkv_proj_kernel(x_ref, m_ref, g_ref, mu_ref, wr_ref, wk_ref, wv_ref, w0_ref, wl1_ref, wl2_ref,
                      a0_ref, al1_ref, al2_ref, gl1_ref, gl2_ref, kk_ref, ka_ref, rk_ref,
                      r_o, v_o, kk_o, ld0_o, b0_o, kt0_o, ld1_o, b1_o, kt1_o, g_o, bonus_o,
                      xr_scr, xk_scr, xv_scr, tw_scr, ta_scr, sg_scr, *, seq_len):
    j = pl.program_id(1)

    def chunk():
        e = _head_indicator(RWP_CW)
        r = jnp.dot(xr_scr[...], wr_ref[0].astype(BF16), preferred_element_type=F32)
        k = jnp.dot(xk_scr[...], wk_ref[0].astype(BF16), preferred_element_type=F32)
        v = jnp.dot(xv_scr[...], wv_ref[0].astype(BF16), preferred_element_type=F32)
        g_o[...] = jnp.dot(sg_scr[...], gl2_ref[0].astype(BF16), preferred_element_type=F32).astype(g_o.dtype)
        zws = [jnp.dot(tw_scr[d], wl2_ref[0, d].astype(BF16), preferred_element_type=F32) for d in range(2)]
        zas = [jnp.dot(ta_scr[d], al2_ref[0, d].astype(BF16), preferred_element_type=F32) for d in range(2)]
        kk = k * kk_ref[...]
        kk = kk * lax.rsqrt(_dot(kk * kk, e) + 1e-12)
        r_o[...] = r.astype(r_o.dtype)
        v_o[...] = v.astype(v_o.dtype)
        kk_o[...] = kk.astype(kk_o.dtype)
        k_scaled = k * ka_ref[...]
        kt_sum = None
        for d, (ld_o, b_o, kt_o) in enumerate(((ld0_o, b0_o, kt0_o), (ld1_o, b1_o, kt1_o))):
            ld_o[...] = -DECAY_SCALE * _sigmoid(w0_ref[0, d:d + 1, :] + zws[d])
            a = _sigmoid(a0_ref[0, d:d + 1, :] + zas[d])
            kt = k + k_scaled * (a - 1.0)
            b_o[...] = (kk * a).astype(b_o.dtype)
            kt_o[...] = kt.astype(kt_o.dtype)
            kt_sum = kt if kt_sum is None else kt_sum + kt
        bonus_o[...] = (_dot(r * rk_ref[...] * kt_sum, e) * v).astype(bonus_o.dtype)

    @pl.when(j == 0)
    def _():
        m = m_ref[0, 0]
        g = g_ref[0]
        mu = mu_ref[0]
        h = _rms(x_ref[...], g[0:1]) * (1.0 + m[1:2]) + m[0:1]
        h_prev, h_next = _shift_rows(h, seq_len)
        dx = 0.5 * (h_prev + h_next) - h
        xr_scr[...] = (h + dx * mu[0:1]).astype(BF16)
        xk_scr[...] = (h + dx * mu[2:3]).astype(BF16)
        xv_scr[...] = (h + dx * mu[3:4]).astype(BF16)
        xw = (h + dx * mu[1:2]).astype(BF16)
        xa = (h + dx * mu[4:5]).astype(BF16)
        xg = (h + dx * mu[5:6]).astype(BF16)
        for d in range(2):
            tw_scr[d] = jnp.tanh(_dot(xw, wl1_ref[0, d])).astype(BF16)
            ta_scr[d] = _dot(xa, al1_ref[0, d]).astype(BF16)
        sg_scr[...] = _sigmoid(_dot(xg, gl1_ref[0])).astype(BF16)
        chunk()

    @pl.when(j > 0)
    def _():
        chunk()


def _rwkv_proj(src, mod, norm_g, p, layer, j, seq_len, group_of_tile):
    n = src.ntiles * TM
    nj = D // RWP_CW
    tile = pl.BlockSpec((TM, RWP_CW), lambda i, c: (i, c))
    wcol = lambda: pl.BlockSpec((1, D, RWP_CW), lambda i, c: (j, 0, c))
    vec = lambda: pl.BlockSpec((1, RWP_CW), lambda i, c: (j, c))
    return pl.pallas_call(
        functools.partial(_rwkv_proj_kernel, seq_len=seq_len),
        grid=(src.ntiles, nj),
        in_specs=[
            pl.BlockSpec((TM, D), lambda i, c: (i + src.first, 0)),
            _mod_spec(layer, group_of_tile, 2),
            _normg_spec(layer, 2),
            pl.BlockSpec((1, 6, D), lambda i, c: (j, 0, 0)),
            wcol(), wcol(), wcol(),
            pl.BlockSpec((1, 2, RWP_CW), lambda i, c: (j, 0, c)),
            pl.BlockSpec((1, 2, D, LORA_W), lambda i, c: (j, 0, 0, 0)),
            pl.BlockSpec((1, 2, LORA_W, RWP_CW), lambda i, c: (j, 0, 0, c)),
            pl.BlockSpec((1, 2, RWP_CW), lambda i, c: (j, 0, c)),
            pl.BlockSpec((1, 2, D, LORA_A), lambda i, c: (j, 0, 0, 0)),
            pl.BlockSpec((1, 2, LORA_A, RWP_CW), lambda i, c: (j, 0, 0, c)),
            pl.BlockSpec((1, D, LORA_G), lambda i, c: (j, 0, 0)),
            pl.BlockSpec((1, LORA_G, RWP_CW), lambda i, c: (j, 0, c)),
            vec(), vec(), vec(),
        ],
        out_specs=[tile] * len(RW_OUT_DTYPES),
        out_shape=[jax.ShapeDtypeStruct((n, D), dt) for dt in RW_OUT_DTYPES],
        scratch_shapes=[pltpu.VMEM((TM, D), BF16), pltpu.VMEM((TM, D), BF16), pltpu.VMEM((TM, D), BF16),
                        pltpu.VMEM((2, TM, LORA_W), BF16), pltpu.VMEM((2, TM, LORA_A), BF16),
                        pltpu.VMEM((TM, LORA_G), BF16)],
        compiler_params=_cparams(("arbitrary", "arbitrary")),
        name="rwkv_proj",
    )(src.array, mod, norm_g, p["mu"], p["w_r"], p["w_k"], p["w_v"], p["w0"], p["w_l1"], p["w_l2"],
      p["a0"], p["a_l1"], p["a_l2"], p["g_l1"], p["g_l2"], p["k_k"], p["k_a"], p["r_k"])


def _scan_precompute(units):
    c = CHUNK
    c2 = 2 * c
    row = lax.broadcasted_iota(jnp.int32, (c2, c2), 0)
    col = lax.broadcasted_iota(jnp.int32, (c2, c2), 1)
    head_a = lax.broadcasted_iota(jnp.int32, (c, LANES), 1) < HS
    own_lanes = jnp.concatenate([head_a, jnp.logical_not(head_a)], axis=0)
    eye = jnp.where(row == col, 1.0, 0.0)

    def two_heads(t):
        return jnp.concatenate([jnp.where(head_a, t, 0.0), jnp.where(head_a, 0.0, t)], axis=0)

    def causal(reverse):
        if reverse:
            return (col % c) > (row % c), (col % c) >= (row % c)
        return (col % c) < (row % c), (col % c) <= (row % c)

    masks = {rev: causal(rev) for rev in sorted({u[6] for u in units})}

    cums = [_cumsum_rows(u[0], u[6]) for u in units]
    st = []
    for (ld, kk, beta, kt, r, v, rev), cum in zip(units, cums):
        tot = cum[0:1] if rev else cum[c - 1:c]
        ginv = jnp.exp(-cum)
        tail = jnp.exp(tot - cum)
        st.append(dict(
            rev=rev, etot=jnp.exp(tot),
            a_t=two_heads(-kk * jnp.exp(cum - ld)), r_t=two_heads(r * jnp.exp(cum)),
            bk=jnp.concatenate([two_heads(beta * ginv), two_heads(kt * ginv)], axis=0),
            bkg=jnp.concatenate([two_heads(beta * tail), two_heads(kt * tail)], axis=0),
            v2=jnp.concatenate([v, v], axis=0), vh=two_heads(v)))
    grams = [_dot_nt(jnp.concatenate([s["a_t"], s["r_t"]], axis=0), s["bk"]) for s in st]
    for s, gram in zip(st, grams):
        strict, incl = masks[s["rev"]]
        s["l_ab"] = jnp.where(strict, gram[:c2, :c2], 0.0)
        s["l_ak"] = jnp.where(strict, gram[:c2, c2:], 0.0)
        s["t_rb"] = jnp.where(incl, gram[c2:, :c2], 0.0)
        s["t_rk"] = jnp.where(incl, gram[c2:, c2:], 0.0)
    lvs = [_dot(s["l_ak"], s["v2"]) for s in st]
    minvs = [eye for _ in st]
    b = 1
    while b < c:
        same = (row // (2 * b)) == (col // (2 * b))
        es = []
        for s in st:
            first, second = (col % (2 * b)) < b, (row % (2 * b)) >= b
            if s["rev"]:
                first, second = (row % (2 * b)) < b, (col % (2 * b)) >= b
            es.append(jnp.where(same & first & second, s["l_ab"], 0.0))
        if b == 1:
            minvs = [m + e for m, e in zip(minvs, es)]
        else:
            half = [_dot(m, e) for m, e in zip(minvs, es)]
            minvs = [m + _dot(h, m) for m, h in zip(minvs, half)]
        b *= 2
    mms = [_dot(m, jnp.concatenate([s["a_t"], lv], axis=1)) for m, s, lv in zip(minvs, st, lvs)]
    zero = jnp.zeros((c2, LANES), BF16)
    tts = [_dot(jnp.concatenate([s["t_rb"], s["t_rk"]], axis=1),
                jnp.concatenate([mm.astype(BF16), jnp.concatenate([zero, s["v2"].astype(BF16)], axis=1)], axis=0))
           for s, mm in zip(st, mms)]
    ps = [_dot_tn(mm[:, :LANES], s["bkg"][:c2]) for s, mm in zip(st, mms)]
    qs = [_dot_tn(jnp.concatenate([jnp.where(own_lanes, mm[:, LANES:], 0.0), s["vh"]], axis=0), s["bkg"])
          for s, mm in zip(st, mms)]
    out = []
    for s, p, q, tt in zip(st, ps, qs, tts):
        reff = s["r_t"] + tt[:, :LANES]
        out.append(dict(etot=s["etot"], p=p, q=q, reff=reff[:c] + reff[c:],
                        y0=jnp.where(head_a, tt[:c, LANES:], tt[c:, LANES:])))
    return out


def _rwkv_scan_kernel(*refs, seq_len, has_init, want_final):
    r_ref, v_ref, kk_ref, ld0_ref, b0_ref, kt0_ref, ld1_ref, b1_ref, kt1_ref = refs[:9]
    pos = 9
    if has_init:
        s0_ref = refs[pos]
        pos += 1
    y_ref = refs[pos]
    pos += 1
    if want_final:
        sf_ref = refs[pos]
        pos += 1
    g_scr, yf_scr, yb_scr = refs[pos:pos + 3]
    nc = seq_len // CHUNK
    ngroups = nc // SCAN_UNROLL
    per_dir = ((ld0_ref, b0_ref, kt0_ref), (ld1_ref, b1_ref, kt1_ref))
    y_dst = (yf_scr, yb_scr)
    nchain = 2 * SCAN_PAIRS

    def group(gi, states):
        where, units = [], []
        for u in range(SCAN_UNROLL):
            for pp in range(SCAN_PAIRS):
                lanes = slice(pp * LANES, (pp + 1) * LANES)
                for d in range(2):
                    ld_ref, b_ref, kt_ref = per_dir[d]
                    cidx = gi * SCAN_UNROLL + u
                    if d == 1:
                        cidx = nc - 1 - cidx
                    start = cidx * CHUNK
                    rw = slice(start, start + CHUNK) if isinstance(start, int) else pl.ds(
                        pl.multiple_of(start, CHUNK), CHUNK)
                    where.append((rw, lanes))
                    units.append(tuple(ref[rw, lanes].astype(F32) for ref in
                                       (ld_ref, kk_ref, b_ref, kt_ref, r_ref, v_ref)) + (d == 1,))
        pre = _scan_precompute(units)
        states = list(states)
        for u in range(SCAN_UNROLL):
            cur = pre[u * nchain:(u + 1) * nchain]
            ys = [_dot_nt(cu["reff"], g) for cu, g in zip(cur, states)]
            gp = [_dot(g, cu["p"]) for cu, g in zip(cur, states)]
            states = [states[ch] * cur[ch]["etot"] + gp[ch] + cur[ch]["q"] for ch in range(nchain)]
            for ch in range(nchain):
                rw, lanes = where[u * nchain + ch]
                y_dst[ch % 2][rw, lanes] = ys[ch] + cur[ch]["y0"]
        return states

    init = [_block_diag2(s0_ref[0, ch % 2, 2 * (ch // 2)], s0_ref[0, ch % 2, 2 * (ch // 2) + 1])
            if has_init else jnp.zeros((LANES, LANES), F32) for ch in range(nchain)]
    if ngroups == 1:
        final = group(0, init)
    else:
        for ch in range(nchain):
            g_scr[ch] = init[ch]

        def body(gi, carry):
            new = group(gi, [g_scr[ch] for ch in range(nchain)])
            for ch in range(nchain):
                g_scr[ch] = new[ch]
            return carry

        lax.fori_loop(0, ngroups, body, 0)
        final = [g_scr[ch] for ch in range(nchain)]
    y_ref[...] = (yf_scr[...] + yb_scr[...]).astype(y_ref.dtype)
    if want_final:
        for ch in range(nchain):
            sf_ref[0, ch % 2, 2 * (ch // 2)] = final[ch][:HS, :HS]
            sf_ref[0, ch % 2, 2 * (ch // 2) + 1] = final[ch][HS:, HS:]


def _rwkv_scan(proj, s_init, n_seq, seq_len, want_final):
    r, v, kk, ld0, b0, kt0, ld1, b1, kt1 = proj[:9]
    n = r.shape[0]
    npair = D // LANES
    width = SCAN_PAIRS * LANES
    blk = pl.BlockSpec((seq_len, width), lambda b, p: (b, p))
    st_spec = pl.BlockSpec((1, 2, 2 * SCAN_PAIRS, HS, HS), lambda b, p: (b, 0, p, 0, 0))
    in_specs = [blk] * 9
    args = [r, v, kk, ld0, b0, kt0, ld1, b1, kt1]
    has_init = s_init is not None
    if has_init:
        in_specs.append(st_spec)
        args.append(s_init)
    out_specs = [blk]
    out_shape = [jax.ShapeDtypeStruct((n, D), BF16)]
    if want_final:
        out_specs.append(st_spec)
        out_shape.append(jax.ShapeDtypeStruct((n_seq, 2, NH, HS, HS), F32))
    res = pl.pallas_call(
        functools.partial(_rwkv_scan_kernel, seq_len=seq_len, has_init=has_init, want_final=want_final),
        grid=(n_seq, npair // SCAN_PAIRS),
        in_specs=in_specs,
        out_specs=out_specs,
        out_shape=out_shape,
        scratch_shapes=[pltpu.VMEM((2 * SCAN_PAIRS, LANES, LANES), F32), pltpu.VMEM((seq_len, width), F32),
                        pltpu.VMEM((seq_len, width), F32)],
        compiler_params=_cparams(("arbitrary", "arbitrary")),
        name="rwkv_scan",
    )(*args)
    return res


def _rwkv_out_kernel(x_ref, m_ref, g_ref, y_ref, bonus_ref, gate_ref, lng_ref, lnb_ref, wo_ref, o_ref,
                     acc_scr):
    j = pl.program_id(1)
    last = pl.num_programs(1) - 1

    def chunk():
        e = _head_indicator(RW_CW)
        mean = jnp.dot(y_ref[...], e, preferred_element_type=F32) * (1.0 / HS)
        yc = y_ref[...].astype(F32) - mean
        var = _dot(yc * yc, e) * (1.0 / HS)
        yn = yc * lax.rsqrt(var + GN_EPS) * lng_ref[...] + lnb_ref[...]
        yn = (yn + bonus_ref[...].astype(F32)) * gate_ref[...].astype(F32)
        return _dot(yn, wo_ref[0])

    @pl.when(j == 0)
    def _():
        acc_scr[...] = chunk()

    @pl.when((j > 0) & (j < last))
    def _():
        acc_scr[...] += chunk()

    @pl.when(j == last)
    def _():
        m = m_ref[0, 0]
        g = g_ref[0]
        o_ref[...] = x_ref[...] + m[2:3] * _rms(acc_scr[...] + chunk(), g[1:2])


def _rwkv_out(src, dst, mod, norm_g, y, bonus, gate, p, layer, j, group_of_tile):
    nj = D // RW_CW
    assert nj >= 2
    tile = pl.BlockSpec((TM, RW_CW), lambda i, c: (i, c))
    vec = pl.BlockSpec((1, RW_CW), lambda i, c: (j, c))
    return _activation_call(
        _rwkv_out_kernel,
        grid=(src.ntiles, nj),
        in_specs=[
            pl.BlockSpec((TM, D), lambda i, c: (i + src.first, 0)),
            _mod_spec(layer, group_of_tile, 2),
            _normg_spec(layer, 2),
            tile, tile, tile, vec, vec,
            pl.BlockSpec((1, RW_CW, D), lambda i, c: (j, c, 0)),
        ],
        args=[src.array, mod, norm_g, y, bonus, gate, p["ln_g"], p["ln_b"], p["w_o"]],
        dst=dst, block_rows=TM, out_index=lambda i, c: (i + dst.first, 0),
        scratch_shapes=[pltpu.VMEM((TM, D), F32)],
        name="rwkv_out")


def _rope_tables(n):
    rows = n // GRID_W
    row = np.repeat(np.arange(rows), GRID_W)
    col = np.tile(np.arange(GRID_W), rows)
    pos = np.stack([row, col], axis=-1).astype(np.float64)
    quarter = D_ROPE // 4
    inv = ROPE_BASE ** (-np.arange(quarter, dtype=np.float64) / quarter)
    ang = pos[:, :, None] * inv
    cos = np.cos(ang)
    sin = np.sin(ang)
    cos_t = np.concatenate([cos, cos], axis=-1).reshape(n, D_ROPE)
    sin_t = np.concatenate([-sin, sin], axis=-1).reshape(n, D_ROPE)
    return cos_t.astype(np.float32), sin_t.astype(np.float32)


def _rope_swap_perm():
    quarter = D_ROPE // 4
    base = np.arange(D_ROPE)
    return np.where((base % (2 * quarter)) < quarter, base + quarter, base - quarter)


def _mla_proj_kernel(*refs, positional):
    (x_ref, m_ref, g_ref, wdq_ref, gq_ref, wqn_ref, wqr_ref, wqs_ref, wdkv_ref, wkr_ref, wks_ref,
     gkv_ref) = refs[:12]
    pos = 12
    if positional:
        cosq_ref, sinq_ref, cosk_ref, sink_ref = refs[pos:pos + 4]
        pos += 4
    qn_o, qr_o, ckv_o, kr_o = refs[pos:pos + 4]
    m = m_ref[0, 0]
    g = g_ref[0]
    h = (_rms(x_ref[...], g[0:1]) * (1.0 + m[1:2]) + m[0:1]).astype(BF16)
    ql = jnp.dot(h, wdq_ref[0].astype(BF16), preferred_element_type=F32)
    ql = (ql * lax.rsqrt(jnp.mean(ql * ql, axis=-1, keepdims=True) + EPS) * gq_ref[...]).astype(BF16)
    qn_o[...] = jnp.dot(ql, wqn_ref[...].astype(BF16), preferred_element_type=F32).astype(qn_o.dtype)
    qr = jnp.dot(ql, wqr_ref[...].astype(BF16), preferred_element_type=F32)
    ckv = jnp.dot(h, wdkv_ref[...].astype(BF16), preferred_element_type=F32)
    ckv_o[...] = ckv * lax.rsqrt(jnp.mean(ckv * ckv, axis=-1, keepdims=True) + EPS) * gkv_ref[...]
    kr = jnp.dot(h, wkr_ref[...].astype(BF16), preferred_element_type=F32)
    if positional:
        qs = jnp.dot(ql, wqs_ref[...].astype(BF16), preferred_element_type=F32)
        ks = jnp.dot(h, wks_ref[...].astype(BF16), preferred_element_type=F32)
        qr = qr * cosq_ref[...] + qs * sinq_ref[...]
        kr = kr * cosk_ref[...] + ks * sink_ref[...]
    qr_o[...] = qr.astype(qr_o.dtype)
    kr_o[...] = kr


def _mla_proj(src, mod, norm_g, p, layer, j, positional, group_of_tile):
    n = src.ntiles * TM
    full = lambda shape: pl.BlockSpec(shape, lambda i: (0,) * len(shape))
    in_specs = [
        pl.BlockSpec((TM, D), lambda i: (i + src.first, 0)),
        _mod_spec(layer, group_of_tile, 1),
        _normg_spec(layer, 1),
        pl.BlockSpec((1, D, Q_RANK), lambda i: (j, 0, 0)),
        pl.BlockSpec((1, Q_RANK), lambda i: (j, 0)),
        full((Q_RANK, MLA_H * D_NOPE)), full((Q_RANK, MLA_H * D_ROPE)), full((Q_RANK, MLA_H * D_ROPE)),
        full((D, KV_RANK)), full((D, D_ROPE)), full((D, D_ROPE)),
        pl.BlockSpec((1, KV_RANK), lambda i: (j, 0)),
    ]
    args = [src.array, mod, norm_g, p["w_dq"], p["g_q"], p["w_uq_nope"], p["w_uq_rope"], p["w_uq_rope_sw"],
            p["w_dkv_c"], p["w_dkv_r"], p["w_dkv_r_sw"], p["g_kv"]]
    if positional:
        cos_t, sin_t = _rope_tables(TM)
        in_specs += [full((TM, MLA_H * D_ROPE)), full((TM, MLA_H * D_ROPE)),
                     full((TM, D_ROPE)), full((TM, D_ROPE))]
        args += [jnp.asarray(np.tile(cos_t, (1, MLA_H))), jnp.asarray(np.tile(sin_t, (1, MLA_H))),
                 jnp.asarray(cos_t), jnp.asarray(sin_t)]
    outs = ((MLA_H * D_NOPE, BF16), (MLA_H * D_ROPE, BF16), (KV_RANK, F32), (D_ROPE, F32))
    return pl.pallas_call(
        functools.partial(_mla_proj_kernel, positional=positional),
        grid=(src.ntiles,),
        in_specs=in_specs,
        out_specs=[pl.BlockSpec((TM, w), lambda i: (i, 0)) for w, _ in outs],
        out_shape=[jax.ShapeDtypeStruct((n, w), dt) for w, dt in outs],
        compiler_params=_cparams(("arbitrary",)),
        name="mla_proj",
    )(*args)


def _mla_attn_kernel(x_ref, m_ref, g_ref, qn_ref, qr_ref, ckv_ref, kr_ref, wuk_ref, wuv_ref, wo_ref,
                     o_ref, kn_scr, vv_scr, oh_scr, *, nb, tq, k_len):
    qi = pl.program_id(1)

    @pl.when(qi == 0)
    def _():
        ckv = ckv_ref[...].astype(BF16)
        kn_scr[...] = jnp.dot(ckv, wuk_ref[0].astype(BF16), preferred_element_type=F32).astype(BF16)
        vv_scr[...] = jnp.dot(ckv, wuv_ref[0].astype(BF16), preferred_element_type=F32).astype(BF16)

    units = [(b, hd) for b in range(nb) for hd in range(MLA_H)]

    def scores(unit):
        b, hd = unit
        qrows = slice(b * tq, (b + 1) * tq)
        krows = slice(b * k_len, (b + 1) * k_len)
        q = jnp.concatenate([qn_ref[qrows, hd * D_NOPE:(hd + 1) * D_NOPE].astype(BF16),
                             qr_ref[qrows, hd * D_ROPE:(hd + 1) * D_ROPE].astype(BF16)], axis=1)
        k = jnp.concatenate([kn_scr[krows, hd * D_NOPE:(hd + 1) * D_NOPE],
                             kr_ref[krows, :].astype(BF16)], axis=1)
        return _dot_nt(q, k) * MLA_SCALE

    pending = [scores(u) for u in units[:ATTN_LOOKAHEAD]]
    for idx, (b, hd) in enumerate(units):
        if idx + ATTN_LOOKAHEAD < len(units):
            pending.append(scores(units[idx + ATTN_LOOKAHEAD]))
        s = pending[idx]
        pexp = jnp.exp(s - jnp.max(s, axis=-1, keepdims=True))
        pv = jnp.dot(pexp.astype(BF16), vv_scr[b * k_len:(b + 1) * k_len, hd * D_V:(hd + 1) * D_V],
                     preferred_element_type=F32)
        oh_scr[b * tq:(b + 1) * tq, hd * D_V:(hd + 1) * D_V] = (
            pv / jnp.sum(pexp, axis=-1, keepdims=True)).astype(BF16)
    o = jnp.dot(oh_scr[...], wo_ref[0].astype(BF16), preferred_element_type=F32)
    m = m_ref[0, 0]
    g = g_ref[0]
    o_ref[...] = x_ref[...] + m[2:3] * _rms(o, g[1:2])


def _mla_attn(src, dst, mod, norm_g, qn, qr, ckv_all, kr_all, p, layer, j, n_seq, q_len, k_len, nb, tq,
              group_of_step):
    nq = q_len // tq
    assert nb == 1 or nq == 1
    rows = nb * tq
    x_first, o_first = src.first * TM // rows, dst.first * TM // rows
    return _activation_call(
        functools.partial(_mla_attn_kernel, nb=nb, tq=tq, k_len=k_len),
        grid=(n_seq // nb, nq),
        in_specs=[
            pl.BlockSpec((rows, D), lambda s, q: (s * nq + q + x_first, 0)),
            pl.BlockSpec((1, 1, N_MOD, D), lambda s, q: (0, group_of_step(s), 0, 0)),
            pl.BlockSpec((1, 4, D), lambda s, q: (layer, 0, 0)),
            pl.BlockSpec((nb * tq, MLA_H * D_NOPE), lambda s, q: (s * nq + q, 0)),
            pl.BlockSpec((nb * tq, MLA_H * D_ROPE), lambda s, q: (s * nq + q, 0)),
            pl.BlockSpec((nb * k_len, KV_RANK), lambda s, q: (s, 0)),
            pl.BlockSpec((nb * k_len, D_ROPE), lambda s, q: (s, 0)),
            pl.BlockSpec((1, KV_RANK, MLA_H * D_NOPE), lambda s, q: (j, 0, 0)),
            pl.BlockSpec((1, KV_RANK, MLA_H * D_V), lambda s, q: (j, 0, 0)),
            pl.BlockSpec((1, MLA_H * D_V, D), lambda s, q: (j, 0, 0)),
        ],
        args=[src.array, mod, norm_g, qn, qr, ckv_all, kr_all, p["w_uk"], p["w_uv"], p["w_o"]],
        dst=dst, block_rows=rows, out_index=lambda s, q: (s * nq + q + o_first, 0),
        scratch_shapes=[pltpu.VMEM((nb * k_len, MLA_H * D_NOPE), BF16),
                        pltpu.VMEM((nb * k_len, MLA_H * D_V), BF16),
                        pltpu.VMEM((nb * tq, MLA_H * D_V), BF16)],
        name="mla_attn")


def kernel(x_prompt, x_sample, state_rwkv, cache_mla_ckv, cache_mla_krope, c, c_ctx, mod_w, mod_b, norm_g,
           mlp_w1, mlp_w2, fft_w_out, conv_w_in, conv_w, conv_w_out, rwkv_mu, rwkv_w_r, rwkv_w_k, rwkv_w_v,
           rwkv_w_o, rwkv_w0, rwkv_w_l1, rwkv_w_l2, rwkv_a0, rwkv_a_l1, rwkv_a_l2, rwkv_g_l1, rwkv_g_l2,
           rwkv_k_k, rwkv_k_a, rwkv_r_k, rwkv_ln_g, rwkv_ln_b, mla_w_dq, mla_g_q, mla_w_uq, mla_w_dkv,
           mla_g_kv, mla_w_uk, mla_w_uv, mla_w_o):
    batch, seq, _ = x_prompt.shape
    dec_batch, dec_seq, _ = x_sample.shape
    past_len = cache_mla_ckv.shape[2]
    assert (batch * seq) % TM == 0 and TM % seq == 0 and dec_seq == TM and seq % CHUNK == 0

    np_tiles = batch * seq // TM
    ns_tiles = dec_batch * dec_seq // TM
    total_rows = (np_tiles + ns_tiles) * TM
    cs = jnp.concatenate([c_ctx[None, :], c, jnp.zeros((MOD_ROWS - 1 - dec_batch, D), F32)], axis=0)
    mod_b3 = mod_b.reshape(DEPTH, 1, N_MOD * D)
    mod = _modulation(cs, mod_w, mod_b3, 0)

    grp_p = lambda i: 0
    grp_s = lambda i: 1 + i
    grp_all = lambda i: jnp.maximum(i - (np_tiles - 1), 0)
    new_rwkv, new_ckv, new_krope = [], [], []
    streams = [(True, seq, batch, grp_p), (False, dec_seq, dec_batch, grp_s)]
    cur = [_Rows(x_prompt.reshape(batch * seq, D), 0, np_tiles),
           _Rows(x_sample.reshape(dec_batch * dec_seq, D), 0, ns_tiles)]

    for i in range(DEPTH):
        kind, j = i % 4, i // 4
        both_shared = cur[0].array is cur[1].array
        if kind == 1 and both_shared:
            shared = _conv(_Rows(cur[0].array, 0, np_tiles + ns_tiles), _Dest(total_rows, 0, True), mod, norm_g,
                           conv_w_in, conv_w, conv_w_out, i, j, (seq, dec_seq), grp_all, n_first=np_tiles)
            cur = [c._replace(array=shared) for c in cur]
        for idx, (is_prompt, slen, nseq, grp) in enumerate(streams):
            if kind == 1 and both_shared:
                break
            src = cur[idx]
            sharing = src.array.shape[0] == total_rows
            dst = _Dest(total_rows, src.first, True) if sharing else _Dest(src.ntiles * TM, 0, False)
            shared = None
            if kind == 0:
                shared = _fnet(src, dst, mod, norm_g, fft_w_out, i, j, slen, grp)
            elif kind == 1:
                shared = _conv(src, dst, mod, norm_g, conv_w_in, conv_w, conv_w_out, i, j, (slen,), grp)
            elif kind == 2:
                p = dict(mu=rwkv_mu, w_r=rwkv_w_r, w_k=rwkv_w_k, w_v=rwkv_w_v, w_o=rwkv_w_o, w0=rwkv_w0,
                         w_l1=rwkv_w_l1, w_l2=rwkv_w_l2, a0=rwkv_a0, a_l1=rwkv_a_l1, a_l2=rwkv_a_l2,
                         g_l1=rwkv_g_l1, g_l2=rwkv_g_l2, k_k=rwkv_k_k, k_a=rwkv_k_a,
                         r_k=rwkv_r_k.reshape(-1, D), ln_g=rwkv_ln_g, ln_b=rwkv_ln_b)
                proj = _rwkv_proj(src, mod, norm_g, p, i, j, slen, grp)
                if is_prompt:
                    y, s_fin = _rwkv_scan(proj, None, nseq, slen, True)
                    new_rwkv.append(s_fin)
                else:
                    (y,) = _rwkv_scan(proj, state_rwkv[:, j], nseq, slen, False)
                shared = _rwkv_out(src, dst, mod, norm_g, y, proj[10], proj[9], p, i, j, grp)
            else:
                perm = _rope_swap_perm()
                w_uq = mla_w_uq[j].reshape(Q_RANK, MLA_H, D_NOPE + D_ROPE)
                w_uq_rope = w_uq[:, :, D_NOPE:]
                w_dkv_r = mla_w_dkv[j][:, KV_RANK:]
                p = dict(w_dq=mla_w_dq, g_q=mla_g_q, g_kv=mla_g_kv, w_uk=mla_w_uk, w_uv=mla_w_uv, w_o=mla_w_o,
                         w_uq_nope=w_uq[:, :, :D_NOPE].reshape(Q_RANK, MLA_H * D_NOPE),
                         w_uq_rope=w_uq_rope.reshape(Q_RANK, MLA_H * D_ROPE),
                         w_uq_rope_sw=w_uq_rope[:, :, perm].reshape(Q_RANK, MLA_H * D_ROPE),
                         w_dkv_c=mla_w_dkv[j][:, :KV_RANK], w_dkv_r=w_dkv_r, w_dkv_r_sw=w_dkv_r[:, perm])
                qn, qr, ckv, kr = _mla_proj(src, mod, norm_g, p, i, j, not is_prompt, grp)
                if is_prompt:
                    new_ckv.append(ckv.reshape(batch, seq, KV_RANK))
                    new_krope.append(kr.reshape(batch, seq, D_ROPE))
                    shared = _mla_attn(src, dst, mod, norm_g, qn, qr, ckv, kr, p, i, j, nseq, slen, slen,
                                       TM // slen, slen, lambda s: 0)
                else:
                    klen = past_len + slen
                    ckv_all = jnp.concatenate([cache_mla_ckv[:, j], ckv.reshape(nseq, slen, KV_RANK)], axis=1)
                    kr_all = jnp.concatenate([cache_mla_krope[:, j], kr.reshape(nseq, slen, D_ROPE)], axis=1)
                    shared = _mla_attn(src, dst, mod, norm_g, qn, qr, ckv_all.reshape(nseq * klen, KV_RANK),
                                       kr_all.reshape(nseq * klen, D_ROPE), p, i, j, nseq, slen, klen, 1, 256,
                                       lambda s: 1 + s)
            cur[idx] = _Rows(shared, dst.first, src.ntiles)
            if sharing:
                cur[1 - idx] = cur[1 - idx]._replace(array=shared)
        if i < DEPTH - 1:
            srcs = [_Rows(cur[0].array, 0, np_tiles + ns_tiles)] if cur[0].array is cur[1].array else cur
            both, mod = _mlp(srcs, _Dest(total_rows, 0, False), mod, norm_g, mlp_w1, mlp_w2, i, grp_all,
                             next_mod=(cs, mod_w, mod_b3, i + 1))
            cur = [_Rows(both, 0, np_tiles), _Rows(both, np_tiles, ns_tiles)]
        else:
            y_prompt, y_sample = [
                _mlp([cur[idx]], _Dest(cur[idx].ntiles * TM, 0, False), mod, norm_g, mlp_w1, mlp_w2, i,
                     streams[idx][3]) for idx in range(2)]

    return (y_prompt.reshape(batch, seq, D), y_sample.reshape(dec_batch, dec_seq, D),
            jnp.stack(new_rwkv, axis=1), jnp.stack(new_ckv, axis=1), jnp.stack(new_krope, axis=1))
```

```python
import functools
from typing import NamedTuple

import numpy as np
import jax
import jax.numpy as jnp
from jax import lax
from jax.experimental import pallas as pl
from jax.experimental.pallas import tpu as pltpu

D = 1024
DEPTH = 4
N_MOD = 6
D_FF = 4 * D
EPS = 1e-6
GRID_W = 64
FFT_GROUPS = 8
FFT_GW = D // FFT_GROUPS
HS = 64
NH = D // HS
LORA_W = 64
LORA_A = 64
LORA_G = 128
GN_EPS = 64e-5
DECAY_SCALE = float(np.exp(-0.5))
MLA_H = 8
D_NOPE = 128
D_ROPE = 64
D_V = 128
KV_RANK = 256
Q_RANK = 384
ROPE_BASE = 10000.0
MLA_SCALE = (D_NOPE + D_ROPE) ** -0.5

F32 = jnp.float32
BF16 = jnp.bfloat16

TM = 1024
LANES = 128
CHUNK = 64
SCAN_UNROLL = 4
SCAN_PAIRS = 2
ATTN_LOOKAHEAD = 2
VMEM_LIMIT = 58 * 1024 * 1024


def _cparams(sem):
    return pltpu.CompilerParams(dimension_semantics=sem, vmem_limit_bytes=VMEM_LIMIT)


def _dot(a, b):
    return jnp.dot(a.astype(BF16), b.astype(BF16), preferred_element_type=F32)


def _dot_nt(a, b):
    return lax.dot_general(a.astype(BF16), b.astype(BF16), (((1,), (1,)), ((), ())),
                           preferred_element_type=F32)


def _dot_tn(a, b):
    return lax.dot_general(a.astype(BF16), b.astype(BF16), (((0,), (0,)), ((), ())),
                           preferred_element_type=F32)


def _block_diag2(a, b):
    za = jnp.zeros((a.shape[0], b.shape[1]), a.dtype)
    zb = jnp.zeros((b.shape[0], a.shape[1]), a.dtype)
    return jnp.concatenate([jnp.concatenate([a, za], axis=1), jnp.concatenate([zb, b], axis=1)], axis=0)


def _cumsum_rows(x, reverse):
    n = x.shape[0]
    idx = lax.broadcasted_iota(jnp.int32, x.shape, 0)
    s = 1
    while s < n:
        if reverse:
            x = x + jnp.where(idx < n - s, pltpu.roll(x, n - s, 0), 0.0)
        else:
            x = x + jnp.where(idx >= s, pltpu.roll(x, s, 0), 0.0)
        s *= 2
    return x


def _rms(x, g):
    return x * lax.rsqrt(jnp.mean(x * x, axis=-1, keepdims=True) + EPS) * g


def _sigmoid(x):
    return 1.0 / (1.0 + jnp.exp(-x))


def _head_indicator(n):
    r = lax.broadcasted_iota(jnp.int32, (n, n), 0) // HS
    c = lax.broadcasted_iota(jnp.int32, (n, n), 1) // HS
    return jnp.where(r == c, 1.0, 0.0).astype(BF16)


def _shift_rows(z, seq_len):
    n, w = z.shape
    pos = lax.broadcasted_iota(jnp.int32, (n, LANES), 0) & (seq_len - 1)
    keep_prev = jnp.concatenate([jnp.where(pos == 0, 0.0, 1.0)] * (w // LANES), axis=1)
    keep_next = jnp.concatenate([jnp.where(pos == seq_len - 1, 0.0, 1.0)] * (w // LANES), axis=1)
    return pltpu.roll(z, 1, 0) * keep_prev, pltpu.roll(z, n - 1, 0) * keep_next


MOD_TN = 1536
MOD_ROWS = 8


def _mod_block(cs_ref, w_ref, b_ref):
    cs = cs_ref[...]
    return _dot(cs * _sigmoid(cs), w_ref[0]) + b_ref[0]


def _mod_kernel(cs_ref, w_ref, b_ref, o_ref):
    o_ref[...] = _mod_block(cs_ref, w_ref, b_ref)


def _modulation(cs, mod_w, mod_b3, layer):
    nj = (N_MOD * D) // MOD_TN
    out = pl.pallas_call(
        _mod_kernel,
        grid=(nj,),
        in_specs=[
            pl.BlockSpec((MOD_ROWS, D), lambda j: (0, 0)),
            pl.BlockSpec((1, D, MOD_TN), lambda j: (layer, 0, j)),
            pl.BlockSpec((1, 1, MOD_TN), lambda j: (layer, 0, j)),
        ],
        out_specs=pl.BlockSpec((MOD_ROWS, MOD_TN), lambda j: (0, j)),
        out_shape=jax.ShapeDtypeStruct((MOD_ROWS, N_MOD * D), F32),
        compiler_params=_cparams(("arbitrary",)),
        name="modulation",
    )(cs, mod_w, mod_b3)
    return out.reshape(1, MOD_ROWS, N_MOD, D)


class _Rows(NamedTuple):
    array: jax.Array
    first: int
    ntiles: int


class _Dest(NamedTuple):
    rows: int
    first: int
    inplace: bool


def _activation_call(kernel, *, grid, in_specs, args, dst, block_rows, out_index, scratch_shapes, name,
                     extra_out=None):
    assert not dst.inplace or args[0].shape[0] == dst.rows
    out_specs = pl.BlockSpec((block_rows, D), out_index)
    out_shape = jax.ShapeDtypeStruct((dst.rows, D), F32)
    if extra_out is not None:
        out_specs, out_shape = [out_specs, extra_out[0]], [out_shape, extra_out[1]]
    return pl.pallas_call(
        kernel, grid=grid, in_specs=in_specs, out_specs=out_specs, out_shape=out_shape,
        scratch_shapes=scratch_shapes, input_output_aliases={0: 0} if dst.inplace else {},
        compiler_params=_cparams(("arbitrary",) * len(grid)), name=name)(*args)


def _mod_spec(layer, group_of_tile, ngrid):
    del layer
    if ngrid == 1:
        return pl.BlockSpec((1, 1, N_MOD, D), lambda i: (0, group_of_tile(i), 0, 0))
    return pl.BlockSpec((1, 1, N_MOD, D), lambda i, j: (0, group_of_tile(i), 0, 0))


def _normg_spec(layer, ngrid):
    if ngrid == 1:
        return pl.BlockSpec((1, 4, D), lambda i: (layer, 0, 0))
    return pl.BlockSpec((1, 4, D), lambda i, j: (layer, 0, 0))


MLP_FC = 1024


def _mlp_kernel(*refs, n_first, with_next_mod):
    if n_first is None:
        (x_ref,), rest = refs[:1], refs[1:]
        read_x = lambda: x_ref[...]
    else:
        (xa_ref, xb_ref), rest = refs[:2], refs[2:]
        read_x = lambda: jnp.where(pl.program_id(0) < n_first, xa_ref[...], xb_ref[...])
    if with_next_mod:
        m_ref, g_ref, w1_ref, w2_ref, cs_ref, mw_ref, mb_ref, o_ref, mo_ref, h_scr, acc_scr = rest
        mo_ref[...] = _mod_block(cs_ref, mw_ref, mb_ref)
    else:
        m_ref, g_ref, w1_ref, w2_ref, o_ref, h_scr, acc_scr = rest
    j = pl.program_id(1)
    m = m_ref[0, 0]
    g = g_ref[0]

    last = pl.num_programs(1) - 1

    def chunk():
        a = jnp.dot(h_scr[...], w1_ref[0].astype(BF16), preferred_element_type=F32)
        a = jnp.maximum(a, 0.0)
        return _dot(a * a, w2_ref[0])

    @pl.when(j == 0)
    def _():
        h = _rms(read_x(), g[2:3]) * (1.0 + m[4:5]) + m[3:4]
        h_scr[...] = h.astype(BF16)
        acc_scr[...] = chunk()

    @pl.when((j > 0) & (j < last))
    def _():
        acc_scr[...] += chunk()

    @pl.when(j == last)
    def _():
        f = acc_scr[...] + chunk()
        o_ref[...] = read_x() + m[5:6] * _rms(f, g[3:4])


def _mlp(srcs, dst, mod, norm_g, w1, w2, layer, group_of_tile, next_mod=None):
    nj = D_FF // MLP_FC
    assert nj >= 2
    if len(srcs) == 1:
        (src,) = srcs
        ntiles, n_first = src.ntiles, None
        x_specs = [pl.BlockSpec((TM, D), lambda i, j: (i + src.first, 0))]
    else:
        sa, sb = srcs
        ntiles, n_first = sa.ntiles + sb.ntiles, sa.ntiles
        x_specs = [pl.BlockSpec((TM, D), lambda i, j: (jnp.minimum(i, sa.ntiles - 1) + sa.first, 0)),
                   pl.BlockSpec((TM, D), lambda i, j: (jnp.maximum(i - sa.ntiles, 0) + sb.first, 0))]
    in_specs = x_specs + [
        _mod_spec(layer, group_of_tile, 2),
        _normg_spec(layer, 2),
        pl.BlockSpec((1, D, MLP_FC), lambda i, j: (layer, 0, j)),
        pl.BlockSpec((1, MLP_FC, D), lambda i, j: (layer, j, 0)),
    ]
    args = [s.array for s in srcs] + [mod, norm_g, w1, w2]
    extra_out = None
    if next_mod is not None:
        cs, mod_w, mod_b3, nxt = next_mod
        cols = (N_MOD * D) // (ntiles * nj)
        assert cols * ntiles * nj == N_MOD * D and cols % LANES == 0
        in_specs += [pl.BlockSpec((MOD_ROWS, D), lambda i, j: (0, 0)),
                     pl.BlockSpec((1, D, cols), lambda i, j: (nxt, 0, i * nj + j)),
                     pl.BlockSpec((1, 1, cols), lambda i, j: (nxt, 0, i * nj + j))]
        args += [cs, mod_w, mod_b3]
        extra_out = (pl.BlockSpec((MOD_ROWS, cols), lambda i, j: (0, i * nj + j)),
                     jax.ShapeDtypeStruct((MOD_ROWS, N_MOD * D), F32))
    res = _activation_call(
        functools.partial(_mlp_kernel, n_first=n_first, with_next_mod=next_mod is not None),
        grid=(ntiles, nj), in_specs=in_specs, args=args,
        dst=dst, block_rows=TM, out_index=lambda i, j: (i + dst.first, 0),
        scratch_shapes=[pltpu.VMEM((TM, D), BF16), pltpu.VMEM((TM, D), F32)],
        name="mlp", extra_out=extra_out)
    if next_mod is None:
        return res
    return res[0], res[1].reshape(1, MOD_ROWS, N_MOD, D)


def _dft_mats(n):
    idx = np.arange(n, dtype=np.int64)
    ang = (2.0 * np.pi / n) * ((idx[:, None] * idx[None, :]) % n).astype(np.float64)
    scale = 1.0 / np.sqrt(n)
    return np.cos(ang) * scale, np.sin(ang) * scale


def _fnet_kernel(x_ref, m_ref, g_ref, cs_ref, cn_ref, sn_ref, w_ref, o_ref, p_scr, q_scr, f_scr,
                 *, seq_len):
    m = m_ref[0, 0]
    g = g_ref[0]
    x = x_ref[...]
    h = (_rms(x, g[0:1]) * (1.0 + m[1:2]) + m[0:1]).astype(BF16)
    cs = cs_ref[...].astype(BF16)
    for gi in range(FFT_GROUPS):
        pq = jnp.dot(h[:, gi * FFT_GW:(gi + 1) * FFT_GW], cs, preferred_element_type=F32)
        p_scr[:, gi * FFT_GW:(gi + 1) * FFT_GW] = pq[:, :FFT_GW].astype(BF16)
        q_scr[:, gi * FFT_GW:(gi + 1) * FFT_GW] = pq[:, FFT_GW:].astype(BF16)
    cn = cn_ref[...].astype(BF16)
    sn = sn_ref[...].astype(BF16)
    for s in range(TM // seq_len):
        rows = slice(s * seq_len, (s + 1) * seq_len)
        f = (jnp.dot(cn, p_scr[rows, :], preferred_element_type=F32)
             - jnp.dot(sn, q_scr[rows, :], preferred_element_type=F32))
        f_scr[rows, :] = f.astype(BF16)
    o = jnp.dot(f_scr[...], w_ref[0].astype(BF16), preferred_element_type=F32)
    o_ref[...] = x + m[2:3] * _rms(o, g[1:2])


def _fnet(src, dst, mod, norm_g, w_out, layer, j, seq_len, group_of_tile):
    cg, sg = _dft_mats(FFT_GW)
    cs = jnp.asarray(np.concatenate([cg, sg], axis=1), F32)
    cn_np, sn_np = _dft_mats(seq_len)
    cn = jnp.asarray(cn_np, F32)
    sn = jnp.asarray(sn_np, F32)
    return _activation_call(
        functools.partial(_fnet_kernel, seq_len=seq_len),
        grid=(src.ntiles,),
        in_specs=[
            pl.BlockSpec((TM, D), lambda i: (i + src.first, 0)),
            _mod_spec(layer, group_of_tile, 1),
            _normg_spec(layer, 1),
            pl.BlockSpec((FFT_GW, 2 * FFT_GW), lambda i: (0, 0)),
            pl.BlockSpec((seq_len, seq_len), lambda i: (0, 0)),
            pl.BlockSpec((seq_len, seq_len), lambda i: (0, 0)),
            pl.BlockSpec((1, D, D), lambda i: (j, 0, 0)),
        ],
        args=[src.array, mod, norm_g, cs, cn, sn, w_out],
        dst=dst, block_rows=TM, out_index=lambda i: (i + dst.first, 0),
        scratch_shapes=[pltpu.VMEM((TM, D), BF16), pltpu.VMEM((TM, D), BF16),
                        pltpu.VMEM((TM, D), BF16)],
        name="fourier_mix")


CONV_CW = 512


def _conv_kernel(x_ref, m_ref, g_ref, wb_ref, wc_ref, wu_ref, cw_ref, wo_ref, o_ref, h_scr, acc_scr,
                 *, seq_lens, n_first):
    j = pl.program_id(1)
    m = m_ref[0, 0]
    g = g_ref[0]
    seq_len = seq_lens[0] if n_first is None else jnp.where(pl.program_id(0) < n_first, *seq_lens)

    def chunk():
        h = h_scr[...]
        hw = CONV_CW // 2
        proj = []
        for s in range(2):
            cols = slice(s * hw, (s + 1) * hw)
            proj.append([jnp.dot(h, w_ref[0, :, cols].astype(BF16), preferred_element_type=F32)
                         for w_ref in (wb_ref, wc_ref, wu_ref)])
        out = None
        for s in range(2):
            cols = slice(s * hw, (s + 1) * hw)
            bg, cg, u = proj[s]
            z = cg * u
            z_prev, z_next = _shift_rows(z, seq_len)
            conv = z_prev * cw_ref[0, 0:1, cols] + z * cw_ref[0, 1:2, cols] + z_next * cw_ref[0, 2:3, cols]
            part = _dot(bg * conv, wo_ref[0, cols, :])
            out = part if out is None else out + part
        return out

    last = pl.num_programs(1) - 1

    @pl.when(j == 0)
    def _():
        h = _rms(x_ref[...], g[0:1]) * (1.0 + m[1:2]) + m[0:1]
        h_scr[...] = h.astype(BF16)
        acc_scr[...] = chunk()

    @pl.when((j > 0) & (j < last))
    def _():
        acc_scr[...] += chunk()

    @pl.when(j == last)
    def _():
        o_ref[...] = x_ref[...] + m[2:3] * _rms(acc_scr[...] + chunk(), g[1:2])


def _conv(src, dst, mod, norm_g, w_in, w_conv, w_out, layer, j, seq_lens, group_of_tile, n_first=None):
    nj = D // CONV_CW
    assert nj >= 2
    assert all(s & (s - 1) == 0 and TM % s == 0 for s in seq_lens)
    return _activation_call(
        functools.partial(_conv_kernel, seq_lens=seq_lens, n_first=n_first),
        grid=(src.ntiles, nj),
        in_specs=[
            pl.BlockSpec((TM, D), lambda i, c: (i + src.first, 0)),
            _mod_spec(layer, group_of_tile, 2),
            _normg_spec(layer, 2),
            pl.BlockSpec((1, D, CONV_CW), lambda i, c: (j, 0, c)),
            pl.BlockSpec((1, D, CONV_CW), lambda i, c: (j, 0, nj + c)),
            pl.BlockSpec((1, D, CONV_CW), lambda i, c: (j, 0, 2 * nj + c)),
            pl.BlockSpec((1, 3, CONV_CW), lambda i, c: (j, 0, c)),
            pl.BlockSpec((1, CONV_CW, D), lambda i, c: (j, c, 0)),
        ],
        args=[src.array, mod, norm_g, w_in, w_in, w_in, w_conv, w_out],
        dst=dst, block_rows=TM, out_index=lambda i, c: (i + dst.first, 0),
        scratch_shapes=[pltpu.VMEM((TM, D), BF16), pltpu.VMEM((TM, D), F32)],
        name="short_conv")


RWP_CW = 256
RW_OUT_DTYPES = (BF16, BF16, BF16, F32, BF16, BF16, F32, BF16, BF16, BF16, BF16)
RW_CW = 512


def _rwkv_proj_kernel(x_ref, m_ref, g_ref, mu_ref, wr_ref, wk_ref, wv_ref, w0_ref, wl1_ref, wl2_ref,
                      a0_ref, al1_ref, al2_ref, gl1_ref, gl2_ref, kk_ref, ka_ref, rk_ref,
                      r_o, v_o, kk_o, ld0_o, b0_o, kt0_o, ld1_o, b1_o, kt1_o, g_o, bonus_o,
                      xr_scr, xk_scr, xv_scr, tw_scr, ta_scr, sg_scr, *, seq_lens, n_first):
    j = pl.program_id(1)
    seq_len = seq_lens[0] if n_first is None else jnp.where(pl.program_id(0) < n_first, *seq_lens)

    def chunk():
        e = _head_indicator(RWP_CW)
        r = jnp.dot(xr_scr[...], wr_ref[0].astype(BF16), preferred_element_type=F32)
        k = jnp.dot(xk_scr[...], wk_ref[0].astype(BF16), preferred_element_type=F32)
        v = jnp.dot(xv_scr[...], wv_ref[0].astype(BF16), preferred_element_type=F32)
        g_o[...] = jnp.dot(sg_scr[...], gl2_ref[0].astype(BF16), preferred_element_type=F32).astype(g_o.dtype)
        zws = [jnp.dot(tw_scr[d], wl2_ref[0, d].astype(BF16), preferred_element_type=F32) for d in range(2)]
        zas = [jnp.dot(ta_scr[d], al2_ref[0, d].astype(BF16), preferred_element_type=F32) for d in range(2)]
        kk = k * kk_ref[...]
        kk = kk * lax.rsqrt(_dot(kk * kk, e) + 1e-12)
        r_o[...] = r.astype(r_o.dtype)
        v_o[...] = v.astype(v_o.dtype)
        kk_o[...] = kk.astype(kk_o.dtype)
        k_scaled = k * ka_ref[...]
        kt_sum = None
        for d, (ld_o, b_o, kt_o) in enumerate(((ld0_o, b0_o, kt0_o), (ld1_o, b1_o, kt1_o))):
            ld_o[...] = -DECAY_SCALE * _sigmoid(w0_ref[0, d:d + 1, :] + zws[d])
            a = _sigmoid(a0_ref[0, d:d + 1, :] + zas[d])
            kt = k + k_scaled * (a - 1.0)
            b_o[...] = (kk * a).astype(b_o.dtype)
            kt_o[...] = kt.astype(kt_o.dtype)
            kt_sum = kt if kt_sum is None else kt_sum + kt
        bonus_o[...] = (_dot(r * rk_ref[...] * kt_sum, e) * v).astype(bonus_o.dtype)

    @pl.when(j == 0)
    def _():
        m = m_ref[0, 0]
        g = g_ref[0]
        mu = mu_ref[0]
        h = _rms(x_ref[...], g[0:1]) * (1.0 + m[1:2]) + m[0:1]
        h_prev, h_next = _shift_rows(h, seq_len)
        dx = 0.5 * (h_prev + h_next) - h
        xr_scr[...] = (h + dx * mu[0:1]).astype(BF16)
        xk_scr[...] = (h + dx * mu[2:3]).astype(BF16)
        xv_scr[...] = (h + dx * mu[3:4]).astype(BF16)
        xw = (h + dx * mu[1:2]).astype(BF16)
        xa = (h + dx * mu[4:5]).astype(BF16)
        xg = (h + dx * mu[5:6]).astype(BF16)
        for d in range(2):
            tw_scr[d] = jnp.tanh(_dot(xw, wl1_ref[0, d])).astype(BF16)
            ta_scr[d] = _dot(xa, al1_ref[0, d]).astype(BF16)
        sg_scr[...] = _sigmoid(_dot(xg, gl1_ref[0])).astype(BF16)
        chunk()

    @pl.when(j > 0)
    def _():
        chunk()


def _rwkv_proj(src, mod, norm_g, p, layer, j, seq_lens, group_of_tile, n_first=None):
    assert all(s & (s - 1) == 0 and TM % s == 0 for s in seq_lens)
    n = src.ntiles * TM
    nj = D // RWP_CW
    tile = pl.BlockSpec((TM, RWP_CW), lambda i, c: (i, c))
    wcol = lambda: pl.BlockSpec((1, D, RWP_CW), lambda i, c: (j, 0, c))
    vec = lambda: pl.BlockSpec((1, RWP_CW), lambda i, c: (j, c))
    return pl.pallas_call(
        functools.partial(_rwkv_proj_kernel, seq_lens=seq_lens, n_first=n_first),
        grid=(src.ntiles, nj),
        in_specs=[
            pl.BlockSpec((TM, D), lambda i, c: (i + src.first, 0)),
            _mod_spec(layer, group_of_tile, 2),
            _normg_spec(layer, 2),
            pl.BlockSpec((1, 6, D), lambda i, c: (j, 0, 0)),
            wcol(), wcol(), wcol(),
            pl.BlockSpec((1, 2, RWP_CW), lambda i, c: (j, 0, c)),
            pl.BlockSpec((1, 2, D, LORA_W), lambda i, c: (j, 0, 0, 0)),
            pl.BlockSpec((1, 2, LORA_W, RWP_CW), lambda i, c: (j, 0, 0, c)),
            pl.BlockSpec((1, 2, RWP_CW), lambda i, c: (j, 0, c)),
            pl.BlockSpec((1, 2, D, LORA_A), lambda i, c: (j, 0, 0, 0)),
            pl.BlockSpec((1, 2, LORA_A, RWP_CW), lambda i, c: (j, 0, 0, c)),
            pl.BlockSpec((1, D, LORA_G), lambda i, c: (j, 0, 0)),
            pl.BlockSpec((1, LORA_G, RWP_CW), lambda i, c: (j, 0, c)),
            vec(), vec(), vec(),
        ],
        out_specs=[tile] * len(RW_OUT_DTYPES),
        out_shape=[jax.ShapeDtypeStruct((n, D), dt) for dt in RW_OUT_DTYPES],
        scratch_shapes=[pltpu.VMEM((TM, D), BF16), pltpu.VMEM((TM, D), BF16), pltpu.VMEM((TM, D), BF16),
                        pltpu.VMEM((2, TM, LORA_W), BF16), pltpu.VMEM((2, TM, LORA_A), BF16),
                        pltpu.VMEM((TM, LORA_G), BF16)],
        compiler_params=_cparams(("arbitrary", "arbitrary")),
        name="rwkv_proj",
    )(src.array, mod, norm_g, p["mu"], p["w_r"], p["w_k"], p["w_v"], p["w0"], p["w_l1"], p["w_l2"],
      p["a0"], p["a_l1"], p["a_l2"], p["g_l1"], p["g_l2"], p["k_k"], p["k_a"], p["r_k"])


def _scan_precompute(units):
    c = CHUNK
    c2 = 2 * c
    row = lax.broadcasted_iota(jnp.int32, (c2, c2), 0)
    col = lax.broadcasted_iota(jnp.int32, (c2, c2), 1)
    head_a = lax.broadcasted_iota(jnp.int32, (c, LANES), 1) < HS
    own_lanes = jnp.concatenate([head_a, jnp.logical_not(head_a)], axis=0)
    eye = jnp.where(row == col, 1.0, 0.0)

    def two_heads(t):
        return jnp.concatenate([jnp.where(head_a, t, 0.0), jnp.where(head_a, 0.0, t)], axis=0)

    def causal(reverse):
        if reverse:
            return (col % c) > (row % c), (col % c) >= (row % c)
        return (col % c) < (row % c), (col % c) <= (row % c)

    masks = {rev: causal(rev) for rev in sorted({u[6] for u in units})}

    cums = [_cumsum_rows(u[0], u[6]) for u in units]
    st = []
    for (ld, kk, beta, kt, r, v, rev), cum in zip(units, cums):
        tot = cum[0:1] if rev else cum[c - 1:c]
        ginv = jnp.exp(-cum)
        tail = jnp.exp(tot - cum)
        st.append(dict(
            rev=rev, etot=jnp.exp(tot),
            a_t=two_heads(-kk * jnp.exp(cum - ld)), r_t=two_heads(r * jnp.exp(cum)),
            bk=jnp.concatenate([two_heads(beta * ginv), two_heads(kt * ginv)], axis=0),
            bkg=jnp.concatenate([two_heads(beta * tail), two_heads(kt * tail)], axis=0),
            v2=jnp.concatenate([v, v], axis=0), vh=two_heads(v)))
    grams = [_dot_nt(jnp.concatenate([s["a_t"], s["r_t"]], axis=0), s["bk"]) for s in st]
    for s, gram in zip(st, grams):
        strict, incl = masks[s["rev"]]
        s["l_ab"] = jnp.where(strict, gram[:c2, :c2], 0.0)
        s["l_ak"] = jnp.where(strict, gram[:c2, c2:], 0.0)
        s["t_rb"] = jnp.where(incl, gram[c2:, :c2], 0.0)
        s["t_rk"] = jnp.where(incl, gram[c2:, c2:], 0.0)
    lvs = [_dot(s["l_ak"], s["v2"]) for s in st]
    minvs = [eye for _ in st]
    b = 1
    while b < c:
        same = (row // (2 * b)) == (col // (2 * b))
        es = []
        for s in st:
            first, second = (col % (2 * b)) < b, (row % (2 * b)) >= b
            if s["rev"]:
                first, second = (row % (2 * b)) < b, (col % (2 * b)) >= b
            es.append(jnp.where(same & first & second, s["l_ab"], 0.0))
        if b == 1:
            minvs = [m + e for m, e in zip(minvs, es)]
        else:
            half = [_dot(m, e) for m, e in zip(minvs, es)]
            minvs = [m + _dot(h, m) for m, h in zip(minvs, half)]
        b *= 2
    mms = [_dot(m, jnp.concatenate([s["a_t"], lv], axis=1)) for m, s, lv in zip(minvs, st, lvs)]
    zero = jnp.zeros((c2, LANES), BF16)
    tts = [_dot(jnp.concatenate([s["t_rb"], s["t_rk"]], axis=1),
                jnp.concatenate([mm.astype(BF16), jnp.concatenate([zero, s["v2"].astype(BF16)], axis=1)], axis=0))
           for s, mm in zip(st, mms)]
    ps = [_dot_tn(mm[:, :LANES], s["bkg"][:c2]) for s, mm in zip(st, mms)]
    qs = [_dot_tn(jnp.concatenate([jnp.where(own_lanes, mm[:, LANES:], 0.0), s["vh"]], axis=0), s["bkg"])
          for s, mm in zip(st, mms)]
    out = []
    for s, p, q, tt in zip(st, ps, qs, tts):
        reff = s["r_t"] + tt[:, :LANES]
        out.append(dict(etot=s["etot"], p=p, q=q, reff=reff[:c] + reff[c:],
                        y0=jnp.where(head_a, tt[:c, LANES:], tt[c:, LANES:])))
    return out


def _rwkv_scan_kernel(*refs, seq_len, has_init, want_final):
    r_ref, v_ref, kk_ref, ld0_ref, b0_ref, kt0_ref, ld1_ref, b1_ref, kt1_ref = refs[:9]
    pos = 9
    if has_init:
        s0_ref = refs[pos]
        pos += 1
    y_ref = refs[pos]
    pos += 1
    if want_final:
        sf_ref = refs[pos]
        pos += 1
    g_scr, yf_scr, yb_scr = refs[pos:pos + 3]
    nc = seq_len // CHUNK
    ngroups = nc // SCAN_UNROLL
    per_dir = ((ld0_ref, b0_ref, kt0_ref), (ld1_ref, b1_ref, kt1_ref))
    y_dst = (yf_scr, yb_scr)
    nchain = 2 * SCAN_PAIRS

    def group(gi, states):
        where, units = [], []
        for u in range(SCAN_UNROLL):
            for pp in range(SCAN_PAIRS):
                lanes = slice(pp * LANES, (pp + 1) * LANES)
                for d in range(2):
                    ld_ref, b_ref, kt_ref = per_dir[d]
                    cidx = gi * SCAN_UNROLL + u
                    if d == 1:
                        cidx = nc - 1 - cidx
                    start = cidx * CHUNK
                    rw = slice(start, start + CHUNK) if isinstance(start, int) else pl.ds(
                        pl.multiple_of(start, CHUNK), CHUNK)
                    where.append((rw, lanes))
                    units.append(tuple(ref[rw, lanes].astype(F32) for ref in
                                       (ld_ref, kk_ref, b_ref, kt_ref, r_ref, v_ref)) + (d == 1,))
        pre = _scan_precompute(units)
        states = list(states)
        for u in range(SCAN_UNROLL):
            cur = pre[u * nchain:(u + 1) * nchain]
            ys = [_dot_nt(cu["reff"], g) for cu, g in zip(cur, states)]
            gp = [_dot(g, cu["p"]) for cu, g in zip(cur, states)]
            states = [states[ch] * cur[ch]["etot"] + gp[ch] + cur[ch]["q"] for ch in range(nchain)]
            for ch in range(nchain):
                rw, lanes = where[u * nchain + ch]
                y_dst[ch % 2][rw, lanes] = ys[ch] + cur[ch]["y0"]
        return states

    init = [_block_diag2(s0_ref[0, ch % 2, 2 * (ch // 2)], s0_ref[0, ch % 2, 2 * (ch // 2) + 1])
            if has_init else jnp.zeros((LANES, LANES), F32) for ch in range(nchain)]
    if ngroups == 1:
        final = group(0, init)
    else:
        for ch in range(nchain):
            g_scr[ch] = init[ch]

        def body(gi, carry):
            new = group(gi, [g_scr[ch] for ch in range(nchain)])
            for ch in range(nchain):
                g_scr[ch] = new[ch]
            return carry

        lax.fori_loop(0, ngroups, body, 0)
        final = [g_scr[ch] for ch in range(nchain)]
    y_ref[...] = (yf_scr[...] + yb_scr[...]).astype(y_ref.dtype)
    if want_final:
        for ch in range(nchain):
            sf_ref[0, ch % 2, 2 * (ch // 2)] = final[ch][:HS, :HS]
            sf_ref[0, ch % 2, 2 * (ch // 2) + 1] = final[ch][HS:, HS:]


def _rwkv_scan(proj, s_init, n_seq, seq_len, want_final, first_seq=0):
    r, v, kk, ld0, b0, kt0, ld1, b1, kt1 = proj[:9]
    n = n_seq * seq_len
    npair = D // LANES
    width = SCAN_PAIRS * LANES
    blk = pl.BlockSpec((seq_len, width), lambda b, p: (b + first_seq, p))
    st_spec = pl.BlockSpec((1, 2, 2 * SCAN_PAIRS, HS, HS), lambda b, p: (b, 0, p, 0, 0))
    in_specs = [blk] * 9
    args = [r, v, kk, ld0, b0, kt0, ld1, b1, kt1]
    has_init = s_init is not None
    if has_init:
        in_specs.append(st_spec)
        args.append(s_init)
    out_specs = [pl.BlockSpec((seq_len, width), lambda b, p: (b, p))]
    out_shape = [jax.ShapeDtypeStruct((n, D), BF16)]
    if want_final:
        out_specs.append(st_spec)
        out_shape.append(jax.ShapeDtypeStruct((n_seq, 2, NH, HS, HS), F32))
    res = pl.pallas_call(
        functools.partial(_rwkv_scan_kernel, seq_len=seq_len, has_init=has_init, want_final=want_final),
        grid=(n_seq, npair // SCAN_PAIRS),
        in_specs=in_specs,
        out_specs=out_specs,
        out_shape=out_shape,
        scratch_shapes=[pltpu.VMEM((2 * SCAN_PAIRS, LANES, LANES), F32), pltpu.VMEM((seq_len, width), F32),
                        pltpu.VMEM((seq_len, width), F32)],
        compiler_params=_cparams(("arbitrary", "arbitrary")),
        name="rwkv_scan",
    )(*args)
    return res


def _rwkv_out_kernel(x_ref, m_ref, g_ref, y_ref, bonus_ref, gate_ref, lng_ref, lnb_ref, wo_ref, o_ref,
                     acc_scr):
    j = pl.program_id(1)
    last = pl.num_programs(1) - 1

    def chunk():
        e = _head_indicator(RW_CW)
        mean = jnp.dot(y_ref[...], e, preferred_element_type=F32) * (1.0 / HS)
        yc = y_ref[...].astype(F32) - mean
        var = _dot(yc * yc, e) * (1.0 / HS)
        yn = yc * lax.rsqrt(var + GN_EPS) * lng_ref[...] + lnb_ref[...]
        yn = (yn + bonus_ref[...].astype(F32)) * gate_ref[...].astype(F32)
        return _dot(yn, wo_ref[0])

    @pl.when(j == 0)
    def _():
        acc_scr[...] = chunk()

    @pl.when((j > 0) & (j < last))
    def _():
        acc_scr[...] += chunk()

    @pl.when(j == last)
    def _():
        m = m_ref[0, 0]
        g = g_ref[0]
        o_ref[...] = x_ref[...] + m[2:3] * _rms(acc_scr[...] + chunk(), g[1:2])


def _rwkv_out(src, dst, mod, norm_g, y, bonus, gate, p, layer, j, group_of_tile):
    nj = D // RW_CW
    assert nj >= 2
    tile = pl.BlockSpec((TM, RW_CW), lambda i, c: (i, c))
    both = pl.BlockSpec((TM, RW_CW), lambda i, c: (i + src.first, c))
    vec = pl.BlockSpec((1, RW_CW), lambda i, c: (j, c))
    return _activation_call(
        _rwkv_out_kernel,
        grid=(src.ntiles, nj),
        in_specs=[
            pl.BlockSpec((TM, D), lambda i, c: (i + src.first, 0)),
            _mod_spec(layer, group_of_tile, 2),
            _normg_spec(layer, 2),
            tile, both, both, vec, vec,
            pl.BlockSpec((1, RW_CW, D), lambda i, c: (j, c, 0)),
        ],
        args=[src.array, mod, norm_g, y, bonus, gate, p["ln_g"], p["ln_b"], p["w_o"]],
        dst=dst, block_rows=TM, out_index=lambda i, c: (i + dst.first, 0),
        scratch_shapes=[pltpu.VMEM((TM, D), F32)],
        name="rwkv_out")


def _rope_tables(n):
    rows = n // GRID_W
    row = np.repeat(np.arange(rows), GRID_W)
    col = np.tile(np.arange(GRID_W), rows)
    pos = np.stack([row, col], axis=-1).astype(np.float64)
    quarter = D_ROPE // 4
    inv = ROPE_BASE ** (-np.arange(quarter, dtype=np.float64) / quarter)
    ang = pos[:, :, None] * inv
    cos = np.cos(ang)
    sin = np.sin(ang)
    cos_t = np.concatenate([cos, cos], axis=-1).reshape(n, D_ROPE)
    sin_t = np.concatenate([-sin, sin], axis=-1).reshape(n, D_ROPE)
    return cos_t.astype(np.float32), sin_t.astype(np.float32)


def _rope_swap_perm():
    quarter = D_ROPE // 4
    base = np.arange(D_ROPE)
    return np.where((base % (2 * quarter)) < quarter, base + quarter, base - quarter)


def _mla_proj_kernel(*refs, positional):
    (x_ref, m_ref, g_ref, wdq_ref, gq_ref, wqn_ref, wqr_ref, wqs_ref, wdkv_ref, wkr_ref, wks_ref,
     gkv_ref) = refs[:12]
    pos = 12
    if positional:
        cosq_ref, sinq_ref, cosk_ref, sink_ref = refs[pos:pos + 4]
        pos += 4
    qn_o, qr_o, ckv_o, kr_o = refs[pos:pos + 4]
    m = m_ref[0, 0]
    g = g_ref[0]
    h = (_rms(x_ref[...], g[0:1]) * (1.0 + m[1:2]) + m[0:1]).astype(BF16)
    ql = jnp.dot(h, wdq_ref[0].astype(BF16), preferred_element_type=F32)
    ql = (ql * lax.rsqrt(jnp.mean(ql * ql, axis=-1, keepdims=True) + EPS) * gq_ref[...]).astype(BF16)
    qn_o[...] = jnp.dot(ql, wqn_ref[...].astype(BF16), preferred_element_type=F32).astype(qn_o.dtype)
    qr = jnp.dot(ql, wqr_ref[...].astype(BF16), preferred_element_type=F32)
    ckv = jnp.dot(h, wdkv_ref[...].astype(BF16), preferred_element_type=F32)
    ckv_o[...] = ckv * lax.rsqrt(jnp.mean(ckv * ckv, axis=-1, keepdims=True) + EPS) * gkv_ref[...]
    kr = jnp.dot(h, wkr_ref[...].astype(BF16), preferred_element_type=F32)
    if positional:
        qs = jnp.dot(ql, wqs_ref[...].astype(BF16), preferred_element_type=F32)
        ks = jnp.dot(h, wks_ref[...].astype(BF16), preferred_element_type=F32)
        qr = qr * cosq_ref[...] + qs * sinq_ref[...]
        kr = kr * cosk_ref[...] + ks * sink_ref[...]
    qr_o[...] = qr.astype(qr_o.dtype)
    kr_o[...] = kr


def _mla_proj(src, mod, norm_g, p, layer, j, positional, group_of_tile):
    n = src.ntiles * TM
    full = lambda shape: pl.BlockSpec(shape, lambda i: (0,) * len(shape))
    in_specs = [
        pl.BlockSpec((TM, D), lambda i: (i + src.first, 0)),
        _mod_spec(layer, group_of_tile, 1),
        _normg_spec(layer, 1),
        pl.BlockSpec((1, D, Q_RANK), lambda i: (j, 0, 0)),
        pl.BlockSpec((1, Q_RANK), lambda i: (j, 0)),
        full((Q_RANK, MLA_H * D_NOPE)), full((Q_RANK, MLA_H * D_ROPE)), full((Q_RANK, MLA_H * D_ROPE)),
        full((D, KV_RANK)), full((D, D_ROPE)), full((D, D_ROPE)),
        pl.BlockSpec((1, KV_RANK), lambda i: (j, 0)),
    ]
    args = [src.array, mod, norm_g, p["w_dq"], p["g_q"], p["w_uq_nope"], p["w_uq_rope"], p["w_uq_rope_sw"],
            p["w_dkv_c"], p["w_dkv_r"], p["w_dkv_r_sw"], p["g_kv"]]
    if positional:
        cos_t, sin_t = _rope_tables(TM)
        in_specs += [full((TM, MLA_H * D_ROPE)), full((TM, MLA_H * D_ROPE)),
                     full((TM, D_ROPE)), full((TM, D_ROPE))]
        args += [jnp.asarray(np.tile(cos_t, (1, MLA_H))), jnp.asarray(np.tile(sin_t, (1, MLA_H))),
                 jnp.asarray(cos_t), jnp.asarray(sin_t)]
    outs = ((MLA_H * D_NOPE, BF16), (MLA_H * D_ROPE, BF16), (KV_RANK, F32), (D_ROPE, F32))
    return pl.pallas_call(
        functools.partial(_mla_proj_kernel, positional=positional),
        grid=(src.ntiles,),
        in_specs=in_specs,
        out_specs=[pl.BlockSpec((TM, w), lambda i: (i, 0)) for w, _ in outs],
        out_shape=[jax.ShapeDtypeStruct((n, w), dt) for w, dt in outs],
        compiler_params=_cparams(("arbitrary",)),
        name="mla_proj",
    )(*args)


def _mla_attn_kernel(x_ref, m_ref, g_ref, qn_ref, qr_ref, ckv_ref, kr_ref, wuk_ref, wuv_ref, wo_ref,
                     o_ref, kn_scr, vv_scr, oh_scr, *, nb, tq, k_len):
    qi = pl.program_id(1)

    @pl.when(qi == 0)
    def _():
        ckv = ckv_ref[...].astype(BF16)
        kn_scr[...] = jnp.dot(ckv, wuk_ref[0].astype(BF16), preferred_element_type=F32).astype(BF16)
        vv_scr[...] = jnp.dot(ckv, wuv_ref[0].astype(BF16), preferred_element_type=F32).astype(BF16)

    units = [(b, hd) for b in range(nb) for hd in range(MLA_H)]

    def scores(unit):
        b, hd = unit
        qrows = slice(b * tq, (b + 1) * tq)
        krows = slice(b * k_len, (b + 1) * k_len)
        q = jnp.concatenate([qn_ref[qrows, hd * D_NOPE:(hd + 1) * D_NOPE].astype(BF16),
                             qr_ref[qrows, hd * D_ROPE:(hd + 1) * D_ROPE].astype(BF16)], axis=1)
        k = jnp.concatenate([kn_scr[krows, hd * D_NOPE:(hd + 1) * D_NOPE],
                             kr_ref[krows, :].astype(BF16)], axis=1)
        return _dot_nt(q, k) * MLA_SCALE

    pending = [scores(u) for u in units[:ATTN_LOOKAHEAD]]
    for idx, (b, hd) in enumerate(units):
        if idx + ATTN_LOOKAHEAD < len(units):
            pending.append(scores(units[idx + ATTN_LOOKAHEAD]))
        s = pending[idx]
        pexp = jnp.exp(s - jnp.max(s, axis=-1, keepdims=True))
        pv = jnp.dot(pexp.astype(BF16), vv_scr[b * k_len:(b + 1) * k_len, hd * D_V:(hd + 1) * D_V],
                     preferred_element_type=F32)
        oh_scr[b * tq:(b + 1) * tq, hd * D_V:(hd + 1) * D_V] = (
            pv / jnp.sum(pexp, axis=-1, keepdims=True)).astype(BF16)
    o = jnp.dot(oh_scr[...], wo_ref[0].astype(BF16), preferred_element_type=F32)
    m = m_ref[0, 0]
    g = g_ref[0]
    o_ref[...] = x_ref[...] + m[2:3] * _rms(o, g[1:2])


def _mla_attn(src, dst, mod, norm_g, qn, qr, ckv_all, kr_all, p, layer, j, n_seq, q_len, k_len, nb, tq,
              group_of_step):
    nq = q_len // tq
    assert nb == 1 or nq == 1
    rows = nb * tq
    x_first, o_first = src.first * TM // rows, dst.first * TM // rows
    return _activation_call(
        functools.partial(_mla_attn_kernel, nb=nb, tq=tq, k_len=k_len),
        grid=(n_seq // nb, nq),
        in_specs=[
            pl.BlockSpec((rows, D), lambda s, q: (s * nq + q + x_first, 0)),
            pl.BlockSpec((1, 1, N_MOD, D), lambda s, q: (0, group_of_step(s), 0, 0)),
            pl.BlockSpec((1, 4, D), lambda s, q: (layer, 0, 0)),
            pl.BlockSpec((nb * tq, MLA_H * D_NOPE), lambda s, q: (s * nq + q, 0)),
            pl.BlockSpec((nb * tq, MLA_H * D_ROPE), lambda s, q: (s * nq + q, 0)),
            pl.BlockSpec((nb * k_len, KV_RANK), lambda s, q: (s, 0)),
            pl.BlockSpec((nb * k_len, D_ROPE), lambda s, q: (s, 0)),
            pl.BlockSpec((1, KV_RANK, MLA_H * D_NOPE), lambda s, q: (j, 0, 0)),
            pl.BlockSpec((1, KV_RANK, MLA_H * D_V), lambda s, q: (j, 0, 0)),
            pl.BlockSpec((1, MLA_H * D_V, D), lambda s, q: (j, 0, 0)),
        ],
        args=[src.array, mod, norm_g, qn, qr, ckv_all, kr_all, p["w_uk"], p["w_uv"], p["w_o"]],
        dst=dst, block_rows=rows, out_index=lambda s, q: (s * nq + q + o_first, 0),
        scratch_shapes=[pltpu.VMEM((nb * k_len, MLA_H * D_NOPE), BF16),
                        pltpu.VMEM((nb * k_len, MLA_H * D_V), BF16),
                        pltpu.VMEM((nb * tq, MLA_H * D_V), BF16)],
        name="mla_attn")


def kernel(x_prompt, x_sample, state_rwkv, cache_mla_ckv, cache_mla_krope, c, c_ctx, mod_w, mod_b, norm_g,
           mlp_w1, mlp_w2, fft_w_out, conv_w_in, conv_w, conv_w_out, rwkv_mu, rwkv_w_r, rwkv_w_k, rwkv_w_v,
           rwkv_w_o, rwkv_w0, rwkv_w_l1, rwkv_w_l2, rwkv_a0, rwkv_a_l1, rwkv_a_l2, rwkv_g_l1, rwkv_g_l2,
           rwkv_k_k, rwkv_k_a, rwkv_r_k, rwkv_ln_g, rwkv_ln_b, mla_w_dq, mla_g_q, mla_w_uq, mla_w_dkv,
           mla_g_kv, mla_w_uk, mla_w_uv, mla_w_o):
    batch, seq, _ = x_prompt.shape
    dec_batch, dec_seq, _ = x_sample.shape
    past_len = cache_mla_ckv.shape[2]
    assert (batch * seq) % TM == 0 and TM % seq == 0 and dec_seq == TM and seq % CHUNK == 0

    np_tiles = batch * seq // TM
    ns_tiles = dec_batch * dec_seq // TM
    total_rows = (np_tiles + ns_tiles) * TM
    cs = jnp.concatenate([c_ctx[None, :], c, jnp.zeros((MOD_ROWS - 1 - dec_batch, D), F32)], axis=0)
    mod_b3 = mod_b.reshape(DEPTH, 1, N_MOD * D)
    mod = _modulation(cs, mod_w, mod_b3, 0)

    grp_p = lambda i: 0
    grp_s = lambda i: 1 + i
    grp_all = lambda i: jnp.maximum(i - (np_tiles - 1), 0)
    new_rwkv, new_ckv, new_krope = [], [], []
    streams = [(True, seq, batch, grp_p), (False, dec_seq, dec_batch, grp_s)]
    cur = [_Rows(x_prompt.reshape(batch * seq, D), 0, np_tiles),
           _Rows(x_sample.reshape(dec_batch * dec_seq, D), 0, ns_tiles)]

    for i in range(DEPTH):
        kind, j = i % 4, i // 4
        both_shared = cur[0].array is cur[1].array
        if kind == 1 and both_shared:
            shared = _conv(_Rows(cur[0].array, 0, np_tiles + ns_tiles), _Dest(total_rows, 0, True), mod, norm_g,
                           conv_w_in, conv_w, conv_w_out, i, j, (seq, dec_seq), grp_all, n_first=np_tiles)
            cur = [c._replace(array=shared) for c in cur]
        for idx, (is_prompt, slen, nseq, grp) in enumerate(streams):
            if kind == 1 and both_shared:
                break
            src = cur[idx]
            sharing = src.array.shape[0] == total_rows
            dst = _Dest(total_rows, src.first, True) if sharing else _Dest(src.ntiles * TM, 0, False)
            shared = None
            if kind == 0:
                shared = _fnet(src, dst, mod, norm_g, fft_w_out, i, j, slen, grp)
            elif kind == 1:
                shared = _conv(src, dst, mod, norm_g, conv_w_in, conv_w, conv_w_out, i, j, (slen,), grp)
            elif kind == 2:
                p = dict(mu=rwkv_mu, w_r=rwkv_w_r, w_k=rwkv_w_k, w_v=rwkv_w_v, w_o=rwkv_w_o, w0=rwkv_w0,
                         w_l1=rwkv_w_l1, w_l2=rwkv_w_l2, a0=rwkv_a0, a_l1=rwkv_a_l1, a_l2=rwkv_a_l2,
                         g_l1=rwkv_g_l1, g_l2=rwkv_g_l2, k_k=rwkv_k_k, k_a=rwkv_k_a,
                         r_k=rwkv_r_k.reshape(-1, D), ln_g=rwkv_ln_g, ln_b=rwkv_ln_b)
                assert sharing
                if is_prompt:
                    proj = _rwkv_proj(_Rows(src.array, 0, np_tiles + ns_tiles), mod, norm_g, p, i, j,
                                      (seq, dec_seq), grp_all, n_first=np_tiles)
                    y, s_fin = _rwkv_scan(proj, None, nseq, slen, True)
                    new_rwkv.append(s_fin)
                else:
                    (y,) = _rwkv_scan(proj, state_rwkv[:, j], nseq, slen, False,
                                      first_seq=src.first * TM // slen)
                shared = _rwkv_out(src, dst, mod, norm_g, y, proj[10], proj[9], p, i, j, grp)
            else:
                perm = _rope_swap_perm()
                w_uq = mla_w_uq[j].reshape(Q_RANK, MLA_H, D_NOPE + D_ROPE)
                w_uq_rope = w_uq[:, :, D_NOPE:]
                w_dkv_r = mla_w_dkv[j][:, KV_RANK:]
                p = dict(w_dq=mla_w_dq, g_q=mla_g_q, g_kv=mla_g_kv, w_uk=mla_w_uk, w_uv=mla_w_uv, w_o=mla_w_o,
                         w_uq_nope=w_uq[:, :, :D_NOPE].reshape(Q_RANK, MLA_H * D_NOPE),
                         w_uq_rope=w_uq_rope.reshape(Q_RANK, MLA_H * D_ROPE),
                         w_uq_rope_sw=w_uq_rope[:, :, perm].reshape(Q_RANK, MLA_H * D_ROPE),
                         w_dkv_c=mla_w_dkv[j][:, :KV_RANK], w_dkv_r=w_dkv_r, w_dkv_r_sw=w_dkv_r[:, perm])
                qn, qr, ckv, kr = _mla_proj(src, mod, norm_g, p, i, j, not is_prompt, grp)
                if is_prompt:
                    new_ckv.append(ckv.reshape(batch, seq, KV_RANK))
                    new_krope.append(kr.reshape(batch, seq, D_ROPE))
                    shared = _mla_attn(src, dst, mod, norm_g, qn, qr, ckv, kr, p, i, j, nseq, slen, slen,
                                       TM // slen, slen, lambda s: 0)
                else:
                    klen = past_len + slen
                    ckv_all = jnp.concatenate([cache_mla_ckv[:, j], ckv.reshape(nseq, slen, KV_RANK)], axis=1)
                    kr_all = jnp.concatenate([cache_mla_krope[:, j], kr.reshape(nseq, slen, D_ROPE)], axis=1)
                    shared = _mla_attn(src, dst, mod, norm_g, qn, qr, ckv_all.reshape(nseq * klen, KV_RANK),
                                       kr_all.reshape(nseq * klen, D_ROPE), p, i, j, nseq, slen, klen, 1, 256,
                                       lambda s: 1 + s)
            cur[idx] = _Rows(shared, dst.first, src.ntiles)
            if sharing:
                cur[1 - idx] = cur[1 - idx]._replace(array=shared)
        if i < DEPTH - 1:
            srcs = [_Rows(cur[0].array, 0, np_tiles + ns_tiles)] if cur[0].array is cur[1].array else cur
            both, mod = _mlp(srcs, _Dest(total_rows, 0, False), mod, norm_g, mlp_w1, mlp_w2, i, grp_all,
                             next_mod=(cs, mod_w, mod_b3, i + 1))
            cur = [_Rows(both, 0, np_tiles), _Rows(both, np_tiles, ns_tiles)]
        else:
            y_prompt, y_sample = [
                _mlp([cur[idx]], _Dest(cur[idx].ntiles * TM, 0, False), mod, norm_g, mlp_w1, mlp_w2, i,
                     streams[idx][3]) for idx in range(2)]

    return (y_prompt.reshape(batch, seq, D), y_sample.reshape(dec_batch, dec_seq, D),
            jnp.stack(new_rwkv, axis=1), jnp.stack(new_ckv, axis=1), jnp.stack(new_krope, axis=1))
```
